```python
import math
import jax
import jax.numpy as jnp
from jax import lax
import numpy as np

D_MODEL = 1024
BATCH = 4
SEQ = 8192
DEPTH = 2

GRID_W = 64
CTX_LEN = 256
HEAD_DIM = 64
ROPE_THETA = 10000.0
NORM_EPS = 1e-6
Q_BLOCK = 128
ATTN_SCALE = HEAD_DIM ** -0.5

GDN_HEADS = 4
GDN_DK = 64
GDN_DV = 64
GDN_CONV = 5
GDN_CHUNK = 64
GDN_CONV_CH = 2 * GDN_HEADS * GDN_DK + GDN_HEADS * GDN_DV

MLA_HEADS = 4
MLA_Q_RANK = 256
MLA_KV_RANK = 128
MLA_NOPE = 64
MLA_ROPE = 32
MLA_V = 64
MLA_SCALE = (MLA_NOPE + MLA_ROPE) ** -0.5

SWA_HEADS = 4
SWA_KV_HEADS = 2
WINDOW = 128

GQA_HEADS = 4
GQA_KV_HEADS = 2

D_MIX = GDN_HEADS * GDN_DV + MLA_HEADS * MLA_V + SWA_HEADS * HEAD_DIM + GQA_HEADS * HEAD_DIM

IN_SPLITS = (
    GDN_HEADS * GDN_DK, GDN_HEADS * GDN_DK, GDN_HEADS * GDN_DV, GDN_HEADS * GDN_DV,
    2 * GDN_HEADS, 2 * GDN_HEADS,
    MLA_Q_RANK, MLA_KV_RANK, MLA_ROPE,
    SWA_HEADS * HEAD_DIM, SWA_KV_HEADS * HEAD_DIM, SWA_KV_HEADS * HEAD_DIM,
    GQA_HEADS * HEAD_DIM, GQA_KV_HEADS * HEAD_DIM, GQA_KV_HEADS * HEAD_DIM,
)
D_IN = sum(IN_SPLITS)

N_EXPERTS = 64
TOP_K = 6
D_EXPERT = 384
ROUTED_SCALE = 2.5
MOE_BLOCK = 128

kernel_name = 'hybrid_dit_prefix_block'


def rms_norm(x, g):
    xf = x.astype(jnp.float32)
    y = xf * lax.rsqrt(jnp.mean(xf * xf, axis=-1, keepdims=True) + NORM_EPS)
    return (y * g.astype(jnp.float32)).astype(x.dtype)


def l2_normalize(x):
    return x * lax.rsqrt(jnp.sum(x * x, axis=-1, keepdims=True) + NORM_EPS)


def modulate(h, shift, scale):
    return h * (1 + scale) + shift


def swiglu(x, w_gu, w_down):
    g, u = jnp.split(x @ w_gu, 2, axis=-1)
    return (jax.nn.silu(g) * u) @ w_down


def split_columns(p):
    idx = [int(i) for i in np.cumsum(IN_SPLITS)[:-1]]
    return jnp.split(p, idx, axis=-1)


def axial_rope_tables(n_rows, rot_dim):
    rows = jnp.repeat(jnp.arange(n_rows), GRID_W).astype(jnp.float32)
    cols = jnp.tile(jnp.arange(GRID_W), n_rows).astype(jnp.float32)
    axis_dim = rot_dim // 2
    inv_freq = ROPE_THETA ** (-jnp.arange(0, axis_dim, 2, dtype=jnp.float32) / axis_dim)
    ang_r = rows[:, None] * inv_freq
    ang_c = cols[:, None] * inv_freq
    ang = jnp.concatenate([ang_r, ang_r, ang_c, ang_c], axis=-1)
    return jnp.cos(ang), jnp.sin(ang)


def rotate_half(x):
    x1, x2 = jnp.split(x, 2, axis=-1)
    return jnp.concatenate([-x2, x1], axis=-1)


def apply_axial_rope(x, rope):
    cos, sin = rope
    xr, xc = jnp.split(x, 2, axis=-1)
    rot = jnp.concatenate([rotate_half(xr), rotate_half(xc)], axis=-1)
    out = x.astype(jnp.float32) * cos[None, :, None, :] + rot.astype(jnp.float32) * sin[None, :, None, :]
    return out.astype(x.dtype)


def depthwise_conv_centred(x, w):
    k = w.shape[0]
    return lax.conv_general_dilated(
        x, w[:, None, :].astype(x.dtype), window_strides=(1,), padding=[((k - 1) // 2, k // 2)],
        dimension_numbers=('NWC', 'WIO', 'NWC'), feature_group_count=x.shape[-1])


def gated_delta_chunked(q, k, v, log_a, beta, s0):
    bsz, nh, t_len, dk = q.shape
    dv = v.shape[-1]
    c = GDN_CHUNK
    n = t_len // c
    q = q.reshape(bsz, nh, n, c, dk)
    k = k.reshape(bsz, nh, n, c, dk)
    v = v.reshape(bsz, nh, n, c, dv)
    beta = beta.reshape(bsz, nh, n, c)
    g = jnp.cumsum(log_a.reshape(bsz, nh, n, c), axis=-1)
    incl = jnp.tril(jnp.ones((c, c), bool))
    strict = jnp.tril(jnp.ones((c, c), bool), -1)
    decay_ij = jnp.exp(jnp.where(incl, g[..., :, None] - g[..., None, :], -jnp.inf))
    kb = k * beta[..., None]
    a_mat = jnp.where(strict, jnp.einsum('bhncd,bhnsd->bhncs', kb, k) * decay_ij, 0.0)
    eye = jnp.eye(c, dtype=jnp.float32)
    t_inv = lax.linalg.triangular_solve(eye + a_mat, jnp.broadcast_to(eye, a_mat.shape),
                                        left_side=True, lower=True, unit_diagonal=True)
    u = t_inv @ (v * beta[..., None])
    w = t_inv @ (kb * jnp.exp(g)[..., None])
    qk = jnp.einsum('bhncd,bhnsd->bhncs', q, k) * decay_ij

    def step(s, xs):
        q_i, k_i, u_i, w_i, g_i, qk_i = xs
        v_new = u_i - jnp.einsum('bhcd,bhde->bhce', w_i, s)
        o_i = (jnp.einsum('bhcd,bhde->bhce', q_i * jnp.exp(g_i)[..., None], s)
               + jnp.einsum('bhcs,bhse->bhce', qk_i, v_new))
        g_last = g_i[..., -1:]
        s = (s * jnp.exp(g_last)[..., None]
             + jnp.einsum('bhcd,bhce->bhde', k_i * jnp.exp(g_last - g_i)[..., None], v_new))
        return s, o_i

    xs = tuple(jnp.moveaxis(a, 2, 0) for a in (q, k, u, w, g, qk))
    s_fin, o = lax.scan(step, s0, xs)
    return jnp.moveaxis(o, 0, 2).reshape(bsz, nh, t_len, dv), s_fin


def gdn_prepare(q, k, v, a, b, conv_w, a_log, dt_bias):
    bsz, t_len, _ = q.shape
    qkv = jax.nn.silu(depthwise_conv_centred(jnp.concatenate([q, k, v], axis=-1), conv_w)).astype(jnp.float32)
    q, k, v = jnp.split(qkv, [GDN_HEADS * GDN_DK, 2 * GDN_HEADS * GDN_DK], axis=-1)
    q = l2_normalize(q.reshape(bsz, t_len, GDN_HEADS, GDN_DK)) * GDN_DK ** -0.5
    k = l2_normalize(k.reshape(bsz, t_len, GDN_HEADS, GDN_DK))
    v = v.reshape(bsz, t_len, GDN_HEADS, GDN_DV)
    a = a.astype(jnp.float32).reshape(bsz, t_len, 2, GDN_HEADS)
    b = b.astype(jnp.float32).reshape(bsz, t_len, 2, GDN_HEADS)
    log_a = -jnp.exp(a_log.astype(jnp.float32)) * jax.nn.softplus(a + dt_bias.astype(jnp.float32))
    beta = jax.nn.sigmoid(b)
    to_bhtd = lambda t: t.transpose(0, 2, 1, 3)
    return (to_bhtd(q), to_bhtd(k), to_bhtd(v),
            log_a.transpose(2, 0, 3, 1), beta.transpose(2, 0, 3, 1))


def gdn_bidirectional(ctx_in, lat_in):
    qc, kc, vc, lac, bc = ctx_in
    ql, kl, vl, lal, bl = lat_in
    bsz, nh, _, dk = qc.shape
    s0 = jnp.zeros((bsz, nh, dk, vc.shape[-1]), jnp.float32)
    oc_f, sc_f = gated_delta_chunked(qc, kc, vc, lac[0], bc[0], s0)
    ol_f, _ = gated_delta_chunked(ql, kl, vl, lal[0], bl[0], sc_f)
    r = lambda t: jnp.flip(t, axis=2)
    oc_b, sc_b = gated_delta_chunked(r(qc), r(kc), r(vc), r(lac[1]), r(bc[1]), s0)
    ol_b, _ = gated_delta_chunked(r(ql), r(kl), r(vl), r(lal[1]), r(bl[1]), sc_b)
    return oc_f + r(oc_b), ol_f + r(ol_b)


def gdn_output(o, z, norm_g):
    bsz, nh, t_len, dv = o.shape
    y = rms_norm(o.transpose(0, 2, 1, 3), norm_g) * jax.nn.silu(z.astype(jnp.float32).reshape(bsz, t_len, nh, dv))
    return y.reshape(bsz, t_len, nh * dv).astype(z.dtype)


def mla_heads(cq, ckv, kr, qn_g, kvn_g, w_uq, w_ukv, qk_g, rope):
    bsz, t_len, _ = cq.shape
    q = (rms_norm(cq, qn_g) @ w_uq).reshape(bsz, t_len, MLA_HEADS, MLA_NOPE + MLA_ROPE)
    kv = (rms_norm(ckv, kvn_g) @ w_ukv).reshape(bsz, t_len, MLA_HEADS, MLA_NOPE + MLA_V)
    k_nope, v = jnp.split(kv, [MLA_NOPE], axis=-1)
    q_nope = rms_norm(q[..., :MLA_NOPE], qk_g[0, :MLA_NOPE])
    q_pe = rms_norm(q[..., MLA_NOPE:], qk_g[0, MLA_NOPE:])
    k_nope = rms_norm(k_nope, qk_g[1, :MLA_NOPE])
    k_pe = rms_norm(kr[:, :, None, :], qk_g[1, MLA_NOPE:])
    if rope is not None:
        q_pe = apply_axial_rope(q_pe, rope)
        k_pe = apply_axial_rope(k_pe, rope)
    q = jnp.concatenate([q_nope, q_pe], axis=-1)
    k = jnp.concatenate([k_nope, jnp.broadcast_to(k_pe, (bsz, t_len, MLA_HEADS, MLA_ROPE))], axis=-1)
    return q, k, v


def gqa_heads(q, k, v, qk_g, rope):
    bsz, t_len, _ = q.shape
    q = rms_norm(q.reshape(bsz, t_len, -1, HEAD_DIM), qk_g[0])
    k = rms_norm(k.reshape(bsz, t_len, -1, HEAD_DIM), qk_g[1])
    v = v.reshape(bsz, t_len, -1, HEAD_DIM)
    if rope is not None:
        q = apply_axial_rope(q, rope)
        k = apply_axial_rope(k, rope)
    return q, k, v


def softmax_attend(q, k, v, scale, sink=None):
    bsz, tq, hq, d = q.shape
    hkv = k.shape[2]
    grp = hq // hkv
    dv = v.shape[-1]
    nb = tq // Q_BLOCK
    qb = q.reshape(bsz, nb, Q_BLOCK, hkv, grp, d).transpose(1, 0, 2, 3, 4, 5)

    def block(q_blk):
        s = jnp.einsum('bqhgd,bshd->bhgqs', q_blk, k, preferred_element_type=jnp.float32) * scale
        if sink is not None:
            s_sink = jnp.broadcast_to(sink.astype(jnp.float32).reshape(1, hkv, grp, 1, 1), s.shape[:-1] + (1,))
            p = jax.nn.softmax(jnp.concatenate([s, s_sink], axis=-1), axis=-1)[..., :-1]
        else:
            p = jax.nn.softmax(s, axis=-1)
        return jnp.einsum('bhgqs,bshd->bqhgd', p.astype(v.dtype), v)

    o = lax.map(block, qb)
    return o.transpose(1, 0, 2, 3, 4, 5).reshape(bsz, tq, hq, dv)


def banded_window_attention(q, k, v, k_ctx, v_ctx, sink, scale):
    bsz, t_len, hq, d = q.shape
    hkv = k.shape[2]
    grp = hq // hkv
    dv = v.shape[-1]
    w = WINDOW
    nb = t_len // w
    pad = ((0, 0), (w, w), (0, 0), (0, 0))
    kp = jnp.pad(k, pad)
    vp = jnp.pad(v, pad)
    kb = jnp.concatenate([kp[:, j * w:j * w + t_len].reshape(bsz, nb, w, hkv, d) for j in range(3)], axis=2)
    vb = jnp.concatenate([vp[:, j * w:j * w + t_len].reshape(bsz, nb, w, hkv, dv) for j in range(3)], axis=2)
    q_pos = jnp.arange(t_len).reshape(nb, w)
    k_pos = jnp.arange(nb)[:, None] * w + jnp.arange(3 * w)[None, :] - w
    valid = ((jnp.abs(q_pos[:, :, None] - k_pos[:, None, :]) <= w)
             & (k_pos >= 0)[:, None, :] & (k_pos < t_len)[:, None, :])
    qb = q.reshape(bsz, nb, w, hkv, grp, d)
    s_loc = jnp.einsum('bnqhgd,bnkhd->bnhgqk', qb, kb, preferred_element_type=jnp.float32) * scale
    s_loc = jnp.where(valid[None, :, None, None], s_loc, -jnp.inf)
    s_ctx = jnp.einsum('bnqhgd,bchd->bnhgqc', qb, k_ctx, preferred_element_type=jnp.float32) * scale
    s_sink = jnp.broadcast_to(sink.astype(jnp.float32).reshape(1, 1, hkv, grp, 1, 1), s_loc.shape[:-1] + (1,))
    p = jax.nn.softmax(jnp.concatenate([s_loc, s_ctx, s_sink], axis=-1), axis=-1)
    n_loc = 3 * w
    n_ctx = k_ctx.shape[1]
    o = (jnp.einsum('bnhgqk,bnkhd->bnqhgd', p[..., :n_loc].astype(v.dtype), vb)
         + jnp.einsum('bnhgqc,bchd->bnqhgd', p[..., n_loc:n_loc + n_ctx].astype(v.dtype), v_ctx))
    return o.reshape(bsz, t_len, hq, dv)


def token_mixers(pc, pl, conv_w, a_log, dt_bias, gdn_g, mla_qn_g, mla_kvn_g, w_uq, w_ukv, mla_qk_g,
                 swa_qk_g, swa_sink, gqa_qk_g, rope_hd, rope_mla, need_ctx):
    (aq_c, ak_c, av_c, az_c, aa_c, ab_c, mq_c, mkv_c, mkr_c, sq_c, sk_c, sv_c, gq_c, gk_c, gv_c) = pc
    (aq_l, ak_l, av_l, az_l, aa_l, ab_l, mq_l, mkv_l, mkr_l, sq_l, sk_l, sv_l, gq_l, gk_l, gv_l) = pl
    bsz, t_len, _ = aq_l.shape
    cat = lambda a, b: jnp.concatenate([a, b], axis=1)
    oa_c, oa_l = gdn_bidirectional(gdn_prepare(aq_c, ak_c, av_c, aa_c, ab_c, conv_w, a_log, dt_bias),
                                   gdn_prepare(aq_l, ak_l, av_l, aa_l, ab_l, conv_w, a_log, dt_bias))
    ya_l = gdn_output(oa_l, az_l, gdn_g)
    bq_c, bk_c, bv_c = mla_heads(mq_c, mkv_c, mkr_c, mla_qn_g, mla_kvn_g, w_uq, w_ukv, mla_qk_g, None)
    bq_l, bk_l, bv_l = mla_heads(mq_l, mkv_l, mkr_l, mla_qn_g, mla_kvn_g, w_uq, w_ukv, mla_qk_g, rope_mla)
    yb_l = softmax_attend(bq_l, cat(bk_c, bk_l), cat(bv_c, bv_l), MLA_SCALE)
    cq_c, ck_c, cv_c = gqa_heads(sq_c, sk_c, sv_c, swa_qk_g, None)
    cq_l, ck_l, cv_l = gqa_heads(sq_l, sk_l, sv_l, swa_qk_g, rope_hd)
    yc_l = banded_window_attention(cq_l, ck_l, cv_l, ck_c, cv_c, swa_sink, ATTN_SCALE)
    dq_c, dk_c, dv_c = gqa_heads(gq_c, gk_c, gv_c, gqa_qk_g, None)
    dq_l, dk_l, dv_l = gqa_heads(gq_l, gk_l, gv_l, gqa_qk_g, rope_hd)
    yd_l = softmax_attend(dq_l, cat(dk_c, dk_l), cat(dv_c, dv_l), ATTN_SCALE)
    mix_l = jnp.concatenate([ya_l, yb_l.reshape(bsz, t_len, -1), yc_l.reshape(bsz, t_len, -1),
                             yd_l.reshape(bsz, t_len, -1)], axis=-1)
    if not need_ctx:
        return None, mix_l
    n_ctx = aq_c.shape[1]
    ya_c = gdn_output(oa_c, az_c, gdn_g)
    yb_c = softmax_attend(bq_c, bk_c, bv_c, MLA_SCALE)
    yc_c = softmax_attend(cq_c, ck_c, cv_c, ATTN_SCALE, sink=swa_sink)
    yd_c = softmax_attend(dq_c, dk_c, dv_c, ATTN_SCALE)
    mix_c = jnp.concatenate([ya_c, yb_c.reshape(bsz, n_ctx, -1), yc_c.reshape(bsz, n_ctx, -1),
                             yd_c.reshape(bsz, n_ctx, -1)], axis=-1)
    return mix_c, mix_l


def moe_ffn(h, router_w, router_bias, w_gu, w_down, shared_gu, shared_down):
    n_tok, d = h.shape
    n_assign = n_tok * TOP_K
    scores = jax.nn.sigmoid(jnp.dot(h, router_w, preferred_element_type=jnp.float32))
    _, idx = lax.top_k(scores + router_bias.astype(jnp.float32), TOP_K)
    gates = jnp.take_along_axis(scores, idx, axis=-1)
    gates = gates / jnp.sum(gates, axis=-1, keepdims=True) * ROUTED_SCALE
    flat_e = idx.reshape(-1)
    order = jnp.argsort(flat_e)
    sorted_e = flat_e[order]
    counts = jnp.bincount(flat_e, length=N_EXPERTS)
    padded = (counts + MOE_BLOCK - 1) // MOE_BLOCK * MOE_BLOCK
    pad_end = jnp.cumsum(padded)
    run_start = jnp.cumsum(counts) - counts
    row = (pad_end - padded)[sorted_e] + jnp.arange(n_assign) - run_start[sorted_e]
    n_rows = -(-n_assign // MOE_BLOCK) * MOE_BLOCK + N_EXPERTS * MOE_BLOCK
    n_blk = n_rows // MOE_BLOCK
    src = jnp.full((n_rows,), n_tok, jnp.int32).at[row].set((order // TOP_K).astype(jnp.int32))
    row_gate = jnp.zeros((n_rows,), jnp.float32).at[row].set(gates.reshape(-1)[order])
    blk_expert = jnp.minimum(jnp.searchsorted(pad_end, jnp.arange(n_blk) * MOE_BLOCK, side='right'), N_EXPERTS - 1)
    h_pad = jnp.concatenate([h, jnp.zeros((1, d), h.dtype)], axis=0)

    def expert_block(args):
        src_b, gate_b, e = args
        y = swiglu(h_pad[src_b], w_gu[e], w_down[e])
        return y * gate_b[:, None].astype(y.dtype)

    y_rows = lax.map(expert_block, (src.reshape(n_blk, MOE_BLOCK), row_gate.reshape(n_blk, MOE_BLOCK), blk_expert))
    routed = jax.ops.segment_sum(y_rows.reshape(n_rows, d), src, num_segments=n_tok + 1)[:n_tok]
    return routed + swiglu(h, shared_gu, shared_down)


def setup_inputs(seed: int = 0) -> dict:
    key = jax.random.key(seed)
    keys = iter(jax.random.split(key, 40))
    f32 = jnp.float32
    L, D = DEPTH, D_MODEL

    def normal(shape, scale):
        return jax.random.normal(next(keys), shape, f32) * scale

    def gain(shape):
        return 1.0 + normal(shape, 0.05)

    a_log = jnp.log(jax.random.uniform(next(keys), (L, 2, GDN_HEADS), f32, 1.0, 16.0))
    dt = jnp.exp(jax.random.uniform(next(keys), (L, 2, GDN_HEADS), f32, math.log(1e-3), math.log(1e-1)))
    dt_bias = dt + jnp.log(-jnp.expm1(-dt))
    return {
        'x': normal((BATCH, SEQ, D), 1.0),
        'c': normal((BATCH, D), 1.0),
        'ctx': normal((BATCH, CTX_LEN, D), 1.0),
        'c_ctx': normal((D,), 1.0),
        'w_mod': normal((L, D, 6 * D), 0.5 * D ** -0.5),
        'b_mod': normal((L, 6 * D), 0.02),
        'norm1_g': gain((L, D)),
        'norm2_g': gain((L, D)),
        'w_in': normal((L, D, D_IN), D ** -0.5),
        'gdn_conv_w': normal((L, GDN_CONV, GDN_CONV_CH), GDN_CONV ** -0.5),
        'gdn_a_log': a_log,
        'gdn_dt_bias': dt_bias,
        'gdn_norm_g': gain((L, GDN_DV)),
        'mla_qn_g': gain((L, MLA_Q_RANK)),
        'mla_kvn_g': gain((L, MLA_KV_RANK)),
        'mla_w_uq': normal((L, MLA_Q_RANK, MLA_HEADS * (MLA_NOPE + MLA_ROPE)), MLA_Q_RANK ** -0.5),
        'mla_w_ukv': normal((L, MLA_KV_RANK, MLA_HEADS * (MLA_NOPE + MLA_V)), MLA_KV_RANK ** -0.5),
        'mla_qk_g': gain((L, 2, MLA_NOPE + MLA_ROPE)),
        'swa_qk_g': gain((L, 2, HEAD_DIM)),
        'swa_sink': normal((L, SWA_HEADS), 1.0),
        'gqa_qk_g': gain((L, 2, HEAD_DIM)),
        'w_out': normal((L, D_MIX, D), D_MIX ** -0.5),
        'router_w': normal((L, D, N_EXPERTS), D ** -0.5),
        'router_bias': normal((L, N_EXPERTS), 0.01),
        'exp_w_gu': normal((L, N_EXPERTS, D, 2 * D_EXPERT), D ** -0.5),
        'exp_w_down': normal((L, N_EXPERTS, D_EXPERT, D), D_EXPERT ** -0.5),
        'shared_w_gu': normal((L, D, 2 * D_EXPERT), D ** -0.5),
        'shared_w_down': normal((L, D_EXPERT, D), D_EXPERT ** -0.5),
    }


def reference(x, c, ctx, c_ctx, w_mod, b_mod, norm1_g, norm2_g, w_in, gdn_conv_w, gdn_a_log, gdn_dt_bias,
              gdn_norm_g, mla_qn_g, mla_kvn_g, mla_w_uq, mla_w_ukv, mla_qk_g, swa_qk_g, swa_sink, gqa_qk_g,
              w_out, router_w, router_bias, exp_w_gu, exp_w_down, shared_w_gu, shared_w_down):
    bsz, t_len, d = x.shape
    n_ctx = ctx.shape[1]
    n_rows = t_len // GRID_W
    rope_hd = axial_rope_tables(n_rows, HEAD_DIM)
    rope_mla = axial_rope_tables(n_rows, MLA_ROPE)
    silu_c = jax.nn.silu(c)
    silu_cc = jax.nn.silu(c_ctx)
    xl, xc = x, ctx
    for l in range(DEPTH):
        need_ctx = l < DEPTH - 1
        mod_l = jnp.split((silu_c @ w_mod[l] + b_mod[l])[:, None, :], 6, axis=-1)
        mod_c = jnp.split((silu_cc @ w_mod[l] + b_mod[l])[None, None, :], 6, axis=-1)
        hl = modulate(rms_norm(xl, norm1_g[l]), mod_l[0], mod_l[1])
        hc = modulate(rms_norm(xc, norm1_g[l]), mod_c[0], mod_c[1])
        mix_c, mix_l = token_mixers(
            split_columns(hc @ w_in[l]), split_columns(hl @ w_in[l]),
            gdn_conv_w[l], gdn_a_log[l], gdn_dt_bias[l], gdn_norm_g[l],
            mla_qn_g[l], mla_kvn_g[l], mla_w_uq[l], mla_w_ukv[l], mla_qk_g[l],
            swa_qk_g[l], swa_sink[l], gqa_qk_g[l], rope_hd, rope_mla, need_ctx)
        xl = xl + mod_l[2] * (mix_l @ w_out[l])
        hl = modulate(rms_norm(xl, norm2_g[l]), mod_l[3], mod_l[4])
        if need_ctx:
            xc = xc + mod_c[2] * (mix_c @ w_out[l])
            hc = modulate(rms_norm(xc, norm2_g[l]), mod_c[3], mod_c[4])
            y = moe_ffn(jnp.concatenate([hc, hl], axis=1).reshape(-1, d), router_w[l], router_bias[l],
                        exp_w_gu[l], exp_w_down[l], shared_w_gu[l], shared_w_down[l]).reshape(bsz, n_ctx + t_len, d)
            xc = xc + mod_c[5] * y[:, :n_ctx]
            xl = xl + mod_l[5] * y[:, n_ctx:]
        else:
            y = moe_ffn(hl.reshape(-1, d), router_w[l], router_bias[l], exp_w_gu[l], exp_w_down[l],
                        shared_w_gu[l], shared_w_down[l]).reshape(bsz, t_len, d)
            xl = xl + mod_l[5] * y
    return xl
```

```python
import functools
import math

import numpy as np
import jax
import jax.numpy as jnp
from jax import lax
from jax.experimental import pallas as pl
from jax.experimental.pallas import tpu as pltpu

F32 = jnp.float32
BF16 = jnp.bfloat16

LANES = 128
TM = 256
VMEM_LIMIT = 56 * 1024 * 1024

D_MODEL = 1024
GRID_W = 64
HEAD_DIM = 64
ROPE_THETA = 10000.0
NORM_EPS = 1e-6
ATTN_SCALE = HEAD_DIM ** -0.5
GDN_HEADS = 4
GDN_DK = 64
GDN_CHUNK = 64
GDN_CONV = 5
MLA_HEADS = 4
MLA_Q_RANK = 256
MLA_KV_RANK = 128
MLA_NOPE = 64
MLA_ROPE = 32
MLA_V = 64
MLA_SCALE = (MLA_NOPE + MLA_ROPE) ** -0.5
SWA_HEADS = 4
SWA_KV_HEADS = 2
WINDOW = 128
GQA_HEADS = 4
GQA_KV_HEADS = 2
N_EXPERTS = 64
TOP_K = 6
D_EXPERT = 384
ROUTED_SCALE = 2.5
MOE_CHUNK = 1024
MOE_ROWS = 128

O_AQ, O_AK, O_AV, O_AZ, O_AA, O_AB = 0, 256, 512, 768, 1024, 1032
O_MQ, O_MKV, O_MKR = 1040, 1296, 1424
O_SQ, O_SK, O_SV = 1456, 1712, 1840
O_GQ, O_GK, O_GV = 1968, 2224, 2352
D_IN = 2480
C_GDN, C_MLA, C_MISC, C_SWA, C_GQA, N_COL = 0, 1024, 1408, 1536, 2560, 3584


def _cparams(sem):
    return pltpu.CompilerParams(dimension_semantics=sem, vmem_limit_bytes=VMEM_LIMIT)


def _dot(a, b):
    return jnp.dot(a, b, preferred_element_type=F32)


def _dot_nt(a, b):
    return lax.dot_general(a, b, (((1,), (1,)), ((), ())), preferred_element_type=F32)


def _dot_tn(a, b):
    return lax.dot_general(a, b, (((0,), (0,)), ((), ())), preferred_element_type=F32)


def _split3(x):
    hi = x.astype(BF16)
    r1 = x - hi.astype(F32)
    mid = r1.astype(BF16)
    lo = (r1 - mid.astype(F32)).astype(BF16)
    return hi, mid, lo


def _dot_exact_lhs(a_bf16, x):
    hi, mid, lo = _split3(x)
    return _dot(a_bf16, hi) + _dot(a_bf16, mid) + _dot(a_bf16, lo)


def _dot_exact_rhs(x, b_bf16):
    hi, mid, lo = _split3(x)
    return _dot(hi, b_bf16) + _dot(mid, b_bf16) + _dot(lo, b_bf16)


def _dot_hp(a, b):
    ah = a.astype(BF16)
    al = (a - ah.astype(F32)).astype(BF16)
    bh = b.astype(BF16)
    bl = (b - bh.astype(F32)).astype(BF16)
    return _dot(ah, bh) + _dot(ah, bl) + _dot(al, bh)


def _silu(x):
    return x * (1.0 / (1.0 + jnp.exp(-x)))


def _lane(shape):
    return lax.broadcasted_iota(jnp.int32, shape, len(shape) - 1)


def _mod_kernel(c_ref, w_ref, b_ref, o_ref):
    s = _silu(c_ref[...])
    o_ref[...] = _dot(s.astype(BF16), w_ref[...].astype(BF16)) + b_ref[...]


def _modulation(c8, w_mod_l, b_mod_l):
    d = c8.shape[1]
    n = w_mod_l.shape[1]
    return pl.pallas_call(
        _mod_kernel,
        grid=(n // d,),
        in_specs=[pl.BlockSpec((8, d), lambda j: (0, 0)),
                  pl.BlockSpec((d, d), lambda j: (0, j)),
                  pl.BlockSpec((1, d), lambda j: (0, j))],
        out_specs=pl.BlockSpec((8, d), lambda j: (0, j)),
        out_shape=jax.ShapeDtypeStruct((8, n), F32),
        compiler_params=_cparams(("arbitrary",)),
        name="modulation",
    )(c8, w_mod_l, b_mod_l.reshape(1, n))


def _rope_slab(x, cos, sin_signed, half):
    lane = _lane(x.shape)
    fwd = pltpu.roll(x, LANES - half, 1)
    bwd = pltpu.roll(x, half, 1)
    rot = jnp.where(lane % (2 * half) < half, fwd, bwd)
    return x * cos + rot * sin_signed


def _inproj_kernel(x_ref, mod_ref, g1_ref, win_ref, qn_g_ref, kvn_g_ref, wuq_ref, wukv_ref,
                   mqg_ref, mkg_ref, krg_ref, sqg_ref, skg_ref, gqg_ref, gkg_ref,
                   cos_hd_ref, sin_hd_ref, cos_mq_ref, sin_mq_ref, cos_kr_ref, sin_kr_ref,
                   ga_ref, misc_ref, mq_ref, mk_ref, mv_ref, sq_ref, sk_ref, sv_ref,
                   gq_ref, gk_ref, gv_ref):
    d = D_MODEL
    x = x_ref[0]
    m = mod_ref[0, 0]
    shift, scale = m[:, 0:d], m[:, d:2 * d]
    xn = x * lax.rsqrt(jnp.mean(x * x, axis=-1, keepdims=True) + NORM_EPS) * g1_ref[...]
    h = xn * (1.0 + scale) + shift
    p = _dot(h.astype(BF16), win_ref[...])

    ga_ref[0] = p[:, C_GDN:C_GDN + 1024]
    misc = p[:, C_MISC:C_MISC + LANES]
    misc_ref[0] = misc

    lane = _lane((TM, LANES))
    cos_hd, sin_hd = cos_hd_ref[...], sin_hd_ref[...]

    cq = p[:, C_MLA:C_MLA + MLA_Q_RANK]
    cqn = cq * lax.rsqrt(jnp.mean(cq * cq, axis=-1, keepdims=True) + NORM_EPS) * qn_g_ref[...]
    qup = _dot(cqn.astype(BF16), wuq_ref[...])
    ckv = p[:, C_MLA + MLA_Q_RANK:C_MLA + MLA_Q_RANK + MLA_KV_RANK]
    ckvn = ckv * lax.rsqrt(jnp.mean(ckv * ckv, axis=-1, keepdims=True) + NORM_EPS) * kvn_g_ref[...]
    kvup = _dot(ckvn.astype(BF16), wukv_ref[...])
    kr = jnp.where(lane < MLA_ROPE, misc, 0.0)
    kr = kr * lax.rsqrt(jnp.sum(kr * kr, axis=-1, keepdims=True) / MLA_ROPE + NORM_EPS) * krg_ref[...]
    kr = _rope_slab(kr, cos_kr_ref[...], sin_kr_ref[...], MLA_ROPE // 4)
    kpe = pltpu.roll(kr, MLA_NOPE, 1)
    is_nope = lane < MLA_NOPE
    for hh in range(MLA_HEADS):
        q = qup[:, hh * LANES:(hh + 1) * LANES]
        q2 = q * q
        ss_n = jnp.sum(jnp.where(is_nope, q2, 0.0), axis=-1, keepdims=True)
        ss_p = jnp.sum(jnp.where(is_nope, 0.0, q2), axis=-1, keepdims=True)
        r = jnp.where(is_nope, lax.rsqrt(ss_n / MLA_NOPE + NORM_EPS), lax.rsqrt(ss_p / MLA_ROPE + NORM_EPS))
        q = q * r * mqg_ref[...]
        q = _rope_slab(q, cos_mq_ref[...], sin_mq_ref[...], MLA_ROPE // 4)
        mq_ref[0, hh] = (q * MLA_SCALE).astype(BF16)
        k = kvup[:, hh * LANES:(hh + 1) * LANES]
        k = k * lax.rsqrt(jnp.sum(k * k, axis=-1, keepdims=True) / MLA_NOPE + NORM_EPS) * mkg_ref[...]
        mk_ref[0, hh] = (k + kpe).astype(BF16)
        mv_ref[0, hh] = kvup[:, (MLA_HEADS + hh) * LANES:(MLA_HEADS + hh + 1) * LANES].astype(BF16)

    def gqa_prep(base, qg_ref, kg_ref, q_ref, k_ref, v_ref, nq, nkv):
        for hh in range(nq):
            q = p[:, base + hh * LANES:base + (hh + 1) * LANES]
            q = q * lax.rsqrt(jnp.sum(q * q, axis=-1, keepdims=True) / HEAD_DIM + NORM_EPS) * qg_ref[...]
            q = _rope_slab(q, cos_hd, sin_hd, HEAD_DIM // 4)
            q_ref[0, hh] = (q * ATTN_SCALE).astype(BF16)
        for hh in range(nkv):
            k = p[:, base + (nq + hh) * LANES:base + (nq + hh + 1) * LANES]
            k = k * lax.rsqrt(jnp.sum(k * k, axis=-1, keepdims=True) / HEAD_DIM + NORM_EPS) * kg_ref[...]
            k = _rope_slab(k, cos_hd, sin_hd, HEAD_DIM // 4)
            k_ref[0, hh] = k.astype(BF16)
            v_ref[0, hh] = p[:, base + (nq + nkv + hh) * LANES:base + (nq + nkv + hh + 1) * LANES].astype(BF16)

    gqa_prep(C_SWA, sqg_ref, skg_ref, sq_ref, sk_ref, sv_ref, SWA_HEADS, SWA_KV_HEADS)
    gqa_prep(C_GQA, gqg_ref, gkg_ref, gq_ref, gk_ref, gv_ref, GQA_HEADS, GQA_KV_HEADS)


def _inproj(x, mod_tab, g1, win, lw, tabs, nbl):
    bsz, s, d = x.shape
    nb = s // TM
    row = lambda a: a.reshape(1, -1)
    full = lambda a: pl.BlockSpec(a.shape, lambda b, i: (0,) * a.ndim)
    tab_spec = pl.BlockSpec((TM, LANES), lambda b, i: (i, 0))
    head_out = lambda nh: pl.BlockSpec((1, nh, TM, LANES), lambda b, i: (b, 0, i, 0))
    head_shape = lambda nh: jax.ShapeDtypeStruct((bsz, nh, s, LANES), BF16)
    small = [row(lw['mla_qn_g']), row(lw['mla_kvn_g']), lw['wuq'], lw['wukv'],
             lw['mqg'], lw['mkg'], lw['krg'], lw['sqg'], lw['skg'], lw['gqg'], lw['gkg']]
    return pl.pallas_call(
        _inproj_kernel,
        grid=(bsz, nb),
        in_specs=[pl.BlockSpec((1, TM, d), lambda b, i: (b, i, 0)),
                  pl.BlockSpec((1, 1, 1, 6 * d), lambda b, i: (b, i // nbl, 0, 0)),
                  full(g1), full(win)] + [full(a) for a in small] + [tab_spec] * 6,
        out_specs=[pl.BlockSpec((1, TM, 1024), lambda b, i: (b, i, 0)),
                   pl.BlockSpec((1, TM, LANES), lambda b, i: (b, i, 0)),
                   head_out(4), head_out(4), head_out(4),
                   head_out(4), head_out(2), head_out(2),
                   head_out(4), head_out(2), head_out(2)],
        out_shape=[jax.ShapeDtypeStruct((bsz, s, 1024), F32),
                   jax.ShapeDtypeStruct((bsz, s, LANES), F32),
                   head_shape(4), head_shape(4), head_shape(4),
                   head_shape(4), head_shape(2), head_shape(2),
                   head_shape(4), head_shape(2), head_shape(2)],
        compiler_params=_cparams(("parallel", "parallel")),
        name="inproj",
    )(x, mod_tab, g1, win, *small, *tabs)


def _flash_kernel(*refs, grp, tq, nk, use_sink, has_prev):
    refs = list(refs)
    sink_ref = refs.pop(0) if use_sink else None
    q_ref, k_ref, v_ref = refs[0], refs[1], refs[2]
    o_ref, m_sc, l_sc, acc_sc = refs[-4], refs[-3], refs[-2], refs[-1]
    kj = pl.program_id(3)

    @pl.when(kj == 0)
    def _():
        m_sc[...] = jnp.full(m_sc.shape, -jnp.inf, F32)
        l_sc[...] = jnp.zeros(l_sc.shape, F32)
        acc_sc[...] = jnp.zeros(acc_sc.shape, F32)

    q = q_ref[0].reshape(grp * tq, LANES)
    s = _dot_nt(q, k_ref[0, 0])
    m_prev = m_sc[...]
    m_new = jnp.maximum(m_prev, jnp.max(s, axis=-1, keepdims=True))
    alpha = jnp.exp(m_prev - m_new)
    pr = jnp.exp(s - m_new)
    l_sc[...] = alpha * l_sc[...] + jnp.sum(pr, axis=-1, keepdims=True)
    acc_sc[...] = alpha * acc_sc[...] + _dot(pr.astype(BF16), v_ref[0, 0])
    m_sc[...] = m_new

    @pl.when(kj == nk - 1)
    def _():
        l = l_sc[...]
        acc = acc_sc[...]
        if use_sink:
            sk = sink_ref[0]
            m_old = m_sc[...]
            m_fin = jnp.maximum(m_old, sk)
            a = jnp.exp(m_old - m_fin)
            l = l * a + jnp.exp(sk - m_fin)
            acc = acc * a
        o_ref[0] = (acc / l).reshape(grp, tq, LANES).astype(o_ref.dtype)


def _flash(q, k, v, *, grp, tq, tk, q_blk0, nq, k_blk0, nk, sink_rows=None, prev=None):
    bsz, hq, s, _ = q.shape
    hkv = hq // grp
    use_sink = sink_rows is not None
    in_specs = []
    args = []
    if use_sink:
        in_specs.append(pl.BlockSpec((1, grp * tq, 1), lambda b, h, i, j: (h, 0, 0)))
        args.append(sink_rows)
    in_specs += [pl.BlockSpec((1, grp, tq, LANES), lambda b, h, i, j: (b, h, i + q_blk0, 0)),
                 pl.BlockSpec((1, 1, tk, LANES), lambda b, h, i, j: (b, h, j + k_blk0, 0)),
                 pl.BlockSpec((1, 1, tk, LANES), lambda b, h, i, j: (b, h, j + k_blk0, 0))]
    args += [q, k, v]
    aliases = {}
    if prev is not None:
        in_specs.append(pl.BlockSpec(memory_space=pl.ANY))
        aliases = {len(args): 0}
        args.append(prev)
    rows = grp * tq
    return pl.pallas_call(
        functools.partial(_flash_kernel, grp=grp, tq=tq, nk=nk, use_sink=use_sink, has_prev=prev is not None),
        grid=(bsz, hkv, nq, nk),
        in_specs=in_specs,
        out_specs=pl.BlockSpec((1, grp, tq, LANES), lambda b, h, i, j: (b, h, i + q_blk0, 0)),
        out_shape=jax.ShapeDtypeStruct((bsz, hq, s, LANES), BF16),
        scratch_shapes=[pltpu.VMEM((rows, 1), F32), pltpu.VMEM((rows, 1), F32),
                        pltpu.VMEM((rows, LANES), F32)],
        input_output_aliases=aliases,
        compiler_params=_cparams(("parallel", "parallel", "parallel", "arbitrary")),
        name="flash",
    )(*args)


def _swa_kernel(sink_ref, q_ref, k_ref, v_ref, o_ref, *, grp, t_len, n_ctx):
    w = WINDOW
    n = pl.program_id(2)
    start = pl.multiple_of(jnp.clip((n - 1) * w, 0, t_len - 3 * w), w)
    q = q_ref[0].reshape(grp * w, LANES)
    kl = k_ref[0, 0, pl.ds(start, 3 * w), :]
    vl = v_ref[0, 0, pl.ds(start, 3 * w), :]
    kc = k_ref[0, 0, pl.ds(t_len, n_ctx), :]
    vc = v_ref[0, 0, pl.ds(t_len, n_ctx), :]
    s_loc = _dot_nt(q, kl)
    qpos = n * w + (lax.broadcasted_iota(jnp.int32, s_loc.shape, 0) % w)
    kpos = start + lax.broadcasted_iota(jnp.int32, s_loc.shape, 1)
    s_loc = jnp.where(jnp.abs(qpos - kpos) <= w, s_loc, -jnp.inf)
    s_ctx = _dot_nt(q, kc)
    sk = sink_ref[0]
    m = jnp.maximum(jnp.maximum(jnp.max(s_loc, axis=-1, keepdims=True),
                                jnp.max(s_ctx, axis=-1, keepdims=True)), sk)
    p_loc = jnp.exp(s_loc - m)
    p_ctx = jnp.exp(s_ctx - m)
    l = (jnp.sum(p_loc, axis=-1, keepdims=True) + jnp.sum(p_ctx, axis=-1, keepdims=True)
         + jnp.exp(sk - m))
    o = _dot(p_loc.astype(BF16), vl) + _dot(p_ctx.astype(BF16), vc)
    o_ref[0] = (o / l).reshape(grp, w, LANES).astype(o_ref.dtype)


def _swa(q, k, v, sink_rows, *, grp, t_len, n_ctx):
    bsz, hq, s, _ = q.shape
    hkv = hq // grp
    w = WINDOW
    return pl.pallas_call(
        functools.partial(_swa_kernel, grp=grp, t_len=t_len, n_ctx=n_ctx),
        grid=(bsz, hkv, t_len // w),
        in_specs=[pl.BlockSpec((1, grp * w, 1), lambda b, h, n: (h, 0, 0)),
                  pl.BlockSpec((1, grp, w, LANES), lambda b, h, n: (b, h, n, 0)),
                  pl.BlockSpec((1, 1, s, LANES), lambda b, h, n: (b, h, 0, 0)),
                  pl.BlockSpec((1, 1, s, LANES), lambda b, h, n: (b, h, 0, 0))],
        out_specs=pl.BlockSpec((1, grp, w, LANES), lambda b, h, n: (b, h, n, 0)),
        out_shape=jax.ShapeDtypeStruct((bsz, hq, s, LANES), BF16),
        compiler_params=_cparams(("parallel", "parallel", "arbitrary")),
        name="swa",
    )(sink_rows, q, k, v)


def _gdn_prep_kernel(cur_ref, prev_ref, next_ref, misc_ref, cw_ref, alog_ref, dtb_ref,
                     qkv_ref, lab_ref, *, nbl, nb):
    i = pl.program_id(1)
    first = jnp.logical_or(i == 0, i == nbl)
    last = jnp.logical_or(i == nbl - 1, i == nb - 1)
    cur = cur_ref[0]
    prev = jnp.where(first, 0.0, prev_ref[0])
    nxt = jnp.where(last, 0.0, next_ref[0])
    xe = jnp.concatenate([prev, cur, nxt], axis=0)
    cw = cw_ref[...]
    acc = jnp.zeros(cur.shape, F32)
    for j in range(GDN_CONV):
        off = 8 + j - (GDN_CONV - 1) // 2
        acc = acc + xe[off:off + TM, :] * cw[j:j + 1, :]
    y = _silu(acc)
    lane = _lane((TM, LANES))
    lo = lane < GDN_DK
    for sl in range(6):
        t = y[:, sl * LANES:(sl + 1) * LANES]
        if sl < 4:
            t2 = t * t
            ss0 = jnp.sum(jnp.where(lo, t2, 0.0), axis=-1, keepdims=True)
            ss1 = jnp.sum(jnp.where(lo, 0.0, t2), axis=-1, keepdims=True)
            t = t * jnp.where(lo, lax.rsqrt(ss0 + NORM_EPS), lax.rsqrt(ss1 + NORM_EPS))
            if sl < 2:
                t = t * GDN_DK ** -0.5
        qkv_ref[0, :, sl * LANES:(sl + 1) * LANES] = t
    xm = misc_ref[0]
    za = xm + dtb_ref[...]
    softplus = jnp.maximum(za, 0.0) + jnp.log(1.0 + jnp.exp(-jnp.abs(za)))
    log_a = -jnp.exp(alog_ref[...]) * softplus
    beta = 1.0 / (1.0 + jnp.exp(-xm))
    is_a = (lane % 8) < 4
    vals = jnp.where(is_a, log_a, beta)
    lab_ref[0, 0] = pltpu.roll(vals, LANES - 32, 1)
    lab_ref[0, 1] = pltpu.roll(vals, LANES - 40, 1)


def _gdn_prep(ga, misc, cw, alog_row, dtb_row, nbl):
    bsz, s, _ = ga.shape
    nb = s // TM
    r8 = TM // 8
    n8 = s // 8
    return pl.pallas_call(
        functools.partial(_gdn_prep_kernel, nbl=nbl, nb=nb),
        grid=(bsz, nb),
        in_specs=[pl.BlockSpec((1, TM, 768), lambda b, i: (b, i, 0)),
                  pl.BlockSpec((1, 8, 768), lambda b, i: (b, jnp.maximum(i * r8 - 1, 0), 0)),
                  pl.BlockSpec((1, 8, 768), lambda b, i: (b, jnp.minimum((i + 1) * r8, n8 - 1), 0)),
                  pl.BlockSpec((1, TM, LANES), lambda b, i: (b, i, 0)),
                  pl.BlockSpec(cw.shape, lambda b, i: (0, 0)),
                  pl.BlockSpec((1, LANES), lambda b, i: (0, 0)),
                  pl.BlockSpec((1, LANES), lambda b, i: (0, 0))],
        out_specs=[pl.BlockSpec((1, TM, 768), lambda b, i: (b, i, 0)),
                   pl.BlockSpec((1, 2, TM, LANES), lambda b, i: (b, 0, i, 0))],
        out_shape=[jax.ShapeDtypeStruct((bsz, s, 768), F32),
                   jax.ShapeDtypeStruct((bsz, 2, s, LANES), F32)],
        compiler_params=_cparams(("parallel", "parallel")),
        name="gdn_prep",
    )(ga, ga, ga, misc, cw, alog_row, dtb_row)


def _gdn_scan_kernel(qkv_ref, lab_ref, labt_ref, o_ref, s_sc, *, reverse):
    c = GDN_CHUNK
    c2 = 2 * c
    i = pl.program_id(1)

    @pl.when(i == 0)
    def _():
        s_sc[...] = jnp.zeros(s_sc.shape, F32)

    ri = lax.broadcasted_iota(jnp.int32, (c2, c2), 0)
    ci = lax.broadcasted_iota(jnp.int32, (c2, c2), 1)
    same = (ri // c) == (ci // c)
    if reverse:
        incl = jnp.logical_and(same, ri <= ci)
        strict = jnp.logical_and(same, ri < ci)
    else:
        incl = jnp.logical_and(same, ri >= ci)
        strict = jnp.logical_and(same, ri > ci)
    eye = (ri == ci).astype(F32)
    r1 = lax.broadcasted_iota(jnp.int32, (c, c), 0)
    c1 = lax.broadcasted_iota(jnp.int32, (c, c), 1)
    cum_col = ((r1 <= c1) if reverse else (r1 >= c1)).astype(BF16)
    rr = lax.broadcasted_iota(jnp.int32, (c, c2), 0)
    cc = lax.broadcasted_iota(jnp.int32, (c, c2), 1) % c
    cum_row = ((rr >= cc) if reverse else (rr <= cc)).astype(BF16)
    lane = _lane((c, LANES))
    lo = lane < GDN_DK
    lane2 = _lane((1, c2))
    lo2 = lane2 < c
    g_last_row = 0 if reverse else c - 1

    def stack(x):
        return jnp.concatenate([jnp.where(lo, x, 0.0), jnp.where(lo, 0.0, x)], axis=0)

    def fold(mat):
        return mat[0:c, :] + mat[c:c2, :]

    order = range(TM // c - 1, -1, -1) if reverse else range(TM // c)
    for ch in order:
        rows = slice(ch * c, (ch + 1) * c)
        lab = lab_ref[0, 0, rows, :]
        g_cols = _dot_exact_lhs(cum_col, lab)
        la_rows = labt_ref[0, 0, :, rows]
        g_rows = _dot_exact_rhs(la_rows, cum_row)
        for pr in range(2):
            h0, h1 = 2 * pr, 2 * pr + 1
            q = qkv_ref[0, rows, pr * LANES:(pr + 1) * LANES]
            k = qkv_ref[0, rows, (2 + pr) * LANES:(3 + pr) * LANES]
            v = qkv_ref[0, rows, (4 + pr) * LANES:(5 + pr) * LANES]
            beta = jnp.where(lo, lab[:, 4 + h0:5 + h0], lab[:, 4 + h1:5 + h1])
            g = jnp.where(lo, g_cols[:, h0:h0 + 1], g_cols[:, h1:h1 + 1])
            g_st_col = jnp.concatenate([g_cols[:, h0:h0 + 1], g_cols[:, h1:h1 + 1]], axis=0)
            g_st_row = jnp.where(lo2, g_rows[h0:h0 + 1, :], g_rows[h1:h1 + 1, :])
            diff = g_st_col - g_st_row
            dec = jnp.where(incl, jnp.exp(jnp.where(incl, diff, 0.0)), 0.0)
            kb = k * beta
            eg = jnp.exp(g)
            k_st = stack(k).astype(BF16)
            a = jnp.where(strict, _dot_nt(stack(kb).astype(BF16), k_st) * dec, 0.0)
            qk = _dot_nt(stack(q).astype(BF16), k_st) * dec
            tinv = eye - a
            pw = a
            for _ in range(5):
                pw = _dot_hp(pw, pw)
                tinv = tinv + _dot_hp(tinv, pw)
            tf = fold(tinv).astype(BF16)
            u = _dot(tf, stack(v * beta).astype(BF16))
            wm = _dot(tf, stack(kb * eg).astype(BF16))
            st = s_sc[pr]
            st_b = st.astype(BF16)
            v_new = u - _dot(wm.astype(BF16), st_b)
            o = _dot((q * eg).astype(BF16), st_b) + _dot(fold(qk).astype(BF16), stack(v_new).astype(BF16))
            o_ref[0, rows, pr * LANES:(pr + 1) * LANES] = o
            gl = g[g_last_row:g_last_row + 1, :]
            kd = k * jnp.exp(gl - g)
            upd = _dot_tn(kd.astype(BF16), v_new.astype(BF16))
            gl_col = jnp.concatenate([jnp.broadcast_to(g_cols[g_last_row:g_last_row + 1, h0:h0 + 1], (c, 1)),
                                      jnp.broadcast_to(g_cols[g_last_row:g_last_row + 1, h1:h1 + 1], (c, 1))],
                                     axis=0)
            s_sc[pr] = st * jnp.exp(gl_col) + jnp.where(same, upd, 0.0)


def _gdn_scan(qkv, lab, labt, *, reverse, nbl):
    bsz, s, _ = qkv.shape
    nb = s // TM
    nbc = nb - nbl
    dsel = 1 if reverse else 0

    def blk(i):
        if reverse:
            return jnp.where(i < nbc, nb - 1 - i, nbl - 1 - (i - nbc))
        return jnp.where(i < nbc, nbl + i, i - nbc)

    return pl.pallas_call(
        functools.partial(_gdn_scan_kernel, reverse=reverse),
        grid=(bsz, nb),
        in_specs=[pl.BlockSpec((1, TM, 768), lambda b, i: (b, blk(i), 0)),
                  pl.BlockSpec((1, 1, TM, LANES), lambda b, i: (b, dsel, blk(i), 0)),
                  pl.BlockSpec((1, 1, 8, TM), lambda b, i: (b, dsel, 0, blk(i)))],
        out_specs=pl.BlockSpec((1, TM, 2 * LANES), lambda b, i: (b, blk(i), 0)),
        out_shape=jax.ShapeDtypeStruct((bsz, s, 2 * LANES), F32),
        scratch_shapes=[pltpu.VMEM((2, LANES, LANES), F32)],
        compiler_params=_cparams(("parallel", "arbitrary")),
        name="gdn_scan_bwd" if reverse else "gdn_scan_fwd",
    )(qkv, lab, labt)


def _outproj_kernel(x_ref, mod_ref, of_ref, ob_ref, z_ref, gg_ref, mo_ref, so_ref, go_ref, wout_ref,
                    g2_ref, rw_ref, rb_ref,
                    x1_ref, h2_ref, idx_ref, gate_ref, rank_ref, base_ref, run_sc):
    d = D_MODEL
    first = jnp.logical_and(pl.program_id(0) == 0, pl.program_id(1) == 0)

    @pl.when(first)
    def _():
        run_sc[...] = jnp.zeros(run_sc.shape, F32)

    m = mod_ref[0, 0]
    lane = _lane((TM, LANES))
    lo = lane < HEAD_DIM
    o = of_ref[0] + ob_ref[0]
    z = z_ref[0]
    proj = jnp.zeros((TM, d), F32)
    for pr in range(2):
        t = o[:, pr * LANES:(pr + 1) * LANES]
        t2 = t * t
        ms0 = jnp.sum(jnp.where(lo, t2, 0.0), axis=-1, keepdims=True) / HEAD_DIM
        ms1 = jnp.sum(jnp.where(lo, 0.0, t2), axis=-1, keepdims=True) / HEAD_DIM
        t = t * jnp.where(lo, lax.rsqrt(ms0 + NORM_EPS), lax.rsqrt(ms1 + NORM_EPS)) * gg_ref[...]
        t = t * _silu(z[:, pr * LANES:(pr + 1) * LANES])
        proj = proj + _dot(t.astype(BF16), wout_ref[pr * LANES:(pr + 1) * LANES, :])
    for gi, ref in enumerate((mo_ref, so_ref, go_ref)):
        for hh in range(4):
            r0 = (2 + gi * 4 + hh) * LANES
            proj = proj + _dot(ref[0, hh], wout_ref[r0:r0 + LANES, :])
    x1 = x_ref[0] + m[:, 2 * d:3 * d] * proj
    x1_ref[0] = x1
    xn = x1 * lax.rsqrt(jnp.mean(x1 * x1, axis=-1, keepdims=True) + NORM_EPS) * g2_ref[...]
    h2 = xn * (1.0 + m[:, 4 * d:5 * d]) + m[:, 3 * d:4 * d]
    h2_ref[0] = h2.astype(BF16)

    logits = _dot_hp(h2, rw_ref[...])
    scores = 1.0 / (1.0 + jnp.exp(-logits))
    valid = lane < N_EXPERTS
    sel = jnp.where(valid, scores + rb_ref[...], -jnp.inf)
    member = jnp.zeros((TM, LANES), F32)
    picks = []
    for _ in range(TOP_K):
        mx = jnp.max(sel, axis=-1, keepdims=True)
        idx = jnp.min(jnp.where(sel == mx, lane, LANES), axis=-1, keepdims=True)
        hit = lane == idx
        gate = jnp.sum(jnp.where(hit, scores, 0.0), axis=-1, keepdims=True)
        sel = jnp.where(hit, -jnp.inf, sel)
        member = member + hit.astype(F32)
        picks.append((idx, hit, gate))
    gsum = picks[0][2]
    for kk in range(1, TOP_K):
        gsum = gsum + picks[kk][2]
    ri = lax.broadcasted_iota(jnp.int32, (TM, TM), 0)
    ci = lax.broadcasted_iota(jnp.int32, (TM, TM), 1)
    before = (ri > ci).astype(BF16)
    run = run_sc[...]
    base_ref[0] = run
    cum = _dot(before, member.astype(BF16)) + run
    idx_out = jnp.full((TM, LANES), -1, jnp.int32)
    gate_out = jnp.zeros((TM, LANES), F32)
    rank_out = jnp.zeros((TM, LANES), F32)
    for kk, (idx, hit, gate) in enumerate(picks):
        rank = jnp.sum(jnp.where(hit, cum, 0.0), axis=-1, keepdims=True)
        here = lane == kk
        idx_out = jnp.where(here, idx, idx_out)
        gate_out = jnp.where(here, gate / gsum * ROUTED_SCALE, gate_out)
        rank_out = jnp.where(here, rank, rank_out)
    idx_ref[0] = idx_out
    gate_ref[0] = gate_out
    rank_ref[0] = rank_out
    run_sc[...] = run + jnp.sum(member, axis=0, keepdims=True)


def _outproj(x, mod_tab, o_f, o_b, ga, gg_row, mo, so, go, wout, g2, rw, rb, *, n_blk, nbl):
    bsz, _, d = x.shape
    s_out = n_blk * TM
    full = lambda a: pl.BlockSpec(a.shape, lambda b, i: (0,) * a.ndim)
    tok = lambda w: pl.BlockSpec((1, TM, w), lambda b, i: (b, i, 0))
    head = pl.BlockSpec((1, 4, TM, LANES), lambda b, i: (b, 0, i, 0))
    return pl.pallas_call(
        _outproj_kernel,
        grid=(bsz, n_blk),
        in_specs=[tok(d),
                  pl.BlockSpec((1, 1, 1, 6 * d), lambda b, i: (b, i // nbl, 0, 0)),
                  tok(2 * LANES), tok(2 * LANES),
                  pl.BlockSpec((1, TM, 2 * LANES), lambda b, i: (b, i, 3)),
                  full(gg_row), head, head, head, full(wout), full(g2), full(rw), full(rb)],
        out_specs=[tok(d), tok(d), tok(LANES), tok(LANES), tok(LANES),
                   pl.BlockSpec((1, 1, LANES), lambda b, i: (b * n_blk + i, 0, 0))],
        out_shape=[jax.ShapeDtypeStruct((bsz, s_out, d), F32),
                   jax.ShapeDtypeStruct((bsz, s_out, d), BF16),
                   jax.ShapeDtypeStruct((bsz, s_out, LANES), jnp.int32),
                   jax.ShapeDtypeStruct((bsz, s_out, LANES), F32),
                   jax.ShapeDtypeStruct((bsz, s_out, LANES), F32),
                   jax.ShapeDtypeStruct((bsz * n_blk, 1, LANES), F32)],
        scratch_shapes=[pltpu.VMEM((1, LANES), F32)],
        compiler_params=_cparams(("arbitrary", "arbitrary")),
        name="outproj_router",
    )(x, mod_tab, o_f, o_b, ga, gg_row, mo, so, go, wout, g2, rw, rb)


def _moe_kernel(h_ref, idxc_ref, rankc_ref, gatec_ref, idxr_ref, rankr_ref, base_ref,
                wgu_ref, wdn_ref, sgu_ref, sdn_ref, y_ref):
    e = pl.program_id(1)
    tc = h_ref.shape[0]
    h = h_ref[...]

    @pl.when(e == 0)
    def _():
        gu = _dot(h, sgu_ref[...])
        act = _silu(gu[:, :D_EXPERT]) * gu[:, D_EXPERT:]
        y_ref[...] = _dot(act.astype(BF16), sdn_ref[...])

    lane1 = _lane((1, LANES))
    base_e = jnp.sum(jnp.where(lane1 == e, base_ref[0], 0.0), axis=-1, keepdims=True)
    mc = idxc_ref[...] == e
    has_c = jnp.sum(mc.astype(F32), axis=-1, keepdims=True) > 0.0
    loc_c = jnp.sum(jnp.where(mc, rankc_ref[...], 0.0), axis=-1, keepdims=True) - base_e
    gate_c = jnp.sum(jnp.where(mc, gatec_ref[...], 0.0), axis=-1, keepdims=True)
    loc_c = jnp.where(has_c, loc_c, -1.0)
    mr = idxr_ref[...] == e
    has_r = jnp.sum(mr.astype(F32), axis=0, keepdims=True) > 0.0
    loc_r = jnp.sum(jnp.where(mr, rankr_ref[...], 0.0), axis=0, keepdims=True) - base_e
    loc_r = jnp.where(has_r, loc_r, -1.0)
    cnt = (jnp.max(loc_c) + 1.0).astype(jnp.int32)
    nsub = (cnt + MOE_ROWS - 1) // MOE_ROWS
    row_id = lax.broadcasted_iota(jnp.int32, (MOE_ROWS, tc), 0).astype(F32)
    col_id = lax.broadcasted_iota(jnp.int32, (tc, MOE_ROWS), 1).astype(F32)

    def body(j, carry):
        r0 = (j * MOE_ROWS).astype(F32)
        gather = jnp.where(loc_r - r0 == row_id, 1.0, 0.0).astype(BF16)
        xg = _dot(gather, h).astype(BF16)
        gu = _dot(xg, wgu_ref[0])
        act = _silu(gu[:, :D_EXPERT]) * gu[:, D_EXPERT:]
        y = _dot(act.astype(BF16), wdn_ref[0])
        scatter = jnp.where(loc_c - r0 == col_id, gate_c, 0.0).astype(BF16)
        y_ref[...] += _dot(scatter, y.astype(BF16))
        return carry

    lax.fori_loop(0, nsub, body, 0)


def _moe(h2, idx, rank, gate, idx_t, rank_t, base, wgu, wdn, sgu, sdn):
    n_tok, d = h2.shape
    tc = MOE_CHUNK
    nc = n_tok // tc
    blocks_per_chunk = tc // TM
    ne = wgu.shape[0]
    return pl.pallas_call(
        _moe_kernel,
        grid=(nc, ne),
        in_specs=[pl.BlockSpec((tc, d), lambda c, e: (c, 0)),
                  pl.BlockSpec((tc, LANES), lambda c, e: (c, 0)),
                  pl.BlockSpec((tc, LANES), lambda c, e: (c, 0)),
                  pl.BlockSpec((tc, LANES), lambda c, e: (c, 0)),
                  pl.BlockSpec((8, tc), lambda c, e: (0, c)),
                  pl.BlockSpec((8, tc), lambda c, e: (0, c)),
                  pl.BlockSpec((1, 1, LANES), lambda c, e: (c * blocks_per_chunk, 0, 0)),
                  pl.BlockSpec((1, d, 2 * D_EXPERT), lambda c, e: (e, 0, 0)),
                  pl.BlockSpec((1, D_EXPERT, d), lambda c, e: (e, 0, 0)),
                  pl.BlockSpec(sgu.shape, lambda c, e: (0, 0)),
                  pl.BlockSpec(sdn.shape, lambda c, e: (0, 0))],
        out_specs=pl.BlockSpec((tc, d), lambda c, e: (c, 0)),
        out_shape=jax.ShapeDtypeStruct((n_tok, d), F32),
        compiler_params=_cparams(("parallel", "arbitrary")),
        name="moe",
    )(h2, idx, rank, gate, idx_t, rank_t, base, wgu, wdn, sgu, sdn)


def _residual_kernel(x_ref, y_ref, mod_ref, o_ref):
    d = D_MODEL
    o_ref[0] = x_ref[0] + mod_ref[0, 0][:, 5 * d:6 * d] * y_ref[0]


def _residual(x1, y, mod_tab, *, nbl):
    bsz, s, d = x1.shape
    tok = pl.BlockSpec((1, TM, d), lambda b, i: (b, i, 0))
    return pl.pallas_call(
        _residual_kernel,
        grid=(bsz, s // TM),
        in_specs=[tok, tok, pl.BlockSpec((1, 1, 1, 6 * d), lambda b, i: (b, i // nbl, 0, 0))],
        out_specs=tok,
        out_shape=jax.ShapeDtypeStruct((bsz, s, d), F32),
        compiler_params=_cparams(("parallel", "parallel")),
        name="residual",
    )(x1, y, mod_tab)


def _axial_tables(n_rows, rot_dim):
    rows = jnp.repeat(jnp.arange(n_rows), GRID_W).astype(F32)
    cols = jnp.tile(jnp.arange(GRID_W), n_rows).astype(F32)
    axis_dim = rot_dim // 2
    inv_freq = ROPE_THETA ** (-jnp.arange(0, axis_dim, 2, dtype=F32) / axis_dim)
    ang_r = rows[:, None] * inv_freq
    ang_c = cols[:, None] * inv_freq
    ang = jnp.concatenate([ang_r, ang_r, ang_c, ang_c], axis=-1)
    return jnp.cos(ang), jnp.sin(ang)


def _rope_slab_tables(t_len, n_ctx, rot_dim, lane0):
    cos, sin = _axial_tables(t_len // GRID_W, rot_dim)
    half = rot_dim // 4
    sign = jnp.where((jnp.arange(rot_dim) % (2 * half)) < half, -1.0, 1.0).astype(F32)
    cos_t = jnp.ones((t_len + n_ctx, LANES), F32).at[:t_len, lane0:lane0 + rot_dim].set(cos)
    sin_t = jnp.zeros((t_len + n_ctx, LANES), F32).at[:t_len, lane0:lane0 + rot_dim].set(sin * sign)
    return cos_t, sin_t


def _slab_cols(starts, width):
    out = []
    for st in starts:
        out += list(range(st, st + width)) + [-1] * (LANES - width)
    return out


def _gather_cols(w, cols):
    w_ext = jnp.concatenate([w, jnp.zeros((w.shape[0], 1), w.dtype)], axis=1)
    idx = np.array([c if c >= 0 else w.shape[1] for c in cols], np.int32)
    return w_ext[:, idx]


def _pad_row(v, lane0=0):
    return jnp.zeros((1, LANES), F32).at[0, lane0:lane0 + v.shape[0]].set(v.astype(F32))


def _layer_weights(l, w_in, gdn_conv_w, gdn_a_log, gdn_dt_bias, gdn_norm_g, mla_qn_g, mla_kvn_g,
                   mla_w_uq, mla_w_ukv, mla_qk_g, swa_qk_g, swa_sink, gqa_qk_g, w_out, router_w,
                   router_bias):
    hd = HEAD_DIM
    cols = list(range(0, 1024))
    cols += list(range(O_MQ, O_MQ + MLA_Q_RANK)) + list(range(O_MKV, O_MKV + MLA_KV_RANK))
    misc = list(range(O_MKR, O_MKR + MLA_ROPE))
    misc += list(range(O_AA, O_AA + 4)) + list(range(O_AB, O_AB + 4))
    misc += list(range(O_AA + 4, O_AA + 8)) + list(range(O_AB + 4, O_AB + 8))
    cols += misc + [-1] * (LANES - len(misc))
    cols += _slab_cols([O_SQ + hd * i for i in range(4)] + [O_SK + hd * i for i in range(2)]
                       + [O_SV + hd * i for i in range(2)], hd)
    cols += _slab_cols([O_GQ + hd * i for i in range(4)] + [O_GK + hd * i for i in range(2)]
                       + [O_GV + hd * i for i in range(2)], hd)
    assert len(cols) == N_COL
    qd = MLA_NOPE + MLA_ROPE
    uq_cols = []
    for hh in range(MLA_HEADS):
        uq_cols += list(range(hh * qd, hh * qd + qd)) + [-1] * (LANES - qd)
    kvd = MLA_NOPE + MLA_V
    ukv_cols = _slab_cols([hh * kvd for hh in range(MLA_HEADS)], MLA_NOPE)
    ukv_cols += _slab_cols([hh * kvd + MLA_NOPE for hh in range(MLA_HEADS)], MLA_V)
    orow = list(range(0, 256)) + _slab_cols([256 + hd * i for i in range(12)], hd)
    alog = jnp.zeros((1, LANES), F32)
    dtb = jnp.zeros((1, LANES), F32)
    for dd in range(2):
        alog = alog.at[0, 32 + 8 * dd:36 + 8 * dd].set(gdn_a_log[l, dd])
        dtb = dtb.at[0, 32 + 8 * dd:36 + 8 * dd].set(gdn_dt_bias[l, dd])
    return dict(
        win=_gather_cols(w_in[l], cols).astype(BF16),
        conv=gdn_conv_w[l][:, :768],
        alog=alog, dtb=dtb,
        gdn_g=jnp.concatenate([gdn_norm_g[l], gdn_norm_g[l]]).reshape(1, LANES),
        mla_qn_g=mla_qn_g[l], mla_kvn_g=mla_kvn_g[l],
        wuq=_gather_cols(mla_w_uq[l], uq_cols).astype(BF16),
        wukv=_gather_cols(mla_w_ukv[l], ukv_cols).astype(BF16),
        mqg=_pad_row(mla_qk_g[l, 0]), mkg=_pad_row(mla_qk_g[l, 1, :MLA_NOPE]),
        krg=_pad_row(mla_qk_g[l, 1, MLA_NOPE:]),
        sqg=_pad_row(swa_qk_g[l, 0]), skg=_pad_row(swa_qk_g[l, 1]),
        gqg=_pad_row(gqa_qk_g[l, 0]), gkg=_pad_row(gqa_qk_g[l, 1]),
        wout=_gather_cols(w_out[l].T, orow).T.astype(BF16),
        rw=jnp.concatenate([router_w[l], jnp.zeros((D_MODEL, LANES - N_EXPERTS), F32)], axis=1),
        rb=_pad_row(router_bias[l]),
        sink=swa_sink[l],
    )


def _sink_rows(sink, grp, tq):
    hkv = sink.shape[0] // grp
    return jnp.repeat(sink.astype(F32).reshape(hkv, grp), tq, axis=1).reshape(hkv, grp * tq, 1)


def kernel(x, c, ctx, c_ctx, w_mod, b_mod, norm1_g, norm2_g, w_in, gdn_conv_w, gdn_a_log, gdn_dt_bias,
           gdn_norm_g, mla_qn_g, mla_kvn_g, mla_w_uq, mla_w_ukv, mla_qk_g, swa_qk_g, swa_sink, gqa_qk_g,
           w_out, router_w, router_bias, exp_w_gu, exp_w_down, shared_w_gu, shared_w_down):
    bsz, t_len, d = x.shape
    n_ctx = ctx.shape[1]
    depth = w_mod.shape[0]
    s = t_len + n_ctx
    assert d == D_MODEL and t_len % TM == 0 and n_ctx % TM == 0 and t_len >= 3 * WINDOW
    assert bsz + 1 <= 8
    nbl = t_len // TM
    nb = s // TM

    tabs = (_rope_slab_tables(t_len, n_ctx, HEAD_DIM, 0) + _rope_slab_tables(t_len, n_ctx, MLA_ROPE, MLA_NOPE)
            + _rope_slab_tables(t_len, n_ctx, MLA_ROPE, 0))
    c8 = jnp.zeros((8, d), F32).at[:bsz].set(c).at[bsz].set(c_ctx)
    xs = jnp.concatenate([x, ctx], axis=1)

    tq_d = 512 if t_len % 512 == 0 else TM
    tk_d = 768 if (s % 768 == 0) else TM

    for l in range(depth):
        need_ctx = l < depth - 1
        lw = _layer_weights(l, w_in, gdn_conv_w, gdn_a_log, gdn_dt_bias, gdn_norm_g, mla_qn_g, mla_kvn_g,
                            mla_w_uq, mla_w_ukv, mla_qk_g, swa_qk_g, swa_sink, gqa_qk_g, w_out, router_w,
                            router_bias)
        mod = _modulation(c8, w_mod[l], b_mod[l])
        mod_tab = jnp.stack([mod[:bsz], jnp.broadcast_to(mod[bsz], (bsz, 6 * d))], axis=1)[:, :, None, :]

        (ga, misc, mq, mk, mv, sq, sk, sv, gq, gk, gv) = _inproj(
            xs, mod_tab, norm1_g[l].reshape(1, d), lw['win'], lw, tabs, nbl)

        qkv, lab = _gdn_prep(ga, misc, lw['conv'], lw['alog'], lw['dtb'], nbl)
        labt = jnp.swapaxes(lab[..., :8], 2, 3)
        o_f = _gdn_scan(qkv, lab, labt, reverse=False, nbl=nbl)
        o_b = _gdn_scan(qkv, lab, labt, reverse=True, nbl=nbl)

        mo = _flash(mq, mk, mv, grp=1, tq=2 * tq_d, tk=tk_d, q_blk0=0, nq=t_len // (2 * tq_d),
                    k_blk0=0, nk=s // tk_d)
        go = _flash(gq, gk, gv, grp=2, tq=tq_d, tk=tk_d, q_blk0=0, nq=t_len // tq_d, k_blk0=0, nk=s // tk_d)
        so = _swa(sq, sk, sv, _sink_rows(lw['sink'], 2, WINDOW), grp=2, t_len=t_len, n_ctx=n_ctx)
        if need_ctx:
            cb0 = t_len // n_ctx
            mo = _flash(mq, mk, mv, grp=1, tq=n_ctx, tk=n_ctx, q_blk0=cb0, nq=1, k_blk0=cb0, nk=1, prev=mo)
            go = _flash(gq, gk, gv, grp=2, tq=n_ctx, tk=n_ctx, q_blk0=cb0, nq=1, k_blk0=cb0, nk=1, prev=go)
            so = _flash(sq, sk, sv, grp=2, tq=n_ctx, tk=n_ctx, q_blk0=cb0, nq=1, k_blk0=cb0, nk=1,
                        sink_rows=_sink_rows(lw['sink'], 2, n_ctx), prev=so)

        n_blk = nb if need_ctx else nbl
        x1, h2, idx, gate, rank, base = _outproj(
            xs, mod_tab, o_f, o_b, ga, lw['gdn_g'], mo, so, go, lw['wout'], norm2_g[l].reshape(1, d),
            lw['rw'], lw['rb'], n_blk=n_blk, nbl=nbl)

        n_tok = bsz * n_blk * TM
        assert n_tok % MOE_CHUNK == 0
        idx2 = idx.reshape(n_tok, LANES)
        rank2 = rank.reshape(n_tok, LANES)
        y = _moe(h2.reshape(n_tok, d), idx2, rank2, gate.reshape(n_tok, LANES),
                 idx2[:, :8].T, rank2[:, :8].T, base,
                 exp_w_gu[l].astype(BF16), exp_w_down[l].astype(BF16),
                 shared_w_gu[l].astype(BF16), shared_w_down[l].astype(BF16))
        xs = _residual(x1, y.reshape(bsz, n_blk * TM, d), mod_tab, nbl=nbl)
    return xs[:, :t_len]
```

```python
import functools
import math

import numpy as np
import jax
import jax.numpy as jnp
from jax import lax
from jax.experimental import pallas as pl
from jax.experimental.pallas import tpu as pltpu

F32 = jnp.float32
BF16 = jnp.bfloat16

LANES = 128
TM = 256
VMEM_LIMIT = 56 * 1024 * 1024

D_MODEL = 1024
GRID_W = 64
HEAD_DIM = 64
ROPE_THETA = 10000.0
NORM_EPS = 1e-6
ATTN_SCALE = HEAD_DIM ** -0.5
GDN_HEADS = 4
GDN_DK = 64
GDN_CHUNK = 64
GDN_CONV = 5
MLA_HEADS = 4
MLA_Q_RANK = 256
MLA_KV_RANK = 128
MLA_NOPE = 64
MLA_ROPE = 32
MLA_V = 64
MLA_SCALE = (MLA_NOPE + MLA_ROPE) ** -0.5
SWA_HEADS = 4
SWA_KV_HEADS = 2
WINDOW = 128
GQA_HEADS = 4
GQA_KV_HEADS = 2
N_EXPERTS = 64
TOP_K = 6
D_EXPERT = 384
ROUTED_SCALE = 2.5
LOG2E = math.log2(math.e)
ONES_LANE = HEAD_DIM
FLASH_KB = 256
MOE_CHUNK = 1024
MOE_ROWS = 128
MOE_EG = 2
MOE_SEG = 16
MOE_KSPLIT = 3

O_AQ, O_AK, O_AV, O_AZ, O_AA, O_AB = 0, 256, 512, 768, 1024, 1032
O_MQ, O_MKV, O_MKR = 1040, 1296, 1424
O_SQ, O_SK, O_SV = 1456, 1712, 1840
O_GQ, O_GK, O_GV = 1968, 2224, 2352
D_IN = 2480
C_GDN, C_MLA, C_MISC, C_SWA, C_GQA, N_COL = 0, 1024, 1408, 1536, 2560, 3584


def _cparams(sem):
    return pltpu.CompilerParams(dimension_semantics=sem, vmem_limit_bytes=VMEM_LIMIT)


def _dot(a, b):
    return jnp.dot(a, b, preferred_element_type=F32)


def _dot_nt(a, b):
    return lax.dot_general(a, b, (((1,), (1,)), ((), ())), preferred_element_type=F32)


def _dot_tn(a, b):
    return lax.dot_general(a, b, (((0,), (0,)), ((), ())), preferred_element_type=F32)


def _split3(x):
    hi = x.astype(BF16)
    r1 = x - hi.astype(F32)
    mid = r1.astype(BF16)
    lo = (r1 - mid.astype(F32)).astype(BF16)
    return hi, mid, lo


def _dot_exact_lhs(a_bf16, x):
    hi, mid, lo = _split3(x)
    return _dot(a_bf16, hi) + _dot(a_bf16, mid) + _dot(a_bf16, lo)


def _dot_exact_rhs(x, b_bf16):
    hi, mid, lo = _split3(x)
    return _dot(hi, b_bf16) + _dot(mid, b_bf16) + _dot(lo, b_bf16)


def _dot_hp(a, b):
    ah = a.astype(BF16)
    al = (a - ah.astype(F32)).astype(BF16)
    bh = b.astype(BF16)
    bl = (b - bh.astype(F32)).astype(BF16)
    return _dot(ah, bh) + _dot(ah, bl) + _dot(al, bh)


def _silu(x):
    return x * (1.0 / (1.0 + jnp.exp(-x)))


def _lane(shape):
    return lax.broadcasted_iota(jnp.int32, shape, len(shape) - 1)


def _mod_kernel(c_ref, w_ref, b_ref, o_ref):
    s = _silu(c_ref[...])
    o_ref[...] = _dot(s.astype(BF16), w_ref[...].astype(BF16)) + b_ref[...]


def _modulation(c8, w_mod_l, b_mod_l):
    d = c8.shape[1]
    n = w_mod_l.shape[1]
    return pl.pallas_call(
        _mod_kernel,
        grid=(n // d,),
        in_specs=[pl.BlockSpec((8, d), lambda j: (0, 0)),
                  pl.BlockSpec((d, d), lambda j: (0, j)),
                  pl.BlockSpec((1, d), lambda j: (0, j))],
        out_specs=pl.BlockSpec((8, d), lambda j: (0, j)),
        out_shape=jax.ShapeDtypeStruct((8, n), F32),
        compiler_params=_cparams(("arbitrary",)),
        name="modulation",
    )(c8, w_mod_l, b_mod_l.reshape(1, n))


def _rope_slab(x, cos, sin_signed, half):
    lane = _lane(x.shape)
    fwd = pltpu.roll(x, LANES - half, 1)
    bwd = pltpu.roll(x, half, 1)
    rot = jnp.where(lane % (2 * half) < half, fwd, bwd)
    return x * cos + rot * sin_signed


def _inproj_kernel(x_ref, mod_ref, g1_ref, win_ref, qn_g_ref, kvn_g_ref, wuq_ref, wukv_ref,
                   mqg_ref, mkg_ref, krg_ref, sqg_ref, skg_ref, gqg_ref, gkg_ref,
                   cos_hd_ref, sin_hd_ref, cos_mq_ref, sin_mq_ref, cos_kr_ref, sin_kr_ref,
                   ga_ref, misc_ref, mq_ref, mk_ref, mv_ref, sq_ref, sk_ref, sv_ref,
                   gq_ref, gk_ref, gv_ref):
    d = D_MODEL
    x = x_ref[0]
    m = mod_ref[0, 0]
    shift, scale = m[:, 0:d], m[:, d:2 * d]
    xn = x * lax.rsqrt(jnp.mean(x * x, axis=-1, keepdims=True) + NORM_EPS) * g1_ref[...]
    h = xn * (1.0 + scale) + shift
    p = _dot(h.astype(BF16), win_ref[...])

    ga_ref[0] = p[:, C_GDN:C_GDN + 1024]
    misc = p[:, C_MISC:C_MISC + LANES]
    misc_ref[0] = misc

    lane = _lane((TM, LANES))
    cos_hd, sin_hd = cos_hd_ref[...], sin_hd_ref[...]

    cq = p[:, C_MLA:C_MLA + MLA_Q_RANK]
    cqn = cq * lax.rsqrt(jnp.mean(cq * cq, axis=-1, keepdims=True) + NORM_EPS) * qn_g_ref[...]
    qup = _dot(cqn.astype(BF16), wuq_ref[...])
    ckv = p[:, C_MLA + MLA_Q_RANK:C_MLA + MLA_Q_RANK + MLA_KV_RANK]
    ckvn = ckv * lax.rsqrt(jnp.mean(ckv * ckv, axis=-1, keepdims=True) + NORM_EPS) * kvn_g_ref[...]
    kvup = _dot(ckvn.astype(BF16), wukv_ref[...])
    kr = jnp.where(lane < MLA_ROPE, misc, 0.0)
    kr = kr * lax.rsqrt(jnp.sum(kr * kr, axis=-1, keepdims=True) / MLA_ROPE + NORM_EPS) * krg_ref[...]
    kr = _rope_slab(kr, cos_kr_ref[...], sin_kr_ref[...], MLA_ROPE // 4)
    kpe = pltpu.roll(kr, MLA_NOPE, 1)
    is_nope = lane < MLA_NOPE
    for hh in range(MLA_HEADS):
        q = qup[:, hh * LANES:(hh + 1) * LANES]
        q2 = q * q
        ss_n = jnp.sum(jnp.where(is_nope, q2, 0.0), axis=-1, keepdims=True)
        ss_p = jnp.sum(jnp.where(is_nope, 0.0, q2), axis=-1, keepdims=True)
        r = jnp.where(is_nope, lax.rsqrt(ss_n / MLA_NOPE + NORM_EPS), lax.rsqrt(ss_p / MLA_ROPE + NORM_EPS))
        q = q * r * mqg_ref[...]
        q = _rope_slab(q, cos_mq_ref[...], sin_mq_ref[...], MLA_ROPE // 4)
        mq_ref[0, hh] = (q * (MLA_SCALE * LOG2E)).astype(BF16)
        k = kvup[:, hh * LANES:(hh + 1) * LANES]
        k = k * lax.rsqrt(jnp.sum(k * k, axis=-1, keepdims=True) / MLA_NOPE + NORM_EPS) * mkg_ref[...]
        mk_ref[0, hh] = (k + kpe).astype(BF16)
        v = kvup[:, (MLA_HEADS + hh) * LANES:(MLA_HEADS + hh + 1) * LANES]
        mv_ref[0, hh] = jnp.where(lane == ONES_LANE, 1.0, v).astype(BF16)

    def gqa_prep(base, qg_ref, kg_ref, q_ref, k_ref, v_ref, nq, nkv):
        for hh in range(nq):
            q = p[:, base + hh * LANES:base + (hh + 1) * LANES]
            q = q * lax.rsqrt(jnp.sum(q * q, axis=-1, keepdims=True) / HEAD_DIM + NORM_EPS) * qg_ref[...]
            q = _rope_slab(q, cos_hd, sin_hd, HEAD_DIM // 4)
            q_ref[0, hh] = (q * (ATTN_SCALE * LOG2E)).astype(BF16)
        for hh in range(nkv):
            k = p[:, base + (nq + hh) * LANES:base + (nq + hh + 1) * LANES]
            k = k * lax.rsqrt(jnp.sum(k * k, axis=-1, keepdims=True) / HEAD_DIM + NORM_EPS) * kg_ref[...]
            k = _rope_slab(k, cos_hd, sin_hd, HEAD_DIM // 4)
            k_ref[0, hh] = k.astype(BF16)
            v = p[:, base + (nq + nkv + hh) * LANES:base + (nq + nkv + hh + 1) * LANES]
            v_ref[0, hh] = jnp.where(lane == ONES_LANE, 1.0, v).astype(BF16)

    gqa_prep(C_SWA, sqg_ref, skg_ref, sq_ref, sk_ref, sv_ref, SWA_HEADS, SWA_KV_HEADS)
    gqa_prep(C_GQA, gqg_ref, gkg_ref, gq_ref, gk_ref, gv_ref, GQA_HEADS, GQA_KV_HEADS)


def _inproj(x, mod_tab, g1, win, lw, tabs, nbl):
    bsz, s, d = x.shape
    nb = s // TM
    row = lambda a: a.reshape(1, -1)
    full = lambda a: pl.BlockSpec(a.shape, lambda b, i: (0,) * a.ndim)
    tab_spec = pl.BlockSpec((TM, LANES), lambda b, i: (i, 0))
    head_out = lambda nh: pl.BlockSpec((1, nh, TM, LANES), lambda b, i: (b, 0, i, 0))
    head_shape = lambda nh: jax.ShapeDtypeStruct((bsz, nh, s, LANES), BF16)
    small = [row(lw['mla_qn_g']), row(lw['mla_kvn_g']), lw['wuq'], lw['wukv'],
             lw['mqg'], lw['mkg'], lw['krg'], lw['sqg'], lw['skg'], lw['gqg'], lw['gkg']]
    return pl.pallas_call(
        _inproj_kernel,
        grid=(bsz, nb),
        in_specs=[pl.BlockSpec((1, TM, d), lambda b, i: (b, i, 0)),
                  pl.BlockSpec((1, 1, 1, 6 * d), lambda b, i: (b, i // nbl, 0, 0)),
                  full(g1), full(win)] + [full(a) for a in small] + [tab_spec] * 6,
        out_specs=[pl.BlockSpec((1, TM, 1024), lambda b, i: (b, i, 0)),
                   pl.BlockSpec((1, TM, LANES), lambda b, i: (b, i, 0)),
                   head_out(4), head_out(4), head_out(4),
                   head_out(4), head_out(2), head_out(2),
                   head_out(4), head_out(2), head_out(2)],
        out_shape=[jax.ShapeDtypeStruct((bsz, s, 1024), F32),
                   jax.ShapeDtypeStruct((bsz, s, LANES), F32),
                   head_shape(4), head_shape(4), head_shape(4),
                   head_shape(4), head_shape(2), head_shape(2),
                   head_shape(4), head_shape(2), head_shape(2)],
        compiler_params=_cparams(("parallel", "parallel")),
        name="inproj",
    )(x, mod_tab, g1, win, *small, *tabs)


def _flash_kernel(*refs, grp, tq, nk, use_sink, has_prev):
    refs = list(refs)
    sink_ref = refs.pop(0) if use_sink else None
    q_ref, k_ref, v_ref = refs[0], refs[1], refs[2]
    o_ref, m_sc, acc_sc = refs[-3], refs[-2], refs[-1]
    kj = pl.program_id(3)
    tk = k_ref.shape[2]
    kb = min(FLASH_KB, tk)

    @pl.when(kj == 0)
    def _():
        m_sc[...] = jnp.full(m_sc.shape, -jnp.inf, F32)
        acc_sc[...] = jnp.zeros(acc_sc.shape, F32)

    q = q_ref[0].reshape(grp * tq, LANES)
    m = m_sc[...]
    acc = acc_sc[...]
    for j in range(tk // kb):
        s = _dot_nt(q, k_ref[0, 0, j * kb:(j + 1) * kb, :])
        m_new = jnp.maximum(m, jnp.max(s, axis=-1, keepdims=True))
        alpha = jnp.exp2(m - m_new)
        pr = jnp.exp2(s - jnp.concatenate([m_new] * (kb // LANES), axis=1))
        acc = alpha * acc + _dot(pr.astype(BF16), v_ref[0, 0, j * kb:(j + 1) * kb, :])
        m = m_new
    m_sc[...] = m
    acc_sc[...] = acc

    @pl.when(kj == nk - 1)
    def _():
        l = acc[:, ONES_LANE:ONES_LANE + 1]
        out = acc
        if use_sink:
            sk = sink_ref[0]
            m_old = m[:, 0:1]
            m_fin = jnp.maximum(m_old, sk)
            a = jnp.exp2(m_old - m_fin)
            l = l * a + jnp.exp2(sk - m_fin)
            out = out * a
        o_ref[0] = (out / l).reshape(grp, tq, LANES).astype(o_ref.dtype)


def _flash(q, k, v, *, grp, tq, tk, q_blk0, nq, k_blk0, nk, sink_rows=None, prev=None):
    bsz, hq, s, _ = q.shape
    hkv = hq // grp
    use_sink = sink_rows is not None
    in_specs = []
    args = []
    if use_sink:
        in_specs.append(pl.BlockSpec((1, grp * tq, 1), lambda b, h, i, j: (h, 0, 0)))
        args.append(sink_rows)
    in_specs += [pl.BlockSpec((1, grp, tq, LANES), lambda b, h, i, j: (b, h, i + q_blk0, 0)),
                 pl.BlockSpec((1, 1, tk, LANES), lambda b, h, i, j: (b, h, j + k_blk0, 0)),
                 pl.BlockSpec((1, 1, tk, LANES), lambda b, h, i, j: (b, h, j + k_blk0, 0))]
    args += [q, k, v]
    aliases = {}
    if prev is not None:
        in_specs.append(pl.BlockSpec(memory_space=pl.ANY))
        aliases = {len(args): 0}
        args.append(prev)
    rows = grp * tq
    return pl.pallas_call(
        functools.partial(_flash_kernel, grp=grp, tq=tq, nk=nk, use_sink=use_sink, has_prev=prev is not None),
        grid=(bsz, hkv, nq, nk),
        in_specs=in_specs,
        out_specs=pl.BlockSpec((1, grp, tq, LANES), lambda b, h, i, j: (b, h, i + q_blk0, 0)),
        out_shape=jax.ShapeDtypeStruct((bsz, hq, s, LANES), BF16),
        scratch_shapes=[pltpu.VMEM((rows, LANES), F32), pltpu.VMEM((rows, LANES), F32)],
        input_output_aliases=aliases,
        compiler_params=_cparams(("parallel", "parallel", "parallel", "arbitrary")),
        name="flash",
    )(*args)


def _swa_kernel(sink_ref, q_ref, k_ref, v_ref, o_ref, *, grp, t_len, n_ctx):
    w = WINDOW
    n = pl.program_id(2)
    start = pl.multiple_of(jnp.clip((n - 1) * w, 0, t_len - 3 * w), w)
    q = q_ref[0].reshape(grp * w, LANES)
    kl = k_ref[0, 0, pl.ds(start, 3 * w), :]
    vl = v_ref[0, 0, pl.ds(start, 3 * w), :]
    kc = k_ref[0, 0, pl.ds(t_len, n_ctx), :]
    vc = v_ref[0, 0, pl.ds(t_len, n_ctx), :]
    s_loc = _dot_nt(q, kl)
    qpos = n * w + (lax.broadcasted_iota(jnp.int32, s_loc.shape, 0) % w)
    kpos = start + lax.broadcasted_iota(jnp.int32, s_loc.shape, 1)
    s_loc = jnp.where(jnp.abs(qpos - kpos) <= w, s_loc, -jnp.inf)
    s_ctx = _dot_nt(q, kc)
    sk = sink_ref[0]
    m = jnp.maximum(jnp.maximum(jnp.max(s_loc, axis=-1, keepdims=True),
                                jnp.max(s_ctx, axis=-1, keepdims=True)), sk)
    p_loc = jnp.exp2(s_loc - m)
    p_ctx = jnp.exp2(s_ctx - m)
    o = _dot(p_loc.astype(BF16), vl) + _dot(p_ctx.astype(BF16), vc)
    l = o[:, ONES_LANE:ONES_LANE + 1] + jnp.exp2(sk - m)
    o_ref[0] = (o / l).reshape(grp, w, LANES).astype(o_ref.dtype)


def _swa(q, k, v, sink_rows, *, grp, t_len, n_ctx):
    bsz, hq, s, _ = q.shape
    hkv = hq // grp
    w = WINDOW
    return pl.pallas_call(
        functools.partial(_swa_kernel, grp=grp, t_len=t_len, n_ctx=n_ctx),
        grid=(bsz, hkv, t_len // w),
        in_specs=[pl.BlockSpec((1, grp * w, 1), lambda b, h, n: (h, 0, 0)),
                  pl.BlockSpec((1, grp, w, LANES), lambda b, h, n: (b, h, n, 0)),
                  pl.BlockSpec((1, 1, s, LANES), lambda b, h, n: (b, h, 0, 0)),
                  pl.BlockSpec((1, 1, s, LANES), lambda b, h, n: (b, h, 0, 0))],
        out_specs=pl.BlockSpec((1, grp, w, LANES), lambda b, h, n: (b, h, n, 0)),
        out_shape=jax.ShapeDtypeStruct((bsz, hq, s, LANES), BF16),
        compiler_params=_cparams(("parallel", "parallel", "arbitrary")),
        name="swa",
    )(sink_rows, q, k, v)


def _gdn_prep_kernel(cur_ref, prev_ref, next_ref, misc_ref, cw_ref, alog_ref, dtb_ref,
                     qkv_ref, lab_ref, *, nbl, nb):
    i = pl.program_id(1)
    first = jnp.logical_or(i == 0, i == nbl)
    last = jnp.logical_or(i == nbl - 1, i == nb - 1)
    cur = cur_ref[0]
    prev = jnp.where(first, 0.0, prev_ref[0])
    nxt = jnp.where(last, 0.0, next_ref[0])
    xe = jnp.concatenate([prev, cur, nxt], axis=0)
    cw = cw_ref[...]
    acc = jnp.zeros(cur.shape, F32)
    for j in range(GDN_CONV):
        off = 8 + j - (GDN_CONV - 1) // 2
        acc = acc + xe[off:off + TM, :] * cw[j:j + 1, :]
    y = _silu(acc)
    lane = _lane((TM, LANES))
    lo = lane < GDN_DK
    for sl in range(6):
        t = y[:, sl * LANES:(sl + 1) * LANES]
        if sl < 4:
            t2 = t * t
            ss0 = jnp.sum(jnp.where(lo, t2, 0.0), axis=-1, keepdims=True)
            ss1 = jnp.sum(jnp.where(lo, 0.0, t2), axis=-1, keepdims=True)
            t = t * jnp.where(lo, lax.rsqrt(ss0 + NORM_EPS), lax.rsqrt(ss1 + NORM_EPS))
            if sl < 2:
                t = t * GDN_DK ** -0.5
        qkv_ref[0, :, sl * LANES:(sl + 1) * LANES] = t
    xm = misc_ref[0]
    za = xm + dtb_ref[...]
    softplus = jnp.maximum(za, 0.0) + jnp.log(1.0 + jnp.exp(-jnp.abs(za)))
    log_a = -jnp.exp(alog_ref[...]) * softplus
    beta = 1.0 / (1.0 + jnp.exp(-xm))
    is_a = (lane % 8) < 4
    vals = jnp.where(is_a, log_a, beta)
    lab_ref[0, 0] = pltpu.roll(vals, LANES - 32, 1)
    lab_ref[0, 1] = pltpu.roll(vals, LANES - 40, 1)


def _gdn_prep(ga, misc, cw, alog_row, dtb_row, nbl):
    bsz, s, _ = ga.shape
    nb = s // TM
    r8 = TM // 8
    n8 = s // 8
    return pl.pallas_call(
        functools.partial(_gdn_prep_kernel, nbl=nbl, nb=nb),
        grid=(bsz, nb),
        in_specs=[pl.BlockSpec((1, TM, 768), lambda b, i: (b, i, 0)),
                  pl.BlockSpec((1, 8, 768), lambda b, i: (b, jnp.maximum(i * r8 - 1, 0), 0)),
                  pl.BlockSpec((1, 8, 768), lambda b, i: (b, jnp.minimum((i + 1) * r8, n8 - 1), 0)),
                  pl.BlockSpec((1, TM, LANES), lambda b, i: (b, i, 0)),
                  pl.BlockSpec(cw.shape, lambda b, i: (0, 0)),
                  pl.BlockSpec((1, LANES), lambda b, i: (0, 0)),
                  pl.BlockSpec((1, LANES), lambda b, i: (0, 0))],
        out_specs=[pl.BlockSpec((1, TM, 768), lambda b, i: (b, i, 0)),
                   pl.BlockSpec((1, 2, TM, LANES), lambda b, i: (b, 0, i, 0))],
        out_shape=[jax.ShapeDtypeStruct((bsz, s, 768), F32),
                   jax.ShapeDtypeStruct((bsz, 2, s, LANES), F32)],
        compiler_params=_cparams(("parallel", "parallel")),
        name="gdn_prep",
    )(ga, ga, ga, misc, cw, alog_row, dtb_row)


def _gdn_scan_kernel(qkv_ref, lab_ref, labt_ref, o_ref, s_sc, *, reverse):
    c = GDN_CHUNK
    c2 = 2 * c
    i = pl.program_id(1)

    @pl.when(i == 0)
    def _():
        s_sc[...] = jnp.zeros(s_sc.shape, F32)

    ri = lax.broadcasted_iota(jnp.int32, (c2, c2), 0)
    ci = lax.broadcasted_iota(jnp.int32, (c2, c2), 1)
    same = (ri // c) == (ci // c)
    if reverse:
        incl = jnp.logical_and(same, ri <= ci)
        strict = jnp.logical_and(same, ri < ci)
    else:
        incl = jnp.logical_and(same, ri >= ci)
        strict = jnp.logical_and(same, ri > ci)
    eye = (ri == ci).astype(F32)
    r1 = lax.broadcasted_iota(jnp.int32, (c, c), 0)
    c1 = lax.broadcasted_iota(jnp.int32, (c, c), 1)
    cum_col = ((r1 <= c1) if reverse else (r1 >= c1)).astype(BF16)
    rr = lax.broadcasted_iota(jnp.int32, (c, c2), 0)
    cc = lax.broadcasted_iota(jnp.int32, (c, c2), 1) % c
    cum_row = ((rr >= cc) if reverse else (rr <= cc)).astype(BF16)
    lane = _lane((c, LANES))
    lo = lane < GDN_DK
    lane2 = _lane((1, c2))
    lo2 = lane2 < c
    g_last_row = 0 if reverse else c - 1

    def stack(x):
        return jnp.concatenate([jnp.where(lo, x, 0.0), jnp.where(lo, 0.0, x)], axis=0)

    def fold(mat):
        return mat[0:c, :] + mat[c:c2, :]

    order = list(range(TM // c - 1, -1, -1) if reverse else range(TM // c))
    inst = []
    for ch in order:
        rows = slice(ch * c, (ch + 1) * c)
        lab = lab_ref[0, 0, rows, :]
        g_cols = _dot_exact_lhs(cum_col, lab)
        la_rows = labt_ref[0, 0, :, rows]
        g_rows = _dot_exact_rhs(la_rows, cum_row)
        for pr in range(2):
            h0, h1 = 2 * pr, 2 * pr + 1
            q = qkv_ref[0, rows, pr * LANES:(pr + 1) * LANES]
            k = qkv_ref[0, rows, (2 + pr) * LANES:(3 + pr) * LANES]
            v = qkv_ref[0, rows, (4 + pr) * LANES:(5 + pr) * LANES]
            beta = jnp.where(lo, lab[:, 4 + h0:5 + h0], lab[:, 4 + h1:5 + h1])
            g = jnp.where(lo, g_cols[:, h0:h0 + 1], g_cols[:, h1:h1 + 1])
            g_st_col = jnp.concatenate([g_cols[:, h0:h0 + 1], g_cols[:, h1:h1 + 1]], axis=0)
            g_st_row = jnp.where(lo2, g_rows[h0:h0 + 1, :], g_rows[h1:h1 + 1, :])
            diff = g_st_col - g_st_row
            dec = jnp.where(incl, jnp.exp(jnp.where(incl, diff, 0.0)), 0.0)
            kb = k * beta
            eg = jnp.exp(g)
            k_st = stack(k).astype(BF16)
            a = jnp.where(strict, _dot_nt(stack(kb).astype(BF16), k_st) * dec, 0.0)
            qk = _dot_nt(stack(q).astype(BF16), k_st) * dec
            gl = g[g_last_row:g_last_row + 1, :]
            gl_col = jnp.concatenate([jnp.broadcast_to(g_cols[g_last_row:g_last_row + 1, h0:h0 + 1], (c, 1)),
                                      jnp.broadcast_to(g_cols[g_last_row:g_last_row + 1, h1:h1 + 1], (c, 1))],
                                     axis=0)
            inst.append(dict(rows=rows, pr=pr, a=a, vb=stack(v * beta).astype(BF16),
                             kg=stack(kb * eg).astype(BF16), qg=(q * eg).astype(BF16),
                             qkf=fold(qk).astype(BF16), kd=(k * jnp.exp(gl - g)).astype(BF16),
                             sdec=jnp.exp(gl_col)))
    tinv = [eye - it['a'] for it in inst]
    pw = [it['a'] for it in inst]
    for _ in range(5):
        pw = [_dot_hp(x, x) for x in pw]
        tinv = [t + _dot_hp(t, x) for t, x in zip(tinv, pw)]
    for it, t in zip(inst, tinv):
        tf = fold(t).astype(BF16)
        it['u'] = _dot(tf, it['vb'])
        it['w'] = _dot(tf, it['kg']).astype(BF16)
    state = [s_sc[0], s_sc[1]]
    for it in inst:
        st = state[it['pr']]
        st_b = st.astype(BF16)
        v_new = it['u'] - _dot(it['w'], st_b)
        o = _dot(it['qg'], st_b) + _dot(it['qkf'], stack(v_new).astype(BF16))
        o_ref[0, it['rows'], it['pr'] * LANES:(it['pr'] + 1) * LANES] = o
        upd = _dot_tn(it['kd'], v_new.astype(BF16))
        state[it['pr']] = st * it['sdec'] + jnp.where(same, upd, 0.0)
    s_sc[0] = state[0]
    s_sc[1] = state[1]


def _gdn_scan(qkv, lab, labt, *, reverse, nbl):
    bsz, s, _ = qkv.shape
    nb = s // TM
    nbc = nb - nbl
    dsel = 1 if reverse else 0

    def blk(i):
        if reverse:
            return jnp.where(i < nbc, nb - 1 - i, nbl - 1 - (i - nbc))
        return jnp.where(i < nbc, nbl + i, i - nbc)

    return pl.pallas_call(
        functools.partial(_gdn_scan_kernel, reverse=reverse),
        grid=(bsz, nb),
        in_specs=[pl.BlockSpec((1, TM, 768), lambda b, i: (b, blk(i), 0)),
                  pl.BlockSpec((1, 1, TM, LANES), lambda b, i: (b, dsel, blk(i), 0)),
                  pl.BlockSpec((1, 1, 8, TM), lambda b, i: (b, dsel, 0, blk(i)))],
        out_specs=pl.BlockSpec((1, TM, 2 * LANES), lambda b, i: (b, blk(i), 0)),
        out_shape=jax.ShapeDtypeStruct((bsz, s, 2 * LANES), F32),
        scratch_shapes=[pltpu.VMEM((2, LANES, LANES), F32)],
        compiler_params=_cparams(("parallel", "arbitrary")),
        name="gdn_scan_bwd" if reverse else "gdn_scan_fwd",
    )(qkv, lab, labt)


def _outproj_kernel(x_ref, mod_ref, of_ref, ob_ref, z_ref, gg_ref, mo_ref, so_ref, go_ref, wout_ref,
                    g2_ref, rw_ref, rb_ref,
                    x1_ref, h2_ref, idx_ref, gate_ref, rank_ref, base_ref, run_sc):
    d = D_MODEL
    first = jnp.logical_and(pl.program_id(0) == 0, pl.program_id(1) == 0)

    @pl.when(first)
    def _():
        run_sc[...] = jnp.zeros(run_sc.shape, F32)

    m = mod_ref[0, 0]
    lane = _lane((TM, LANES))
    lo = lane < HEAD_DIM
    o = of_ref[0] + ob_ref[0]
    z = z_ref[0]
    proj = jnp.zeros((TM, d), F32)
    for pr in range(2):
        t = o[:, pr * LANES:(pr + 1) * LANES]
        t2 = t * t
        ms0 = jnp.sum(jnp.where(lo, t2, 0.0), axis=-1, keepdims=True) / HEAD_DIM
        ms1 = jnp.sum(jnp.where(lo, 0.0, t2), axis=-1, keepdims=True) / HEAD_DIM
        t = t * jnp.where(lo, lax.rsqrt(ms0 + NORM_EPS), lax.rsqrt(ms1 + NORM_EPS)) * gg_ref[...]
        t = t * _silu(z[:, pr * LANES:(pr + 1) * LANES])
        proj = proj + _dot(t.astype(BF16), wout_ref[pr * LANES:(pr + 1) * LANES, :])
    for gi, ref in enumerate((mo_ref, so_ref, go_ref)):
        for hh in range(4):
            r0 = (2 + gi * 4 + hh) * LANES
            proj = proj + _dot(ref[0, hh], wout_ref[r0:r0 + LANES, :])
    x1 = x_ref[0] + m[:, 2 * d:3 * d] * proj
    x1_ref[0] = x1
    xn = x1 * lax.rsqrt(jnp.mean(x1 * x1, axis=-1, keepdims=True) + NORM_EPS) * g2_ref[...]
    h2 = xn * (1.0 + m[:, 4 * d:5 * d]) + m[:, 3 * d:4 * d]
    h2_ref[0] = h2.astype(BF16)

    logits = _dot_hp(h2, rw_ref[...])
    scores = 1.0 / (1.0 + jnp.exp(-logits))
    valid = lane < N_EXPERTS
    sel = jnp.where(valid, scores + rb_ref[...], -jnp.inf)
    member = jnp.zeros((TM, LANES), F32)
    picks = []
    for _ in range(TOP_K):
        mx = jnp.max(sel, axis=-1, keepdims=True)
        idx = jnp.min(jnp.where(sel == mx, lane, LANES), axis=-1, keepdims=True)
        hit = lane == idx
        gate = jnp.sum(jnp.where(hit, scores, 0.0), axis=-1, keepdims=True)
        sel = jnp.where(hit, -jnp.inf, sel)
        member = member + hit.astype(F32)
        picks.append((idx, hit, gate))
    gsum = picks[0][2]
    for kk in range(1, TOP_K):
        gsum = gsum + picks[kk][2]
    ri = lax.broadcasted_iota(jnp.int32, (TM, TM), 0)
    ci = lax.broadcasted_iota(jnp.int32, (TM, TM), 1)
    before = (ri > ci).astype(BF16)
    run = run_sc[...]
    base_ref[0] = run
    cum = _dot(before, member.astype(BF16)) + run
    idx_out = jnp.full((TM, LANES), -1, jnp.int32)
    gate_out = jnp.zeros((TM, LANES), F32)
    rank_out = jnp.zeros((TM, LANES), F32)
    for kk, (idx, hit, gate) in enumerate(picks):
        rank = jnp.sum(jnp.where(hit, cum, 0.0), axis=-1, keepdims=True)
        here = lane == kk
        idx_out = jnp.where(here, idx, idx_out)
        gate_out = jnp.where(here, gate / gsum * ROUTED_SCALE, gate_out)
        rank_out = jnp.where(here, rank, rank_out)
    idx_ref[0] = idx_out
    gate_ref[0] = gate_out
    rank_ref[0] = rank_out
    run_sc[...] = run + jnp.sum(member, axis=0, keepdims=True)


def _outproj(x, mod_tab, o_f, o_b, ga, gg_row, mo, so, go, wout, g2, rw, rb, *, n_blk, nbl):
    bsz, _, d = x.shape
    s_out = n_blk * TM
    full = lambda a: pl.BlockSpec(a.shape, lambda b, i: (0,) * a.ndim)
    tok = lambda w: pl.BlockSpec((1, TM, w), lambda b, i: (b, i, 0))
    head = pl.BlockSpec((1, 4, TM, LANES), lambda b, i: (b, 0, i, 0))
    return pl.pallas_call(
        _outproj_kernel,
        grid=(bsz, n_blk),
        in_specs=[tok(d),
                  pl.BlockSpec((1, 1, 1, 6 * d), lambda b, i: (b, i // nbl, 0, 0)),
                  tok(2 * LANES), tok(2 * LANES),
                  pl.BlockSpec((1, TM, 2 * LANES), lambda b, i: (b, i, 3)),
                  full(gg_row), head, head, head, full(wout), full(g2), full(rw), full(rb)],
        out_specs=[tok(d), tok(d), tok(LANES), tok(LANES), tok(LANES),
                   pl.BlockSpec((1, 1, LANES), lambda b, i: (b * n_blk + i, 0, 0))],
        out_shape=[jax.ShapeDtypeStruct((bsz, s_out, d), F32),
                   jax.ShapeDtypeStruct((bsz, s_out, d), BF16),
                   jax.ShapeDtypeStruct((bsz, s_out, LANES), jnp.int32),
                   jax.ShapeDtypeStruct((bsz, s_out, LANES), F32),
                   jax.ShapeDtypeStruct((bsz, s_out, LANES), F32),
                   jax.ShapeDtypeStruct((bsz * n_blk, 1, LANES), F32)],
        scratch_shapes=[pltpu.VMEM((1, LANES), F32)],
        compiler_params=_cparams(("arbitrary", "arbitrary")),
        name="outproj_router",
    )(x, mod_tab, o_f, o_b, ga, gg_row, mo, so, go, wout, g2, rw, rb)


def _moe_kernel(h_ref, idx_ref, rank_ref, gate_ref, base_ref, wgu_ref, wdn_ref, sgu_ref, sdn_ref,
                y_ref, xy_sc, pos_sc, off_sc, cnt_sc):
    g = pl.program_id(1)
    tc = h_ref.shape[0]
    n_rows = xy_sc.shape[0]
    lane_t = _lane((tc, LANES))
    lane1 = _lane((1, LANES))

    @pl.when(g == 0)
    def _():
        h = h_ref[...]
        gu = _dot(h, sgu_ref[...])
        act = _silu(gu[:, :D_EXPERT]) * gu[:, D_EXPERT:]
        y_ref[...] = _dot(act.astype(BF16), sdn_ref[...])

        idx = idx_ref[...]
        rank = rank_ref[...]
        hits = [lane_t == idx[:, k:k + 1] for k in range(TOP_K)]
        cnt = jnp.zeros((1, LANES), F32)
        for hit in hits:
            cnt = cnt + jnp.sum(hit.astype(F32), axis=0, keepdims=True)
        seg = jnp.ceil(cnt * (1.0 / MOE_SEG)) * MOE_SEG
        ri = lax.broadcasted_iota(jnp.int32, (LANES, LANES), 0)
        ci = lax.broadcasted_iota(jnp.int32, (LANES, LANES), 1)
        off = _dot_exact_rhs(jnp.broadcast_to(seg, (8, LANES)), (ri < ci).astype(BF16))[0:1]
        off_sc[...] = off
        cnt_sc[...] = cnt
        adj = off - base_ref[0]
        pos = jnp.full((tc, LANES), -1.0, F32)
        for k in range(TOP_K):
            p_k = rank[:, k:k + 1] + jnp.sum(jnp.where(hits[k], adj, 0.0), axis=-1, keepdims=True)
            pos = jnp.where(lane_t == k, p_k, pos)
        pos_sc[...] = pos
        sel8 = (lax.broadcasted_iota(jnp.int32, (8, LANES), 0)
                == lax.broadcasted_iota(jnp.int32, (8, LANES), 1)).astype(BF16)
        hi, mid, lo = _split3(pos)
        pos_row = _dot_nt(sel8, hi) + _dot_nt(sel8, mid) + _dot_nt(sel8, lo)

        def gather_body(rb, carry):
            r0 = pl.multiple_of(rb * MOE_ROWS, MOE_ROWS)
            rid = (lax.broadcasted_iota(jnp.int32, (MOE_ROWS, tc), 0) + r0).astype(F32)
            onehot = jnp.zeros((MOE_ROWS, tc), F32)
            for k in range(TOP_K):
                onehot = jnp.where(pos_row[k:k + 1, :] == rid, 1.0, onehot)
            xy_sc[pl.ds(r0, MOE_ROWS), :] = _dot(onehot.astype(BF16), h).astype(BF16)
            return carry

        lax.fori_loop(0, n_rows // MOE_ROWS, gather_body, 0)

    for j in range(MOE_EG):
        e = g * MOE_EG + j
        off_e = jnp.sum(jnp.where(lane1 == e, off_sc[...], 0.0)).astype(jnp.int32)
        cnt_e = jnp.sum(jnp.where(lane1 == e, cnt_sc[...], 0.0)).astype(jnp.int32)

        def expert_body(sb, carry, j=j, off_e=off_e, cnt_e=cnt_e):
            r0 = pl.multiple_of(off_e + sb * MOE_ROWS, MOE_SEG)
            x = xy_sc[pl.ds(r0, MOE_ROWS), :]
            gu = _dot(x, wgu_ref[j])
            act = _silu(gu[:, :D_EXPERT]) * gu[:, D_EXPERT:]
            y = _dot(act.astype(BF16), wdn_ref[j])
            mine = lax.broadcasted_iota(jnp.int32, (MOE_ROWS, 1), 0) + sb * MOE_ROWS < cnt_e
            xy_sc[pl.ds(r0, MOE_ROWS), :] = jnp.where(mine, y, x.astype(F32)).astype(BF16)
            return carry

        lax.fori_loop(0, (cnt_e + MOE_ROWS - 1) // MOE_ROWS, expert_body, 0)

    @pl.when(g == pl.num_programs(1) - 1)
    def _():
        pos = pos_sc[...]
        gate = gate_ref[...]
        kblk = n_rows // MOE_KSPLIT
        for tb in range(tc // TM):
            pos_t = pos[tb * TM:(tb + 1) * TM]
            gate_t = gate[tb * TM:(tb + 1) * TM]
            out = y_ref[tb * TM:(tb + 1) * TM, :]
            for kb in range(MOE_KSPLIT):
                cid = (lax.broadcasted_iota(jnp.int32, (TM, kblk), 1) + kb * kblk).astype(F32)
                scat = jnp.zeros((TM, kblk), F32)
                for k in range(TOP_K):
                    scat = jnp.where(pos_t[:, k:k + 1] == cid, gate_t[:, k:k + 1], scat)
                out = out + _dot(scat.astype(BF16), xy_sc[kb * kblk:(kb + 1) * kblk, :])
            y_ref[tb * TM:(tb + 1) * TM, :] = out


def _moe(h2, idx, rank, gate, base, wgu, wdn, sgu, sdn):
    n_tok, d = h2.shape
    tc = MOE_CHUNK
    nc = n_tok // tc
    blocks_per_chunk = tc // TM
    ne = wgu.shape[0]
    n_rows = tc * TOP_K + ne * MOE_SEG + MOE_ROWS
    n_rows = -(-n_rows // (MOE_ROWS * MOE_KSPLIT)) * (MOE_ROWS * MOE_KSPLIT)
    return pl.pallas_call(
        _moe_kernel,
        grid=(nc, ne // MOE_EG),
        in_specs=[pl.BlockSpec((tc, d), lambda c, g: (c, 0)),
                  pl.BlockSpec((tc, LANES), lambda c, g: (c, 0)),
                  pl.BlockSpec((tc, LANES), lambda c, g: (c, 0)),
                  pl.BlockSpec((tc, LANES), lambda c, g: (c, 0)),
                  pl.BlockSpec((1, 1, LANES), lambda c, g: (c * blocks_per_chunk, 0, 0)),
                  pl.BlockSpec((MOE_EG, d, 2 * D_EXPERT), lambda c, g: (g, 0, 0)),
                  pl.BlockSpec((MOE_EG, D_EXPERT, d), lambda c, g: (g, 0, 0)),
                  pl.BlockSpec(sgu.shape, lambda c, g: (0, 0)),
                  pl.BlockSpec(sdn.shape, lambda c, g: (0, 0))],
        out_specs=pl.BlockSpec((tc, d), lambda c, g: (c, 0)),
        out_shape=jax.ShapeDtypeStruct((n_tok, d), F32),
        scratch_shapes=[pltpu.VMEM((n_rows, d), BF16), pltpu.VMEM((tc, LANES), F32),
                        pltpu.VMEM((1, LANES), F32), pltpu.VMEM((1, LANES), F32)],
        compiler_params=_cparams(("parallel", "arbitrary")),
        name="moe",
    )(h2, idx, rank, gate, base, wgu, wdn, sgu, sdn)


def _residual_kernel(x_ref, y_ref, mod_ref, o_ref):
    d = D_MODEL
    o_ref[0] = x_ref[0] + mod_ref[0, 0][:, 5 * d:6 * d] * y_ref[0]


def _residual(x1, y, mod_tab, *, nbl):
    bsz, s, d = x1.shape
    tok = pl.BlockSpec((1, TM, d), lambda b, i: (b, i, 0))
    return pl.pallas_call(
        _residual_kernel,
        grid=(bsz, s // TM),
        in_specs=[tok, tok, pl.BlockSpec((1, 1, 1, 6 * d), lambda b, i: (b, i // nbl, 0, 0))],
        out_specs=tok,
        out_shape=jax.ShapeDtypeStruct((bsz, s, d), F32),
        compiler_params=_cparams(("parallel", "parallel")),
        name="residual",
    )(x1, y, mod_tab)


def _axial_tables(n_rows, rot_dim):
    rows = jnp.repeat(jnp.arange(n_rows), GRID_W).astype(F32)
    cols = jnp.tile(jnp.arange(GRID_W), n_rows).astype(F32)
    axis_dim = rot_dim // 2
    inv_freq = ROPE_THETA ** (-jnp.arange(0, axis_dim, 2, dtype=F32) / axis_dim)
    ang_r = rows[:, None] * inv_freq
    ang_c = cols[:, None] * inv_freq
    ang = jnp.concatenate([ang_r, ang_r, ang_c, ang_c], axis=-1)
    return jnp.cos(ang), jnp.sin(ang)


def _rope_slab_tables(t_len, n_ctx, rot_dim, lane0):
    cos, sin = _axial_tables(t_len // GRID_W, rot_dim)
    half = rot_dim // 4
    sign = jnp.where((jnp.arange(rot_dim) % (2 * half)) < half, -1.0, 1.0).astype(F32)
    cos_t = jnp.ones((t_len + n_ctx, LANES), F32).at[:t_len, lane0:lane0 + rot_dim].set(cos)
    sin_t = jnp.zeros((t_len + n_ctx, LANES), F32).at[:t_len, lane0:lane0 + rot_dim].set(sin * sign)
    return cos_t, sin_t


def _slab_cols(starts, width):
    out = []
    for st in starts:
        out += list(range(st, st + width)) + [-1] * (LANES - width)
    return out


def _gather_cols(w, cols):
    w_ext = jnp.concatenate([w, jnp.zeros((w.shape[0], 1), w.dtype)], axis=1)
    idx = np.array([c if c >= 0 else w.shape[1] for c in cols], np.int32)
    return w_ext[:, idx]


def _pad_row(v, lane0=0):
    return jnp.zeros((1, LANES), F32).at[0, lane0:lane0 + v.shape[0]].set(v.astype(F32))


def _layer_weights(l, w_in, gdn_conv_w, gdn_a_log, gdn_dt_bias, gdn_norm_g, mla_qn_g, mla_kvn_g,
                   mla_w_uq, mla_w_ukv, mla_qk_g, swa_qk_g, swa_sink, gqa_qk_g, w_out, router_w,
                   router_bias):
    hd = HEAD_DIM
    cols = list(range(0, 1024))
    cols += list(range(O_MQ, O_MQ + MLA_Q_RANK)) + list(range(O_MKV, O_MKV + MLA_KV_RANK))
    misc = list(range(O_MKR, O_MKR + MLA_ROPE))
    misc += list(range(O_AA, O_AA + 4)) + list(range(O_AB, O_AB + 4))
    misc += list(range(O_AA + 4, O_AA + 8)) + list(range(O_AB + 4, O_AB + 8))
    cols += misc + [-1] * (LANES - len(misc))
    cols += _slab_cols([O_SQ + hd * i for i in range(4)] + [O_SK + hd * i for i in range(2)]
                       + [O_SV + hd * i for i in range(2)], hd)
    cols += _slab_cols([O_GQ + hd * i for i in range(4)] + [O_GK + hd * i for i in range(2)]
                       + [O_GV + hd * i for i in range(2)], hd)
    assert len(cols) == N_COL
    qd = MLA_NOPE + MLA_ROPE
    uq_cols = []
    for hh in range(MLA_HEADS):
        uq_cols += list(range(hh * qd, hh * qd + qd)) + [-1] * (LANES - qd)
    kvd = MLA_NOPE + MLA_V
    ukv_cols = _slab_cols([hh * kvd for hh in range(MLA_HEADS)], MLA_NOPE)
    ukv_cols += _slab_cols([hh * kvd + MLA_NOPE for hh in range(MLA_HEADS)], MLA_V)
    orow = list(range(0, 256)) + _slab_cols([256 + hd * i for i in range(12)], hd)
    alog = jnp.zeros((1, LANES), F32)
    dtb = jnp.zeros((1, LANES), F32)
    for dd in range(2):
        alog = alog.at[0, 32 + 8 * dd:36 + 8 * dd].set(gdn_a_log[l, dd])
        dtb = dtb.at[0, 32 + 8 * dd:36 + 8 * dd].set(gdn_dt_bias[l, dd])
    return dict(
        win=_gather_cols(w_in[l], cols).astype(BF16),
        conv=gdn_conv_w[l][:, :768],
        alog=alog, dtb=dtb,
        gdn_g=jnp.concatenate([gdn_norm_g[l], gdn_norm_g[l]]).reshape(1, LANES),
        mla_qn_g=mla_qn_g[l], mla_kvn_g=mla_kvn_g[l],
        wuq=_gather_cols(mla_w_uq[l], uq_cols).astype(BF16),
        wukv=_gather_cols(mla_w_ukv[l], ukv_cols).astype(BF16),
        mqg=_pad_row(mla_qk_g[l, 0]), mkg=_pad_row(mla_qk_g[l, 1, :MLA_NOPE]),
        krg=_pad_row(mla_qk_g[l, 1, MLA_NOPE:]),
        sqg=_pad_row(swa_qk_g[l, 0]), skg=_pad_row(swa_qk_g[l, 1]),
        gqg=_pad_row(gqa_qk_g[l, 0]), gkg=_pad_row(gqa_qk_g[l, 1]),
        wout=_gather_cols(w_out[l].T, orow).T.astype(BF16),
        rw=jnp.concatenate([router_w[l], jnp.zeros((D_MODEL, LANES - N_EXPERTS), F32)], axis=1),
        rb=_pad_row(router_bias[l]),
        sink=swa_sink[l],
    )


def _sink_rows(sink, grp, tq):
    hkv = sink.shape[0] // grp
    return jnp.repeat((sink.astype(F32) * LOG2E).reshape(hkv, grp), tq, axis=1).reshape(hkv, grp * tq, 1)


def kernel(x, c, ctx, c_ctx, w_mod, b_mod, norm1_g, norm2_g, w_in, gdn_conv_w, gdn_a_log, gdn_dt_bias,
           gdn_norm_g, mla_qn_g, mla_kvn_g, mla_w_uq, mla_w_ukv, mla_qk_g, swa_qk_g, swa_sink, gqa_qk_g,
           w_out, router_w, router_bias, exp_w_gu, exp_w_down, shared_w_gu, shared_w_down):
    bsz, t_len, d = x.shape
    n_ctx = ctx.shape[1]
    depth = w_mod.shape[0]
    s = t_len + n_ctx
    assert d == D_MODEL and t_len % TM == 0 and n_ctx % TM == 0 and t_len >= 3 * WINDOW
    assert bsz + 1 <= 8
    nbl = t_len // TM
    nb = s // TM

    tabs = (_rope_slab_tables(t_len, n_ctx, HEAD_DIM, 0) + _rope_slab_tables(t_len, n_ctx, MLA_ROPE, MLA_NOPE)
            + _rope_slab_tables(t_len, n_ctx, MLA_ROPE, 0))
    c8 = jnp.zeros((8, d), F32).at[:bsz].set(c).at[bsz].set(c_ctx)
    xs = jnp.concatenate([x, ctx], axis=1)

    tq_d = 512 if t_len % 512 == 0 else TM
    tk_d = 768 if (s % 768 == 0) else TM

    for l in range(depth):
        need_ctx = l < depth - 1
        lw = _layer_weights(l, w_in, gdn_conv_w, gdn_a_log, gdn_dt_bias, gdn_norm_g, mla_qn_g, mla_kvn_g,
                            mla_w_uq, mla_w_ukv, mla_qk_g, swa_qk_g, swa_sink, gqa_qk_g, w_out, router_w,
                            router_bias)
        mod = _modulation(c8, w_mod[l], b_mod[l])
        mod_tab = jnp.stack([mod[:bsz], jnp.broadcast_to(mod[bsz], (bsz, 6 * d))], axis=1)[:, :, None, :]

        (ga, misc, mq, mk, mv, sq, sk, sv, gq, gk, gv) = _inproj(
            xs, mod_tab, norm1_g[l].reshape(1, d), lw['win'], lw, tabs, nbl)

        qkv, lab = _gdn_prep(ga, misc, lw['conv'], lw['alog'], lw['dtb'], nbl)
        labt = jnp.swapaxes(lab[..., :8], 2, 3)
        o_f = _gdn_scan(qkv, lab, labt, reverse=False, nbl=nbl)
        o_b = _gdn_scan(qkv, lab, labt, reverse=True, nbl=nbl)

        mo = _flash(mq, mk, mv, grp=1, tq=2 * tq_d, tk=tk_d, q_blk0=0, nq=t_len // (2 * tq_d),
                    k_blk0=0, nk=s // tk_d)
        go = _flash(gq, gk, gv, grp=2, tq=tq_d, tk=tk_d, q_blk0=0, nq=t_len // tq_d, k_blk0=0, nk=s // tk_d)
        so = _swa(sq, sk, sv, _sink_rows(lw['sink'], 2, WINDOW), grp=2, t_len=t_len, n_ctx=n_ctx)
        if need_ctx:
            cb0 = t_len // n_ctx
            mo = _flash(mq, mk, mv, grp=1, tq=n_ctx, tk=n_ctx, q_blk0=cb0, nq=1, k_blk0=cb0, nk=1, prev=mo)
            go = _flash(gq, gk, gv, grp=2, tq=n_ctx, tk=n_ctx, q_blk0=cb0, nq=1, k_blk0=cb0, nk=1, prev=go)
            so = _flash(sq, sk, sv, grp=2, tq=n_ctx, tk=n_ctx, q_blk0=cb0, nq=1, k_blk0=cb0, nk=1,
                        sink_rows=_sink_rows(lw['sink'], 2, n_ctx), prev=so)

        n_blk = nb if need_ctx else nbl
        x1, h2, idx, gate, rank, base = _outproj(
            xs, mod_tab, o_f, o_b, ga, lw['gdn_g'], mo, so, go, lw['wout'], norm2_g[l].reshape(1, d),
            lw['rw'], lw['rb'], n_blk=n_blk, nbl=nbl)

        n_tok = bsz * n_blk * TM
        assert n_tok % MOE_CHUNK == 0
        y = _moe(h2.reshape(n_tok, d), idx.reshape(n_tok, LANES), rank.reshape(n_tok, LANES),
                 gate.reshape(n_tok, LANES), base,
                 exp_w_gu[l].astype(BF16), exp_w_down[l].astype(BF16),
                 shared_w_gu[l].astype(BF16), shared_w_down[l].astype(BF16))
        xs = _residual(x1, y.reshape(bsz, n_blk * TM, d), mod_tab, nbl=nbl)
    return xs[:, :t_len]
```

```python
import functools
import math

import numpy as np
import jax
import jax.numpy as jnp
from jax import lax
from jax.experimental import pallas as pl
from jax.experimental.pallas import tpu as pltpu

F32 = jnp.float32
BF16 = jnp.bfloat16

LANES = 128
TM = 256
VMEM_LIMIT = 56 * 1024 * 1024

D_MODEL = 1024
GRID_W = 64
HEAD_DIM = 64
ROPE_THETA = 10000.0
NORM_EPS = 1e-6
ATTN_SCALE = HEAD_DIM ** -0.5
GDN_HEADS = 4
GDN_DK = 64
GDN_CHUNK = 64
GDN_CONV = 5
MLA_HEADS = 4
MLA_Q_RANK = 256
MLA_KV_RANK = 128
MLA_NOPE = 64
MLA_ROPE = 32
MLA_V = 64
MLA_SCALE = (MLA_NOPE + MLA_ROPE) ** -0.5
SWA_HEADS = 4
SWA_KV_HEADS = 2
WINDOW = 128
GQA_HEADS = 4
GQA_KV_HEADS = 2
N_EXPERTS = 64
TOP_K = 6
D_EXPERT = 384
ROUTED_SCALE = 2.5
LOG2E = math.log2(math.e)
ONES_LANE = HEAD_DIM
FLASH_KB = 256
MOE_CHUNK = 1024
MOE_ROWS = 128
MOE_EG = 2
MOE_SEG = 16
MOE_KSPLIT = 3

O_AQ, O_AK, O_AV, O_AZ, O_AA, O_AB = 0, 256, 512, 768, 1024, 1032
O_MQ, O_MKV, O_MKR = 1040, 1296, 1424
O_SQ, O_SK, O_SV = 1456, 1712, 1840
O_GQ, O_GK, O_GV = 1968, 2224, 2352
D_IN = 2480
C_GDN, C_MLA, C_MISC, C_SWA, C_GQA, N_COL = 0, 1024, 1408, 1536, 2560, 3584


def _cparams(sem):
    return pltpu.CompilerParams(dimension_semantics=sem, vmem_limit_bytes=VMEM_LIMIT)


def _dot(a, b):
    return jnp.dot(a, b, preferred_element_type=F32)


def _dot_nt(a, b):
    return lax.dot_general(a, b, (((1,), (1,)), ((), ())), preferred_element_type=F32)


def _dot_tn(a, b):
    return lax.dot_general(a, b, (((0,), (0,)), ((), ())), preferred_element_type=F32)


def _split3(x):
    hi = x.astype(BF16)
    r1 = x - hi.astype(F32)
    mid = r1.astype(BF16)
    lo = (r1 - mid.astype(F32)).astype(BF16)
    return hi, mid, lo


def _dot_exact_lhs(a_bf16, x):
    hi, mid, lo = _split3(x)
    return _dot(a_bf16, hi) + _dot(a_bf16, mid) + _dot(a_bf16, lo)


def _dot_exact_rhs(x, b_bf16):
    hi, mid, lo = _split3(x)
    return _dot(hi, b_bf16) + _dot(mid, b_bf16) + _dot(lo, b_bf16)


def _dot_hp(a, b):
    ah = a.astype(BF16)
    al = (a - ah.astype(F32)).astype(BF16)
    bh = b.astype(BF16)
    bl = (b - bh.astype(F32)).astype(BF16)
    return _dot(ah, bh) + _dot(ah, bl) + _dot(al, bh)


def _silu(x):
    return x * (1.0 / (1.0 + jnp.exp(-x)))


def _lane(shape):
    return lax.broadcasted_iota(jnp.int32, shape, len(shape) - 1)


def _mod_kernel(c_ref, w_ref, b_ref, o_ref):
    s = _silu(c_ref[...])
    o_ref[...] = _dot(s.astype(BF16), w_ref[...].astype(BF16)) + b_ref[...]


def _modulation(c8, w_mod_l, b_mod_l):
    d = c8.shape[1]
    n = w_mod_l.shape[1]
    return pl.pallas_call(
        _mod_kernel,
        grid=(n // d,),
        in_specs=[pl.BlockSpec((8, d), lambda j: (0, 0)),
                  pl.BlockSpec((d, d), lambda j: (0, j)),
                  pl.BlockSpec((1, d), lambda j: (0, j))],
        out_specs=pl.BlockSpec((8, d), lambda j: (0, j)),
        out_shape=jax.ShapeDtypeStruct((8, n), F32),
        compiler_params=_cparams(("arbitrary",)),
        name="modulation",
    )(c8, w_mod_l, b_mod_l.reshape(1, n))


def _rope_slab(x, cos, sin_signed, half):
    lane = _lane(x.shape)
    fwd = pltpu.roll(x, LANES - half, 1)
    bwd = pltpu.roll(x, half, 1)
    rot = jnp.where(lane % (2 * half) < half, fwd, bwd)
    return x * cos + rot * sin_signed


def _inproj_kernel(x_ref, mod_ref, g1_ref, win_ref, qn_g_ref, kvn_g_ref, wuq_ref, wukv_ref,
                   mqg_ref, mkg_ref, krg_ref, sqg_ref, skg_ref, gqg_ref, gkg_ref,
                   cos_hd_ref, sin_hd_ref, cos_mq_ref, sin_mq_ref, cos_kr_ref, sin_kr_ref,
                   ga_ref, misc_ref, mq_ref, mk_ref, mv_ref, sq_ref, sk_ref, sv_ref,
                   gq_ref, gk_ref, gv_ref):
    d = D_MODEL
    x = x_ref[0]
    m = mod_ref[0, 0]
    shift, scale = m[:, 0:d], m[:, d:2 * d]
    xn = x * lax.rsqrt(jnp.mean(x * x, axis=-1, keepdims=True) + NORM_EPS) * g1_ref[...]
    h = xn * (1.0 + scale) + shift
    p = _dot(h.astype(BF16), win_ref[...])

    ga_ref[0] = p[:, C_GDN:C_GDN + 1024]
    misc = p[:, C_MISC:C_MISC + LANES]
    misc_ref[0] = misc

    lane = _lane((TM, LANES))
    cos_hd, sin_hd = cos_hd_ref[...], sin_hd_ref[...]

    cq = p[:, C_MLA:C_MLA + MLA_Q_RANK]
    cqn = cq * lax.rsqrt(jnp.mean(cq * cq, axis=-1, keepdims=True) + NORM_EPS) * qn_g_ref[...]
    qup = _dot(cqn.astype(BF16), wuq_ref[...])
    ckv = p[:, C_MLA + MLA_Q_RANK:C_MLA + MLA_Q_RANK + MLA_KV_RANK]
    ckvn = ckv * lax.rsqrt(jnp.mean(ckv * ckv, axis=-1, keepdims=True) + NORM_EPS) * kvn_g_ref[...]
    kvup = _dot(ckvn.astype(BF16), wukv_ref[...])
    kr = jnp.where(lane < MLA_ROPE, misc, 0.0)
    kr = kr * lax.rsqrt(jnp.sum(kr * kr, axis=-1, keepdims=True) / MLA_ROPE + NORM_EPS) * krg_ref[...]
    kr = _rope_slab(kr, cos_kr_ref[...], sin_kr_ref[...], MLA_ROPE // 4)
    kpe = pltpu.roll(kr, MLA_NOPE, 1)
    is_nope = lane < MLA_NOPE
    for hh in range(MLA_HEADS):
        q = qup[:, hh * LANES:(hh + 1) * LANES]
        q2 = q * q
        ss_n = jnp.sum(jnp.where(is_nope, q2, 0.0), axis=-1, keepdims=True)
        ss_p = jnp.sum(jnp.where(is_nope, 0.0, q2), axis=-1, keepdims=True)
        r = jnp.where(is_nope, lax.rsqrt(ss_n / MLA_NOPE + NORM_EPS), lax.rsqrt(ss_p / MLA_ROPE + NORM_EPS))
        q = q * r * mqg_ref[...]
        q = _rope_slab(q, cos_mq_ref[...], sin_mq_ref[...], MLA_ROPE // 4)
        mq_ref[0, hh] = (q * (MLA_SCALE * LOG2E)).astype(BF16)
        k = kvup[:, hh * LANES:(hh + 1) * LANES]
        k = k * lax.rsqrt(jnp.sum(k * k, axis=-1, keepdims=True) / MLA_NOPE + NORM_EPS) * mkg_ref[...]
        mk_ref[0, hh] = (k + kpe).astype(BF16)
        v = kvup[:, (MLA_HEADS + hh) * LANES:(MLA_HEADS + hh + 1) * LANES]
        mv_ref[0, hh] = jnp.where(lane == ONES_LANE, 1.0, v).astype(BF16)

    def gqa_prep(base, qg_ref, kg_ref, q_ref, k_ref, v_ref, nq, nkv):
        for hh in range(nq):
            q = p[:, base + hh * LANES:base + (hh + 1) * LANES]
            q = q * lax.rsqrt(jnp.sum(q * q, axis=-1, keepdims=True) / HEAD_DIM + NORM_EPS) * qg_ref[...]
            q = _rope_slab(q, cos_hd, sin_hd, HEAD_DIM // 4)
            q_ref[0, hh] = (q * (ATTN_SCALE * LOG2E)).astype(BF16)
        for hh in range(nkv):
            k = p[:, base + (nq + hh) * LANES:base + (nq + hh + 1) * LANES]
            k = k * lax.rsqrt(jnp.sum(k * k, axis=-1, keepdims=True) / HEAD_DIM + NORM_EPS) * kg_ref[...]
            k = _rope_slab(k, cos_hd, sin_hd, HEAD_DIM // 4)
            k_ref[0, hh] = k.astype(BF16)
            v = p[:, base + (nq + nkv + hh) * LANES:base + (nq + nkv + hh + 1) * LANES]
            v_ref[0, hh] = jnp.where(lane == ONES_LANE, 1.0, v).astype(BF16)

    gqa_prep(C_SWA, sqg_ref, skg_ref, sq_ref, sk_ref, sv_ref, SWA_HEADS, SWA_KV_HEADS)
    gqa_prep(C_GQA, gqg_ref, gkg_ref, gq_ref, gk_ref, gv_ref, GQA_HEADS, GQA_KV_HEADS)


def _inproj(x, mod_tab, g1, win, lw, tabs, nbl):
    bsz, s, d = x.shape
    nb = s // TM
    row = lambda a: a.reshape(1, -1)
    full = lambda a: pl.BlockSpec(a.shape, lambda b, i: (0,) * a.ndim)
    tab_spec = pl.BlockSpec((TM, LANES), lambda b, i: (i, 0))
    head_out = lambda nh: pl.BlockSpec((1, nh, TM, LANES), lambda b, i: (b, 0, i, 0))
    head_shape = lambda nh: jax.ShapeDtypeStruct((bsz, nh, s, LANES), BF16)
    small = [row(lw['mla_qn_g']), row(lw['mla_kvn_g']), lw['wuq'], lw['wukv'],
             lw['mqg'], lw['mkg'], lw['krg'], lw['sqg'], lw['skg'], lw['gqg'], lw['gkg']]
    return pl.pallas_call(
        _inproj_kernel,
        grid=(bsz, nb),
        in_specs=[pl.BlockSpec((1, TM, d), lambda b, i: (b, i, 0)),
                  pl.BlockSpec((1, 1, 1, 6 * d), lambda b, i: (b, i // nbl, 0, 0)),
                  full(g1), full(win)] + [full(a) for a in small] + [tab_spec] * 6,
        out_specs=[pl.BlockSpec((1, TM, 1024), lambda b, i: (b, i, 0)),
                   pl.BlockSpec((1, TM, LANES), lambda b, i: (b, i, 0)),
                   head_out(4), head_out(4), head_out(4),
                   head_out(4), head_out(2), head_out(2),
                   head_out(4), head_out(2), head_out(2)],
        out_shape=[jax.ShapeDtypeStruct((bsz, s, 1024), F32),
                   jax.ShapeDtypeStruct((bsz, s, LANES), F32),
                   head_shape(4), head_shape(4), head_shape(4),
                   head_shape(4), head_shape(2), head_shape(2),
                   head_shape(4), head_shape(2), head_shape(2)],
        compiler_params=_cparams(("parallel", "parallel")),
        name="inproj",
    )(x, mod_tab, g1, win, *small, *tabs)


def _flash_kernel(*refs, grp, tq, nk, use_sink):
    refs = list(refs)
    sink_ref = refs.pop(0) if use_sink else None
    q_ref, k_ref, v_ref = refs[0], refs[1], refs[2]
    o_ref, m_sc, acc_sc = refs[-3], refs[-2], refs[-1]
    kj = pl.program_id(3)
    tk = k_ref.shape[2]
    kb = min(FLASH_KB, tk)

    @pl.when(kj == 0)
    def _():
        m_sc[...] = jnp.full(m_sc.shape, -jnp.inf, F32)
        acc_sc[...] = jnp.zeros(acc_sc.shape, F32)

    q = q_ref[0].reshape(grp * tq, LANES)
    m = m_sc[...]
    acc = acc_sc[...]
    for j in range(tk // kb):
        s = _dot_nt(q, k_ref[0, 0, j * kb:(j + 1) * kb, :])
        m_new = jnp.maximum(m, jnp.max(s, axis=-1, keepdims=True))
        alpha = jnp.exp2(m - m_new)
        pr = jnp.exp2(s - jnp.concatenate([m_new] * (kb // LANES), axis=1))
        acc = alpha * acc + _dot(pr.astype(BF16), v_ref[0, 0, j * kb:(j + 1) * kb, :])
        m = m_new
    m_sc[...] = m
    acc_sc[...] = acc

    @pl.when(kj == nk - 1)
    def _():
        l = acc[:, ONES_LANE:ONES_LANE + 1]
        out = acc
        if use_sink:
            sk = sink_ref[0]
            m_old = m[:, 0:1]
            m_fin = jnp.maximum(m_old, sk)
            a = jnp.exp2(m_old - m_fin)
            l = l * a + jnp.exp2(sk - m_fin)
            out = out * a
        o_ref[0] = (out / l).reshape(grp, tq, LANES).astype(o_ref.dtype)


def _flash(q, k, v, *, grp, tq, tk, q_blk0, nq, k_blk0, nk, sink_rows=None):
    bsz, hq, s, _ = q.shape
    hkv = hq // grp
    use_sink = sink_rows is not None
    in_specs = []
    args = []
    if use_sink:
        in_specs.append(pl.BlockSpec((1, grp * tq, 1), lambda b, h, i, j: (h, 0, 0)))
        args.append(sink_rows)
    in_specs += [pl.BlockSpec((1, grp, tq, LANES), lambda b, h, i, j: (b, h, i + q_blk0, 0)),
                 pl.BlockSpec((1, 1, tk, LANES), lambda b, h, i, j: (b, h, j + k_blk0, 0)),
                 pl.BlockSpec((1, 1, tk, LANES), lambda b, h, i, j: (b, h, j + k_blk0, 0))]
    args += [q, k, v]
    rows = grp * tq
    return pl.pallas_call(
        functools.partial(_flash_kernel, grp=grp, tq=tq, nk=nk, use_sink=use_sink),
        grid=(bsz, hkv, nq, nk),
        in_specs=in_specs,
        out_specs=pl.BlockSpec((1, grp, tq, LANES), lambda b, h, i, j: (b, h, i, 0)),
        out_shape=jax.ShapeDtypeStruct((bsz, hq, nq * tq, LANES), BF16),
        scratch_shapes=[pltpu.VMEM((rows, LANES), F32), pltpu.VMEM((rows, LANES), F32)],
        compiler_params=_cparams(("parallel", "parallel", "parallel", "arbitrary")),
        name="flash",
    )(*args)


def _swa_kernel(sink_ref, q_ref, k_ref, v_ref, o_ref, *, grp, tq, t_len, n_ctx):
    w = WINDOW
    span = tq + 2 * w
    n = pl.program_id(2)
    start = pl.multiple_of(jnp.clip(n * tq - w, 0, t_len - span), w)
    q = q_ref[0].reshape(grp * tq, LANES)
    kl = k_ref[0, 0, pl.ds(start, span), :]
    vl = v_ref[0, 0, pl.ds(start, span), :]
    kc = k_ref[0, 0, pl.ds(t_len, n_ctx), :]
    vc = v_ref[0, 0, pl.ds(t_len, n_ctx), :]
    s_loc = _dot_nt(q, kl)
    qpos = n * tq + (lax.broadcasted_iota(jnp.int32, s_loc.shape, 0) % tq)
    kpos = start + lax.broadcasted_iota(jnp.int32, s_loc.shape, 1)
    s_loc = jnp.where(jnp.abs(qpos - kpos) <= w, s_loc, -jnp.inf)
    s_ctx = _dot_nt(q, kc)
    sk = sink_ref[0]
    m = jnp.maximum(jnp.maximum(jnp.max(s_loc, axis=-1, keepdims=True),
                                jnp.max(s_ctx, axis=-1, keepdims=True)), sk)
    p_loc = jnp.exp2(s_loc - m)
    p_ctx = jnp.exp2(s_ctx - m)
    o = _dot(p_loc.astype(BF16), vl) + _dot(p_ctx.astype(BF16), vc)
    l = o[:, ONES_LANE:ONES_LANE + 1] + jnp.exp2(sk - m)
    o_ref[0] = (o / l).reshape(grp, tq, LANES).astype(o_ref.dtype)


def _swa(q, k, v, sink_rows, *, grp, tq, t_len, n_ctx):
    bsz, hq, s, _ = q.shape
    hkv = hq // grp
    return pl.pallas_call(
        functools.partial(_swa_kernel, grp=grp, tq=tq, t_len=t_len, n_ctx=n_ctx),
        grid=(bsz, hkv, t_len // tq),
        in_specs=[pl.BlockSpec((1, grp * tq, 1), lambda b, h, n: (h, 0, 0)),
                  pl.BlockSpec((1, grp, tq, LANES), lambda b, h, n: (b, h, n, 0)),
                  pl.BlockSpec((1, 1, s, LANES), lambda b, h, n: (b, h, 0, 0)),
                  pl.BlockSpec((1, 1, s, LANES), lambda b, h, n: (b, h, 0, 0))],
        out_specs=pl.BlockSpec((1, grp, tq, LANES), lambda b, h, n: (b, h, n, 0)),
        out_shape=jax.ShapeDtypeStruct((bsz, hq, t_len, LANES), BF16),
        compiler_params=_cparams(("parallel", "parallel", "arbitrary")),
        name="swa",
    )(sink_rows, q, k, v)


def _gdn_prep_kernel(cur_ref, prev_ref, next_ref, misc_ref, cw_ref, alog_ref, dtb_ref,
                     qkv_ref, lab_ref, *, nbl, nb):
    i = pl.program_id(1)
    first = jnp.logical_or(i == 0, i == nbl)
    last = jnp.logical_or(i == nbl - 1, i == nb - 1)
    cur = cur_ref[0]
    prev = jnp.where(first, 0.0, prev_ref[0])
    nxt = jnp.where(last, 0.0, next_ref[0])
    xe = jnp.concatenate([prev, cur, nxt], axis=0)
    cw = cw_ref[...]
    acc = jnp.zeros(cur.shape, F32)
    for j in range(GDN_CONV):
        off = 8 + j - (GDN_CONV - 1) // 2
        acc = acc + xe[off:off + TM, :] * cw[j:j + 1, :]
    y = _silu(acc)
    lane = _lane((TM, LANES))
    lo = lane < GDN_DK
    for sl in range(6):
        t = y[:, sl * LANES:(sl + 1) * LANES]
        if sl < 4:
            t2 = t * t
            ss0 = jnp.sum(jnp.where(lo, t2, 0.0), axis=-1, keepdims=True)
            ss1 = jnp.sum(jnp.where(lo, 0.0, t2), axis=-1, keepdims=True)
            t = t * jnp.where(lo, lax.rsqrt(ss0 + NORM_EPS), lax.rsqrt(ss1 + NORM_EPS))
            if sl < 2:
                t = t * GDN_DK ** -0.5
        qkv_ref[0, :, sl * LANES:(sl + 1) * LANES] = t
    xm = misc_ref[0]
    za = xm + dtb_ref[...]
    softplus = jnp.maximum(za, 0.0) + jnp.log(1.0 + jnp.exp(-jnp.abs(za)))
    log_a = -jnp.exp(alog_ref[...]) * softplus
    beta = 1.0 / (1.0 + jnp.exp(-xm))
    is_a = (lane % 8) < 4
    vals = jnp.where(is_a, log_a, beta)
    lab_ref[0, 0] = pltpu.roll(vals, LANES - 32, 1)
    lab_ref[0, 1] = pltpu.roll(vals, LANES - 40, 1)


def _gdn_prep(ga, misc, cw, alog_row, dtb_row, nbl):
    bsz, s, _ = ga.shape
    nb = s // TM
    r8 = TM // 8
    n8 = s // 8
    return pl.pallas_call(
        functools.partial(_gdn_prep_kernel, nbl=nbl, nb=nb),
        grid=(bsz, nb),
        in_specs=[pl.BlockSpec((1, TM, 768), lambda b, i: (b, i, 0)),
                  pl.BlockSpec((1, 8, 768), lambda b, i: (b, jnp.maximum(i * r8 - 1, 0), 0)),
                  pl.BlockSpec((1, 8, 768), lambda b, i: (b, jnp.minimum((i + 1) * r8, n8 - 1), 0)),
                  pl.BlockSpec((1, TM, LANES), lambda b, i: (b, i, 0)),
                  pl.BlockSpec(cw.shape, lambda b, i: (0, 0)),
                  pl.BlockSpec((1, LANES), lambda b, i: (0, 0)),
                  pl.BlockSpec((1, LANES), lambda b, i: (0, 0))],
        out_specs=[pl.BlockSpec((1, TM, 768), lambda b, i: (b, i, 0)),
                   pl.BlockSpec((1, 2, TM, LANES), lambda b, i: (b, 0, i, 0))],
        out_shape=[jax.ShapeDtypeStruct((bsz, s, 768), F32),
                   jax.ShapeDtypeStruct((bsz, 2, s, LANES), F32)],
        compiler_params=_cparams(("parallel", "parallel")),
        name="gdn_prep",
    )(ga, ga, ga, misc, cw, alog_row, dtb_row)


def _gdn_scan_kernel(qkv_ref, lab_ref, labt_ref, o_ref, s_sc, *, reverse):
    c = GDN_CHUNK
    c2 = 2 * c
    i = pl.program_id(1)

    @pl.when(i == 0)
    def _():
        s_sc[...] = jnp.zeros(s_sc.shape, F32)

    ri = lax.broadcasted_iota(jnp.int32, (c2, c2), 0)
    ci = lax.broadcasted_iota(jnp.int32, (c2, c2), 1)
    same = (ri // c) == (ci // c)
    if reverse:
        incl = jnp.logical_and(same, ri <= ci)
        strict = jnp.logical_and(same, ri < ci)
    else:
        incl = jnp.logical_and(same, ri >= ci)
        strict = jnp.logical_and(same, ri > ci)
    eye = (ri == ci).astype(F32)
    r1 = lax.broadcasted_iota(jnp.int32, (c, c), 0)
    c1 = lax.broadcasted_iota(jnp.int32, (c, c), 1)
    cum_col = ((r1 <= c1) if reverse else (r1 >= c1)).astype(BF16)
    rr = lax.broadcasted_iota(jnp.int32, (c, c2), 0)
    cc = lax.broadcasted_iota(jnp.int32, (c, c2), 1) % c
    cum_row = ((rr >= cc) if reverse else (rr <= cc)).astype(BF16)
    lane = _lane((c, LANES))
    lo = lane < GDN_DK
    lane2 = _lane((1, c2))
    lo2 = lane2 < c
    g_last_row = 0 if reverse else c - 1

    def stack(x):
        return jnp.concatenate([jnp.where(lo, x, 0.0), jnp.where(lo, 0.0, x)], axis=0)

    def fold(mat):
        return mat[0:c, :] + mat[c:c2, :]

    order = list(range(TM // c - 1, -1, -1) if reverse else range(TM // c))
    inst = []
    for ch in order:
        rows = slice(ch * c, (ch + 1) * c)
        lab = lab_ref[0, 0, rows, :]
        g_cols = _dot_exact_lhs(cum_col, lab)
        la_rows = labt_ref[0, 0, :, rows]
        g_rows = _dot_exact_rhs(la_rows, cum_row)
        for pr in range(2):
            h0, h1 = 2 * pr, 2 * pr + 1
            q = qkv_ref[0, rows, pr * LANES:(pr + 1) * LANES]
            k = qkv_ref[0, rows, (2 + pr) * LANES:(3 + pr) * LANES]
            v = qkv_ref[0, rows, (4 + pr) * LANES:(5 + pr) * LANES]
            beta = jnp.where(lo, lab[:, 4 + h0:5 + h0], lab[:, 4 + h1:5 + h1])
            g = jnp.where(lo, g_cols[:, h0:h0 + 1], g_cols[:, h1:h1 + 1])
            g_st_col = jnp.concatenate([g_cols[:, h0:h0 + 1], g_cols[:, h1:h1 + 1]], axis=0)
            g_st_row = jnp.where(lo2, g_rows[h0:h0 + 1, :], g_rows[h1:h1 + 1, :])
            diff = g_st_col - g_st_row
            dec = jnp.where(incl, jnp.exp(jnp.where(incl, diff, 0.0)), 0.0)
            kb = k * beta
            eg = jnp.exp(g)
            k_st = stack(k).astype(BF16)
            a = jnp.where(strict, _dot_nt(stack(kb).astype(BF16), k_st) * dec, 0.0)
            qk = _dot_nt(stack(q).astype(BF16), k_st) * dec
            gl = g[g_last_row:g_last_row + 1, :]
            gl_col = jnp.concatenate([jnp.broadcast_to(g_cols[g_last_row:g_last_row + 1, h0:h0 + 1], (c, 1)),
                                      jnp.broadcast_to(g_cols[g_last_row:g_last_row + 1, h1:h1 + 1], (c, 1))],
                                     axis=0)
            inst.append(dict(rows=rows, pr=pr, a=a, vb=stack(v * beta).astype(BF16),
                             kg=stack(kb * eg).astype(BF16), qg=(q * eg).astype(BF16),
                             qkf=fold(qk).astype(BF16), kd=(k * jnp.exp(gl - g)).astype(BF16),
                             sdec=jnp.exp(gl_col)))
    tinv = [eye - it['a'] for it in inst]
    pw = [it['a'] for it in inst]
    for _ in range(5):
        pw = [_dot_hp(x, x) for x in pw]
        tinv = [t + _dot_hp(t, x) for t, x in zip(tinv, pw)]
    for it, t in zip(inst, tinv):
        tf = fold(t).astype(BF16)
        it['u'] = _dot(tf, it['vb'])
        it['w'] = _dot(tf, it['kg']).astype(BF16)
    state = [s_sc[0], s_sc[1]]
    for it in inst:
        st = state[it['pr']]
        st_b = st.astype(BF16)
        v_new = it['u'] - _dot(it['w'], st_b)
        o = _dot(it['qg'], st_b) + _dot(it['qkf'], stack(v_new).astype(BF16))
        o_ref[0, it['rows'], it['pr'] * LANES:(it['pr'] + 1) * LANES] = o
        upd = _dot_tn(it['kd'], v_new.astype(BF16))
        state[it['pr']] = st * it['sdec'] + jnp.where(same, upd, 0.0)
    s_sc[0] = state[0]
    s_sc[1] = state[1]


def _gdn_scan(qkv, lab, labt, *, reverse, nbl):
    bsz, s, _ = qkv.shape
    nb = s // TM
    nbc = nb - nbl
    dsel = 1 if reverse else 0

    def blk(i):
        if reverse:
            return jnp.where(i < nbc, nb - 1 - i, nbl - 1 - (i - nbc))
        return jnp.where(i < nbc, nbl + i, i - nbc)

    return pl.pallas_call(
        functools.partial(_gdn_scan_kernel, reverse=reverse),
        grid=(bsz, nb),
        in_specs=[pl.BlockSpec((1, TM, 768), lambda b, i: (b, blk(i), 0)),
                  pl.BlockSpec((1, 1, TM, LANES), lambda b, i: (b, dsel, blk(i), 0)),
                  pl.BlockSpec((1, 1, 8, TM), lambda b, i: (b, dsel, 0, blk(i)))],
        out_specs=pl.BlockSpec((1, TM, 2 * LANES), lambda b, i: (b, blk(i), 0)),
        out_shape=jax.ShapeDtypeStruct((bsz, s, 2 * LANES), F32),
        scratch_shapes=[pltpu.VMEM((2, LANES, LANES), F32)],
        compiler_params=_cparams(("parallel", "arbitrary")),
        name="gdn_scan_bwd" if reverse else "gdn_scan_fwd",
    )(qkv, lab, labt)


def _outproj_kernel(x_ref, mod_ref, of_ref, ob_ref, z_ref, gg_ref, mo_ref, so_ref, go_ref,
                    mc_ref, sc_ref, gc_ref, wout_ref, g2_ref, rw_ref, rb_ref,
                    x1_ref, h2_ref, idx_ref, gate_ref, rank_ref, base_ref, run_sc, *, nbl):
    d = D_MODEL
    first = jnp.logical_and(pl.program_id(0) == 0, pl.program_id(1) == 0)
    is_ctx = pl.program_id(1) >= nbl

    @pl.when(first)
    def _():
        run_sc[...] = jnp.zeros(run_sc.shape, F32)

    m = mod_ref[0, 0]
    lane = _lane((TM, LANES))
    lo = lane < HEAD_DIM
    o = of_ref[0] + ob_ref[0]
    z = z_ref[0]
    mix = []
    for pr in range(2):
        t = o[:, pr * LANES:(pr + 1) * LANES]
        t2 = t * t
        ms0 = jnp.sum(jnp.where(lo, t2, 0.0), axis=-1, keepdims=True) / HEAD_DIM
        ms1 = jnp.sum(jnp.where(lo, 0.0, t2), axis=-1, keepdims=True) / HEAD_DIM
        t = t * jnp.where(lo, lax.rsqrt(ms0 + NORM_EPS), lax.rsqrt(ms1 + NORM_EPS)) * gg_ref[...]
        t = t * _silu(z[:, pr * LANES:(pr + 1) * LANES])
        mix.append(t.astype(BF16))
    for lat_ref, ctx_ref in ((mo_ref, mc_ref), (so_ref, sc_ref), (go_ref, gc_ref)):
        for hh in range(4):
            mix.append(jnp.where(is_ctx, ctx_ref[0, hh], lat_ref[0, hh]))
    proj = _dot(jnp.concatenate(mix, axis=1), wout_ref[...])
    x1 = x_ref[0] + m[:, 2 * d:3 * d] * proj
    x1_ref[0] = x1
    xn = x1 * lax.rsqrt(jnp.mean(x1 * x1, axis=-1, keepdims=True) + NORM_EPS) * g2_ref[...]
    h2 = xn * (1.0 + m[:, 4 * d:5 * d]) + m[:, 3 * d:4 * d]
    h2_ref[0] = h2.astype(BF16)

    logits = _dot_hp(h2, rw_ref[...])
    scores = 1.0 / (1.0 + jnp.exp(-logits))
    valid = lane < N_EXPERTS
    sel = jnp.where(valid, scores + rb_ref[...], -jnp.inf)
    member = jnp.zeros((TM, LANES), F32)
    picks = []
    for _ in range(TOP_K):
        mx = jnp.max(sel, axis=-1, keepdims=True)
        idx = jnp.min(jnp.where(sel == mx, lane, LANES), axis=-1, keepdims=True)
        hit = lane == idx
        gate = jnp.sum(jnp.where(hit, scores, 0.0), axis=-1, keepdims=True)
        sel = jnp.where(hit, -jnp.inf, sel)
        member = member + hit.astype(F32)
        picks.append((idx, hit, gate))
    gsum = picks[0][2]
    for kk in range(1, TOP_K):
        gsum = gsum + picks[kk][2]
    ri = lax.broadcasted_iota(jnp.int32, (TM, TM), 0)
    ci = lax.broadcasted_iota(jnp.int32, (TM, TM), 1)
    before = (ri > ci).astype(BF16)
    run = run_sc[...]
    base_ref[0] = run
    cum = _dot(before, member.astype(BF16)) + run
    idx_out = jnp.full((TM, LANES), -1, jnp.int32)
    gate_out = jnp.zeros((TM, LANES), F32)
    rank_out = jnp.zeros((TM, LANES), F32)
    for kk, (idx, hit, gate) in enumerate(picks):
        rank = jnp.sum(jnp.where(hit, cum, 0.0), axis=-1, keepdims=True)
        here = lane == kk
        idx_out = jnp.where(here, idx, idx_out)
        gate_out = jnp.where(here, gate / gsum * ROUTED_SCALE, gate_out)
        rank_out = jnp.where(here, rank, rank_out)
    idx_ref[0] = idx_out
    gate_ref[0] = gate_out
    rank_ref[0] = rank_out
    run_sc[...] = run + jnp.sum(member, axis=0, keepdims=True)


def _outproj(x, mod_tab, o_f, o_b, ga, gg_row, lat, ctx, wout, g2, rw, rb, *, n_blk, nbl):
    bsz, _, d = x.shape
    s_out = n_blk * TM
    full = lambda a: pl.BlockSpec(a.shape, lambda b, i: (0,) * a.ndim)
    tok = lambda w: pl.BlockSpec((1, TM, w), lambda b, i: (b, i, 0))
    head = pl.BlockSpec((1, 4, TM, LANES), lambda b, i: (b, 0, jnp.minimum(i, nbl - 1), 0))
    head_c = pl.BlockSpec((1, 4, TM, LANES), lambda b, i: (b, 0, jnp.maximum(i - nbl, 0), 0))
    return pl.pallas_call(
        functools.partial(_outproj_kernel, nbl=nbl),
        grid=(bsz, n_blk),
        in_specs=[tok(d),
                  pl.BlockSpec((1, 1, 1, 6 * d), lambda b, i: (b, i // nbl, 0, 0)),
                  tok(2 * LANES), tok(2 * LANES),
                  pl.BlockSpec((1, TM, 2 * LANES), lambda b, i: (b, i, 3)),
                  full(gg_row), head, head, head, head_c, head_c, head_c,
                  full(wout), full(g2), full(rw), full(rb)],
        out_specs=[tok(d), tok(d), tok(LANES), tok(LANES), tok(LANES),
                   pl.BlockSpec((1, 1, LANES), lambda b, i: (b * n_blk + i, 0, 0))],
        out_shape=[jax.ShapeDtypeStruct((bsz, s_out, d), F32),
                   jax.ShapeDtypeStruct((bsz, s_out, d), BF16),
                   jax.ShapeDtypeStruct((bsz, s_out, LANES), jnp.int32),
                   jax.ShapeDtypeStruct((bsz, s_out, LANES), F32),
                   jax.ShapeDtypeStruct((bsz, s_out, LANES), F32),
                   jax.ShapeDtypeStruct((bsz * n_blk, 1, LANES), F32)],
        scratch_shapes=[pltpu.VMEM((1, LANES), F32)],
        compiler_params=_cparams(("arbitrary", "arbitrary")),
        name="outproj_router",
    )(x, mod_tab, o_f, o_b, ga, gg_row, *lat, *ctx, wout, g2, rw, rb)


def _moe_kernel(h_ref, idx_ref, rank_ref, gate_ref, base_ref, wgu_ref, wdn_ref, sgu_ref, sdn_ref,
                y_ref, xy_sc, pos_sc, off_sc, cnt_sc):
    g = pl.program_id(1)
    tc = h_ref.shape[0]
    n_rows = xy_sc.shape[0]
    lane_t = _lane((tc, LANES))
    lane1 = _lane((1, LANES))

    @pl.when(g == 0)
    def _():
        h = h_ref[...]
        gu = _dot(h, sgu_ref[...])
        act = _silu(gu[:, :D_EXPERT]) * gu[:, D_EXPERT:]
        y_ref[...] = _dot(act.astype(BF16), sdn_ref[...])

        idx = idx_ref[...]
        rank = rank_ref[...]
        hits = [lane_t == idx[:, k:k + 1] for k in range(TOP_K)]
        cnt = jnp.zeros((1, LANES), F32)
        for hit in hits:
            cnt = cnt + jnp.sum(hit.astype(F32), axis=0, keepdims=True)
        seg = jnp.ceil(cnt * (1.0 / MOE_SEG)) * MOE_SEG
        ri = lax.broadcasted_iota(jnp.int32, (LANES, LANES), 0)
        ci = lax.broadcasted_iota(jnp.int32, (LANES, LANES), 1)
        off = _dot_exact_rhs(jnp.broadcast_to(seg, (8, LANES)), (ri < ci).astype(BF16))[0:1]
        off_sc[...] = off
        cnt_sc[...] = cnt
        adj = off - base_ref[0]
        pos = jnp.full((tc, LANES), -1.0, F32)
        for k in range(TOP_K):
            p_k = rank[:, k:k + 1] + jnp.sum(jnp.where(hits[k], adj, 0.0), axis=-1, keepdims=True)
            pos = jnp.where(lane_t == k, p_k, pos)
        pos_sc[...] = pos
        sel8 = (lax.broadcasted_iota(jnp.int32, (8, LANES), 0)
                == lax.broadcasted_iota(jnp.int32, (8, LANES), 1)).astype(BF16)
        hi, mid, lo = _split3(pos)
        pos_row = _dot_nt(sel8, hi) + _dot_nt(sel8, mid) + _dot_nt(sel8, lo)

        def gather_body(rb, carry):
            r0 = pl.multiple_of(rb * MOE_ROWS, MOE_ROWS)
            rid = (lax.broadcasted_iota(jnp.int32, (MOE_ROWS, tc), 0) + r0).astype(F32)
            onehot = jnp.zeros((MOE_ROWS, tc), F32)
            for k in range(TOP_K):
                onehot = jnp.where(pos_row[k:k + 1, :] == rid, 1.0, onehot)
            xy_sc[pl.ds(r0, MOE_ROWS), :] = _dot(onehot.astype(BF16), h).astype(BF16)
            return carry

        lax.fori_loop(0, n_rows // MOE_ROWS, gather_body, 0)

    for j in range(MOE_EG):
        e = g * MOE_EG + j
        off_e = jnp.sum(jnp.where(lane1 == e, off_sc[...], 0.0)).astype(jnp.int32)
        cnt_e = jnp.sum(jnp.where(lane1 == e, cnt_sc[...], 0.0)).astype(jnp.int32)

        def expert_body(sb, carry, j=j, off_e=off_e, cnt_e=cnt_e):
            r0 = pl.multiple_of(off_e + sb * MOE_ROWS, MOE_SEG)
            x = xy_sc[pl.ds(r0, MOE_ROWS), :]
            gu = _dot(x, wgu_ref[j])
            act = _silu(gu[:, :D_EXPERT]) * gu[:, D_EXPERT:]
            y = _dot(act.astype(BF16), wdn_ref[j])
            mine = lax.broadcasted_iota(jnp.int32, (MOE_ROWS, 1), 0) + sb * MOE_ROWS < cnt_e
            xy_sc[pl.ds(r0, MOE_ROWS), :] = jnp.where(mine, y, x.astype(F32)).astype(BF16)
            return carry

        lax.fori_loop(0, (cnt_e + MOE_ROWS - 1) // MOE_ROWS, expert_body, 0)

    @pl.when(g == pl.num_programs(1) - 1)
    def _():
        pos = pos_sc[...]
        gate = gate_ref[...]
        kblk = n_rows // MOE_KSPLIT
        for tb in range(tc // TM):
            pos_t = pos[tb * TM:(tb + 1) * TM]
            gate_t = gate[tb * TM:(tb + 1) * TM]
            out = y_ref[tb * TM:(tb + 1) * TM, :]
            for kb in range(MOE_KSPLIT):
                cid = (lax.broadcasted_iota(jnp.int32, (TM, kblk), 1) + kb * kblk).astype(F32)
                scat = jnp.zeros((TM, kblk), F32)
                for k in range(TOP_K):
                    scat = jnp.where(pos_t[:, k:k + 1] == cid, gate_t[:, k:k + 1], scat)
                out = out + _dot(scat.astype(BF16), xy_sc[kb * kblk:(kb + 1) * kblk, :])
            y_ref[tb * TM:(tb + 1) * TM, :] = out


def _moe(h2, idx, rank, gate, base, wgu, wdn, sgu, sdn):
    n_tok, d = h2.shape
    tc = MOE_CHUNK
    nc = n_tok // tc
    blocks_per_chunk = tc // TM
    ne = wgu.shape[0]
    n_rows = tc * TOP_K + ne * MOE_SEG + MOE_ROWS
    n_rows = -(-n_rows // (MOE_ROWS * MOE_KSPLIT)) * (MOE_ROWS * MOE_KSPLIT)
    return pl.pallas_call(
        _moe_kernel,
        grid=(nc, ne // MOE_EG),
        in_specs=[pl.BlockSpec((tc, d), lambda c, g: (c, 0)),
                  pl.BlockSpec((tc, LANES), lambda c, g: (c, 0)),
                  pl.BlockSpec((tc, LANES), lambda c, g: (c, 0)),
                  pl.BlockSpec((tc, LANES), lambda c, g: (c, 0)),
                  pl.BlockSpec((1, 1, LANES), lambda c, g: (c * blocks_per_chunk, 0, 0)),
                  pl.BlockSpec((MOE_EG, d, 2 * D_EXPERT), lambda c, g: (g, 0, 0)),
                  pl.BlockSpec((MOE_EG, D_EXPERT, d), lambda c, g: (g, 0, 0)),
                  pl.BlockSpec(sgu.shape, lambda c, g: (0, 0)),
                  pl.BlockSpec(sdn.shape, lambda c, g: (0, 0))],
        out_specs=pl.BlockSpec((tc, d), lambda c, g: (c, 0)),
        out_shape=jax.ShapeDtypeStruct((n_tok, d), F32),
        scratch_shapes=[pltpu.VMEM((n_rows, d), BF16), pltpu.VMEM((tc, LANES), F32),
                        pltpu.VMEM((1, LANES), F32), pltpu.VMEM((1, LANES), F32)],
        compiler_params=_cparams(("parallel", "arbitrary")),
        name="moe",
    )(h2, idx, rank, gate, base, wgu, wdn, sgu, sdn)


def _residual_kernel(x_ref, y_ref, mod_ref, o_ref):
    d = D_MODEL
    o_ref[0] = x_ref[0] + mod_ref[0, 0][:, 5 * d:6 * d] * y_ref[0]


def _residual(x1, y, mod_tab, *, nbl):
    bsz, s, d = x1.shape
    tok = pl.BlockSpec((1, TM, d), lambda b, i: (b, i, 0))
    return pl.pallas_call(
        _residual_kernel,
        grid=(bsz, s // TM),
        in_specs=[tok, tok, pl.BlockSpec((1, 1, 1, 6 * d), lambda b, i: (b, i // nbl, 0, 0))],
        out_specs=tok,
        out_shape=jax.ShapeDtypeStruct((bsz, s, d), F32),
        compiler_params=_cparams(("parallel", "parallel")),
        name="residual",
    )(x1, y, mod_tab)


def _axial_tables(n_rows, rot_dim):
    rows = jnp.repeat(jnp.arange(n_rows), GRID_W).astype(F32)
    cols = jnp.tile(jnp.arange(GRID_W), n_rows).astype(F32)
    axis_dim = rot_dim // 2
    inv_freq = ROPE_THETA ** (-jnp.arange(0, axis_dim, 2, dtype=F32) / axis_dim)
    ang_r = rows[:, None] * inv_freq
    ang_c = cols[:, None] * inv_freq
    ang = jnp.concatenate([ang_r, ang_r, ang_c, ang_c], axis=-1)
    return jnp.cos(ang), jnp.sin(ang)


def _rope_slab_tables(t_len, n_ctx, rot_dim, lane0):
    cos, sin = _axial_tables(t_len // GRID_W, rot_dim)
    half = rot_dim // 4
    sign = jnp.where((jnp.arange(rot_dim) % (2 * half)) < half, -1.0, 1.0).astype(F32)
    cos_t = jnp.ones((t_len + n_ctx, LANES), F32).at[:t_len, lane0:lane0 + rot_dim].set(cos)
    sin_t = jnp.zeros((t_len + n_ctx, LANES), F32).at[:t_len, lane0:lane0 + rot_dim].set(sin * sign)
    return cos_t, sin_t


def _slab_cols(starts, width):
    out = []
    for st in starts:
        out += list(range(st, st + width)) + [-1] * (LANES - width)
    return out


def _gather_cols(w, cols):
    w_ext = jnp.concatenate([w, jnp.zeros((w.shape[0], 1), w.dtype)], axis=1)
    idx = np.array([c if c >= 0 else w.shape[1] for c in cols], np.int32)
    return w_ext[:, idx]


def _pad_row(v, lane0=0):
    return jnp.zeros((1, LANES), F32).at[0, lane0:lane0 + v.shape[0]].set(v.astype(F32))


def _layer_weights(l, w_in, gdn_conv_w, gdn_a_log, gdn_dt_bias, gdn_norm_g, mla_qn_g, mla_kvn_g,
                   mla_w_uq, mla_w_ukv, mla_qk_g, swa_qk_g, swa_sink, gqa_qk_g, w_out, router_w,
                   router_bias):
    hd = HEAD_DIM
    cols = list(range(0, 1024))
    cols += list(range(O_MQ, O_MQ + MLA_Q_RANK)) + list(range(O_MKV, O_MKV + MLA_KV_RANK))
    misc = list(range(O_MKR, O_MKR + MLA_ROPE))
    misc += list(range(O_AA, O_AA + 4)) + list(range(O_AB, O_AB + 4))
    misc += list(range(O_AA + 4, O_AA + 8)) + list(range(O_AB + 4, O_AB + 8))
    cols += misc + [-1] * (LANES - len(misc))
    cols += _slab_cols([O_SQ + hd * i for i in range(4)] + [O_SK + hd * i for i in range(2)]
                       + [O_SV + hd * i for i in range(2)], hd)
    cols += _slab_cols([O_GQ + hd * i for i in range(4)] + [O_GK + hd * i for i in range(2)]
                       + [O_GV + hd * i for i in range(2)], hd)
    assert len(cols) == N_COL
    qd = MLA_NOPE + MLA_ROPE
    uq_cols = []
    for hh in range(MLA_HEADS):
        uq_cols += list(range(hh * qd, hh * qd + qd)) + [-1] * (LANES - qd)
    kvd = MLA_NOPE + MLA_V
    ukv_cols = _slab_cols([hh * kvd for hh in range(MLA_HEADS)], MLA_NOPE)
    ukv_cols += _slab_cols([hh * kvd + MLA_NOPE for hh in range(MLA_HEADS)], MLA_V)
    orow = list(range(0, 256)) + _slab_cols([256 + hd * i for i in range(12)], hd)
    alog = jnp.zeros((1, LANES), F32)
    dtb = jnp.zeros((1, LANES), F32)
    for dd in range(2):
        alog = alog.at[0, 32 + 8 * dd:36 + 8 * dd].set(gdn_a_log[l, dd])
        dtb = dtb.at[0, 32 + 8 * dd:36 + 8 * dd].set(gdn_dt_bias[l, dd])
    return dict(
        win=_gather_cols(w_in[l], cols).astype(BF16),
        conv=gdn_conv_w[l][:, :768],
        alog=alog, dtb=dtb,
        gdn_g=jnp.concatenate([gdn_norm_g[l], gdn_norm_g[l]]).reshape(1, LANES),
        mla_qn_g=mla_qn_g[l], mla_kvn_g=mla_kvn_g[l],
        wuq=_gather_cols(mla_w_uq[l], uq_cols).astype(BF16),
        wukv=_gather_cols(mla_w_ukv[l], ukv_cols).astype(BF16),
        mqg=_pad_row(mla_qk_g[l, 0]), mkg=_pad_row(mla_qk_g[l, 1, :MLA_NOPE]),
        krg=_pad_row(mla_qk_g[l, 1, MLA_NOPE:]),
        sqg=_pad_row(swa_qk_g[l, 0]), skg=_pad_row(swa_qk_g[l, 1]),
        gqg=_pad_row(gqa_qk_g[l, 0]), gkg=_pad_row(gqa_qk_g[l, 1]),
        wout=_gather_cols(w_out[l].T, orow).T.astype(BF16),
        rw=jnp.concatenate([router_w[l], jnp.zeros((D_MODEL, LANES - N_EXPERTS), F32)], axis=1),
        rb=_pad_row(router_bias[l]),
        sink=swa_sink[l],
    )


def _sink_rows(sink, grp, tq):
    hkv = sink.shape[0] // grp
    return jnp.repeat((sink.astype(F32) * LOG2E).reshape(hkv, grp), tq, axis=1).reshape(hkv, grp * tq, 1)


def kernel(x, c, ctx, c_ctx, w_mod, b_mod, norm1_g, norm2_g, w_in, gdn_conv_w, gdn_a_log, gdn_dt_bias,
           gdn_norm_g, mla_qn_g, mla_kvn_g, mla_w_uq, mla_w_ukv, mla_qk_g, swa_qk_g, swa_sink, gqa_qk_g,
           w_out, router_w, router_bias, exp_w_gu, exp_w_down, shared_w_gu, shared_w_down):
    bsz, t_len, d = x.shape
    n_ctx = ctx.shape[1]
    depth = w_mod.shape[0]
    s = t_len + n_ctx
    assert d == D_MODEL and t_len % TM == 0 and n_ctx % TM == 0 and t_len >= 3 * WINDOW
    assert bsz + 1 <= 8
    nbl = t_len // TM
    nb = s // TM

    tabs = (_rope_slab_tables(t_len, n_ctx, HEAD_DIM, 0) + _rope_slab_tables(t_len, n_ctx, MLA_ROPE, MLA_NOPE)
            + _rope_slab_tables(t_len, n_ctx, MLA_ROPE, 0))
    c8 = jnp.zeros((8, d), F32).at[:bsz].set(c).at[bsz].set(c_ctx)
    xs = jnp.concatenate([x, ctx], axis=1)

    tq_d = 512 if t_len % 512 == 0 else TM
    tk_d = 2816 if (s % 2816 == 0) else TM

    for l in range(depth):
        need_ctx = l < depth - 1
        lw = _layer_weights(l, w_in, gdn_conv_w, gdn_a_log, gdn_dt_bias, gdn_norm_g, mla_qn_g, mla_kvn_g,
                            mla_w_uq, mla_w_ukv, mla_qk_g, swa_qk_g, swa_sink, gqa_qk_g, w_out, router_w,
                            router_bias)
        mod = _modulation(c8, w_mod[l], b_mod[l])
        mod_tab = jnp.stack([mod[:bsz], jnp.broadcast_to(mod[bsz], (bsz, 6 * d))], axis=1)[:, :, None, :]

        (ga, misc, mq, mk, mv, sq, sk, sv, gq, gk, gv) = _inproj(
            xs, mod_tab, norm1_g[l].reshape(1, d), lw['win'], lw, tabs, nbl)

        qkv, lab = _gdn_prep(ga, misc, lw['conv'], lw['alog'], lw['dtb'], nbl)
        labt = jnp.swapaxes(lab[..., :8], 2, 3)
        o_f = _gdn_scan(qkv, lab, labt, reverse=False, nbl=nbl)
        o_b = _gdn_scan(qkv, lab, labt, reverse=True, nbl=nbl)

        mo = _flash(mq, mk, mv, grp=1, tq=2 * tq_d, tk=tk_d, q_blk0=0, nq=t_len // (2 * tq_d),
                    k_blk0=0, nk=s // tk_d)
        go = _flash(gq, gk, gv, grp=2, tq=tq_d, tk=tk_d, q_blk0=0, nq=t_len // tq_d, k_blk0=0, nk=s // tk_d)
        so = _swa(sq, sk, sv, _sink_rows(lw['sink'], 2, tq_d), grp=2, tq=tq_d, t_len=t_len, n_ctx=n_ctx)
        lat = (mo, so, go)
        ctx_out = lat
        if need_ctx:
            cb0 = t_len // n_ctx
            ctx_out = (
                _flash(mq, mk, mv, grp=1, tq=n_ctx, tk=n_ctx, q_blk0=cb0, nq=1, k_blk0=cb0, nk=1),
                _flash(sq, sk, sv, grp=2, tq=n_ctx, tk=n_ctx, q_blk0=cb0, nq=1, k_blk0=cb0, nk=1,
                       sink_rows=_sink_rows(lw['sink'], 2, n_ctx)),
                _flash(gq, gk, gv, grp=2, tq=n_ctx, tk=n_ctx, q_blk0=cb0, nq=1, k_blk0=cb0, nk=1))

        n_blk = nb if need_ctx else nbl
        x1, h2, idx, gate, rank, base = _outproj(
            xs, mod_tab, o_f, o_b, ga, lw['gdn_g'], lat, ctx_out, lw['wout'], norm2_g[l].reshape(1, d),
            lw['rw'], lw['rb'], n_blk=n_blk, nbl=nbl)

        n_tok = bsz * n_blk * TM
        assert n_tok % MOE_CHUNK == 0
        y = _moe(h2.reshape(n_tok, d), idx.reshape(n_tok, LANES), rank.reshape(n_tok, LANES),
                 gate.reshape(n_tok, LANES), base,
                 exp_w_gu[l].astype(BF16), exp_w_down[l].astype(BF16),
                 shared_w_gu[l].astype(BF16), shared_w_down[l].astype(BF16))
        xs = _residual(x1, y.reshape(bsz, n_blk * TM, d), mod_tab, nbl=nbl)
    return xs[:, :t_len]
```

```python
import functools
import math

import numpy as np
import jax
import jax.numpy as jnp
from jax import lax
from jax.experimental import pallas as pl
from jax.experimental.pallas import tpu as pltpu

F32 = jnp.float32
BF16 = jnp.bfloat16

LANES = 128
TM = 256
VMEM_LIMIT = 56 * 1024 * 1024

D_MODEL = 1024
GRID_W = 64
HEAD_DIM = 64
ROPE_THETA = 10000.0
NORM_EPS = 1e-6
ATTN_SCALE = HEAD_DIM ** -0.5
GDN_HEADS = 4
GDN_DK = 64
GDN_CHUNK = 64
GDN_CONV = 5
MLA_HEADS = 4
MLA_Q_RANK = 256
MLA_KV_RANK = 128
MLA_NOPE = 64
MLA_ROPE = 32
MLA_V = 64
MLA_SCALE = (MLA_NOPE + MLA_ROPE) ** -0.5
SWA_HEADS = 4
SWA_KV_HEADS = 2
WINDOW = 128
GQA_HEADS = 4
GQA_KV_HEADS = 2
N_EXPERTS = 64
TOP_K = 6
D_EXPERT = 384
ROUTED_SCALE = 2.5
LOG2E = math.log2(math.e)
ONES_LANE = HEAD_DIM
FLASH_KB = 256
MOE_CHUNK = 1024
MOE_ROWS = 128
MOE_BR = 256
MOE_SEG = 16
MOE_KSPLIT = 3

O_AQ, O_AK, O_AV, O_AZ, O_AA, O_AB = 0, 256, 512, 768, 1024, 1032
O_MQ, O_MKV, O_MKR = 1040, 1296, 1424
O_SQ, O_SK, O_SV = 1456, 1712, 1840
O_GQ, O_GK, O_GV = 1968, 2224, 2352
D_IN = 2480
C_GDN, C_MLA, C_MISC, C_SWA, C_GQA, N_COL = 0, 1024, 1408, 1536, 2560, 3584


def _cparams(sem):
    return pltpu.CompilerParams(dimension_semantics=sem, vmem_limit_bytes=VMEM_LIMIT)


def _dot(a, b):
    return jnp.dot(a, b, preferred_element_type=F32)


def _dot_nt(a, b):
    return lax.dot_general(a, b, (((1,), (1,)), ((), ())), preferred_element_type=F32)


def _dot_tn(a, b):
    return lax.dot_general(a, b, (((0,), (0,)), ((), ())), preferred_element_type=F32)


def _split3(x):
    hi = x.astype(BF16)
    r1 = x - hi.astype(F32)
    mid = r1.astype(BF16)
    lo = (r1 - mid.astype(F32)).astype(BF16)
    return hi, mid, lo


def _dot_exact_lhs(a_bf16, x):
    hi, mid, lo = _split3(x)
    return _dot(a_bf16, hi) + _dot(a_bf16, mid) + _dot(a_bf16, lo)


def _dot_exact_rhs(x, b_bf16):
    hi, mid, lo = _split3(x)
    return _dot(hi, b_bf16) + _dot(mid, b_bf16) + _dot(lo, b_bf16)


def _dot_hp(a, b):
    ah = a.astype(BF16)
    al = (a - ah.astype(F32)).astype(BF16)
    bh = b.astype(BF16)
    bl = (b - bh.astype(F32)).astype(BF16)
    return _dot(ah, bh) + _dot(ah, bl) + _dot(al, bh)


def _silu(x):
    return x * (1.0 / (1.0 + jnp.exp(-x)))


def _lane(shape):
    return lax.broadcasted_iota(jnp.int32, shape, len(shape) - 1)


def _mod_kernel(c_ref, w_ref, b_ref, o_ref):
    s = _silu(c_ref[...])
    o_ref[...] = _dot(s.astype(BF16), w_ref[...].astype(BF16)) + b_ref[...]


def _modulation(c8, w_mod_l, b_mod_l):
    d = c8.shape[1]
    n = w_mod_l.shape[1]
    return pl.pallas_call(
        _mod_kernel,
        grid=(n // d,),
        in_specs=[pl.BlockSpec((8, d), lambda j: (0, 0)),
                  pl.BlockSpec((d, d), lambda j: (0, j)),
                  pl.BlockSpec((1, d), lambda j: (0, j))],
        out_specs=pl.BlockSpec((8, d), lambda j: (0, j)),
        out_shape=jax.ShapeDtypeStruct((8, n), F32),
        compiler_params=_cparams(("arbitrary",)),
        name="modulation",
    )(c8, w_mod_l, b_mod_l.reshape(1, n))


def _rope_slab(x, cos, sin_signed, half):
    lane = _lane(x.shape)
    fwd = pltpu.roll(x, LANES - half, 1)
    bwd = pltpu.roll(x, half, 1)
    rot = jnp.where(lane % (2 * half) < half, fwd, bwd)
    return x * cos + rot * sin_signed


def _inproj_kernel(x_ref, mod_ref, g1_ref, win_ref, qn_g_ref, kvn_g_ref, wuq_ref, wukv_ref,
                   mqg_ref, mkg_ref, krg_ref, sqg_ref, skg_ref, gqg_ref, gkg_ref,
                   cos_hd_ref, sin_hd_ref, cos_mq_ref, sin_mq_ref, cos_kr_ref, sin_kr_ref,
                   ga_ref, misc_ref, mq_ref, mk_ref, mv_ref, sq_ref, sk_ref, sv_ref,
                   gq_ref, gk_ref, gv_ref):
    d = D_MODEL
    x = x_ref[0]
    m = mod_ref[0, 0]
    shift, scale = m[:, 0:d], m[:, d:2 * d]
    xn = x * lax.rsqrt(jnp.mean(x * x, axis=-1, keepdims=True) + NORM_EPS) * g1_ref[...]
    h = xn * (1.0 + scale) + shift
    p = _dot(h.astype(BF16), win_ref[...])

    ga_ref[0] = p[:, C_GDN:C_GDN + 1024]
    misc = p[:, C_MISC:C_MISC + LANES]
    misc_ref[0] = misc

    lane = _lane((TM, LANES))
    cos_hd, sin_hd = cos_hd_ref[...], sin_hd_ref[...]

    cq = p[:, C_MLA:C_MLA + MLA_Q_RANK]
    cqn = cq * lax.rsqrt(jnp.mean(cq * cq, axis=-1, keepdims=True) + NORM_EPS) * qn_g_ref[...]
    qup = _dot(cqn.astype(BF16), wuq_ref[...])
    ckv = p[:, C_MLA + MLA_Q_RANK:C_MLA + MLA_Q_RANK + MLA_KV_RANK]
    ckvn = ckv * lax.rsqrt(jnp.mean(ckv * ckv, axis=-1, keepdims=True) + NORM_EPS) * kvn_g_ref[...]
    kvup = _dot(ckvn.astype(BF16), wukv_ref[...])
    kr = jnp.where(lane < MLA_ROPE, misc, 0.0)
    kr = kr * lax.rsqrt(jnp.sum(kr * kr, axis=-1, keepdims=True) / MLA_ROPE + NORM_EPS) * krg_ref[...]
    kr = _rope_slab(kr, cos_kr_ref[...], sin_kr_ref[...], MLA_ROPE // 4)
    kpe = pltpu.roll(kr, MLA_NOPE, 1)
    is_nope = lane < MLA_NOPE
    for hh in range(MLA_HEADS):
        q = qup[:, hh * LANES:(hh + 1) * LANES]
        q2 = q * q
        ss_n = jnp.sum(jnp.where(is_nope, q2, 0.0), axis=-1, keepdims=True)
        ss_p = jnp.sum(jnp.where(is_nope, 0.0, q2), axis=-1, keepdims=True)
        r = jnp.where(is_nope, lax.rsqrt(ss_n / MLA_NOPE + NORM_EPS), lax.rsqrt(ss_p / MLA_ROPE + NORM_EPS))
        q = q * r * mqg_ref[...]
        q = _rope_slab(q, cos_mq_ref[...], sin_mq_ref[...], MLA_ROPE // 4)
        mq_ref[0, hh] = (q * (MLA_SCALE * LOG2E)).astype(BF16)
        k = kvup[:, hh * LANES:(hh + 1) * LANES]
        k = k * lax.rsqrt(jnp.sum(k * k, axis=-1, keepdims=True) / MLA_NOPE + NORM_EPS) * mkg_ref[...]
        mk_ref[0, hh] = (k + kpe).astype(BF16)
        v = kvup[:, (MLA_HEADS + hh) * LANES:(MLA_HEADS + hh + 1) * LANES]
        mv_ref[0, hh] = jnp.where(lane == ONES_LANE, 1.0, v).astype(BF16)

    def gqa_prep(base, qg_ref, kg_ref, q_ref, k_ref, v_ref, nq, nkv):
        for hh in range(nq):
            q = p[:, base + hh * LANES:base + (hh + 1) * LANES]
            q = q * lax.rsqrt(jnp.sum(q * q, axis=-1, keepdims=True) / HEAD_DIM + NORM_EPS) * qg_ref[...]
            q = _rope_slab(q, cos_hd, sin_hd, HEAD_DIM // 4)
            q_ref[0, hh] = (q * (ATTN_SCALE * LOG2E)).astype(BF16)
        for hh in range(nkv):
            k = p[:, base + (nq + hh) * LANES:base + (nq + hh + 1) * LANES]
            k = k * lax.rsqrt(jnp.sum(k * k, axis=-1, keepdims=True) / HEAD_DIM + NORM_EPS) * kg_ref[...]
            k = _rope_slab(k, cos_hd, sin_hd, HEAD_DIM // 4)
            k_ref[0, hh] = k.astype(BF16)
            v = p[:, base + (nq + nkv + hh) * LANES:base + (nq + nkv + hh + 1) * LANES]
            v_ref[0, hh] = jnp.where(lane == ONES_LANE, 1.0, v).astype(BF16)

    gqa_prep(C_SWA, sqg_ref, skg_ref, sq_ref, sk_ref, sv_ref, SWA_HEADS, SWA_KV_HEADS)
    gqa_prep(C_GQA, gqg_ref, gkg_ref, gq_ref, gk_ref, gv_ref, GQA_HEADS, GQA_KV_HEADS)


def _inproj(x, mod_tab, g1, win, lw, tabs, nbl):
    bsz, s, d = x.shape
    nb = s // TM
    row = lambda a: a.reshape(1, -1)
    full = lambda a: pl.BlockSpec(a.shape, lambda b, i: (0,) * a.ndim)
    tab_spec = pl.BlockSpec((TM, LANES), lambda b, i: (i, 0))
    head_out = lambda nh: pl.BlockSpec((1, nh, TM, LANES), lambda b, i: (b, 0, i, 0))
    head_shape = lambda nh: jax.ShapeDtypeStruct((bsz, nh, s, LANES), BF16)
    small = [row(lw['mla_qn_g']), row(lw['mla_kvn_g']), lw['wuq'], lw['wukv'],
             lw['mqg'], lw['mkg'], lw['krg'], lw['sqg'], lw['skg'], lw['gqg'], lw['gkg']]
    return pl.pallas_call(
        _inproj_kernel,
        grid=(bsz, nb),
        in_specs=[pl.BlockSpec((1, TM, d), lambda b, i: (b, i, 0)),
                  pl.BlockSpec((1, 1, 1, 6 * d), lambda b, i: (b, i // nbl, 0, 0)),
                  full(g1), full(win)] + [full(a) for a in small] + [tab_spec] * 6,
        out_specs=[pl.BlockSpec((1, TM, 1024), lambda b, i: (b, i, 0)),
                   pl.BlockSpec((1, TM, LANES), lambda b, i: (b, i, 0)),
                   head_out(4), head_out(4), head_out(4),
                   head_out(4), head_out(2), head_out(2),
                   head_out(4), head_out(2), head_out(2)],
        out_shape=[jax.ShapeDtypeStruct((bsz, s, 1024), F32),
                   jax.ShapeDtypeStruct((bsz, s, LANES), F32),
                   head_shape(4), head_shape(4), head_shape(4),
                   head_shape(4), head_shape(2), head_shape(2),
                   head_shape(4), head_shape(2), head_shape(2)],
        compiler_params=_cparams(("parallel", "parallel")),
        name="inproj",
    )(x, mod_tab, g1, win, *small, *tabs)


def _flash_kernel(*refs, grp, tq, nk, use_sink):
    refs = list(refs)
    sink_ref = refs.pop(0) if use_sink else None
    q_ref, k_ref, v_ref = refs[0], refs[1], refs[2]
    o_ref, m_sc, acc_sc = refs[-3], refs[-2], refs[-1]
    kj = pl.program_id(3)
    tk = k_ref.shape[2]
    kb = min(FLASH_KB, tk)

    @pl.when(kj == 0)
    def _():
        m_sc[...] = jnp.full(m_sc.shape, -jnp.inf, F32)
        acc_sc[...] = jnp.zeros(acc_sc.shape, F32)

    q = q_ref[0].reshape(grp * tq, LANES)
    m = m_sc[...]
    acc = acc_sc[...]
    for j in range(tk // kb):
        s = _dot_nt(q, k_ref[0, 0, j * kb:(j + 1) * kb, :])
        m_new = jnp.maximum(m, jnp.max(s, axis=-1, keepdims=True))
        alpha = jnp.exp2(m - m_new)
        pr = jnp.exp2(s - jnp.concatenate([m_new] * (kb // LANES), axis=1))
        acc = alpha * acc + _dot(pr.astype(BF16), v_ref[0, 0, j * kb:(j + 1) * kb, :])
        m = m_new
    m_sc[...] = m
    acc_sc[...] = acc

    @pl.when(kj == nk - 1)
    def _():
        l = acc[:, ONES_LANE:ONES_LANE + 1]
        out = acc
        if use_sink:
            sk = sink_ref[0]
            m_old = m[:, 0:1]
            m_fin = jnp.maximum(m_old, sk)
            a = jnp.exp2(m_old - m_fin)
            l = l * a + jnp.exp2(sk - m_fin)
            out = out * a
        o_ref[0] = (out / l).reshape(grp, tq, LANES).astype(o_ref.dtype)


def _flash(q, k, v, *, grp, tq, tk, q_blk0, nq, k_blk0, nk, sink_rows=None):
    bsz, hq, s, _ = q.shape
    hkv = hq // grp
    use_sink = sink_rows is not None
    in_specs = []
    args = []
    if use_sink:
        in_specs.append(pl.BlockSpec((1, grp * tq, 1), lambda b, h, i, j: (h, 0, 0)))
        args.append(sink_rows)
    in_specs += [pl.BlockSpec((1, grp, tq, LANES), lambda b, h, i, j: (b, h, i + q_blk0, 0)),
                 pl.BlockSpec((1, 1, tk, LANES), lambda b, h, i, j: (b, h, j + k_blk0, 0)),
                 pl.BlockSpec((1, 1, tk, LANES), lambda b, h, i, j: (b, h, j + k_blk0, 0))]
    args += [q, k, v]
    rows = grp * tq
    return pl.pallas_call(
        functools.partial(_flash_kernel, grp=grp, tq=tq, nk=nk, use_sink=use_sink),
        grid=(bsz, hkv, nq, nk),
        in_specs=in_specs,
        out_specs=pl.BlockSpec((1, grp, tq, LANES), lambda b, h, i, j: (b, h, i, 0)),
        out_shape=jax.ShapeDtypeStruct((bsz, hq, nq * tq, LANES), BF16),
        scratch_shapes=[pltpu.VMEM((rows, LANES), F32), pltpu.VMEM((rows, LANES), F32)],
        compiler_params=_cparams(("parallel", "parallel", "parallel", "arbitrary")),
        name="flash",
    )(*args)


def _swa_kernel(sink_ref, q_ref, k_ref, v_ref, o_ref, *, grp, tq, t_len, n_ctx):
    w = WINDOW
    span = tq + 2 * w
    n = pl.program_id(2)
    start = pl.multiple_of(jnp.clip(n * tq - w, 0, t_len - span), w)
    q = q_ref[0].reshape(grp * tq, LANES)
    kl = k_ref[0, 0, pl.ds(start, span), :]
    vl = v_ref[0, 0, pl.ds(start, span), :]
    kc = k_ref[0, 0, pl.ds(t_len, n_ctx), :]
    vc = v_ref[0, 0, pl.ds(t_len, n_ctx), :]
    s_loc = _dot_nt(q, kl)
    qpos = n * tq + (lax.broadcasted_iota(jnp.int32, s_loc.shape, 0) % tq)
    kpos = start + lax.broadcasted_iota(jnp.int32, s_loc.shape, 1)
    s_loc = jnp.where(jnp.abs(qpos - kpos) <= w, s_loc, -jnp.inf)
    s_ctx = _dot_nt(q, kc)
    sk = sink_ref[0]
    m = jnp.maximum(jnp.maximum(jnp.max(s_loc, axis=-1, keepdims=True),
                                jnp.max(s_ctx, axis=-1, keepdims=True)), sk)
    p_loc = jnp.exp2(s_loc - m)
    p_ctx = jnp.exp2(s_ctx - m)
    o = _dot(p_loc.astype(BF16), vl) + _dot(p_ctx.astype(BF16), vc)
    l = o[:, ONES_LANE:ONES_LANE + 1] + jnp.exp2(sk - m)
    o_ref[0] = (o / l).reshape(grp, tq, LANES).astype(o_ref.dtype)


def _swa(q, k, v, sink_rows, *, grp, tq, t_len, n_ctx):
    bsz, hq, s, _ = q.shape
    hkv = hq // grp
    return pl.pallas_call(
        functools.partial(_swa_kernel, grp=grp, tq=tq, t_len=t_len, n_ctx=n_ctx),
        grid=(bsz, hkv, t_len // tq),
        in_specs=[pl.BlockSpec((1, grp * tq, 1), lambda b, h, n: (h, 0, 0)),
                  pl.BlockSpec((1, grp, tq, LANES), lambda b, h, n: (b, h, n, 0)),
                  pl.BlockSpec((1, 1, s, LANES), lambda b, h, n: (b, h, 0, 0)),
                  pl.BlockSpec((1, 1, s, LANES), lambda b, h, n: (b, h, 0, 0))],
        out_specs=pl.BlockSpec((1, grp, tq, LANES), lambda b, h, n: (b, h, n, 0)),
        out_shape=jax.ShapeDtypeStruct((bsz, hq, t_len, LANES), BF16),
        compiler_params=_cparams(("parallel", "parallel", "arbitrary")),
        name="swa",
    )(sink_rows, q, k, v)


def _gdn_prep_kernel(cur_ref, prev_ref, next_ref, misc_ref, cw_ref, alog_ref, dtb_ref,
                     qkv_ref, lab_ref, *, nbl, nb):
    i = pl.program_id(1)
    first = jnp.logical_or(i == 0, i == nbl)
    last = jnp.logical_or(i == nbl - 1, i == nb - 1)
    cur = cur_ref[0]
    prev = jnp.where(first, 0.0, prev_ref[0])
    nxt = jnp.where(last, 0.0, next_ref[0])
    xe = jnp.concatenate([prev, cur, nxt], axis=0)
    cw = cw_ref[...]
    acc = jnp.zeros(cur.shape, F32)
    for j in range(GDN_CONV):
        off = 8 + j - (GDN_CONV - 1) // 2
        acc = acc + xe[off:off + TM, :] * cw[j:j + 1, :]
    y = _silu(acc)
    lane = _lane((TM, LANES))
    lo = lane < GDN_DK
    for sl in range(6):
        t = y[:, sl * LANES:(sl + 1) * LANES]
        if sl < 4:
            t2 = t * t
            ss0 = jnp.sum(jnp.where(lo, t2, 0.0), axis=-1, keepdims=True)
            ss1 = jnp.sum(jnp.where(lo, 0.0, t2), axis=-1, keepdims=True)
            t = t * jnp.where(lo, lax.rsqrt(ss0 + NORM_EPS), lax.rsqrt(ss1 + NORM_EPS))
            if sl < 2:
                t = t * GDN_DK ** -0.5
        qkv_ref[0, :, sl * LANES:(sl + 1) * LANES] = t
    xm = misc_ref[0]
    za = xm + dtb_ref[...]
    softplus = jnp.maximum(za, 0.0) + jnp.log(1.0 + jnp.exp(-jnp.abs(za)))
    log_a = -jnp.exp(alog_ref[...]) * softplus
    beta = 1.0 / (1.0 + jnp.exp(-xm))
    is_a = (lane % 8) < 4
    vals = jnp.where(is_a, log_a, beta)
    lab_ref[0, 0] = pltpu.roll(vals, LANES - 32, 1)
    lab_ref[0, 1] = pltpu.roll(vals, LANES - 40, 1)


def _gdn_prep(ga, misc, cw, alog_row, dtb_row, nbl):
    bsz, s, _ = ga.shape
    nb = s // TM
    r8 = TM // 8
    n8 = s // 8
    return pl.pallas_call(
        functools.partial(_gdn_prep_kernel, nbl=nbl, nb=nb),
        grid=(bsz, nb),
        in_specs=[pl.BlockSpec((1, TM, 768), lambda b, i: (b, i, 0)),
                  pl.BlockSpec((1, 8, 768), lambda b, i: (b, jnp.maximum(i * r8 - 1, 0), 0)),
                  pl.BlockSpec((1, 8, 768), lambda b, i: (b, jnp.minimum((i + 1) * r8, n8 - 1), 0)),
                  pl.BlockSpec((1, TM, LANES), lambda b, i: (b, i, 0)),
                  pl.BlockSpec(cw.shape, lambda b, i: (0, 0)),
                  pl.BlockSpec((1, LANES), lambda b, i: (0, 0)),
                  pl.BlockSpec((1, LANES), lambda b, i: (0, 0))],
        out_specs=[pl.BlockSpec((1, TM, 768), lambda b, i: (b, i, 0)),
                   pl.BlockSpec((1, 2, TM, LANES), lambda b, i: (b, 0, i, 0))],
        out_shape=[jax.ShapeDtypeStruct((bsz, s, 768), F32),
                   jax.ShapeDtypeStruct((bsz, 2, s, LANES), F32)],
        compiler_params=_cparams(("parallel", "parallel")),
        name="gdn_prep",
    )(ga, ga, ga, misc, cw, alog_row, dtb_row)


def _gdn_scan_kernel(qkv_ref, lab_ref, labt_ref, o_ref, s_sc, *, reverse):
    c = GDN_CHUNK
    c2 = 2 * c
    i = pl.program_id(1)

    @pl.when(i == 0)
    def _():
        s_sc[...] = jnp.zeros(s_sc.shape, F32)

    ri = lax.broadcasted_iota(jnp.int32, (c2, c2), 0)
    ci = lax.broadcasted_iota(jnp.int32, (c2, c2), 1)
    same = (ri // c) == (ci // c)
    if reverse:
        incl = jnp.logical_and(same, ri <= ci)
        strict = jnp.logical_and(same, ri < ci)
    else:
        incl = jnp.logical_and(same, ri >= ci)
        strict = jnp.logical_and(same, ri > ci)
    eye = (ri == ci).astype(F32)
    r1 = lax.broadcasted_iota(jnp.int32, (c, c), 0)
    c1 = lax.broadcasted_iota(jnp.int32, (c, c), 1)
    cum_col = ((r1 <= c1) if reverse else (r1 >= c1)).astype(BF16)
    rr = lax.broadcasted_iota(jnp.int32, (c, c2), 0)
    cc = lax.broadcasted_iota(jnp.int32, (c, c2), 1) % c
    cum_row = ((rr >= cc) if reverse else (rr <= cc)).astype(BF16)
    lane = _lane((c, LANES))
    lo = lane < GDN_DK
    lane2 = _lane((1, c2))
    lo2 = lane2 < c
    g_last_row = 0 if reverse else c - 1

    def stack(x):
        return jnp.concatenate([jnp.where(lo, x, 0.0), jnp.where(lo, 0.0, x)], axis=0)

    def fold(mat):
        return mat[0:c, :] + mat[c:c2, :]

    order = list(range(TM // c - 1, -1, -1) if reverse else range(TM // c))
    inst = []
    for ch in order:
        rows = slice(ch * c, (ch + 1) * c)
        lab = lab_ref[0, 0, rows, :]
        g_cols = _dot_exact_lhs(cum_col, lab)
        la_rows = labt_ref[0, 0, :, rows]
        g_rows = _dot_exact_rhs(la_rows, cum_row)
        for pr in range(2):
            h0, h1 = 2 * pr, 2 * pr + 1
            q = qkv_ref[0, rows, pr * LANES:(pr + 1) * LANES]
            k = qkv_ref[0, rows, (2 + pr) * LANES:(3 + pr) * LANES]
            v = qkv_ref[0, rows, (4 + pr) * LANES:(5 + pr) * LANES]
            beta = jnp.where(lo, lab[:, 4 + h0:5 + h0], lab[:, 4 + h1:5 + h1])
            g = jnp.where(lo, g_cols[:, h0:h0 + 1], g_cols[:, h1:h1 + 1])
            g_st_col = jnp.concatenate([g_cols[:, h0:h0 + 1], g_cols[:, h1:h1 + 1]], axis=0)
            g_st_row = jnp.where(lo2, g_rows[h0:h0 + 1, :], g_rows[h1:h1 + 1, :])
            diff = g_st_col - g_st_row
            dec = jnp.where(incl, jnp.exp(jnp.where(incl, diff, 0.0)), 0.0)
            kb = k * beta
            eg = jnp.exp(g)
            k_st = stack(k).astype(BF16)
            a = jnp.where(strict, _dot_nt(stack(kb).astype(BF16), k_st) * dec, 0.0)
            qk = _dot_nt(stack(q).astype(BF16), k_st) * dec
            gl = g[g_last_row:g_last_row + 1, :]
            gl_col = jnp.concatenate([jnp.broadcast_to(g_cols[g_last_row:g_last_row + 1, h0:h0 + 1], (c, 1)),
                                      jnp.broadcast_to(g_cols[g_last_row:g_last_row + 1, h1:h1 + 1], (c, 1))],
                                     axis=0)
            inst.append(dict(rows=rows, pr=pr, a=a, vb=stack(v * beta).astype(BF16),
                             kg=stack(kb * eg).astype(BF16), qg=(q * eg).astype(BF16),
                             qkf=fold(qk).astype(BF16), kd=(k * jnp.exp(gl - g)).astype(BF16),
                             sdec=jnp.exp(gl_col)))
    tinv = [eye - it['a'] for it in inst]
    pw = [it['a'] for it in inst]
    for _ in range(5):
        pw = [_dot_hp(x, x) for x in pw]
        tinv = [t + _dot_hp(t, x) for t, x in zip(tinv, pw)]
    for it, t in zip(inst, tinv):
        tf = fold(t).astype(BF16)
        it['u'] = _dot(tf, it['vb'])
        it['w'] = _dot(tf, it['kg']).astype(BF16)
    state = [s_sc[0], s_sc[1]]
    for it in inst:
        st = state[it['pr']]
        st_b = st.astype(BF16)
        v_new = it['u'] - _dot(it['w'], st_b)
        o = _dot(it['qg'], st_b) + _dot(it['qkf'], stack(v_new).astype(BF16))
        o_ref[0, it['rows'], it['pr'] * LANES:(it['pr'] + 1) * LANES] = o
        upd = _dot_tn(it['kd'], v_new.astype(BF16))
        state[it['pr']] = st * it['sdec'] + jnp.where(same, upd, 0.0)
    s_sc[0] = state[0]
    s_sc[1] = state[1]


def _gdn_scan(qkv, lab, labt, *, reverse, nbl):
    bsz, s, _ = qkv.shape
    nb = s // TM
    nbc = nb - nbl
    dsel = 1 if reverse else 0

    def blk(i):
        if reverse:
            return jnp.where(i < nbc, nb - 1 - i, nbl - 1 - (i - nbc))
        return jnp.where(i < nbc, nbl + i, i - nbc)

    return pl.pallas_call(
        functools.partial(_gdn_scan_kernel, reverse=reverse),
        grid=(bsz, nb),
        in_specs=[pl.BlockSpec((1, TM, 768), lambda b, i: (b, blk(i), 0)),
                  pl.BlockSpec((1, 1, TM, LANES), lambda b, i: (b, dsel, blk(i), 0)),
                  pl.BlockSpec((1, 1, 8, TM), lambda b, i: (b, dsel, 0, blk(i)))],
        out_specs=pl.BlockSpec((1, TM, 2 * LANES), lambda b, i: (b, blk(i), 0)),
        out_shape=jax.ShapeDtypeStruct((bsz, s, 2 * LANES), F32),
        scratch_shapes=[pltpu.VMEM((2, LANES, LANES), F32)],
        compiler_params=_cparams(("parallel", "arbitrary")),
        name="gdn_scan_bwd" if reverse else "gdn_scan_fwd",
    )(qkv, lab, labt)


def _outproj_kernel(x_ref, mod_ref, of_ref, ob_ref, z_ref, gg_ref, mo_ref, so_ref, go_ref,
                    mc_ref, sc_ref, gc_ref, wout_ref, g2_ref, rw_ref, rb_ref,
                    x1_ref, h2_ref, idx_ref, gate_ref, rank_ref, base_ref, end_ref, run_sc, *, nbl):
    d = D_MODEL
    first = jnp.logical_and(pl.program_id(0) == 0, pl.program_id(1) == 0)
    is_ctx = pl.program_id(1) >= nbl

    @pl.when(first)
    def _():
        run_sc[...] = jnp.zeros(run_sc.shape, F32)

    m = mod_ref[0, 0]
    lane = _lane((TM, LANES))
    lo = lane < HEAD_DIM
    o = of_ref[0] + ob_ref[0]
    z = z_ref[0]
    mix = []
    for pr in range(2):
        t = o[:, pr * LANES:(pr + 1) * LANES]
        t2 = t * t
        ms0 = jnp.sum(jnp.where(lo, t2, 0.0), axis=-1, keepdims=True) / HEAD_DIM
        ms1 = jnp.sum(jnp.where(lo, 0.0, t2), axis=-1, keepdims=True) / HEAD_DIM
        t = t * jnp.where(lo, lax.rsqrt(ms0 + NORM_EPS), lax.rsqrt(ms1 + NORM_EPS)) * gg_ref[...]
        t = t * _silu(z[:, pr * LANES:(pr + 1) * LANES])
        mix.append(t.astype(BF16))
    for lat_ref, ctx_ref in ((mo_ref, mc_ref), (so_ref, sc_ref), (go_ref, gc_ref)):
        for hh in range(4):
            mix.append(jnp.where(is_ctx, ctx_ref[0, hh], lat_ref[0, hh]))
    proj = _dot(jnp.concatenate(mix, axis=1), wout_ref[...])
    x1 = x_ref[0] + m[:, 2 * d:3 * d] * proj
    x1_ref[0] = x1
    xn = x1 * lax.rsqrt(jnp.mean(x1 * x1, axis=-1, keepdims=True) + NORM_EPS) * g2_ref[...]
    h2 = xn * (1.0 + m[:, 4 * d:5 * d]) + m[:, 3 * d:4 * d]
    h2_ref[0] = h2.astype(BF16)

    logits = _dot_hp(h2, rw_ref[...])
    scores = 1.0 / (1.0 + jnp.exp(-logits))
    valid = lane < N_EXPERTS
    sel = jnp.where(valid, scores + rb_ref[...], -jnp.inf)
    member = jnp.zeros((TM, LANES), F32)
    picks = []
    for _ in range(TOP_K):
        mx = jnp.max(sel, axis=-1, keepdims=True)
        idx = jnp.min(jnp.where(sel == mx, lane, LANES), axis=-1, keepdims=True)
        hit = lane == idx
        gate = jnp.sum(jnp.where(hit, scores, 0.0), axis=-1, keepdims=True)
        sel = jnp.where(hit, -jnp.inf, sel)
        member = member + hit.astype(F32)
        picks.append((idx, hit, gate))
    gsum = picks[0][2]
    for kk in range(1, TOP_K):
        gsum = gsum + picks[kk][2]
    ri = lax.broadcasted_iota(jnp.int32, (TM, TM), 0)
    ci = lax.broadcasted_iota(jnp.int32, (TM, TM), 1)
    before = (ri > ci).astype(BF16)
    run = run_sc[...]
    base_ref[0] = run
    cum = _dot(before, member.astype(BF16)) + run
    idx_out = jnp.full((TM, LANES), -1, jnp.int32)
    gate_out = jnp.zeros((TM, LANES), F32)
    rank_out = jnp.zeros((TM, LANES), F32)
    for kk, (idx, hit, gate) in enumerate(picks):
        rank = jnp.sum(jnp.where(hit, cum, 0.0), axis=-1, keepdims=True)
        here = lane == kk
        idx_out = jnp.where(here, idx, idx_out)
        gate_out = jnp.where(here, gate / gsum * ROUTED_SCALE, gate_out)
        rank_out = jnp.where(here, rank, rank_out)
    idx_ref[0] = idx_out
    gate_ref[0] = gate_out
    rank_ref[0] = rank_out
    run_end = run + jnp.sum(member, axis=0, keepdims=True)
    end_ref[0] = run_end
    run_sc[...] = run_end


def _outproj(x, mod_tab, o_f, o_b, ga, gg_row, lat, ctx, wout, g2, rw, rb, *, n_blk, nbl):
    bsz, _, d = x.shape
    s_out = n_blk * TM
    full = lambda a: pl.BlockSpec(a.shape, lambda b, i: (0,) * a.ndim)
    tok = lambda w: pl.BlockSpec((1, TM, w), lambda b, i: (b, i, 0))
    head = pl.BlockSpec((1, 4, TM, LANES), lambda b, i: (b, 0, jnp.minimum(i, nbl - 1), 0))
    head_c = pl.BlockSpec((1, 4, TM, LANES), lambda b, i: (b, 0, jnp.maximum(i - nbl, 0), 0))
    return pl.pallas_call(
        functools.partial(_outproj_kernel, nbl=nbl),
        grid=(bsz, n_blk),
        in_specs=[tok(d),
                  pl.BlockSpec((1, 1, 1, 6 * d), lambda b, i: (b, i // nbl, 0, 0)),
                  tok(2 * LANES), tok(2 * LANES),
                  pl.BlockSpec((1, TM, 2 * LANES), lambda b, i: (b, i, 3)),
                  full(gg_row), head, head, head, head_c, head_c, head_c,
                  full(wout), full(g2), full(rw), full(rb)],
        out_specs=[tok(d), tok(d), tok(LANES), tok(LANES), tok(LANES),
                   pl.BlockSpec((1, 1, LANES), lambda b, i: (b * n_blk + i, 0, 0)),
                   pl.BlockSpec((1, 1, LANES), lambda b, i: (b * n_blk + i, 0, 0))],
        out_shape=[jax.ShapeDtypeStruct((bsz, s_out, d), F32),
                   jax.ShapeDtypeStruct((bsz, s_out, d), BF16),
                   jax.ShapeDtypeStruct((bsz, s_out, LANES), jnp.int32),
                   jax.ShapeDtypeStruct((bsz, s_out, LANES), F32),
                   jax.ShapeDtypeStruct((bsz, s_out, LANES), F32),
                   jax.ShapeDtypeStruct((bsz * n_blk, 1, LANES), F32),
                   jax.ShapeDtypeStruct((bsz * n_blk, 1, LANES), F32)],
        scratch_shapes=[pltpu.VMEM((1, LANES), F32)],
        compiler_params=_cparams(("arbitrary", "arbitrary")),
        name="outproj_router",
    )(x, mod_tab, o_f, o_b, ga, gg_row, *lat, *ctx, wout, g2, rw, rb)


def _moe_tables_kernel(lo_ref, hi_ref, offl_ref, seg_ref, gbase_ref, be_ref, nused_ref):
    ncp = lo_ref.shape[0]
    nblkp = be_ref.shape[0]
    seg = jnp.ceil((hi_ref[...] - lo_ref[...]) * (1.0 / MOE_SEG)) * MOE_SEG
    ri = lax.broadcasted_iota(jnp.int32, (LANES, LANES), 0)
    ci = lax.broadcasted_iota(jnp.int32, (LANES, LANES), 1)
    before_lane = (ri < ci).astype(BF16)
    rc = lax.broadcasted_iota(jnp.int32, (ncp, ncp), 0)
    cc = lax.broadcasted_iota(jnp.int32, (ncp, ncp), 1)
    before_row = (rc > cc).astype(BF16)
    seg_ref[...] = seg
    offl_ref[...] = _dot_exact_rhs(seg, before_lane)
    region = jnp.ceil(jnp.sum(seg, axis=0, keepdims=True) * (1.0 / MOE_BR)) * MOE_BR
    goff = _dot_exact_rhs(jnp.broadcast_to(region, (8, LANES)), before_lane)[0:1]
    gbase_ref[...] = goff + _dot_exact_lhs(before_row, seg)
    lane = _lane((nblkp, LANES))
    gend = jnp.where(lane < N_EXPERTS, goff + region, jnp.inf)
    row0 = (lax.broadcasted_iota(jnp.int32, (nblkp, LANES), 0) * MOE_BR).astype(F32)
    be = jnp.sum((gend <= row0).astype(F32), axis=-1, keepdims=True)
    be_ref[...] = jnp.broadcast_to(jnp.minimum(be, N_EXPERTS - 1.0), (nblkp, LANES))
    nused_ref[...] = jnp.broadcast_to(jnp.sum(region, axis=-1, keepdims=True) * (1.0 / MOE_BR), (1, LANES))


def _moe_tables(cb_lo, cb_hi, nblk):
    ncp = cb_lo.shape[0]
    nblkp = -(-nblk // 8) * 8
    tab = jax.ShapeDtypeStruct((ncp, LANES), F32)
    return pl.pallas_call(
        _moe_tables_kernel,
        out_shape=[tab, tab, tab, jax.ShapeDtypeStruct((nblkp, LANES), F32),
                   jax.ShapeDtypeStruct((1, LANES), F32)],
        name="moe_tables",
    )(cb_lo, cb_hi)


def _segment_copies(c, offl_s, seg_s, gbase_s, local_ref, glob_ref, sem, *, to_global, wait):
    def piece(lo, go, size):
        lo = pl.multiple_of(lo, MOE_SEG)
        go = pl.multiple_of(go, MOE_SEG)
        loc = local_ref.at[pl.ds(lo, size)]
        glo = glob_ref.at[pl.ds(go, size)]
        cp = pltpu.make_async_copy(loc, glo, sem) if to_global else pltpu.make_async_copy(glo, loc, sem)
        if wait:
            cp.wait()
        else:
            cp.start()

    def expert_body(e, carry):
        t = c * N_EXPERTS + e
        off, sg, gb = offl_s[t], seg_s[t], gbase_s[t]
        n_full = sg // MOE_ROWS

        def full_body(w, carry2):
            piece(off + w * MOE_ROWS, gb + w * MOE_ROWS, MOE_ROWS)
            return carry2

        lax.fori_loop(0, n_full, full_body, 0)
        done = n_full * MOE_ROWS
        size = MOE_ROWS // 2
        while size >= MOE_SEG:
            bit = sg & size

            @pl.when(bit != 0)
            def _(done=done, size=size):
                piece(off + done, gb + done, size)

            done = done + bit
            size //= 2
        return carry

    lax.fori_loop(0, N_EXPERTS, expert_body, 0)


def _moe_sort_kernel(offl_s, seg_s, gbase_s, h_ref, idx_ref, rank_ref, cb_ref, offv_ref, xs_in_ref,
                     pos_ref, xs_ref, xy_sc, sem):
    del xs_in_ref
    c = pl.program_id(0)
    tc = h_ref.shape[0]
    lane_t = _lane((tc, LANES))
    h = h_ref[...]
    idx = idx_ref[...]
    rank = rank_ref[...]
    adj = offv_ref[0] - cb_ref[0]
    pos = jnp.full((tc, LANES), -1.0, F32)
    for k in range(TOP_K):
        hit = lane_t == idx[:, k:k + 1]
        p_k = rank[:, k:k + 1] + jnp.sum(jnp.where(hit, adj, 0.0), axis=-1, keepdims=True)
        pos = jnp.where(lane_t == k, p_k, pos)
    pos_ref[...] = pos
    sel8 = (lax.broadcasted_iota(jnp.int32, (8, LANES), 0)
            == lax.broadcasted_iota(jnp.int32, (8, LANES), 1)).astype(BF16)
    hi, mid, lo = _split3(pos)
    pos_row = _dot_nt(sel8, hi) + _dot_nt(sel8, mid) + _dot_nt(sel8, lo)

    def gather_body(rb, carry):
        r0 = pl.multiple_of(rb * MOE_ROWS, MOE_ROWS)
        rid = (lax.broadcasted_iota(jnp.int32, (MOE_ROWS, tc), 0) + r0).astype(F32)
        onehot = jnp.zeros((MOE_ROWS, tc), F32)
        for k in range(TOP_K):
            onehot = jnp.where(pos_row[k:k + 1, :] == rid, 1.0, onehot)
        xy_sc[pl.ds(r0, MOE_ROWS), :] = _dot(onehot.astype(BF16), h).astype(BF16)
        return carry

    last = c * N_EXPERTS + N_EXPERTS - 1
    total = offl_s[last] + seg_s[last]
    lax.fori_loop(0, (total + MOE_ROWS - 1) // MOE_ROWS, gather_body, 0)
    _segment_copies(c, offl_s, seg_s, gbase_s, xy_sc, xs_ref, sem, to_global=True, wait=False)
    _segment_copies(c, offl_s, seg_s, gbase_s, xy_sc, xs_ref, sem, to_global=True, wait=True)


def _moe_ffn_kernel(be_s, nused_s, x_ref, wgu_ref, wdn_ref, y_ref, wgu_sc, wdn_sc):
    i = pl.program_id(0)
    used = i < nused_s[0]
    changed = jnp.logical_or(i == 0, be_s[i] != be_s[jnp.maximum(i - 1, 0)])

    @pl.when(jnp.logical_and(used, changed))
    def _():
        wgu_sc[...] = wgu_ref[0].astype(BF16)
        wdn_sc[...] = wdn_ref[0].astype(BF16)

    @pl.when(used)
    def _():
        gu = _dot(x_ref[...], wgu_sc[...])
        act = _silu(gu[:, :D_EXPERT]) * gu[:, D_EXPERT:]
        y_ref[...] = _dot(act.astype(BF16), wdn_sc[...]).astype(BF16)


def _moe_combine_kernel(offl_s, seg_s, gbase_s, h_ref, pos_ref, gate_ref, sgu_ref, sdn_ref, ys_ref,
                        y_ref, yl_sc, sem):
    c = pl.program_id(0)
    tc = h_ref.shape[0]
    n_rows = yl_sc.shape[0]
    yl_sc[...] = jnp.zeros(yl_sc.shape, BF16)
    _segment_copies(c, offl_s, seg_s, gbase_s, yl_sc, ys_ref, sem, to_global=False, wait=False)
    gu = _dot(h_ref[...], sgu_ref[...])
    act = _silu(gu[:, :D_EXPERT]) * gu[:, D_EXPERT:]
    y_ref[...] = _dot(act.astype(BF16), sdn_ref[...])
    _segment_copies(c, offl_s, seg_s, gbase_s, yl_sc, ys_ref, sem, to_global=False, wait=True)
    pos = pos_ref[...]
    gate = gate_ref[...]
    kblk = n_rows // MOE_KSPLIT
    for tb in range(tc // TM):
        pos_t = pos[tb * TM:(tb + 1) * TM]
        gate_t = gate[tb * TM:(tb + 1) * TM]
        out = y_ref[tb * TM:(tb + 1) * TM, :]
        for kb in range(MOE_KSPLIT):
            cid = (lax.broadcasted_iota(jnp.int32, (TM, kblk), 1) + kb * kblk).astype(F32)
            scat = jnp.zeros((TM, kblk), F32)
            for k in range(TOP_K):
                scat = jnp.where(pos_t[:, k:k + 1] == cid, gate_t[:, k:k + 1], scat)
            out = out + _dot(scat.astype(BF16), yl_sc[kb * kblk:(kb + 1) * kblk, :])
        y_ref[tb * TM:(tb + 1) * TM, :] = out


def _moe(h2, idx, rank, gate, base, end, wgu, wdn, sgu, sdn):
    n_tok, d = h2.shape
    tc = MOE_CHUNK
    nc = n_tok // tc
    bpc = tc // TM
    ne = wgu.shape[0]
    assert ne == N_EXPERTS
    ncp = -(-nc // LANES) * LANES
    n_loc = -(-(tc * TOP_K + ne * MOE_SEG) // (MOE_ROWS * MOE_KSPLIT)) * (MOE_ROWS * MOE_KSPLIT)
    nblk = -(-(n_tok * TOP_K + nc * ne * (MOE_SEG - 1) + ne * (MOE_BR - 1)) // MOE_BR)
    pad = lambda a: jnp.concatenate([a, jnp.zeros((ncp - nc, LANES), F32)], axis=0)
    cb_lo = pad(base[0::bpc, 0, :])
    cb_hi = pad(end[bpc - 1::bpc, 0, :])
    offl, seg, gbase, be, nused = _moe_tables(cb_lo, cb_hi, nblk)
    to_smem = lambda a: a[:nc, :ne].astype(jnp.int32).reshape(nc * ne)
    tabs = (to_smem(offl), to_smem(seg), to_smem(gbase))
    be_i = be[:nblk, 0].astype(jnp.int32)
    nused_i = nused[0, :1].astype(jnp.int32)
    chunk = lambda w: pl.BlockSpec((tc, w), lambda c, *_: (c, 0))
    row = pl.BlockSpec((1, 1, LANES), lambda c, *_: (c, 0, 0))
    anyspec = pl.BlockSpec(memory_space=pl.ANY)
    const = lambda a: pl.BlockSpec(a.shape, lambda c, *_: (0,) * a.ndim)

    pos, xs = pl.pallas_call(
        _moe_sort_kernel,
        grid_spec=pltpu.PrefetchScalarGridSpec(
            num_scalar_prefetch=3, grid=(nc,),
            in_specs=[chunk(d), chunk(LANES), chunk(LANES), row, row, anyspec],
            out_specs=[chunk(LANES), anyspec],
            scratch_shapes=[pltpu.VMEM((n_loc, d), BF16), pltpu.SemaphoreType.DMA]),
        out_shape=[jax.ShapeDtypeStruct((n_tok, LANES), F32),
                   jax.ShapeDtypeStruct((nblk * MOE_BR, d), BF16)],
        input_output_aliases={8: 1},
        compiler_params=_cparams(("arbitrary",)),
        name="moe_sort",
    )(*tabs, h2, idx, rank, cb_lo[:nc, None, :], offl[:nc, None, :], jnp.zeros((nblk * MOE_BR, d), BF16))

    last = lambda i, be_s, nu_s: jnp.minimum(i, nu_s[0] - 1)
    ys = pl.pallas_call(
        _moe_ffn_kernel,
        grid_spec=pltpu.PrefetchScalarGridSpec(
            num_scalar_prefetch=2, grid=(nblk,),
            in_specs=[pl.BlockSpec((MOE_BR, d), lambda i, be_s, nu_s: (last(i, be_s, nu_s), 0)),
                      pl.BlockSpec((1, d, 2 * D_EXPERT), lambda i, be_s, nu_s: (be_s[last(i, be_s, nu_s)], 0, 0)),
                      pl.BlockSpec((1, D_EXPERT, d), lambda i, be_s, nu_s: (be_s[last(i, be_s, nu_s)], 0, 0))],
            out_specs=pl.BlockSpec((MOE_BR, d), lambda i, be_s, nu_s: (last(i, be_s, nu_s), 0)),
            scratch_shapes=[pltpu.VMEM((d, 2 * D_EXPERT), BF16), pltpu.VMEM((D_EXPERT, d), BF16)]),
        out_shape=jax.ShapeDtypeStruct((nblk * MOE_BR, d), BF16),
        input_output_aliases={2: 0},
        compiler_params=_cparams(("arbitrary",)),
        name="moe_ffn",
    )(be_i, nused_i, xs, wgu, wdn)

    return pl.pallas_call(
        _moe_combine_kernel,
        grid_spec=pltpu.PrefetchScalarGridSpec(
            num_scalar_prefetch=3, grid=(nc,),
            in_specs=[chunk(d), chunk(LANES), chunk(LANES), const(sgu), const(sdn), anyspec],
            out_specs=chunk(d),
            scratch_shapes=[pltpu.VMEM((n_loc, d), BF16), pltpu.SemaphoreType.DMA]),
        out_shape=jax.ShapeDtypeStruct((n_tok, d), F32),
        compiler_params=_cparams(("arbitrary",)),
        name="moe_combine",
    )(*tabs, h2, pos, gate, sgu, sdn, ys)


def _residual_kernel(x_ref, y_ref, mod_ref, o_ref):
    d = D_MODEL
    o_ref[0] = x_ref[0] + mod_ref[0, 0][:, 5 * d:6 * d] * y_ref[0]


def _residual(x1, y, mod_tab, *, nbl):
    bsz, s, d = x1.shape
    tok = pl.BlockSpec((1, TM, d), lambda b, i: (b, i, 0))
    return pl.pallas_call(
        _residual_kernel,
        grid=(bsz, s // TM),
        in_specs=[tok, tok, pl.BlockSpec((1, 1, 1, 6 * d), lambda b, i: (b, i // nbl, 0, 0))],
        out_specs=tok,
        out_shape=jax.ShapeDtypeStruct((bsz, s, d), F32),
        compiler_params=_cparams(("parallel", "parallel")),
        name="residual",
    )(x1, y, mod_tab)


def _axial_tables(n_rows, rot_dim):
    rows = jnp.repeat(jnp.arange(n_rows), GRID_W).astype(F32)
    cols = jnp.tile(jnp.arange(GRID_W), n_rows).astype(F32)
    axis_dim = rot_dim // 2
    inv_freq = ROPE_THETA ** (-jnp.arange(0, axis_dim, 2, dtype=F32) / axis_dim)
    ang_r = rows[:, None] * inv_freq
    ang_c = cols[:, None] * inv_freq
    ang = jnp.concatenate([ang_r, ang_r, ang_c, ang_c], axis=-1)
    return jnp.cos(ang), jnp.sin(ang)


def _rope_slab_tables(t_len, n_ctx, rot_dim, lane0):
    cos, sin = _axial_tables(t_len // GRID_W, rot_dim)
    half = rot_dim // 4
    sign = jnp.where((jnp.arange(rot_dim) % (2 * half)) < half, -1.0, 1.0).astype(F32)
    cos_t = jnp.ones((t_len + n_ctx, LANES), F32).at[:t_len, lane0:lane0 + rot_dim].set(cos)
    sin_t = jnp.zeros((t_len + n_ctx, LANES), F32).at[:t_len, lane0:lane0 + rot_dim].set(sin * sign)
    return cos_t, sin_t


def _slab_cols(starts, width):
    out = []
    for st in starts:
        out += list(range(st, st + width)) + [-1] * (LANES - width)
    return out


def _gather_cols(w, cols):
    w_ext = jnp.concatenate([w, jnp.zeros((w.shape[0], 1), w.dtype)], axis=1)
    idx = np.array([c if c >= 0 else w.shape[1] for c in cols], np.int32)
    return w_ext[:, idx]


def _pad_row(v, lane0=0):
    return jnp.zeros((1, LANES), F32).at[0, lane0:lane0 + v.shape[0]].set(v.astype(F32))


def _layer_weights(l, w_in, gdn_conv_w, gdn_a_log, gdn_dt_bias, gdn_norm_g, mla_qn_g, mla_kvn_g,
                   mla_w_uq, mla_w_ukv, mla_qk_g, swa_qk_g, swa_sink, gqa_qk_g, w_out, router_w,
                   router_bias):
    hd = HEAD_DIM
    cols = list(range(0, 1024))
    cols += list(range(O_MQ, O_MQ + MLA_Q_RANK)) + list(range(O_MKV, O_MKV + MLA_KV_RANK))
    misc = list(range(O_MKR, O_MKR + MLA_ROPE))
    misc += list(range(O_AA, O_AA + 4)) + list(range(O_AB, O_AB + 4))
    misc += list(range(O_AA + 4, O_AA + 8)) + list(range(O_AB + 4, O_AB + 8))
    cols += misc + [-1] * (LANES - len(misc))
    cols += _slab_cols([O_SQ + hd * i for i in range(4)] + [O_SK + hd * i for i in range(2)]
                       + [O_SV + hd * i for i in range(2)], hd)
    cols += _slab_cols([O_GQ + hd * i for i in range(4)] + [O_GK + hd * i for i in range(2)]
                       + [O_GV + hd * i for i in range(2)], hd)
    assert len(cols) == N_COL
    qd = MLA_NOPE + MLA_ROPE
    uq_cols = []
    for hh in range(MLA_HEADS):
        uq_cols += list(range(hh * qd, hh * qd + qd)) + [-1] * (LANES - qd)
    kvd = MLA_NOPE + MLA_V
    ukv_cols = _slab_cols([hh * kvd for hh in range(MLA_HEADS)], MLA_NOPE)
    ukv_cols += _slab_cols([hh * kvd + MLA_NOPE for hh in range(MLA_HEADS)], MLA_V)
    orow = list(range(0, 256)) + _slab_cols([256 + hd * i for i in range(12)], hd)
    alog = jnp.zeros((1, LANES), F32)
    dtb = jnp.zeros((1, LANES), F32)
    for dd in range(2):
        alog = alog.at[0, 32 + 8 * dd:36 + 8 * dd].set(gdn_a_log[l, dd])
        dtb = dtb.at[0, 32 + 8 * dd:36 + 8 * dd].set(gdn_dt_bias[l, dd])
    return dict(
        win=_gather_cols(w_in[l], cols).astype(BF16),
        conv=gdn_conv_w[l][:, :768],
        alog=alog, dtb=dtb,
        gdn_g=jnp.concatenate([gdn_norm_g[l], gdn_norm_g[l]]).reshape(1, LANES),
        mla_qn_g=mla_qn_g[l], mla_kvn_g=mla_kvn_g[l],
        wuq=_gather_cols(mla_w_uq[l], uq_cols).astype(BF16),
        wukv=_gather_cols(mla_w_ukv[l], ukv_cols).astype(BF16),
        mqg=_pad_row(mla_qk_g[l, 0]), mkg=_pad_row(mla_qk_g[l, 1, :MLA_NOPE]),
        krg=_pad_row(mla_qk_g[l, 1, MLA_NOPE:]),
        sqg=_pad_row(swa_qk_g[l, 0]), skg=_pad_row(swa_qk_g[l, 1]),
        gqg=_pad_row(gqa_qk_g[l, 0]), gkg=_pad_row(gqa_qk_g[l, 1]),
        wout=_gather_cols(w_out[l].T, orow).T.astype(BF16),
        rw=jnp.concatenate([router_w[l], jnp.zeros((D_MODEL, LANES - N_EXPERTS), F32)], axis=1),
        rb=_pad_row(router_bias[l]),
        sink=swa_sink[l],
    )


def _sink_rows(sink, grp, tq):
    hkv = sink.shape[0] // grp
    return jnp.repeat((sink.astype(F32) * LOG2E).reshape(hkv, grp), tq, axis=1).reshape(hkv, grp * tq, 1)


def kernel(x, c, ctx, c_ctx, w_mod, b_mod, norm1_g, norm2_g, w_in, gdn_conv_w, gdn_a_log, gdn_dt_bias,
           gdn_norm_g, mla_qn_g, mla_kvn_g, mla_w_uq, mla_w_ukv, mla_qk_g, swa_qk_g, swa_sink, gqa_qk_g,
           w_out, router_w, router_bias, exp_w_gu, exp_w_down, shared_w_gu, shared_w_down):
    bsz, t_len, d = x.shape
    n_ctx = ctx.shape[1]
    depth = w_mod.shape[0]
    s = t_len + n_ctx
    assert d == D_MODEL and t_len % TM == 0 and n_ctx % TM == 0 and t_len >= 3 * WINDOW
    assert bsz + 1 <= 8
    nbl = t_len // TM
    nb = s // TM

    tabs = (_rope_slab_tables(t_len, n_ctx, HEAD_DIM, 0) + _rope_slab_tables(t_len, n_ctx, MLA_ROPE, MLA_NOPE)
            + _rope_slab_tables(t_len, n_ctx, MLA_ROPE, 0))
    c8 = jnp.zeros((8, d), F32).at[:bsz].set(c).at[bsz].set(c_ctx)
    xs = jnp.concatenate([x, ctx], axis=1)

    tq_d = 512 if t_len % 512 == 0 else TM
    tk_d = 2816 if (s % 2816 == 0) else TM

    for l in range(depth):
        need_ctx = l < depth - 1
        lw = _layer_weights(l, w_in, gdn_conv_w, gdn_a_log, gdn_dt_bias, gdn_norm_g, mla_qn_g, mla_kvn_g,
                            mla_w_uq, mla_w_ukv, mla_qk_g, swa_qk_g, swa_sink, gqa_qk_g, w_out, router_w,
                            router_bias)
        mod = _modulation(c8, w_mod[l], b_mod[l])
        mod_tab = jnp.stack([mod[:bsz], jnp.broadcast_to(mod[bsz], (bsz, 6 * d))], axis=1)[:, :, None, :]

        (ga, misc, mq, mk, mv, sq, sk, sv, gq, gk, gv) = _inproj(
            xs, mod_tab, norm1_g[l].reshape(1, d), lw['win'], lw, tabs, nbl)

        qkv, lab = _gdn_prep(ga, misc, lw['conv'], lw['alog'], lw['dtb'], nbl)
        labt = jnp.swapaxes(lab[..., :8], 2, 3)
        o_f = _gdn_scan(qkv, lab, labt, reverse=False, nbl=nbl)
        o_b = _gdn_scan(qkv, lab, labt, reverse=True, nbl=nbl)

        mo = _flash(mq, mk, mv, grp=1, tq=2 * tq_d, tk=tk_d, q_blk0=0, nq=t_len // (2 * tq_d),
                    k_blk0=0, nk=s // tk_d)
        go = _flash(gq, gk, gv, grp=2, tq=tq_d, tk=tk_d, q_blk0=0, nq=t_len // tq_d, k_blk0=0, nk=s // tk_d)
        so = _swa(sq, sk, sv, _sink_rows(lw['sink'], 2, tq_d), grp=2, tq=tq_d, t_len=t_len, n_ctx=n_ctx)
        lat = (mo, so, go)
        ctx_out = lat
        if need_ctx:
            cb0 = t_len // n_ctx
            ctx_out = (
                _flash(mq, mk, mv, grp=1, tq=n_ctx, tk=n_ctx, q_blk0=cb0, nq=1, k_blk0=cb0, nk=1),
                _flash(sq, sk, sv, grp=2, tq=n_ctx, tk=n_ctx, q_blk0=cb0, nq=1, k_blk0=cb0, nk=1,
                       sink_rows=_sink_rows(lw['sink'], 2, n_ctx)),
                _flash(gq, gk, gv, grp=2, tq=n_ctx, tk=n_ctx, q_blk0=cb0, nq=1, k_blk0=cb0, nk=1))

        n_blk = nb if need_ctx else nbl
        x1, h2, idx, gate, rank, base, end = _outproj(
            xs, mod_tab, o_f, o_b, ga, lw['gdn_g'], lat, ctx_out, lw['wout'], norm2_g[l].reshape(1, d),
            lw['rw'], lw['rb'], n_blk=n_blk, nbl=nbl)

        n_tok = bsz * n_blk * TM
        assert n_tok % MOE_CHUNK == 0
        y = _moe(h2.reshape(n_tok, d), idx.reshape(n_tok, LANES), rank.reshape(n_tok, LANES),
                 gate.reshape(n_tok, LANES), base, end, exp_w_gu[l], exp_w_down[l],
                 shared_w_gu[l].astype(BF16), shared_w_down[l].astype(BF16))
        xs = _residual(x1, y.reshape(bsz, n_blk * TM, d), mod_tab, nbl=nbl)
    return xs[:, :t_len]
```

```python
import functools
import math

import numpy as np
import jax
import jax.numpy as jnp
from jax import lax
from jax.experimental import pallas as pl
from jax.experimental.pallas import tpu as pltpu

F32 = jnp.float32
BF16 = jnp.bfloat16

LANES = 128
TM = 256
VMEM_LIMIT = 56 * 1024 * 1024

D_MODEL = 1024
GRID_W = 64
HEAD_DIM = 64
ROPE_THETA = 10000.0
NORM_EPS = 1e-6
ATTN_SCALE = HEAD_DIM ** -0.5
GDN_HEADS = 4
GDN_DK = 64
GDN_CHUNK = 64
GDN_CONV = 5
MLA_HEADS = 4
MLA_Q_RANK = 256
MLA_KV_RANK = 128
MLA_NOPE = 64
MLA_ROPE = 32
MLA_V = 64
MLA_SCALE = (MLA_NOPE + MLA_ROPE) ** -0.5
SWA_HEADS = 4
SWA_KV_HEADS = 2
WINDOW = 128
GQA_HEADS = 4
GQA_KV_HEADS = 2
N_EXPERTS = 64
TOP_K = 6
D_EXPERT = 384
ROUTED_SCALE = 2.5
LOG2E = math.log2(math.e)
ONES_LANE = HEAD_DIM
FLASH_KB = 256
MOE_CHUNK = 512
MOE_ROWS = 128
MOE_BR = 256
MOE_SEG = 16
MOE_KSPLIT = 3

O_AQ, O_AK, O_AV, O_AZ, O_AA, O_AB = 0, 256, 512, 768, 1024, 1032
O_MQ, O_MKV, O_MKR = 1040, 1296, 1424
O_SQ, O_SK, O_SV = 1456, 1712, 1840
O_GQ, O_GK, O_GV = 1968, 2224, 2352
D_IN = 2480
C_GDN, C_MLA, C_MISC, C_SWA, C_GQA, N_COL = 0, 1024, 1408, 1536, 2560, 3584


def _cparams(sem):
    return pltpu.CompilerParams(dimension_semantics=sem, vmem_limit_bytes=VMEM_LIMIT)


def _dot(a, b):
    return jnp.dot(a, b, preferred_element_type=F32)


def _dot_nt(a, b):
    return lax.dot_general(a, b, (((1,), (1,)), ((), ())), preferred_element_type=F32)


def _dot_tn(a, b):
    return lax.dot_general(a, b, (((0,), (0,)), ((), ())), preferred_element_type=F32)


def _split3(x):
    hi = x.astype(BF16)
    r1 = x - hi.astype(F32)
    mid = r1.astype(BF16)
    lo = (r1 - mid.astype(F32)).astype(BF16)
    return hi, mid, lo


def _dot_exact_lhs(a_bf16, x):
    hi, mid, lo = _split3(x)
    return _dot(a_bf16, hi) + _dot(a_bf16, mid) + _dot(a_bf16, lo)


def _dot_exact_rhs(x, b_bf16):
    hi, mid, lo = _split3(x)
    return _dot(hi, b_bf16) + _dot(mid, b_bf16) + _dot(lo, b_bf16)


def _dot_hp(a, b):
    ah = a.astype(BF16)
    al = (a - ah.astype(F32)).astype(BF16)
    bh = b.astype(BF16)
    bl = (b - bh.astype(F32)).astype(BF16)
    return _dot(ah, bh) + _dot(ah, bl) + _dot(al, bh)


def _silu(x):
    return x * (1.0 / (1.0 + jnp.exp(-x)))


def _lane(shape):
    return lax.broadcasted_iota(jnp.int32, shape, len(shape) - 1)


def _mod_kernel(c_ref, w_ref, b_ref, o_ref):
    s = _silu(c_ref[...])
    o_ref[...] = _dot(s.astype(BF16), w_ref[...].astype(BF16)) + b_ref[...]


def _modulation(c8, w_mod_l, b_mod_l):
    d = c8.shape[1]
    n = w_mod_l.shape[1]
    return pl.pallas_call(
        _mod_kernel,
        grid=(n // d,),
        in_specs=[pl.BlockSpec((8, d), lambda j: (0, 0)),
                  pl.BlockSpec((d, d), lambda j: (0, j)),
                  pl.BlockSpec((1, d), lambda j: (0, j))],
        out_specs=pl.BlockSpec((8, d), lambda j: (0, j)),
        out_shape=jax.ShapeDtypeStruct((8, n), F32),
        compiler_params=_cparams(("arbitrary",)),
        name="modulation",
    )(c8, w_mod_l, b_mod_l.reshape(1, n))


def _rope_slab(x, cos, sin_signed, half):
    lane = _lane(x.shape)
    fwd = pltpu.roll(x, LANES - half, 1)
    bwd = pltpu.roll(x, half, 1)
    rot = jnp.where(lane % (2 * half) < half, fwd, bwd)
    return x * cos + rot * sin_signed


def _inproj_kernel(x_ref, mod_ref, g1_ref, win_ref, qn_g_ref, kvn_g_ref, wuq_ref, wukv_ref,
                   mqg_ref, mkg_ref, krg_ref, sqg_ref, skg_ref, gqg_ref, gkg_ref,
                   cos_hd_ref, sin_hd_ref, cos_mq_ref, sin_mq_ref, cos_kr_ref, sin_kr_ref,
                   ga_ref, misc_ref, mq_ref, mk_ref, mv_ref, sq_ref, sk_ref, sv_ref,
                   gq_ref, gk_ref, gv_ref):
    d = D_MODEL
    x = x_ref[0]
    m = mod_ref[0, 0]
    shift, scale = m[:, 0:d], m[:, d:2 * d]
    xn = x * lax.rsqrt(jnp.mean(x * x, axis=-1, keepdims=True) + NORM_EPS) * g1_ref[...]
    h = xn * (1.0 + scale) + shift
    p = _dot(h.astype(BF16), win_ref[...])

    ga_ref[0] = p[:, C_GDN:C_GDN + 1024]
    misc = p[:, C_MISC:C_MISC + LANES]
    misc_ref[0] = misc

    lane = _lane((TM, LANES))
    cos_hd, sin_hd = cos_hd_ref[...], sin_hd_ref[...]

    cq = p[:, C_MLA:C_MLA + MLA_Q_RANK]
    cqn = cq * lax.rsqrt(jnp.mean(cq * cq, axis=-1, keepdims=True) + NORM_EPS) * qn_g_ref[...]
    qup = _dot(cqn.astype(BF16), wuq_ref[...])
    ckv = p[:, C_MLA + MLA_Q_RANK:C_MLA + MLA_Q_RANK + MLA_KV_RANK]
    ckvn = ckv * lax.rsqrt(jnp.mean(ckv * ckv, axis=-1, keepdims=True) + NORM_EPS) * kvn_g_ref[...]
    kvup = _dot(ckvn.astype(BF16), wukv_ref[...])
    kr = jnp.where(lane < MLA_ROPE, misc, 0.0)
    kr = kr * lax.rsqrt(jnp.sum(kr * kr, axis=-1, keepdims=True) / MLA_ROPE + NORM_EPS) * krg_ref[...]
    kr = _rope_slab(kr, cos_kr_ref[...], sin_kr_ref[...], MLA_ROPE // 4)
    kpe = pltpu.roll(kr, MLA_NOPE, 1)
    is_nope = lane < MLA_NOPE
    for hh in range(MLA_HEADS):
        q = qup[:, hh * LANES:(hh + 1) * LANES]
        q2 = q * q
        ss_n = jnp.sum(jnp.where(is_nope, q2, 0.0), axis=-1, keepdims=True)
        ss_p = jnp.sum(jnp.where(is_nope, 0.0, q2), axis=-1, keepdims=True)
        r = jnp.where(is_nope, lax.rsqrt(ss_n / MLA_NOPE + NORM_EPS), lax.rsqrt(ss_p / MLA_ROPE + NORM_EPS))
        q = q * r * mqg_ref[...]
        q = _rope_slab(q, cos_mq_ref[...], sin_mq_ref[...], MLA_ROPE // 4)
        mq_ref[0, hh] = (q * (MLA_SCALE * LOG2E)).astype(BF16)
        k = kvup[:, hh * LANES:(hh + 1) * LANES]
        k = k * lax.rsqrt(jnp.sum(k * k, axis=-1, keepdims=True) / MLA_NOPE + NORM_EPS) * mkg_ref[...]
        mk_ref[0, hh] = (k + kpe).astype(BF16)
        v = kvup[:, (MLA_HEADS + hh) * LANES:(MLA_HEADS + hh + 1) * LANES]
        mv_ref[0, hh] = jnp.where(lane == ONES_LANE, 1.0, v).astype(BF16)

    def gqa_prep(base, qg_ref, kg_ref, q_ref, k_ref, v_ref, nq, nkv):
        for hh in range(nq):
            q = p[:, base + hh * LANES:base + (hh + 1) * LANES]
            q = q * lax.rsqrt(jnp.sum(q * q, axis=-1, keepdims=True) / HEAD_DIM + NORM_EPS) * qg_ref[...]
            q = _rope_slab(q, cos_hd, sin_hd, HEAD_DIM // 4)
            q_ref[0, hh] = (q * (ATTN_SCALE * LOG2E)).astype(BF16)
        for hh in range(nkv):
            k = p[:, base + (nq + hh) * LANES:base + (nq + hh + 1) * LANES]
            k = k * lax.rsqrt(jnp.sum(k * k, axis=-1, keepdims=True) / HEAD_DIM + NORM_EPS) * kg_ref[...]
            k = _rope_slab(k, cos_hd, sin_hd, HEAD_DIM // 4)
            k_ref[0, hh] = k.astype(BF16)
            v = p[:, base + (nq + nkv + hh) * LANES:base + (nq + nkv + hh + 1) * LANES]
            v_ref[0, hh] = jnp.where(lane == ONES_LANE, 1.0, v).astype(BF16)

    gqa_prep(C_SWA, sqg_ref, skg_ref, sq_ref, sk_ref, sv_ref, SWA_HEADS, SWA_KV_HEADS)
    gqa_prep(C_GQA, gqg_ref, gkg_ref, gq_ref, gk_ref, gv_ref, GQA_HEADS, GQA_KV_HEADS)


def _inproj(x, mod_tab, g1, win, lw, tabs, nbl):
    bsz, s, d = x.shape
    nb = s // TM
    row = lambda a: a.reshape(1, -1)
    full = lambda a: pl.BlockSpec(a.shape, lambda b, i: (0,) * a.ndim)
    tab_spec = pl.BlockSpec((TM, LANES), lambda b, i: (i, 0))
    head_out = lambda nh: pl.BlockSpec((1, nh, TM, LANES), lambda b, i: (b, 0, i, 0))
    head_shape = lambda nh: jax.ShapeDtypeStruct((bsz, nh, s, LANES), BF16)
    small = [row(lw['mla_qn_g']), row(lw['mla_kvn_g']), lw['wuq'], lw['wukv'],
             lw['mqg'], lw['mkg'], lw['krg'], lw['sqg'], lw['skg'], lw['gqg'], lw['gkg']]
    return pl.pallas_call(
        _inproj_kernel,
        grid=(bsz, nb),
        in_specs=[pl.BlockSpec((1, TM, d), lambda b, i: (b, i, 0)),
                  pl.BlockSpec((1, 1, 1, 6 * d), lambda b, i: (b, i // nbl, 0, 0)),
                  full(g1), full(win)] + [full(a) for a in small] + [tab_spec] * 6,
        out_specs=[pl.BlockSpec((1, TM, 1024), lambda b, i: (b, i, 0)),
                   pl.BlockSpec((1, TM, LANES), lambda b, i: (b, i, 0)),
                   head_out(4), head_out(4), head_out(4),
                   head_out(4), head_out(2), head_out(2),
                   head_out(4), head_out(2), head_out(2)],
        out_shape=[jax.ShapeDtypeStruct((bsz, s, 1024), F32),
                   jax.ShapeDtypeStruct((bsz, s, LANES), F32),
                   head_shape(4), head_shape(4), head_shape(4),
                   head_shape(4), head_shape(2), head_shape(2),
                   head_shape(4), head_shape(2), head_shape(2)],
        compiler_params=_cparams(("parallel", "parallel")),
        name="inproj",
    )(x, mod_tab, g1, win, *small, *tabs)


def _flash_kernel(*refs, grp, tq, nk, use_sink):
    refs = list(refs)
    sink_ref = refs.pop(0) if use_sink else None
    q_ref, k_ref, v_ref = refs[0], refs[1], refs[2]
    o_ref, m_sc, acc_sc = refs[-3], refs[-2], refs[-1]
    kj = pl.program_id(3)
    tk = k_ref.shape[2]
    kb = min(FLASH_KB, tk)

    @pl.when(kj == 0)
    def _():
        m_sc[...] = jnp.full(m_sc.shape, -jnp.inf, F32)
        acc_sc[...] = jnp.zeros(acc_sc.shape, F32)

    q = q_ref[0].reshape(grp * tq, LANES)
    m = m_sc[...]
    acc = acc_sc[...]
    for j in range(tk // kb):
        s = _dot_nt(q, k_ref[0, 0, j * kb:(j + 1) * kb, :])
        m_new = jnp.maximum(m, jnp.max(s, axis=-1, keepdims=True))
        alpha = jnp.exp2(m - m_new)
        pr = jnp.exp2(s - jnp.concatenate([m_new] * (kb // LANES), axis=1))
        acc = alpha * acc + _dot(pr.astype(BF16), v_ref[0, 0, j * kb:(j + 1) * kb, :])
        m = m_new
    m_sc[...] = m
    acc_sc[...] = acc

    @pl.when(kj == nk - 1)
    def _():
        l = acc[:, ONES_LANE:ONES_LANE + 1]
        out = acc
        if use_sink:
            sk = sink_ref[0]
            m_old = m[:, 0:1]
            m_fin = jnp.maximum(m_old, sk)
            a = jnp.exp2(m_old - m_fin)
            l = l * a + jnp.exp2(sk - m_fin)
            out = out * a
        o_ref[0] = (out / l).reshape(grp, tq, LANES).astype(o_ref.dtype)


def _flash(q, k, v, *, grp, tq, tk, q_blk0, nq, k_blk0, nk, sink_rows=None):
    bsz, hq, s, _ = q.shape
    hkv = hq // grp
    use_sink = sink_rows is not None
    in_specs = []
    args = []
    if use_sink:
        in_specs.append(pl.BlockSpec((1, grp * tq, 1), lambda b, h, i, j: (h, 0, 0)))
        args.append(sink_rows)
    in_specs += [pl.BlockSpec((1, grp, tq, LANES), lambda b, h, i, j: (b, h, i + q_blk0, 0)),
                 pl.BlockSpec((1, 1, tk, LANES), lambda b, h, i, j: (b, h, j + k_blk0, 0)),
                 pl.BlockSpec((1, 1, tk, LANES), lambda b, h, i, j: (b, h, j + k_blk0, 0))]
    args += [q, k, v]
    rows = grp * tq
    return pl.pallas_call(
        functools.partial(_flash_kernel, grp=grp, tq=tq, nk=nk, use_sink=use_sink),
        grid=(bsz, hkv, nq, nk),
        in_specs=in_specs,
        out_specs=pl.BlockSpec((1, grp, tq, LANES), lambda b, h, i, j: (b, h, i, 0)),
        out_shape=jax.ShapeDtypeStruct((bsz, hq, nq * tq, LANES), BF16),
        scratch_shapes=[pltpu.VMEM((rows, LANES), F32), pltpu.VMEM((rows, LANES), F32)],
        compiler_params=_cparams(("parallel", "parallel", "parallel", "arbitrary")),
        name="flash",
    )(*args)


def _swa_kernel(sink_ref, q_ref, k_ref, v_ref, o_ref, *, grp, tq, t_len, n_ctx):
    w = WINDOW
    span = tq + 2 * w
    n = pl.program_id(2)
    start = pl.multiple_of(jnp.clip(n * tq - w, 0, t_len - span), w)
    q = q_ref[0].reshape(grp * tq, LANES)
    kl = k_ref[0, 0, pl.ds(start, span), :]
    vl = v_ref[0, 0, pl.ds(start, span), :]
    kc = k_ref[0, 0, pl.ds(t_len, n_ctx), :]
    vc = v_ref[0, 0, pl.ds(t_len, n_ctx), :]
    s_loc = _dot_nt(q, kl)
    qpos = n * tq + (lax.broadcasted_iota(jnp.int32, s_loc.shape, 0) % tq)
    kpos = start + lax.broadcasted_iota(jnp.int32, s_loc.shape, 1)
    s_loc = jnp.where(jnp.abs(qpos - kpos) <= w, s_loc, -jnp.inf)
    s_ctx = _dot_nt(q, kc)
    sk = sink_ref[0]
    m = jnp.maximum(jnp.maximum(jnp.max(s_loc, axis=-1, keepdims=True),
                                jnp.max(s_ctx, axis=-1, keepdims=True)), sk)
    p_loc = jnp.exp2(s_loc - m)
    p_ctx = jnp.exp2(s_ctx - m)
    o = _dot(p_loc.astype(BF16), vl) + _dot(p_ctx.astype(BF16), vc)
    l = o[:, ONES_LANE:ONES_LANE + 1] + jnp.exp2(sk - m)
    o_ref[0] = (o / l).reshape(grp, tq, LANES).astype(o_ref.dtype)


def _swa(q, k, v, sink_rows, *, grp, tq, t_len, n_ctx):
    bsz, hq, s, _ = q.shape
    hkv = hq // grp
    return pl.pallas_call(
        functools.partial(_swa_kernel, grp=grp, tq=tq, t_len=t_len, n_ctx=n_ctx),
        grid=(bsz, hkv, t_len // tq),
        in_specs=[pl.BlockSpec((1, grp * tq, 1), lambda b, h, n: (h, 0, 0)),
                  pl.BlockSpec((1, grp, tq, LANES), lambda b, h, n: (b, h, n, 0)),
                  pl.BlockSpec((1, 1, s, LANES), lambda b, h, n: (b, h, 0, 0)),
                  pl.BlockSpec((1, 1, s, LANES), lambda b, h, n: (b, h, 0, 0))],
        out_specs=pl.BlockSpec((1, grp, tq, LANES), lambda b, h, n: (b, h, n, 0)),
        out_shape=jax.ShapeDtypeStruct((bsz, hq, t_len, LANES), BF16),
        compiler_params=_cparams(("parallel", "parallel", "arbitrary")),
        name="swa",
    )(sink_rows, q, k, v)


def _gdn_prep_kernel(cur_ref, prev_ref, next_ref, misc_ref, cw_ref, alog_ref, dtb_ref,
                     qkv_ref, lab_ref, *, nbl, nb):
    i = pl.program_id(1)
    first = jnp.logical_or(i == 0, i == nbl)
    last = jnp.logical_or(i == nbl - 1, i == nb - 1)
    cur = cur_ref[0]
    prev = jnp.where(first, 0.0, prev_ref[0])
    nxt = jnp.where(last, 0.0, next_ref[0])
    xe = jnp.concatenate([prev, cur, nxt], axis=0)
    cw = cw_ref[...]
    acc = jnp.zeros(cur.shape, F32)
    for j in range(GDN_CONV):
        off = 8 + j - (GDN_CONV - 1) // 2
        acc = acc + xe[off:off + TM, :] * cw[j:j + 1, :]
    y = _silu(acc)
    lane = _lane((TM, LANES))
    lo = lane < GDN_DK
    for sl in range(6):
        t = y[:, sl * LANES:(sl + 1) * LANES]
        if sl < 4:
            t2 = t * t
            ss0 = jnp.sum(jnp.where(lo, t2, 0.0), axis=-1, keepdims=True)
            ss1 = jnp.sum(jnp.where(lo, 0.0, t2), axis=-1, keepdims=True)
            t = t * jnp.where(lo, lax.rsqrt(ss0 + NORM_EPS), lax.rsqrt(ss1 + NORM_EPS))
            if sl < 2:
                t = t * GDN_DK ** -0.5
        qkv_ref[0, :, sl * LANES:(sl + 1) * LANES] = t
    xm = misc_ref[0]
    za = xm + dtb_ref[...]
    softplus = jnp.maximum(za, 0.0) + jnp.log(1.0 + jnp.exp(-jnp.abs(za)))
    log_a = -jnp.exp(alog_ref[...]) * softplus
    beta = 1.0 / (1.0 + jnp.exp(-xm))
    is_a = (lane % 8) < 4
    vals = jnp.where(is_a, log_a, beta)
    lab_ref[0, 0] = pltpu.roll(vals, LANES - 32, 1)
    lab_ref[0, 1] = pltpu.roll(vals, LANES - 40, 1)


def _gdn_prep(ga, misc, cw, alog_row, dtb_row, nbl):
    bsz, s, _ = ga.shape
    nb = s // TM
    r8 = TM // 8
    n8 = s // 8
    return pl.pallas_call(
        functools.partial(_gdn_prep_kernel, nbl=nbl, nb=nb),
        grid=(bsz, nb),
        in_specs=[pl.BlockSpec((1, TM, 768), lambda b, i: (b, i, 0)),
                  pl.BlockSpec((1, 8, 768), lambda b, i: (b, jnp.maximum(i * r8 - 1, 0), 0)),
                  pl.BlockSpec((1, 8, 768), lambda b, i: (b, jnp.minimum((i + 1) * r8, n8 - 1), 0)),
                  pl.BlockSpec((1, TM, LANES), lambda b, i: (b, i, 0)),
                  pl.BlockSpec(cw.shape, lambda b, i: (0, 0)),
                  pl.BlockSpec((1, LANES), lambda b, i: (0, 0)),
                  pl.BlockSpec((1, LANES), lambda b, i: (0, 0))],
        out_specs=[pl.BlockSpec((1, TM, 768), lambda b, i: (b, i, 0)),
                   pl.BlockSpec((1, 2, TM, LANES), lambda b, i: (b, 0, i, 0))],
        out_shape=[jax.ShapeDtypeStruct((bsz, s, 768), F32),
                   jax.ShapeDtypeStruct((bsz, 2, s, LANES), F32)],
        compiler_params=_cparams(("parallel", "parallel")),
        name="gdn_prep",
    )(ga, ga, ga, misc, cw, alog_row, dtb_row)


def _gdn_scan_kernel(qkv_ref, lab_ref, labt_ref, o_ref, s_sc, *, reverse):
    c = GDN_CHUNK
    c2 = 2 * c
    i = pl.program_id(1)

    @pl.when(i == 0)
    def _():
        s_sc[...] = jnp.zeros(s_sc.shape, F32)

    ri = lax.broadcasted_iota(jnp.int32, (c2, c2), 0)
    ci = lax.broadcasted_iota(jnp.int32, (c2, c2), 1)
    same = (ri // c) == (ci // c)
    if reverse:
        incl = jnp.logical_and(same, ri <= ci)
        strict = jnp.logical_and(same, ri < ci)
    else:
        incl = jnp.logical_and(same, ri >= ci)
        strict = jnp.logical_and(same, ri > ci)
    eye = (ri == ci).astype(F32)
    r1 = lax.broadcasted_iota(jnp.int32, (c, c), 0)
    c1 = lax.broadcasted_iota(jnp.int32, (c, c), 1)
    cum_col = ((r1 <= c1) if reverse else (r1 >= c1)).astype(BF16)
    rr = lax.broadcasted_iota(jnp.int32, (c, c2), 0)
    cc = lax.broadcasted_iota(jnp.int32, (c, c2), 1) % c
    cum_row = ((rr >= cc) if reverse else (rr <= cc)).astype(BF16)
    lane = _lane((c, LANES))
    lo = lane < GDN_DK
    lane2 = _lane((1, c2))
    lo2 = lane2 < c
    g_last_row = 0 if reverse else c - 1

    def stack(x):
        return jnp.concatenate([jnp.where(lo, x, 0.0), jnp.where(lo, 0.0, x)], axis=0)

    def fold(mat):
        return mat[0:c, :] + mat[c:c2, :]

    order = list(range(TM // c - 1, -1, -1) if reverse else range(TM // c))
    inst = []
    for ch in order:
        rows = slice(ch * c, (ch + 1) * c)
        lab = lab_ref[0, 0, rows, :]
        g_cols = _dot_exact_lhs(cum_col, lab)
        la_rows = labt_ref[0, 0, :, rows]
        g_rows = _dot_exact_rhs(la_rows, cum_row)
        for pr in range(2):
            h0, h1 = 2 * pr, 2 * pr + 1
            q = qkv_ref[0, rows, pr * LANES:(pr + 1) * LANES]
            k = qkv_ref[0, rows, (2 + pr) * LANES:(3 + pr) * LANES]
            v = qkv_ref[0, rows, (4 + pr) * LANES:(5 + pr) * LANES]
            beta = jnp.where(lo, lab[:, 4 + h0:5 + h0], lab[:, 4 + h1:5 + h1])
            g = jnp.where(lo, g_cols[:, h0:h0 + 1], g_cols[:, h1:h1 + 1])
            g_st_col = jnp.concatenate([g_cols[:, h0:h0 + 1], g_cols[:, h1:h1 + 1]], axis=0)
            g_st_row = jnp.where(lo2, g_rows[h0:h0 + 1, :], g_rows[h1:h1 + 1, :])
            diff = g_st_col - g_st_row
            dec = jnp.where(incl, jnp.exp(jnp.where(incl, diff, 0.0)), 0.0)
            kb = k * beta
            eg = jnp.exp(g)
            k_st = stack(k).astype(BF16)
            a = jnp.where(strict, _dot_nt(stack(kb).astype(BF16), k_st) * dec, 0.0)
            qk = _dot_nt(stack(q).astype(BF16), k_st) * dec
            gl = g[g_last_row:g_last_row + 1, :]
            gl_col = jnp.concatenate([jnp.broadcast_to(g_cols[g_last_row:g_last_row + 1, h0:h0 + 1], (c, 1)),
                                      jnp.broadcast_to(g_cols[g_last_row:g_last_row + 1, h1:h1 + 1], (c, 1))],
                                     axis=0)
            inst.append(dict(rows=rows, pr=pr, a=a, vb=stack(v * beta).astype(BF16),
                             kg=stack(kb * eg).astype(BF16), qg=(q * eg).astype(BF16),
                             qkf=fold(qk).astype(BF16), kd=(k * jnp.exp(gl - g)).astype(BF16),
                             sdec=jnp.exp(gl_col)))
    tinv = [eye - it['a'] for it in inst]
    pw = [it['a'] for it in inst]
    for _ in range(5):
        pw = [_dot_hp(x, x) for x in pw]
        tinv = [t + _dot_hp(t, x) for t, x in zip(tinv, pw)]
    for it, t in zip(inst, tinv):
        tf = fold(t).astype(BF16)
        it['u'] = _dot(tf, it['vb'])
        it['w'] = _dot(tf, it['kg']).astype(BF16)
    state = [s_sc[0], s_sc[1]]
    for it in inst:
        st = state[it['pr']]
        st_b = st.astype(BF16)
        v_new = it['u'] - _dot(it['w'], st_b)
        o = _dot(it['qg'], st_b) + _dot(it['qkf'], stack(v_new).astype(BF16))
        o_ref[0, it['rows'], it['pr'] * LANES:(it['pr'] + 1) * LANES] = o
        upd = _dot_tn(it['kd'], v_new.astype(BF16))
        state[it['pr']] = st * it['sdec'] + jnp.where(same, upd, 0.0)
    s_sc[0] = state[0]
    s_sc[1] = state[1]


def _gdn_scan(qkv, lab, labt, *, reverse, nbl):
    bsz, s, _ = qkv.shape
    nb = s // TM
    nbc = nb - nbl
    dsel = 1 if reverse else 0

    def blk(i):
        if reverse:
            return jnp.where(i < nbc, nb - 1 - i, nbl - 1 - (i - nbc))
        return jnp.where(i < nbc, nbl + i, i - nbc)

    return pl.pallas_call(
        functools.partial(_gdn_scan_kernel, reverse=reverse),
        grid=(bsz, nb),
        in_specs=[pl.BlockSpec((1, TM, 768), lambda b, i: (b, blk(i), 0)),
                  pl.BlockSpec((1, 1, TM, LANES), lambda b, i: (b, dsel, blk(i), 0)),
                  pl.BlockSpec((1, 1, 8, TM), lambda b, i: (b, dsel, 0, blk(i)))],
        out_specs=pl.BlockSpec((1, TM, 2 * LANES), lambda b, i: (b, blk(i), 0)),
        out_shape=jax.ShapeDtypeStruct((bsz, s, 2 * LANES), F32),
        scratch_shapes=[pltpu.VMEM((2, LANES, LANES), F32)],
        compiler_params=_cparams(("parallel", "arbitrary")),
        name="gdn_scan_bwd" if reverse else "gdn_scan_fwd",
    )(qkv, lab, labt)


def _outproj_kernel(x_ref, mod_ref, of_ref, ob_ref, z_ref, gg_ref, mo_ref, so_ref, go_ref,
                    mc_ref, sc_ref, gc_ref, wout_ref, g2_ref, rw_ref, rb_ref,
                    x1_ref, h2_ref, idx_ref, gate_ref, rank_ref, base_ref, end_ref, run_sc, *, nbl):
    d = D_MODEL
    first = jnp.logical_and(pl.program_id(0) == 0, pl.program_id(1) == 0)
    is_ctx = pl.program_id(1) >= nbl

    @pl.when(first)
    def _():
        run_sc[...] = jnp.zeros(run_sc.shape, F32)

    m = mod_ref[0, 0]
    lane = _lane((TM, LANES))
    lo = lane < HEAD_DIM
    o = of_ref[0] + ob_ref[0]
    z = z_ref[0]
    mix = []
    for pr in range(2):
        t = o[:, pr * LANES:(pr + 1) * LANES]
        t2 = t * t
        ms0 = jnp.sum(jnp.where(lo, t2, 0.0), axis=-1, keepdims=True) / HEAD_DIM
        ms1 = jnp.sum(jnp.where(lo, 0.0, t2), axis=-1, keepdims=True) / HEAD_DIM
        t = t * jnp.where(lo, lax.rsqrt(ms0 + NORM_EPS), lax.rsqrt(ms1 + NORM_EPS)) * gg_ref[...]
        t = t * _silu(z[:, pr * LANES:(pr + 1) * LANES])
        mix.append(t.astype(BF16))
    for lat_ref, ctx_ref in ((mo_ref, mc_ref), (so_ref, sc_ref), (go_ref, gc_ref)):
        for hh in range(4):
            mix.append(jnp.where(is_ctx, ctx_ref[0, hh], lat_ref[0, hh]))
    proj = _dot(jnp.concatenate(mix, axis=1), wout_ref[...])
    x1 = x_ref[0] + m[:, 2 * d:3 * d] * proj
    x1_ref[0] = x1
    xn = x1 * lax.rsqrt(jnp.mean(x1 * x1, axis=-1, keepdims=True) + NORM_EPS) * g2_ref[...]
    h2 = xn * (1.0 + m[:, 4 * d:5 * d]) + m[:, 3 * d:4 * d]
    h2_ref[0] = h2.astype(BF16)

    logits = _dot_hp(h2, rw_ref[...])
    scores = 1.0 / (1.0 + jnp.exp(-logits))
    valid = lane < N_EXPERTS
    sel = jnp.where(valid, scores + rb_ref[...], -jnp.inf)
    member = jnp.zeros((TM, LANES), F32)
    picks = []
    for _ in range(TOP_K):
        mx = jnp.max(sel, axis=-1, keepdims=True)
        idx = jnp.min(jnp.where(sel == mx, lane, LANES), axis=-1, keepdims=True)
        hit = lane == idx
        gate = jnp.sum(jnp.where(hit, scores, 0.0), axis=-1, keepdims=True)
        sel = jnp.where(hit, -jnp.inf, sel)
        member = member + hit.astype(F32)
        picks.append((idx, hit, gate))
    gsum = picks[0][2]
    for kk in range(1, TOP_K):
        gsum = gsum + picks[kk][2]
    ri = lax.broadcasted_iota(jnp.int32, (TM, TM), 0)
    ci = lax.broadcasted_iota(jnp.int32, (TM, TM), 1)
    before = (ri > ci).astype(BF16)
    run = run_sc[...]
    base_ref[0] = run
    cum = _dot(before, member.astype(BF16)) + run
    idx_out = jnp.full((TM, LANES), -1, jnp.int32)
    gate_out = jnp.zeros((TM, LANES), F32)
    rank_out = jnp.zeros((TM, LANES), F32)
    for kk, (idx, hit, gate) in enumerate(picks):
        rank = jnp.sum(jnp.where(hit, cum, 0.0), axis=-1, keepdims=True)
        here = lane == kk
        idx_out = jnp.where(here, idx, idx_out)
        gate_out = jnp.where(here, gate / gsum * ROUTED_SCALE, gate_out)
        rank_out = jnp.where(here, rank, rank_out)
    idx_ref[0] = idx_out
    gate_ref[0] = gate_out
    rank_ref[0] = rank_out
    run_end = run + jnp.sum(member, axis=0, keepdims=True)
    end_ref[0] = run_end
    run_sc[...] = run_end


def _outproj(x, mod_tab, o_f, o_b, ga, gg_row, lat, ctx, wout, g2, rw, rb, *, n_blk, nbl):
    bsz, _, d = x.shape
    s_out = n_blk * TM
    full = lambda a: pl.BlockSpec(a.shape, lambda b, i: (0,) * a.ndim)
    tok = lambda w: pl.BlockSpec((1, TM, w), lambda b, i: (b, i, 0))
    head = pl.BlockSpec((1, 4, TM, LANES), lambda b, i: (b, 0, jnp.minimum(i, nbl - 1), 0))
    head_c = pl.BlockSpec((1, 4, TM, LANES), lambda b, i: (b, 0, jnp.maximum(i - nbl, 0), 0))
    return pl.pallas_call(
        functools.partial(_outproj_kernel, nbl=nbl),
        grid=(bsz, n_blk),
        in_specs=[tok(d),
                  pl.BlockSpec((1, 1, 1, 6 * d), lambda b, i: (b, i // nbl, 0, 0)),
                  tok(2 * LANES), tok(2 * LANES),
                  pl.BlockSpec((1, TM, 2 * LANES), lambda b, i: (b, i, 3)),
                  full(gg_row), head, head, head, head_c, head_c, head_c,
                  full(wout), full(g2), full(rw), full(rb)],
        out_specs=[tok(d), tok(d), tok(LANES), tok(LANES), tok(LANES),
                   pl.BlockSpec((1, 1, LANES), lambda b, i: (b * n_blk + i, 0, 0)),
                   pl.BlockSpec((1, 1, LANES), lambda b, i: (b * n_blk + i, 0, 0))],
        out_shape=[jax.ShapeDtypeStruct((bsz, s_out, d), F32),
                   jax.ShapeDtypeStruct((bsz, s_out, d), BF16),
                   jax.ShapeDtypeStruct((bsz, s_out, LANES), jnp.int32),
                   jax.ShapeDtypeStruct((bsz, s_out, LANES), F32),
                   jax.ShapeDtypeStruct((bsz, s_out, LANES), F32),
                   jax.ShapeDtypeStruct((bsz * n_blk, 1, LANES), F32),
                   jax.ShapeDtypeStruct((bsz * n_blk, 1, LANES), F32)],
        scratch_shapes=[pltpu.VMEM((1, LANES), F32)],
        compiler_params=_cparams(("arbitrary", "arbitrary")),
        name="outproj_router",
    )(x, mod_tab, o_f, o_b, ga, gg_row, *lat, *ctx, wout, g2, rw, rb)


def _moe_tables_kernel(lo_ref, hi_ref, offl_ref, seg_ref, gbase_ref, be_ref, nused_ref):
    ncp = lo_ref.shape[0]
    nblkp = be_ref.shape[0]
    seg = jnp.ceil((hi_ref[...] - lo_ref[...]) * (1.0 / MOE_SEG)) * MOE_SEG
    ri = lax.broadcasted_iota(jnp.int32, (LANES, LANES), 0)
    ci = lax.broadcasted_iota(jnp.int32, (LANES, LANES), 1)
    before_lane = (ri < ci).astype(BF16)
    rc = lax.broadcasted_iota(jnp.int32, (ncp, ncp), 0)
    cc = lax.broadcasted_iota(jnp.int32, (ncp, ncp), 1)
    before_row = (rc > cc).astype(BF16)
    seg_ref[...] = seg
    offl_ref[...] = _dot_exact_rhs(seg, before_lane)
    region = jnp.ceil(jnp.sum(seg, axis=0, keepdims=True) * (1.0 / MOE_BR)) * MOE_BR
    goff = _dot_exact_rhs(jnp.broadcast_to(region, (8, LANES)), before_lane)[0:1]
    gbase_ref[...] = goff + _dot_exact_lhs(before_row, seg)
    lane = _lane((nblkp, LANES))
    gend = jnp.where(lane < N_EXPERTS, goff + region, jnp.inf)
    row0 = (lax.broadcasted_iota(jnp.int32, (nblkp, LANES), 0) * MOE_BR).astype(F32)
    be = jnp.sum((gend <= row0).astype(F32), axis=-1, keepdims=True)
    be_ref[...] = jnp.broadcast_to(jnp.minimum(be, N_EXPERTS - 1.0), (nblkp, LANES))
    nused_ref[...] = jnp.broadcast_to(jnp.sum(region, axis=-1, keepdims=True) * (1.0 / MOE_BR), (1, LANES))


def _moe_tables(cb_lo, cb_hi, nblk):
    ncp = cb_lo.shape[0]
    nblkp = -(-nblk // 8) * 8
    tab = jax.ShapeDtypeStruct((ncp, LANES), F32)
    return pl.pallas_call(
        _moe_tables_kernel,
        out_shape=[tab, tab, tab, jax.ShapeDtypeStruct((nblkp, LANES), F32),
                   jax.ShapeDtypeStruct((1, LANES), F32)],
        name="moe_tables",
    )(cb_lo, cb_hi)


def _segment_copies(c, offl_s, seg_s, gbase_s, local_ref, glob_ref, sem, *, to_global, wait):
    def piece(lo, go, size):
        lo = pl.multiple_of(lo, MOE_SEG)
        go = pl.multiple_of(go, MOE_SEG)
        loc = local_ref.at[pl.ds(lo, size)]
        glo = glob_ref.at[pl.ds(go, size)]
        cp = pltpu.make_async_copy(loc, glo, sem) if to_global else pltpu.make_async_copy(glo, loc, sem)
        if wait:
            cp.wait()
        else:
            cp.start()

    def expert_body(e, carry):
        t = c * N_EXPERTS + e
        off, sg, gb = offl_s[t], seg_s[t], gbase_s[t]
        n_full = sg // MOE_ROWS

        def full_body(w, carry2):
            piece(off + w * MOE_ROWS, gb + w * MOE_ROWS, MOE_ROWS)
            return carry2

        lax.fori_loop(0, n_full, full_body, 0)
        done = n_full * MOE_ROWS
        size = MOE_ROWS // 2
        while size >= MOE_SEG:
            bit = sg & size

            @pl.when(bit != 0)
            def _(done=done, size=size):
                piece(off + done, gb + done, size)

            done = done + bit
            size //= 2
        return carry

    lax.fori_loop(0, N_EXPERTS, expert_body, 0)


def _moe_sort_kernel(offl_s, seg_s, gbase_s, h_ref, idx_ref, rank_ref, cb_ref, offv_ref, xs_in_ref,
                     pos_ref, xs_ref, xy_sc, sem):
    del xs_in_ref
    c = pl.program_id(0)
    tc = h_ref.shape[0]
    lane_t = _lane((tc, LANES))
    h = h_ref[...]
    idx = idx_ref[...]
    rank = rank_ref[...]
    adj = offv_ref[0] - cb_ref[0]
    pos = jnp.full((tc, LANES), -1.0, F32)
    for k in range(TOP_K):
        hit = lane_t == idx[:, k:k + 1]
        p_k = rank[:, k:k + 1] + jnp.sum(jnp.where(hit, adj, 0.0), axis=-1, keepdims=True)
        pos = jnp.where(lane_t == k, p_k, pos)
    pos_ref[...] = pos
    sel8 = (lax.broadcasted_iota(jnp.int32, (8, LANES), 0)
            == lax.broadcasted_iota(jnp.int32, (8, LANES), 1)).astype(BF16)
    hi, mid, lo = _split3(pos)
    pos_row = _dot_nt(sel8, hi) + _dot_nt(sel8, mid) + _dot_nt(sel8, lo)

    def gather_body(rb, carry):
        for u in range(2):
            r0 = pl.multiple_of((2 * rb + u) * MOE_ROWS, MOE_ROWS)
            rid = (lax.broadcasted_iota(jnp.int32, (MOE_ROWS, tc), 0) + r0).astype(F32)
            onehot = jnp.zeros((MOE_ROWS, tc), F32)
            for k in range(TOP_K):
                onehot = jnp.where(pos_row[k:k + 1, :] == rid, 1.0, onehot)
            xy_sc[pl.ds(r0, MOE_ROWS), :] = _dot(onehot.astype(BF16), h).astype(BF16)
        return carry

    last = c * N_EXPERTS + N_EXPERTS - 1
    total = offl_s[last] + seg_s[last]
    lax.fori_loop(0, (total + 2 * MOE_ROWS - 1) // (2 * MOE_ROWS), gather_body, 0)
    _segment_copies(c, offl_s, seg_s, gbase_s, xy_sc, xs_ref, sem, to_global=True, wait=False)
    _segment_copies(c, offl_s, seg_s, gbase_s, xy_sc, xs_ref, sem, to_global=True, wait=True)


def _moe_ffn_kernel(be_s, nused_s, x_ref, wgu_ref, wdn_ref, y_ref, wgu_sc, wdn_sc):
    i = pl.program_id(0)
    used = i < nused_s[0]
    changed = jnp.logical_or(i == 0, be_s[i] != be_s[jnp.maximum(i - 1, 0)])

    @pl.when(jnp.logical_and(used, changed))
    def _():
        wgu_sc[...] = wgu_ref[0, 0].astype(BF16)
        wdn_sc[...] = wdn_ref[0, 0].astype(BF16)

    @pl.when(used)
    def _():
        starts = range(0, MOE_BR, MOE_ROWS)
        gus = [_dot(x_ref[r0:r0 + MOE_ROWS, :], wgu_sc[...]) for r0 in starts]
        for r0, gu in zip(starts, gus):
            act = _silu(gu[:, :D_EXPERT]) * gu[:, D_EXPERT:]
            y_ref[r0:r0 + MOE_ROWS, :] = _dot(act.astype(BF16), wdn_sc[...]).astype(BF16)


def _moe_combine_kernel(offl_s, seg_s, gbase_s, h_ref, pos_ref, gate_ref, sgu_ref, sdn_ref, ys_ref,
                        y_ref, yl_sc, sem):
    c = pl.program_id(0)
    tc = h_ref.shape[0]
    n_rows = yl_sc.shape[0]
    yl_sc[...] = jnp.zeros(yl_sc.shape, BF16)
    _segment_copies(c, offl_s, seg_s, gbase_s, yl_sc, ys_ref, sem, to_global=False, wait=False)
    gu = _dot(h_ref[...], sgu_ref[...])
    act = _silu(gu[:, :D_EXPERT]) * gu[:, D_EXPERT:]
    y_ref[...] = _dot(act.astype(BF16), sdn_ref[...])
    _segment_copies(c, offl_s, seg_s, gbase_s, yl_sc, ys_ref, sem, to_global=False, wait=True)
    pos = pos_ref[...]
    gate = gate_ref[...]
    kblk = n_rows // MOE_KSPLIT
    for tb in range(tc // TM):
        pos_t = pos[tb * TM:(tb + 1) * TM]
        gate_t = gate[tb * TM:(tb + 1) * TM]
        out = y_ref[tb * TM:(tb + 1) * TM, :]
        for kb in range(MOE_KSPLIT):
            cid = (lax.broadcasted_iota(jnp.int32, (TM, kblk), 1) + kb * kblk).astype(F32)
            scat = jnp.zeros((TM, kblk), F32)
            for k in range(TOP_K):
                scat = jnp.where(pos_t[:, k:k + 1] == cid, gate_t[:, k:k + 1], scat)
            out = out + _dot(scat.astype(BF16), yl_sc[kb * kblk:(kb + 1) * kblk, :])
        y_ref[tb * TM:(tb + 1) * TM, :] = out


def _moe(h2, idx, rank, gate, base, end, wgu, wdn, layer, sgu, sdn):
    n_tok, d = h2.shape
    tc = MOE_CHUNK
    nc = n_tok // tc
    bpc = tc // TM
    ne = wgu.shape[1]
    assert ne == N_EXPERTS
    ncp = -(-nc // LANES) * LANES
    n_loc = -(-(tc * TOP_K + ne * MOE_SEG) // (MOE_ROWS * MOE_KSPLIT)) * (MOE_ROWS * MOE_KSPLIT)
    assert n_loc >= -(-(tc * TOP_K + ne * (MOE_SEG - 1)) // (2 * MOE_ROWS)) * (2 * MOE_ROWS)
    nblk = -(-(n_tok * TOP_K + nc * ne * (MOE_SEG - 1) + ne * (MOE_BR - 1)) // MOE_BR)
    pad = lambda a: jnp.concatenate([a, jnp.zeros((ncp - nc, LANES), F32)], axis=0)
    cb_lo = pad(base[0::bpc, 0, :])
    cb_hi = pad(end[bpc - 1::bpc, 0, :])
    offl, seg, gbase, be, nused = _moe_tables(cb_lo, cb_hi, nblk)
    to_smem = lambda a: a[:nc, :ne].astype(jnp.int32).reshape(nc * ne)
    tabs = (to_smem(offl), to_smem(seg), to_smem(gbase))
    be_i = be[:nblk, 0].astype(jnp.int32)
    nused_i = nused[0, :1].astype(jnp.int32)
    chunk = lambda w: pl.BlockSpec((tc, w), lambda c, *_: (c, 0))
    row = pl.BlockSpec((1, 1, LANES), lambda c, *_: (c, 0, 0))
    anyspec = pl.BlockSpec(memory_space=pl.ANY)
    const = lambda a: pl.BlockSpec(a.shape, lambda c, *_: (0,) * a.ndim)

    pos, xs = pl.pallas_call(
        _moe_sort_kernel,
        grid_spec=pltpu.PrefetchScalarGridSpec(
            num_scalar_prefetch=3, grid=(nc,),
            in_specs=[chunk(d), chunk(LANES), chunk(LANES), row, row, anyspec],
            out_specs=[chunk(LANES), anyspec],
            scratch_shapes=[pltpu.VMEM((n_loc, d), BF16), pltpu.SemaphoreType.DMA]),
        out_shape=[jax.ShapeDtypeStruct((n_tok, LANES), F32),
                   jax.ShapeDtypeStruct((nblk * MOE_BR, d), BF16)],
        input_output_aliases={8: 1},
        compiler_params=_cparams(("arbitrary",)),
        name="moe_sort",
    )(*tabs, h2, idx, rank, cb_lo[:nc, None, :], offl[:nc, None, :], jnp.zeros((nblk * MOE_BR, d), BF16))

    last = lambda i, be_s, nu_s: jnp.minimum(i, nu_s[0] - 1)
    ys = pl.pallas_call(
        _moe_ffn_kernel,
        grid_spec=pltpu.PrefetchScalarGridSpec(
            num_scalar_prefetch=2, grid=(nblk,),
            in_specs=[pl.BlockSpec((MOE_BR, d), lambda i, be_s, nu_s: (last(i, be_s, nu_s), 0)),
                      pl.BlockSpec((1, 1, d, 2 * D_EXPERT),
                                   lambda i, be_s, nu_s: (layer, be_s[last(i, be_s, nu_s)], 0, 0)),
                      pl.BlockSpec((1, 1, D_EXPERT, d),
                                   lambda i, be_s, nu_s: (layer, be_s[last(i, be_s, nu_s)], 0, 0))],
            out_specs=pl.BlockSpec((MOE_BR, d), lambda i, be_s, nu_s: (last(i, be_s, nu_s), 0)),
            scratch_shapes=[pltpu.VMEM((d, 2 * D_EXPERT), BF16), pltpu.VMEM((D_EXPERT, d), BF16)]),
        out_shape=jax.ShapeDtypeStruct((nblk * MOE_BR, d), BF16),
        input_output_aliases={2: 0},
        compiler_params=_cparams(("arbitrary",)),
        name="moe_ffn",
    )(be_i, nused_i, xs, wgu, wdn)

    return pl.pallas_call(
        _moe_combine_kernel,
        grid_spec=pltpu.PrefetchScalarGridSpec(
            num_scalar_prefetch=3, grid=(nc,),
            in_specs=[chunk(d), chunk(LANES), chunk(LANES), const(sgu), const(sdn), anyspec],
            out_specs=chunk(d),
            scratch_shapes=[pltpu.VMEM((n_loc, d), BF16), pltpu.SemaphoreType.DMA]),
        out_shape=jax.ShapeDtypeStruct((n_tok, d), F32),
        compiler_params=_cparams(("arbitrary",)),
        name="moe_combine",
    )(*tabs, h2, pos, gate, sgu, sdn, ys)


def _residual_kernel(x_ref, y_ref, mod_ref, o_ref):
    d = D_MODEL
    o_ref[0] = x_ref[0] + mod_ref[0, 0][:, 5 * d:6 * d] * y_ref[0]


def _residual(x1, y, mod_tab, *, nbl):
    bsz, s, d = x1.shape
    tok = pl.BlockSpec((1, TM, d), lambda b, i: (b, i, 0))
    return pl.pallas_call(
        _residual_kernel,
        grid=(bsz, s // TM),
        in_specs=[tok, tok, pl.BlockSpec((1, 1, 1, 6 * d), lambda b, i: (b, i // nbl, 0, 0))],
        out_specs=tok,
        out_shape=jax.ShapeDtypeStruct((bsz, s, d), F32),
        compiler_params=_cparams(("parallel", "parallel")),
        name="residual",
    )(x1, y, mod_tab)


def _axial_tables(n_rows, rot_dim):
    rows = jnp.repeat(jnp.arange(n_rows), GRID_W).astype(F32)
    cols = jnp.tile(jnp.arange(GRID_W), n_rows).astype(F32)
    axis_dim = rot_dim // 2
    inv_freq = ROPE_THETA ** (-jnp.arange(0, axis_dim, 2, dtype=F32) / axis_dim)
    ang_r = rows[:, None] * inv_freq
    ang_c = cols[:, None] * inv_freq
    ang = jnp.concatenate([ang_r, ang_r, ang_c, ang_c], axis=-1)
    return jnp.cos(ang), jnp.sin(ang)


def _rope_slab_tables(t_len, n_ctx, rot_dim, lane0):
    cos, sin = _axial_tables(t_len // GRID_W, rot_dim)
    half = rot_dim // 4
    sign = jnp.where((jnp.arange(rot_dim) % (2 * half)) < half, -1.0, 1.0).astype(F32)
    cos_t = jnp.ones((t_len + n_ctx, LANES), F32).at[:t_len, lane0:lane0 + rot_dim].set(cos)
    sin_t = jnp.zeros((t_len + n_ctx, LANES), F32).at[:t_len, lane0:lane0 + rot_dim].set(sin * sign)
    return cos_t, sin_t


def _slab_cols(starts, width):
    out = []
    for st in starts:
        out += list(range(st, st + width)) + [-1] * (LANES - width)
    return out


def _gather_cols(w, cols):
    w_ext = jnp.concatenate([w, jnp.zeros((w.shape[0], 1), w.dtype)], axis=1)
    idx = np.array([c if c >= 0 else w.shape[1] for c in cols], np.int32)
    return w_ext[:, idx]


def _pad_row(v, lane0=0):
    return jnp.zeros((1, LANES), F32).at[0, lane0:lane0 + v.shape[0]].set(v.astype(F32))


def _layer_weights(l, w_in, gdn_conv_w, gdn_a_log, gdn_dt_bias, gdn_norm_g, mla_qn_g, mla_kvn_g,
                   mla_w_uq, mla_w_ukv, mla_qk_g, swa_qk_g, swa_sink, gqa_qk_g, w_out, router_w,
                   router_bias):
    hd = HEAD_DIM
    cols = list(range(0, 1024))
    cols += list(range(O_MQ, O_MQ + MLA_Q_RANK)) + list(range(O_MKV, O_MKV + MLA_KV_RANK))
    misc = list(range(O_MKR, O_MKR + MLA_ROPE))
    misc += list(range(O_AA, O_AA + 4)) + list(range(O_AB, O_AB + 4))
    misc += list(range(O_AA + 4, O_AA + 8)) + list(range(O_AB + 4, O_AB + 8))
    cols += misc + [-1] * (LANES - len(misc))
    cols += _slab_cols([O_SQ + hd * i for i in range(4)] + [O_SK + hd * i for i in range(2)]
                       + [O_SV + hd * i for i in range(2)], hd)
    cols += _slab_cols([O_GQ + hd * i for i in range(4)] + [O_GK + hd * i for i in range(2)]
                       + [O_GV + hd * i for i in range(2)], hd)
    assert len(cols) == N_COL
    qd = MLA_NOPE + MLA_ROPE
    uq_cols = []
    for hh in range(MLA_HEADS):
        uq_cols += list(range(hh * qd, hh * qd + qd)) + [-1] * (LANES - qd)
    kvd = MLA_NOPE + MLA_V
    ukv_cols = _slab_cols([hh * kvd for hh in range(MLA_HEADS)], MLA_NOPE)
    ukv_cols += _slab_cols([hh * kvd + MLA_NOPE for hh in range(MLA_HEADS)], MLA_V)
    orow = list(range(0, 256)) + _slab_cols([256 + hd * i for i in range(12)], hd)
    alog = jnp.zeros((1, LANES), F32)
    dtb = jnp.zeros((1, LANES), F32)
    for dd in range(2):
        alog = alog.at[0, 32 + 8 * dd:36 + 8 * dd].set(gdn_a_log[l, dd])
        dtb = dtb.at[0, 32 + 8 * dd:36 + 8 * dd].set(gdn_dt_bias[l, dd])
    return dict(
        win=_gather_cols(w_in[l], cols).astype(BF16),
        conv=gdn_conv_w[l][:, :768],
        alog=alog, dtb=dtb,
        gdn_g=jnp.concatenate([gdn_norm_g[l], gdn_norm_g[l]]).reshape(1, LANES),
        mla_qn_g=mla_qn_g[l], mla_kvn_g=mla_kvn_g[l],
        wuq=_gather_cols(mla_w_uq[l], uq_cols).astype(BF16),
        wukv=_gather_cols(mla_w_ukv[l], ukv_cols).astype(BF16),
        mqg=_pad_row(mla_qk_g[l, 0]), mkg=_pad_row(mla_qk_g[l, 1, :MLA_NOPE]),
        krg=_pad_row(mla_qk_g[l, 1, MLA_NOPE:]),
        sqg=_pad_row(swa_qk_g[l, 0]), skg=_pad_row(swa_qk_g[l, 1]),
        gqg=_pad_row(gqa_qk_g[l, 0]), gkg=_pad_row(gqa_qk_g[l, 1]),
        wout=_gather_cols(w_out[l].T, orow).T.astype(BF16),
        rw=jnp.concatenate([router_w[l], jnp.zeros((D_MODEL, LANES - N_EXPERTS), F32)], axis=1),
        rb=_pad_row(router_bias[l]),
        sink=swa_sink[l],
    )


def _sink_rows(sink, grp, tq):
    hkv = sink.shape[0] // grp
    return jnp.repeat((sink.astype(F32) * LOG2E).reshape(hkv, grp), tq, axis=1).reshape(hkv, grp * tq, 1)


def kernel(x, c, ctx, c_ctx, w_mod, b_mod, norm1_g, norm2_g, w_in, gdn_conv_w, gdn_a_log, gdn_dt_bias,
           gdn_norm_g, mla_qn_g, mla_kvn_g, mla_w_uq, mla_w_ukv, mla_qk_g, swa_qk_g, swa_sink, gqa_qk_g,
           w_out, router_w, router_bias, exp_w_gu, exp_w_down, shared_w_gu, shared_w_down):
    bsz, t_len, d = x.shape
    n_ctx = ctx.shape[1]
    depth = w_mod.shape[0]
    s = t_len + n_ctx
    assert d == D_MODEL and t_len % TM == 0 and n_ctx % TM == 0 and t_len >= 3 * WINDOW
    assert bsz + 1 <= 8
    nbl = t_len // TM
    nb = s // TM

    tabs = (_rope_slab_tables(t_len, n_ctx, HEAD_DIM, 0) + _rope_slab_tables(t_len, n_ctx, MLA_ROPE, MLA_NOPE)
            + _rope_slab_tables(t_len, n_ctx, MLA_ROPE, 0))
    c8 = jnp.zeros((8, d), F32).at[:bsz].set(c).at[bsz].set(c_ctx)
    xs = jnp.concatenate([x, ctx], axis=1)

    tq_d = 512 if t_len % 512 == 0 else TM
    tk_d = 8448 if (s % 8448 == 0) else TM

    for l in range(depth):
        need_ctx = l < depth - 1
        lw = _layer_weights(l, w_in, gdn_conv_w, gdn_a_log, gdn_dt_bias, gdn_norm_g, mla_qn_g, mla_kvn_g,
                            mla_w_uq, mla_w_ukv, mla_qk_g, swa_qk_g, swa_sink, gqa_qk_g, w_out, router_w,
                            router_bias)
        mod = _modulation(c8, w_mod[l], b_mod[l])
        mod_tab = jnp.stack([mod[:bsz], jnp.broadcast_to(mod[bsz], (bsz, 6 * d))], axis=1)[:, :, None, :]

        (ga, misc, mq, mk, mv, sq, sk, sv, gq, gk, gv) = _inproj(
            xs, mod_tab, norm1_g[l].reshape(1, d), lw['win'], lw, tabs, nbl)

        qkv, lab = _gdn_prep(ga, misc, lw['conv'], lw['alog'], lw['dtb'], nbl)
        labt = jnp.swapaxes(lab[..., :8], 2, 3)
        o_f = _gdn_scan(qkv, lab, labt, reverse=False, nbl=nbl)
        o_b = _gdn_scan(qkv, lab, labt, reverse=True, nbl=nbl)

        mo = _flash(mq, mk, mv, grp=1, tq=2 * tq_d, tk=tk_d, q_blk0=0, nq=t_len // (2 * tq_d),
                    k_blk0=0, nk=s // tk_d)
        go = _flash(gq, gk, gv, grp=2, tq=tq_d, tk=tk_d, q_blk0=0, nq=t_len // tq_d, k_blk0=0, nk=s // tk_d)
        so = _swa(sq, sk, sv, _sink_rows(lw['sink'], 2, tq_d), grp=2, tq=tq_d, t_len=t_len, n_ctx=n_ctx)
        lat = (mo, so, go)
        ctx_out = lat
        if need_ctx:
            cb0 = t_len // n_ctx
            ctx_out = (
                _flash(mq, mk, mv, grp=1, tq=n_ctx, tk=n_ctx, q_blk0=cb0, nq=1, k_blk0=cb0, nk=1),
                _flash(sq, sk, sv, grp=2, tq=n_ctx, tk=n_ctx, q_blk0=cb0, nq=1, k_blk0=cb0, nk=1,
                       sink_rows=_sink_rows(lw['sink'], 2, n_ctx)),
                _flash(gq, gk, gv, grp=2, tq=n_ctx, tk=n_ctx, q_blk0=cb0, nq=1, k_blk0=cb0, nk=1))

        n_blk = nb if need_ctx else nbl
        x1, h2, idx, gate, rank, base, end = _outproj(
            xs, mod_tab, o_f, o_b, ga, lw['gdn_g'], lat, ctx_out, lw['wout'], norm2_g[l].reshape(1, d),
            lw['rw'], lw['rb'], n_blk=n_blk, nbl=nbl)

        n_tok = bsz * n_blk * TM
        assert n_tok % MOE_CHUNK == 0
        y = _moe(h2.reshape(n_tok, d), idx.reshape(n_tok, LANES), rank.reshape(n_tok, LANES),
                 gate.reshape(n_tok, LANES), base, end, exp_w_gu, exp_w_down, l,
                 shared_w_gu[l].astype(BF16), shared_w_down[l].astype(BF16))
        xs = _residual(x1, y.reshape(bsz, n_blk * TM, d), mod_tab, nbl=nbl)
    return xs[:, :t_len]
```

```python
import functools
import math

import numpy as np
import jax
import jax.numpy as jnp
from jax import lax
from jax.experimental import pallas as pl
from jax.experimental.pallas import tpu as pltpu

F32 = jnp.float32
BF16 = jnp.bfloat16

LANES = 128
TM = 256
VMEM_LIMIT = 56 * 1024 * 1024

D_MODEL = 1024
GRID_W = 64
HEAD_DIM = 64
ROPE_THETA = 10000.0
NORM_EPS = 1e-6
ATTN_SCALE = HEAD_DIM ** -0.5
GDN_HEADS = 4
GDN_DK = 64
GDN_CHUNK = 64
GDN_CONV = 5
MLA_HEADS = 4
MLA_Q_RANK = 256
MLA_KV_RANK = 128
MLA_NOPE = 64
MLA_ROPE = 32
MLA_V = 64
MLA_SCALE = (MLA_NOPE + MLA_ROPE) ** -0.5
SWA_HEADS = 4
SWA_KV_HEADS = 2
WINDOW = 128
GQA_HEADS = 4
GQA_KV_HEADS = 2
N_EXPERTS = 64
TOP_K = 6
D_EXPERT = 384
ROUTED_SCALE = 2.5
LOG2E = math.log2(math.e)
ONES_LANE = HEAD_DIM
FLASH_KB = 256
MOE_CHUNK = 512
MOE_ROWS = 128
MOE_BR = 512
MOE_SEG = 16
MOE_KSPLIT = 3

O_AQ, O_AK, O_AV, O_AZ, O_AA, O_AB = 0, 256, 512, 768, 1024, 1032
O_MQ, O_MKV, O_MKR = 1040, 1296, 1424
O_SQ, O_SK, O_SV = 1456, 1712, 1840
O_GQ, O_GK, O_GV = 1968, 2224, 2352
D_IN = 2480
C_GDN, C_MLA, C_MISC, C_SWA, C_GQA, N_COL = 0, 1024, 1408, 1536, 2560, 3584


def _cparams(sem):
    return pltpu.CompilerParams(dimension_semantics=sem, vmem_limit_bytes=VMEM_LIMIT)


def _dot(a, b):
    return jnp.dot(a, b, preferred_element_type=F32)


def _dot_nt(a, b):
    return lax.dot_general(a, b, (((1,), (1,)), ((), ())), preferred_element_type=F32)


def _dot_tn(a, b):
    return lax.dot_general(a, b, (((0,), (0,)), ((), ())), preferred_element_type=F32)


def _split3(x):
    hi = x.astype(BF16)
    r1 = x - hi.astype(F32)
    mid = r1.astype(BF16)
    lo = (r1 - mid.astype(F32)).astype(BF16)
    return hi, mid, lo


def _dot_exact_lhs(a_bf16, x):
    hi, mid, lo = _split3(x)
    return _dot(a_bf16, hi) + _dot(a_bf16, mid) + _dot(a_bf16, lo)


def _dot_exact_rhs(x, b_bf16):
    hi, mid, lo = _split3(x)
    return _dot(hi, b_bf16) + _dot(mid, b_bf16) + _dot(lo, b_bf16)


def _dot_hp(a, b):
    ah = a.astype(BF16)
    al = (a - ah.astype(F32)).astype(BF16)
    bh = b.astype(BF16)
    bl = (b - bh.astype(F32)).astype(BF16)
    return _dot(ah, bh) + _dot(ah, bl) + _dot(al, bh)


def _silu(x):
    return x * (1.0 / (1.0 + jnp.exp(-x)))


def _lane(shape):
    return lax.broadcasted_iota(jnp.int32, shape, len(shape) - 1)


def _mod_kernel(c_ref, w_ref, b_ref, o_ref):
    s = _silu(c_ref[...])
    o_ref[...] = _dot(s.astype(BF16), w_ref[...].astype(BF16)) + b_ref[...]


def _modulation(c8, w_mod_l, b_mod_l):
    d = c8.shape[1]
    n = w_mod_l.shape[1]
    return pl.pallas_call(
        _mod_kernel,
        grid=(n // d,),
        in_specs=[pl.BlockSpec((8, d), lambda j: (0, 0)),
                  pl.BlockSpec((d, d), lambda j: (0, j)),
                  pl.BlockSpec((1, d), lambda j: (0, j))],
        out_specs=pl.BlockSpec((8, d), lambda j: (0, j)),
        out_shape=jax.ShapeDtypeStruct((8, n), F32),
        compiler_params=_cparams(("arbitrary",)),
        name="modulation",
    )(c8, w_mod_l, b_mod_l.reshape(1, n))


def _rope_slab(x, cos, sin_signed, half):
    lane = _lane(x.shape)
    fwd = pltpu.roll(x, LANES - half, 1)
    bwd = pltpu.roll(x, half, 1)
    rot = jnp.where(lane % (2 * half) < half, fwd, bwd)
    return x * cos + rot * sin_signed


def _inproj_kernel(x_ref, mod_ref, g1_ref, win_ref, qn_g_ref, kvn_g_ref, wuq_ref, wukv_ref,
                   mqg_ref, mkg_ref, krg_ref, sqg_ref, skg_ref, gqg_ref, gkg_ref,
                   cos_hd_ref, sin_hd_ref, cos_mq_ref, sin_mq_ref, cos_kr_ref, sin_kr_ref,
                   ga_ref, misc_ref, mq_ref, mk_ref, mv_ref, sq_ref, sk_ref, sv_ref,
                   gq_ref, gk_ref, gv_ref):
    d = D_MODEL
    x = x_ref[0]
    m = mod_ref[0, 0]
    shift, scale = m[:, 0:d], m[:, d:2 * d]
    xn = x * lax.rsqrt(jnp.mean(x * x, axis=-1, keepdims=True) + NORM_EPS) * g1_ref[...]
    h = xn * (1.0 + scale) + shift
    hb = h.astype(BF16)

    def proj(c0, width):
        return _dot(hb, win_ref[:, c0:c0 + width])

    ga_ref[0] = proj(C_GDN, 1024)
    pm = proj(C_MLA, C_SWA - C_MLA)
    misc = pm[:, C_MISC - C_MLA:C_MISC - C_MLA + LANES]
    misc_ref[0] = misc

    lane = _lane((TM, LANES))
    cos_hd, sin_hd = cos_hd_ref[...], sin_hd_ref[...]

    cq = pm[:, 0:MLA_Q_RANK]
    cqn = cq * lax.rsqrt(jnp.mean(cq * cq, axis=-1, keepdims=True) + NORM_EPS) * qn_g_ref[...]
    qup = _dot(cqn.astype(BF16), wuq_ref[...])
    ckv = pm[:, MLA_Q_RANK:MLA_Q_RANK + MLA_KV_RANK]
    ckvn = ckv * lax.rsqrt(jnp.mean(ckv * ckv, axis=-1, keepdims=True) + NORM_EPS) * kvn_g_ref[...]
    kvup = _dot(ckvn.astype(BF16), wukv_ref[...])
    kr = jnp.where(lane < MLA_ROPE, misc, 0.0)
    kr = kr * lax.rsqrt(jnp.sum(kr * kr, axis=-1, keepdims=True) / MLA_ROPE + NORM_EPS) * krg_ref[...]
    kr = _rope_slab(kr, cos_kr_ref[...], sin_kr_ref[...], MLA_ROPE // 4)
    kpe = pltpu.roll(kr, MLA_NOPE, 1)
    is_nope = lane < MLA_NOPE
    for hh in range(MLA_HEADS):
        q = qup[:, hh * LANES:(hh + 1) * LANES]
        q2 = q * q
        ss_n = jnp.sum(jnp.where(is_nope, q2, 0.0), axis=-1, keepdims=True)
        ss_p = jnp.sum(jnp.where(is_nope, 0.0, q2), axis=-1, keepdims=True)
        r = jnp.where(is_nope, lax.rsqrt(ss_n / MLA_NOPE + NORM_EPS), lax.rsqrt(ss_p / MLA_ROPE + NORM_EPS))
        q = q * r * mqg_ref[...]
        q = _rope_slab(q, cos_mq_ref[...], sin_mq_ref[...], MLA_ROPE // 4)
        mq_ref[0, hh] = (q * (MLA_SCALE * LOG2E)).astype(BF16)
        k = kvup[:, hh * LANES:(hh + 1) * LANES]
        k = k * lax.rsqrt(jnp.sum(k * k, axis=-1, keepdims=True) / MLA_NOPE + NORM_EPS) * mkg_ref[...]
        mk_ref[0, hh] = (k + kpe).astype(BF16)
        v = kvup[:, (MLA_HEADS + hh) * LANES:(MLA_HEADS + hh + 1) * LANES]
        mv_ref[0, hh] = jnp.where(lane == ONES_LANE, 1.0, v).astype(BF16)

    def gqa_prep(base, qg_ref, kg_ref, q_ref, k_ref, v_ref, nq, nkv):
        p = proj(base, (nq + 2 * nkv) * LANES)
        base = 0
        for hh in range(nq):
            q = p[:, base + hh * LANES:base + (hh + 1) * LANES]
            q = q * lax.rsqrt(jnp.sum(q * q, axis=-1, keepdims=True) / HEAD_DIM + NORM_EPS) * qg_ref[...]
            q = _rope_slab(q, cos_hd, sin_hd, HEAD_DIM // 4) * (ATTN_SCALE * LOG2E)
            q_ref[0, hh] = q.astype(BF16)
        for hh in range(nkv):
            k = p[:, base + (nq + hh) * LANES:base + (nq + hh + 1) * LANES]
            k = k * lax.rsqrt(jnp.sum(k * k, axis=-1, keepdims=True) / HEAD_DIM + NORM_EPS) * kg_ref[...]
            k = _rope_slab(k, cos_hd, sin_hd, HEAD_DIM // 4)
            k_ref[0, hh] = k.astype(BF16)
            v = p[:, base + (nq + nkv + hh) * LANES:base + (nq + nkv + hh + 1) * LANES]
            v_ref[0, hh] = jnp.where(lane == ONES_LANE, 1.0, v).astype(BF16)

    gqa_prep(C_SWA, sqg_ref, skg_ref, sq_ref, sk_ref, sv_ref, SWA_HEADS, SWA_KV_HEADS)
    gqa_prep(C_GQA, gqg_ref, gkg_ref, gq_ref, gk_ref, gv_ref, GQA_HEADS, GQA_KV_HEADS)


def _inproj(x, mod_tab, g1, win, lw, tabs, nbl):
    bsz, s, d = x.shape
    nb = s // TM
    row = lambda a: a.reshape(1, -1)
    full = lambda a: pl.BlockSpec(a.shape, lambda b, i: (0,) * a.ndim)
    tab_spec = pl.BlockSpec((TM, LANES), lambda b, i: (i, 0))
    head_out = lambda nh: pl.BlockSpec((1, nh, TM, LANES), lambda b, i: (b, 0, i, 0))
    head_shape = lambda nh: jax.ShapeDtypeStruct((bsz, nh, s, LANES), BF16)
    small = [row(lw['mla_qn_g']), row(lw['mla_kvn_g']), lw['wuq'], lw['wukv'],
             lw['mqg'], lw['mkg'], lw['krg'], lw['sqg'], lw['skg'], lw['gqg'], lw['gkg']]
    return pl.pallas_call(
        _inproj_kernel,
        grid=(bsz, nb),
        in_specs=[pl.BlockSpec((1, TM, d), lambda b, i: (b, i, 0)),
                  pl.BlockSpec((1, 1, 1, 6 * d), lambda b, i: (b, i // nbl, 0, 0)),
                  full(g1), full(win)] + [full(a) for a in small] + [tab_spec] * 6,
        out_specs=[pl.BlockSpec((1, TM, 1024), lambda b, i: (b, i, 0)),
                   pl.BlockSpec((1, TM, LANES), lambda b, i: (b, i, 0)),
                   head_out(4), head_out(4), head_out(4),
                   head_out(4), head_out(2), head_out(2),
                   head_out(4), head_out(2), head_out(2)],
        out_shape=[jax.ShapeDtypeStruct((bsz, s, 1024), F32),
                   jax.ShapeDtypeStruct((bsz, s, LANES), F32),
                   head_shape(4), head_shape(4), head_shape(4),
                   head_shape(4), head_shape(2), head_shape(2),
                   head_shape(4), head_shape(2), head_shape(2)],
        compiler_params=_cparams(("parallel", "parallel")),
        name="inproj",
    )(x, mod_tab, g1, win, *small, *tabs)


def _flash_kernel(*refs, grp, tq, nk, use_sink):
    refs = list(refs)
    sink_ref = refs.pop(0) if use_sink else None
    q_ref, k_ref, v_ref = refs[0], refs[1], refs[2]
    o_ref, m_sc, acc_sc = refs[-3], refs[-2], refs[-1]
    kj = pl.program_id(3)
    tk = k_ref.shape[2]
    kb = min(FLASH_KB, tk)

    @pl.when(kj == 0)
    def _():
        m_sc[...] = jnp.full(m_sc.shape, -jnp.inf, F32)
        acc_sc[...] = jnp.zeros(acc_sc.shape, F32)

    q = q_ref[0].reshape(grp * tq, LANES)
    m = m_sc[...]
    acc = acc_sc[...]
    for j in range(tk // kb):
        s = _dot_nt(q, k_ref[0, 0, j * kb:(j + 1) * kb, :])
        m_new = jnp.maximum(m, jnp.max(s, axis=-1, keepdims=True))
        alpha = jnp.exp2(m - m_new)
        pr = jnp.exp2(s - jnp.concatenate([m_new] * (kb // LANES), axis=1))
        acc = alpha * acc + _dot(pr.astype(BF16), v_ref[0, 0, j * kb:(j + 1) * kb, :])
        m = m_new
    m_sc[...] = m
    acc_sc[...] = acc

    @pl.when(kj == nk - 1)
    def _():
        l = acc[:, ONES_LANE:ONES_LANE + 1]
        out = acc
        if use_sink:
            sk = sink_ref[0]
            m_old = m[:, 0:1]
            m_fin = jnp.maximum(m_old, sk)
            a = jnp.exp2(m_old - m_fin)
            l = l * a + jnp.exp2(sk - m_fin)
            out = out * a
        o_ref[0] = (out / l).reshape(grp, tq, LANES).astype(o_ref.dtype)


def _flash(q, k, v, *, grp, tq, tk, q_blk0, nq, k_blk0, nk, sink_rows=None):
    bsz, hq, s, _ = q.shape
    hkv = hq // grp
    use_sink = sink_rows is not None
    in_specs = []
    args = []
    if use_sink:
        in_specs.append(pl.BlockSpec((1, grp * tq, 1), lambda b, h, i, j: (h, 0, 0)))
        args.append(sink_rows)
    in_specs += [pl.BlockSpec((1, grp, tq, LANES), lambda b, h, i, j: (b, h, i + q_blk0, 0)),
                 pl.BlockSpec((1, 1, tk, LANES), lambda b, h, i, j: (b, h, j + k_blk0, 0)),
                 pl.BlockSpec((1, 1, tk, LANES), lambda b, h, i, j: (b, h, j + k_blk0, 0))]
    args += [q, k, v]
    rows = grp * tq
    return pl.pallas_call(
        functools.partial(_flash_kernel, grp=grp, tq=tq, nk=nk, use_sink=use_sink),
        grid=(bsz, hkv, nq, nk),
        in_specs=in_specs,
        out_specs=pl.BlockSpec((1, grp, tq, LANES), lambda b, h, i, j: (b, h, i, 0)),
        out_shape=jax.ShapeDtypeStruct((bsz, hq, nq * tq, LANES), BF16),
        scratch_shapes=[pltpu.VMEM((rows, LANES), F32), pltpu.VMEM((rows, LANES), F32)],
        compiler_params=_cparams(("parallel", "parallel", "parallel", "arbitrary")),
        name="flash",
    )(*args)


def _swa_kernel(sink_ref, q_ref, k_ref, v_ref, o_ref, *, grp, tq, t_len, n_ctx):
    w = WINDOW
    n = pl.program_id(2)
    kc = k_ref[0, 0, pl.ds(t_len, n_ctx), :]
    vc = v_ref[0, 0, pl.ds(t_len, n_ctx), :]
    for qb in range(tq // w):
        q0 = n * tq + qb * w
        start = pl.multiple_of(jnp.clip(q0 - w, 0, t_len - 3 * w), w)
        q = q_ref[0, :, qb * w:(qb + 1) * w, :].reshape(grp * w, LANES)
        kl = k_ref[0, 0, pl.ds(start, 3 * w), :]
        vl = v_ref[0, 0, pl.ds(start, 3 * w), :]
        s_loc = _dot_nt(q, kl)
        qpos = q0 + (lax.broadcasted_iota(jnp.int32, s_loc.shape, 0) % w)
        kpos = start + lax.broadcasted_iota(jnp.int32, s_loc.shape, 1)
        s_loc = jnp.where(jnp.abs(qpos - kpos) <= w, s_loc, -jnp.inf)
        s_ctx = _dot_nt(q, kc)
        sk = sink_ref[0]
        m = jnp.maximum(jnp.maximum(jnp.max(s_loc, axis=-1, keepdims=True),
                                    jnp.max(s_ctx, axis=-1, keepdims=True)), sk)
        p_loc = jnp.exp2(s_loc - m)
        p_ctx = jnp.exp2(s_ctx - m)
        o = _dot(p_loc.astype(BF16), vl) + _dot(p_ctx.astype(BF16), vc)
        l = o[:, ONES_LANE:ONES_LANE + 1] + jnp.exp2(sk - m)
        o_ref[0, :, qb * w:(qb + 1) * w, :] = (o / l).reshape(grp, w, LANES).astype(o_ref.dtype)


def _swa(q, k, v, sink_rows, *, grp, tq, t_len, n_ctx):
    bsz, hq, s, _ = q.shape
    hkv = hq // grp
    return pl.pallas_call(
        functools.partial(_swa_kernel, grp=grp, tq=tq, t_len=t_len, n_ctx=n_ctx),
        grid=(bsz, hkv, t_len // tq),
        in_specs=[pl.BlockSpec((1, grp * WINDOW, 1), lambda b, h, n: (h, 0, 0)),
                  pl.BlockSpec((1, grp, tq, LANES), lambda b, h, n: (b, h, n, 0)),
                  pl.BlockSpec((1, 1, s, LANES), lambda b, h, n: (b, h, 0, 0)),
                  pl.BlockSpec((1, 1, s, LANES), lambda b, h, n: (b, h, 0, 0))],
        out_specs=pl.BlockSpec((1, grp, tq, LANES), lambda b, h, n: (b, h, n, 0)),
        out_shape=jax.ShapeDtypeStruct((bsz, hq, t_len, LANES), BF16),
        compiler_params=_cparams(("parallel", "parallel", "arbitrary")),
        name="swa",
    )(sink_rows, q, k, v)


def _gdn_prep_kernel(cur_ref, prev_ref, next_ref, misc_ref, cw_ref, alog_ref, dtb_ref,
                     qkv_ref, lab_ref, *, nbl, nb):
    i = pl.program_id(1)
    first = jnp.logical_or(i == 0, i == nbl)
    last = jnp.logical_or(i == nbl - 1, i == nb - 1)
    cur = cur_ref[0]
    prev = jnp.where(first, 0.0, prev_ref[0])
    nxt = jnp.where(last, 0.0, next_ref[0])
    xe = jnp.concatenate([prev, cur, nxt], axis=0)
    cw = cw_ref[...]
    acc = jnp.zeros(cur.shape, F32)
    for j in range(GDN_CONV):
        off = 8 + j - (GDN_CONV - 1) // 2
        acc = acc + xe[off:off + TM, :] * cw[j:j + 1, :]
    y = _silu(acc)
    lane = _lane((TM, LANES))
    lo = lane < GDN_DK
    for sl in range(6):
        t = y[:, sl * LANES:(sl + 1) * LANES]
        if sl < 4:
            t2 = t * t
            ss0 = jnp.sum(jnp.where(lo, t2, 0.0), axis=-1, keepdims=True)
            ss1 = jnp.sum(jnp.where(lo, 0.0, t2), axis=-1, keepdims=True)
            t = t * jnp.where(lo, lax.rsqrt(ss0 + NORM_EPS), lax.rsqrt(ss1 + NORM_EPS))
            if sl < 2:
                t = t * GDN_DK ** -0.5
        qkv_ref[0, :, sl * LANES:(sl + 1) * LANES] = t
    xm = misc_ref[0]
    za = xm + dtb_ref[...]
    softplus = jnp.maximum(za, 0.0) + jnp.log(1.0 + jnp.exp(-jnp.abs(za)))
    log_a = -jnp.exp(alog_ref[...]) * softplus
    beta = 1.0 / (1.0 + jnp.exp(-xm))
    is_a = (lane % 8) < 4
    vals = jnp.where(is_a, log_a, beta)
    lab_ref[0, 0] = pltpu.roll(vals, LANES - 32, 1)
    lab_ref[0, 1] = pltpu.roll(vals, LANES - 40, 1)


def _gdn_prep(ga, misc, cw, alog_row, dtb_row, nbl):
    bsz, s, _ = ga.shape
    nb = s // TM
    r8 = TM // 8
    n8 = s // 8
    return pl.pallas_call(
        functools.partial(_gdn_prep_kernel, nbl=nbl, nb=nb),
        grid=(bsz, nb),
        in_specs=[pl.BlockSpec((1, TM, 768), lambda b, i: (b, i, 0)),
                  pl.BlockSpec((1, 8, 768), lambda b, i: (b, jnp.maximum(i * r8 - 1, 0), 0)),
                  pl.BlockSpec((1, 8, 768), lambda b, i: (b, jnp.minimum((i + 1) * r8, n8 - 1), 0)),
                  pl.BlockSpec((1, TM, LANES), lambda b, i: (b, i, 0)),
                  pl.BlockSpec(cw.shape, lambda b, i: (0, 0)),
                  pl.BlockSpec((1, LANES), lambda b, i: (0, 0)),
                  pl.BlockSpec((1, LANES), lambda b, i: (0, 0))],
        out_specs=[pl.BlockSpec((1, TM, 768), lambda b, i: (b, i, 0)),
                   pl.BlockSpec((1, 2, TM, LANES), lambda b, i: (b, 0, i, 0))],
        out_shape=[jax.ShapeDtypeStruct((bsz, s, 768), F32),
                   jax.ShapeDtypeStruct((bsz, 2, s, LANES), F32)],
        compiler_params=_cparams(("parallel", "parallel")),
        name="gdn_prep",
    )(ga, ga, ga, misc, cw, alog_row, dtb_row)


def _gdn_scan_kernel(qkvf_ref, labf_ref, labtf_ref, qkvb_ref, labb_ref, labtb_ref, of_ref, ob_ref, s_sc):
    c = GDN_CHUNK
    c2 = 2 * c
    i = pl.program_id(1)

    @pl.when(i == 0)
    def _():
        s_sc[...] = jnp.zeros(s_sc.shape, F32)

    ri = lax.broadcasted_iota(jnp.int32, (c2, c2), 0)
    ci = lax.broadcasted_iota(jnp.int32, (c2, c2), 1)
    same = (ri // c) == (ci // c)
    eye = (ri == ci).astype(F32)
    lane = _lane((c, LANES))
    lo = lane < GDN_DK

    def stack(x):
        return jnp.concatenate([jnp.where(lo, x, 0.0), jnp.where(lo, 0.0, x)], axis=0)

    def fold(mat):
        return mat[0:c, :] + mat[c:c2, :]

    inst_f = _gdn_instances(qkvf_ref, labf_ref, labtf_ref, of_ref, 0, False, stack, fold)
    inst_b = _gdn_instances(qkvb_ref, labb_ref, labtb_ref, ob_ref, 2, True, stack, fold)
    inst = [it for pair in zip(inst_f, inst_b) for it in pair]
    tinv = [eye - it['a'] for it in inst]
    pw = [it['a'] for it in inst]
    for _ in range(5):
        pw = [_dot_hp(x, x) for x in pw]
        tinv = [t + _dot_hp(t, x) for t, x in zip(tinv, pw)]
    eye_b = eye.astype(BF16)
    for it, t in zip(inst, tinv):
        tf = fold(t).astype(BF16)
        it['u'] = _dot(tf, it['vb'])
        it['wq'] = jnp.concatenate([_dot(tf, it['kg']).astype(BF16), it['qg']], axis=0)
        it['kdt'] = _dot_nt(eye_b, it['kd']).astype(BF16)
    state = [s_sc[j] for j in range(4)]
    for it in inst:
        st = state[it['slot']]
        ws = _dot(it['wq'], st.astype(BF16))
        v_new = it['u'] - ws[0:c, :]
        o = ws[c:c2, :] + _dot(it['qkf'], stack(v_new).astype(BF16))
        it['o_ref'][0, it['rows'], it['pr'] * LANES:(it['pr'] + 1) * LANES] = o
        upd = _dot(it['kdt'], v_new.astype(BF16))
        state[it['slot']] = st * it['sdec'] + jnp.where(same, upd, 0.0)
    for j in range(4):
        s_sc[j] = state[j]


def _gdn_instances(qkv_ref, lab_ref, labt_ref, o_ref, slot0, reverse, stack, fold):
    c = GDN_CHUNK
    c2 = 2 * c
    ri = lax.broadcasted_iota(jnp.int32, (c2, c2), 0)
    ci = lax.broadcasted_iota(jnp.int32, (c2, c2), 1)
    same = (ri // c) == (ci // c)
    if reverse:
        incl = jnp.logical_and(same, ri <= ci)
        strict = jnp.logical_and(same, ri < ci)
    else:
        incl = jnp.logical_and(same, ri >= ci)
        strict = jnp.logical_and(same, ri > ci)
    r1 = lax.broadcasted_iota(jnp.int32, (c, c), 0)
    c1 = lax.broadcasted_iota(jnp.int32, (c, c), 1)
    cum_col = ((r1 <= c1) if reverse else (r1 >= c1)).astype(BF16)
    rr = lax.broadcasted_iota(jnp.int32, (c, c2), 0)
    cc = lax.broadcasted_iota(jnp.int32, (c, c2), 1) % c
    cum_row = ((rr >= cc) if reverse else (rr <= cc)).astype(BF16)
    lane = _lane((c, LANES))
    lo = lane < GDN_DK
    lane2 = _lane((1, c2))
    lo2 = lane2 < c
    g_last_row = 0 if reverse else c - 1
    order = list(range(TM // c - 1, -1, -1) if reverse else range(TM // c))
    inst = []
    for ch in order:
        rows = slice(ch * c, (ch + 1) * c)
        lab = lab_ref[0, 0, rows, :]
        g_cols = _dot_exact_lhs(cum_col, lab)
        la_rows = labt_ref[0, 0, :, rows]
        g_rows = _dot_exact_rhs(la_rows, cum_row)
        for pr in range(2):
            h0, h1 = 2 * pr, 2 * pr + 1
            q = qkv_ref[0, rows, pr * LANES:(pr + 1) * LANES]
            k = qkv_ref[0, rows, (2 + pr) * LANES:(3 + pr) * LANES]
            v = qkv_ref[0, rows, (4 + pr) * LANES:(5 + pr) * LANES]
            beta = jnp.where(lo, lab[:, 4 + h0:5 + h0], lab[:, 4 + h1:5 + h1])
            g = jnp.where(lo, g_cols[:, h0:h0 + 1], g_cols[:, h1:h1 + 1])
            g_st_col = jnp.concatenate([g_cols[:, h0:h0 + 1], g_cols[:, h1:h1 + 1]], axis=0)
            g_st_row = jnp.where(lo2, g_rows[h0:h0 + 1, :], g_rows[h1:h1 + 1, :])
            diff = g_st_col - g_st_row
            dec = jnp.where(incl, jnp.exp(jnp.where(incl, diff, 0.0)), 0.0)
            kb = k * beta
            eg = jnp.exp(g)
            k_st = stack(k).astype(BF16)
            a = jnp.where(strict, _dot_nt(stack(kb).astype(BF16), k_st) * dec, 0.0)
            qk = _dot_nt(stack(q).astype(BF16), k_st) * dec
            gl = g[g_last_row:g_last_row + 1, :]
            gl_col = jnp.concatenate([jnp.broadcast_to(g_cols[g_last_row:g_last_row + 1, h0:h0 + 1], (c, 1)),
                                      jnp.broadcast_to(g_cols[g_last_row:g_last_row + 1, h1:h1 + 1], (c, 1))],
                                     axis=0)
            inst.append(dict(rows=rows, pr=pr, slot=slot0 + pr, o_ref=o_ref, a=a,
                             vb=stack(v * beta).astype(BF16),
                             kg=stack(kb * eg).astype(BF16), qg=(q * eg).astype(BF16),
                             qkf=fold(qk).astype(BF16), kd=(k * jnp.exp(gl - g)).astype(BF16),
                             sdec=jnp.exp(gl_col)))
    return inst


def _gdn_scan(qkv, lab, labt, *, nbl):
    bsz, s, _ = qkv.shape
    nb = s // TM
    nbc = nb - nbl

    def blk_f(i):
        return jnp.where(i < nbc, nbl + i, i - nbc)

    def blk_b(i):
        return jnp.where(i < nbc, nb - 1 - i, nbl - 1 - (i - nbc))

    def specs(blk, dsel):
        return [pl.BlockSpec((1, TM, 768), lambda b, i: (b, blk(i), 0)),
                pl.BlockSpec((1, 1, TM, LANES), lambda b, i: (b, dsel, blk(i), 0)),
                pl.BlockSpec((1, 1, 8, TM), lambda b, i: (b, dsel, 0, blk(i)))]

    out = jax.ShapeDtypeStruct((bsz, s, 2 * LANES), F32)
    return pl.pallas_call(
        _gdn_scan_kernel,
        grid=(bsz, nb),
        in_specs=specs(blk_f, 0) + specs(blk_b, 1),
        out_specs=[pl.BlockSpec((1, TM, 2 * LANES), lambda b, i: (b, blk_f(i), 0)),
                   pl.BlockSpec((1, TM, 2 * LANES), lambda b, i: (b, blk_b(i), 0))],
        out_shape=[out, out],
        scratch_shapes=[pltpu.VMEM((4, LANES, LANES), F32)],
        compiler_params=_cparams(("parallel", "arbitrary")),
        name="gdn_scan",
    )(qkv, lab, labt, qkv, lab, labt)


def _outproj_kernel(x_ref, mod_ref, of_ref, ob_ref, z_ref, gg_ref, mo_ref, so_ref, go_ref,
                    mc_ref, sc_ref, gc_ref, wout_ref, g2_ref, rw_ref, rb_ref,
                    x1_ref, h2_ref, idx_ref, gate_ref, rank_ref, base_ref, end_ref, run_sc, *, nbl):
    d = D_MODEL
    first = jnp.logical_and(pl.program_id(0) == 0, pl.program_id(1) == 0)
    is_ctx = pl.program_id(1) >= nbl

    @pl.when(first)
    def _():
        run_sc[...] = jnp.zeros(run_sc.shape, F32)

    m = mod_ref[0, 0]
    lane = _lane((TM, LANES))
    lo = lane < HEAD_DIM
    o = of_ref[0] + ob_ref[0]
    z = z_ref[0]
    mix = []
    for pr in range(2):
        t = o[:, pr * LANES:(pr + 1) * LANES]
        t2 = t * t
        ms0 = jnp.sum(jnp.where(lo, t2, 0.0), axis=-1, keepdims=True) / HEAD_DIM
        ms1 = jnp.sum(jnp.where(lo, 0.0, t2), axis=-1, keepdims=True) / HEAD_DIM
        t = t * jnp.where(lo, lax.rsqrt(ms0 + NORM_EPS), lax.rsqrt(ms1 + NORM_EPS)) * gg_ref[...]
        t = t * _silu(z[:, pr * LANES:(pr + 1) * LANES])
        mix.append(t.astype(BF16))
    for lat_ref, ctx_ref in ((mo_ref, mc_ref), (so_ref, sc_ref), (go_ref, gc_ref)):
        for hh in range(4):
            mix.append(jnp.where(is_ctx, ctx_ref[0, hh], lat_ref[0, hh]))
    proj = _dot(jnp.concatenate(mix, axis=1), wout_ref[...])
    x1 = x_ref[0] + m[:, 2 * d:3 * d] * proj
    x1_ref[0] = x1
    xn = x1 * lax.rsqrt(jnp.mean(x1 * x1, axis=-1, keepdims=True) + NORM_EPS) * g2_ref[...]
    h2 = xn * (1.0 + m[:, 4 * d:5 * d]) + m[:, 3 * d:4 * d]
    h2_ref[0] = h2.astype(BF16)

    logits = _dot_hp(h2, rw_ref[...])
    scores = 1.0 / (1.0 + jnp.exp(-logits))
    valid = lane < N_EXPERTS
    sel = jnp.where(valid, scores + rb_ref[...], -jnp.inf)
    member = jnp.zeros((TM, LANES), F32)
    picks = []
    for _ in range(TOP_K):
        mx = jnp.max(sel, axis=-1, keepdims=True)
        idx = jnp.min(jnp.where(sel == mx, lane, LANES), axis=-1, keepdims=True)
        hit = lane == idx
        gate = jnp.sum(jnp.where(hit, scores, 0.0), axis=-1, keepdims=True)
        sel = jnp.where(hit, -jnp.inf, sel)
        member = member + hit.astype(F32)
        picks.append((idx, hit, gate))
    gsum = picks[0][2]
    for kk in range(1, TOP_K):
        gsum = gsum + picks[kk][2]
    ri = lax.broadcasted_iota(jnp.int32, (TM, TM), 0)
    ci = lax.broadcasted_iota(jnp.int32, (TM, TM), 1)
    before = (ri > ci).astype(BF16)
    run = run_sc[...]
    base_ref[0] = run
    cum = _dot(before, member.astype(BF16)) + run
    idx_out = jnp.full((TM, LANES), -1, jnp.int32)
    gate_out = jnp.zeros((TM, LANES), F32)
    rank_out = jnp.zeros((TM, LANES), F32)
    for kk, (idx, hit, gate) in enumerate(picks):
        rank = jnp.sum(jnp.where(hit, cum, 0.0), axis=-1, keepdims=True)
        here = lane == kk
        idx_out = jnp.where(here, idx, idx_out)
        gate_out = jnp.where(here, gate / gsum * ROUTED_SCALE, gate_out)
        rank_out = jnp.where(here, rank, rank_out)
    idx_ref[0] = idx_out
    gate_ref[0] = gate_out
    rank_ref[0] = rank_out
    run_end = run + jnp.sum(member, axis=0, keepdims=True)
    end_ref[0] = run_end
    run_sc[...] = run_end


def _outproj(x, mod_tab, o_f, o_b, ga, gg_row, lat, ctx, wout, g2, rw, rb, *, n_blk, nbl):
    bsz, _, d = x.shape
    s_out = n_blk * TM
    full = lambda a: pl.BlockSpec(a.shape, lambda b, i: (0,) * a.ndim)
    tok = lambda w: pl.BlockSpec((1, TM, w), lambda b, i: (b, i, 0))
    head = pl.BlockSpec((1, 4, TM, LANES), lambda b, i: (b, 0, jnp.minimum(i, nbl - 1), 0))
    head_c = pl.BlockSpec((1, 4, TM, LANES), lambda b, i: (b, 0, jnp.maximum(i - nbl, 0), 0))
    return pl.pallas_call(
        functools.partial(_outproj_kernel, nbl=nbl),
        grid=(bsz, n_blk),
        in_specs=[tok(d),
                  pl.BlockSpec((1, 1, 1, 6 * d), lambda b, i: (b, i // nbl, 0, 0)),
                  tok(2 * LANES), tok(2 * LANES),
                  pl.BlockSpec((1, TM, 2 * LANES), lambda b, i: (b, i, 3)),
                  full(gg_row), head, head, head, head_c, head_c, head_c,
                  full(wout), full(g2), full(rw), full(rb)],
        out_specs=[tok(d), tok(d), tok(LANES), tok(LANES), tok(LANES),
                   pl.BlockSpec((1, 1, LANES), lambda b, i: (b * n_blk + i, 0, 0)),
                   pl.BlockSpec((1, 1, LANES), lambda b, i: (b * n_blk + i, 0, 0))],
        out_shape=[jax.ShapeDtypeStruct((bsz, s_out, d), F32),
                   jax.ShapeDtypeStruct((bsz, s_out, d), BF16),
                   jax.ShapeDtypeStruct((bsz, s_out, LANES), jnp.int32),
                   jax.ShapeDtypeStruct((bsz, s_out, LANES), F32),
                   jax.ShapeDtypeStruct((bsz, s_out, LANES), F32),
                   jax.ShapeDtypeStruct((bsz * n_blk, 1, LANES), F32),
                   jax.ShapeDtypeStruct((bsz * n_blk, 1, LANES), F32)],
        scratch_shapes=[pltpu.VMEM((1, LANES), F32)],
        compiler_params=_cparams(("arbitrary", "arbitrary")),
        name="outproj_router",
    )(x, mod_tab, o_f, o_b, ga, gg_row, *lat, *ctx, wout, g2, rw, rb)


def _moe_tables_kernel(lo_ref, hi_ref, offl_ref, seg_ref, gbase_ref, be_ref, nused_ref):
    ncp = lo_ref.shape[0]
    nblkp = be_ref.shape[0]
    seg = jnp.ceil((hi_ref[...] - lo_ref[...]) * (1.0 / MOE_SEG)) * MOE_SEG
    ri = lax.broadcasted_iota(jnp.int32, (LANES, LANES), 0)
    ci = lax.broadcasted_iota(jnp.int32, (LANES, LANES), 1)
    before_lane = (ri < ci).astype(BF16)
    rc = lax.broadcasted_iota(jnp.int32, (ncp, ncp), 0)
    cc = lax.broadcasted_iota(jnp.int32, (ncp, ncp), 1)
    before_row = (rc > cc).astype(BF16)
    seg_ref[...] = seg
    offl_ref[...] = _dot_exact_rhs(seg, before_lane)
    region = jnp.ceil(jnp.sum(seg, axis=0, keepdims=True) * (1.0 / MOE_BR)) * MOE_BR
    goff = _dot_exact_rhs(jnp.broadcast_to(region, (8, LANES)), before_lane)[0:1]
    gbase_ref[...] = goff + _dot_exact_lhs(before_row, seg)
    lane = _lane((nblkp, LANES))
    gend = jnp.where(lane < N_EXPERTS, goff + region, jnp.inf)
    row0 = (lax.broadcasted_iota(jnp.int32, (nblkp, LANES), 0) * MOE_BR).astype(F32)
    be = jnp.sum((gend <= row0).astype(F32), axis=-1, keepdims=True)
    be_ref[...] = jnp.broadcast_to(jnp.minimum(be, N_EXPERTS - 1.0), (nblkp, LANES))
    nused_ref[...] = jnp.broadcast_to(jnp.sum(region, axis=-1, keepdims=True) * (1.0 / MOE_BR), (1, LANES))


def _moe_tables(cb_lo, cb_hi, nblk):
    ncp = cb_lo.shape[0]
    nblkp = -(-nblk // 8) * 8
    tab = jax.ShapeDtypeStruct((ncp, LANES), F32)
    return pl.pallas_call(
        _moe_tables_kernel,
        out_shape=[tab, tab, tab, jax.ShapeDtypeStruct((nblkp, LANES), F32),
                   jax.ShapeDtypeStruct((1, LANES), F32)],
        name="moe_tables",
    )(cb_lo, cb_hi)


def _segment_copies(c, offl_s, seg_s, gbase_s, local_ref, glob_ref, sem, *, to_global, wait):
    def piece(lo, go, size):
        lo = pl.multiple_of(lo, MOE_SEG)
        go = pl.multiple_of(go, MOE_SEG)
        loc = local_ref.at[pl.ds(lo, size)]
        glo = glob_ref.at[pl.ds(go, size)]
        cp = pltpu.make_async_copy(loc, glo, sem) if to_global else pltpu.make_async_copy(glo, loc, sem)
        if wait:
            cp.wait()
        else:
            cp.start()

    def expert_body(e, carry):
        t = c * N_EXPERTS + e
        off, sg, gb = offl_s[t], seg_s[t], gbase_s[t]
        n_full = sg // MOE_ROWS

        def full_body(w, carry2):
            piece(off + w * MOE_ROWS, gb + w * MOE_ROWS, MOE_ROWS)
            return carry2

        lax.fori_loop(0, n_full, full_body, 0)
        done = n_full * MOE_ROWS
        size = MOE_ROWS // 2
        while size >= MOE_SEG:
            bit = sg & size

            @pl.when(bit != 0)
            def _(done=done, size=size):
                piece(off + done, gb + done, size)

            done = done + bit
            size //= 2
        return carry

    lax.fori_loop(0, N_EXPERTS, expert_body, 0)


def _moe_sort_kernel(offl_s, seg_s, gbase_s, h_ref, idx_ref, rank_ref, cb_ref, offv_ref, xs_in_ref,
                     pos_ref, xs_ref, xy_sc, sem):
    del xs_in_ref
    c = pl.program_id(0)
    tc = h_ref.shape[0]
    lane_t = _lane((tc, LANES))
    h = h_ref[...]
    idx = idx_ref[...]
    rank = rank_ref[...]
    adj = offv_ref[0] - cb_ref[0]
    pos = jnp.full((tc, LANES), -1.0, F32)
    for k in range(TOP_K):
        hit = lane_t == idx[:, k:k + 1]
        p_k = rank[:, k:k + 1] + jnp.sum(jnp.where(hit, adj, 0.0), axis=-1, keepdims=True)
        pos = jnp.where(lane_t == k, p_k, pos)
    pos_ref[...] = pos
    sel8 = (lax.broadcasted_iota(jnp.int32, (8, LANES), 0)
            == lax.broadcasted_iota(jnp.int32, (8, LANES), 1)).astype(BF16)
    hi, mid, lo = _split3(pos)
    pos_row = _dot_nt(sel8, hi) + _dot_nt(sel8, mid) + _dot_nt(sel8, lo)

    def gather_body(rb, carry):
        for u in range(2):
            r0 = pl.multiple_of((2 * rb + u) * MOE_ROWS, MOE_ROWS)
            rid = (lax.broadcasted_iota(jnp.int32, (MOE_ROWS, tc), 0) + r0).astype(F32)
            onehot = jnp.zeros((MOE_ROWS, tc), F32)
            for k in range(TOP_K):
                onehot = jnp.where(pos_row[k:k + 1, :] == rid, 1.0, onehot)
            xy_sc[pl.ds(r0, MOE_ROWS), :] = _dot(onehot.astype(BF16), h).astype(BF16)
        return carry

    last = c * N_EXPERTS + N_EXPERTS - 1
    total = offl_s[last] + seg_s[last]
    lax.fori_loop(0, (total + 2 * MOE_ROWS - 1) // (2 * MOE_ROWS), gather_body, 0)
    _segment_copies(c, offl_s, seg_s, gbase_s, xy_sc, xs_ref, sem, to_global=True, wait=False)
    _segment_copies(c, offl_s, seg_s, gbase_s, xy_sc, xs_ref, sem, to_global=True, wait=True)


def _moe_ffn_kernel(be_s, nused_s, x_ref, wgu_ref, wdn_ref, y_ref, wgu_sc, wdn_sc):
    i = pl.program_id(0)
    used = i < nused_s[0]
    changed = jnp.logical_or(i == 0, be_s[i] != be_s[jnp.maximum(i - 1, 0)])

    @pl.when(jnp.logical_and(used, changed))
    def _():
        wgu_sc[...] = wgu_ref[0, 0].astype(BF16)
        wdn_sc[...] = wdn_ref[0, 0].astype(BF16)

    @pl.when(used)
    def _():
        starts = range(0, MOE_BR, MOE_ROWS)
        gus = [_dot(x_ref[r0:r0 + MOE_ROWS, :], wgu_sc[...]) for r0 in starts]
        for r0, gu in zip(starts, gus):
            act = _silu(gu[:, :D_EXPERT]) * gu[:, D_EXPERT:]
            y_ref[r0:r0 + MOE_ROWS, :] = _dot(act.astype(BF16), wdn_sc[...]).astype(BF16)


def _moe_combine_kernel(offl_s, seg_s, gbase_s, h_ref, pos_ref, gate_ref, x1_ref, g5_ref, sgu_ref, sdn_ref,
                        ys_ref, y_ref, yl_sc, sem):
    c = pl.program_id(0)
    tc = h_ref.shape[0]
    n_rows = yl_sc.shape[0]
    yl_sc[...] = jnp.zeros(yl_sc.shape, BF16)
    _segment_copies(c, offl_s, seg_s, gbase_s, yl_sc, ys_ref, sem, to_global=False, wait=False)
    gu = _dot(h_ref[...], sgu_ref[...])
    act = _silu(gu[:, :D_EXPERT]) * gu[:, D_EXPERT:]
    y_ref[...] = _dot(act.astype(BF16), sdn_ref[...])
    _segment_copies(c, offl_s, seg_s, gbase_s, yl_sc, ys_ref, sem, to_global=False, wait=True)
    pos = pos_ref[...]
    gate = gate_ref[...]
    kblk = n_rows // MOE_KSPLIT
    for tb in range(tc // TM):
        pos_t = pos[tb * TM:(tb + 1) * TM]
        gate_t = gate[tb * TM:(tb + 1) * TM]
        out = y_ref[tb * TM:(tb + 1) * TM, :]
        for kb in range(MOE_KSPLIT):
            cid = (lax.broadcasted_iota(jnp.int32, (TM, kblk), 1) + kb * kblk).astype(F32)
            scat = jnp.zeros((TM, kblk), F32)
            for k in range(TOP_K):
                scat = jnp.where(pos_t[:, k:k + 1] == cid, gate_t[:, k:k + 1], scat)
            out = out + _dot(scat.astype(BF16), yl_sc[kb * kblk:(kb + 1) * kblk, :])
        y_ref[tb * TM:(tb + 1) * TM, :] = x1_ref[tb * TM:(tb + 1) * TM, :] + g5_ref[tb] * out


def _moe(x1, g5, h2, idx, rank, gate, base, end, wgu, wdn, layer, sgu, sdn):
    n_tok, d = h2.shape
    tc = MOE_CHUNK
    nc = n_tok // tc
    bpc = tc // TM
    ne = wgu.shape[1]
    assert ne == N_EXPERTS
    ncp = -(-nc // LANES) * LANES
    n_loc = -(-(tc * TOP_K + ne * MOE_SEG) // (MOE_ROWS * MOE_KSPLIT)) * (MOE_ROWS * MOE_KSPLIT)
    assert n_loc >= -(-(tc * TOP_K + ne * (MOE_SEG - 1)) // (2 * MOE_ROWS)) * (2 * MOE_ROWS)
    nblk = -(-(n_tok * TOP_K + nc * ne * (MOE_SEG - 1) + ne * (MOE_BR - 1)) // MOE_BR)
    pad = lambda a: jnp.concatenate([a, jnp.zeros((ncp - nc, LANES), F32)], axis=0)
    cb_lo = pad(base[0::bpc, 0, :])
    cb_hi = pad(end[bpc - 1::bpc, 0, :])
    offl, seg, gbase, be, nused = _moe_tables(cb_lo, cb_hi, nblk)
    to_smem = lambda a: a[:nc, :ne].astype(jnp.int32).reshape(nc * ne)
    tabs = (to_smem(offl), to_smem(seg), to_smem(gbase))
    be_i = be[:nblk, 0].astype(jnp.int32)
    nused_i = nused[0, :1].astype(jnp.int32)
    chunk = lambda w: pl.BlockSpec((tc, w), lambda c, *_: (c, 0))
    row = pl.BlockSpec((1, 1, LANES), lambda c, *_: (c, 0, 0))
    anyspec = pl.BlockSpec(memory_space=pl.ANY)
    const = lambda a: pl.BlockSpec(a.shape, lambda c, *_: (0,) * a.ndim)

    pos, xs = pl.pallas_call(
        _moe_sort_kernel,
        grid_spec=pltpu.PrefetchScalarGridSpec(
            num_scalar_prefetch=3, grid=(nc,),
            in_specs=[chunk(d), chunk(LANES), chunk(LANES), row, row, anyspec],
            out_specs=[chunk(LANES), anyspec],
            scratch_shapes=[pltpu.VMEM((n_loc, d), BF16), pltpu.SemaphoreType.DMA]),
        out_shape=[jax.ShapeDtypeStruct((n_tok, LANES), F32),
                   jax.ShapeDtypeStruct((nblk * MOE_BR, d), BF16)],
        input_output_aliases={8: 1},
        compiler_params=_cparams(("arbitrary",)),
        name="moe_sort",
    )(*tabs, h2, idx, rank, cb_lo[:nc, None, :], offl[:nc, None, :], jnp.zeros((nblk * MOE_BR, d), BF16))

    last = lambda i, be_s, nu_s: jnp.minimum(i, nu_s[0] - 1)
    ys = pl.pallas_call(
        _moe_ffn_kernel,
        grid_spec=pltpu.PrefetchScalarGridSpec(
            num_scalar_prefetch=2, grid=(nblk,),
            in_specs=[pl.BlockSpec((MOE_BR, d), lambda i, be_s, nu_s: (last(i, be_s, nu_s), 0)),
                      pl.BlockSpec((1, 1, d, 2 * D_EXPERT),
                                   lambda i, be_s, nu_s: (layer, be_s[last(i, be_s, nu_s)], 0, 0)),
                      pl.BlockSpec((1, 1, D_EXPERT, d),
                                   lambda i, be_s, nu_s: (layer, be_s[last(i, be_s, nu_s)], 0, 0))],
            out_specs=pl.BlockSpec((MOE_BR, d), lambda i, be_s, nu_s: (last(i, be_s, nu_s), 0)),
            scratch_shapes=[pltpu.VMEM((d, 2 * D_EXPERT), BF16), pltpu.VMEM((D_EXPERT, d), BF16)]),
        out_shape=jax.ShapeDtypeStruct((nblk * MOE_BR, d), BF16),
        input_output_aliases={2: 0},
        compiler_params=_cparams(("arbitrary",)),
        name="moe_ffn",
    )(be_i, nused_i, xs, wgu, wdn)

    return pl.pallas_call(
        _moe_combine_kernel,
        grid_spec=pltpu.PrefetchScalarGridSpec(
            num_scalar_prefetch=3, grid=(nc,),
            in_specs=[chunk(d), chunk(LANES), chunk(LANES), chunk(d),
                      pl.BlockSpec((bpc, 1, d), lambda c, *_: (c, 0, 0)),
                      const(sgu), const(sdn), anyspec],
            out_specs=chunk(d),
            scratch_shapes=[pltpu.VMEM((n_loc, d), BF16), pltpu.SemaphoreType.DMA]),
        out_shape=jax.ShapeDtypeStruct((n_tok, d), F32),
        compiler_params=_cparams(("arbitrary",)),
        name="moe_combine",
    )(*tabs, h2, pos, gate, x1, g5, sgu, sdn, ys)


def _axial_tables(n_rows, rot_dim):
    rows = jnp.repeat(jnp.arange(n_rows), GRID_W).astype(F32)
    cols = jnp.tile(jnp.arange(GRID_W), n_rows).astype(F32)
    axis_dim = rot_dim // 2
    inv_freq = ROPE_THETA ** (-jnp.arange(0, axis_dim, 2, dtype=F32) / axis_dim)
    ang_r = rows[:, None] * inv_freq
    ang_c = cols[:, None] * inv_freq
    ang = jnp.concatenate([ang_r, ang_r, ang_c, ang_c], axis=-1)
    return jnp.cos(ang), jnp.sin(ang)


def _rope_slab_tables(t_len, n_ctx, rot_dim, lane0):
    cos, sin = _axial_tables(t_len // GRID_W, rot_dim)
    half = rot_dim // 4
    sign = jnp.where((jnp.arange(rot_dim) % (2 * half)) < half, -1.0, 1.0).astype(F32)
    cos_t = jnp.ones((t_len + n_ctx, LANES), F32).at[:t_len, lane0:lane0 + rot_dim].set(cos)
    sin_t = jnp.zeros((t_len + n_ctx, LANES), F32).at[:t_len, lane0:lane0 + rot_dim].set(sin * sign)
    return cos_t, sin_t


def _slab_cols(starts, width):
    out = []
    for st in starts:
        out += list(range(st, st + width)) + [-1] * (LANES - width)
    return out


def _gather_cols(w, cols):
    w_ext = jnp.concatenate([w, jnp.zeros((w.shape[0], 1), w.dtype)], axis=1)
    idx = np.array([c if c >= 0 else w.shape[1] for c in cols], np.int32)
    return w_ext[:, idx]


def _pad_row(v, lane0=0):
    return jnp.zeros((1, LANES), F32).at[0, lane0:lane0 + v.shape[0]].set(v.astype(F32))


def _layer_weights(l, w_in, gdn_conv_w, gdn_a_log, gdn_dt_bias, gdn_norm_g, mla_qn_g, mla_kvn_g,
                   mla_w_uq, mla_w_ukv, mla_qk_g, swa_qk_g, swa_sink, gqa_qk_g, w_out, router_w,
                   router_bias):
    hd = HEAD_DIM
    cols = list(range(0, 1024))
    cols += list(range(O_MQ, O_MQ + MLA_Q_RANK)) + list(range(O_MKV, O_MKV + MLA_KV_RANK))
    misc = list(range(O_MKR, O_MKR + MLA_ROPE))
    misc += list(range(O_AA, O_AA + 4)) + list(range(O_AB, O_AB + 4))
    misc += list(range(O_AA + 4, O_AA + 8)) + list(range(O_AB + 4, O_AB + 8))
    cols += misc + [-1] * (LANES - len(misc))
    cols += _slab_cols([O_SQ + hd * i for i in range(4)] + [O_SK + hd * i for i in range(2)]
                       + [O_SV + hd * i for i in range(2)], hd)
    cols += _slab_cols([O_GQ + hd * i for i in range(4)] + [O_GK + hd * i for i in range(2)]
                       + [O_GV + hd * i for i in range(2)], hd)
    assert len(cols) == N_COL
    qd = MLA_NOPE + MLA_ROPE
    uq_cols = []
    for hh in range(MLA_HEADS):
        uq_cols += list(range(hh * qd, hh * qd + qd)) + [-1] * (LANES - qd)
    kvd = MLA_NOPE + MLA_V
    ukv_cols = _slab_cols([hh * kvd for hh in range(MLA_HEADS)], MLA_NOPE)
    ukv_cols += _slab_cols([hh * kvd + MLA_NOPE for hh in range(MLA_HEADS)], MLA_V)
    orow = list(range(0, 256)) + _slab_cols([256 + hd * i for i in range(12)], hd)
    alog = jnp.zeros((1, LANES), F32)
    dtb = jnp.zeros((1, LANES), F32)
    for dd in range(2):
        alog = alog.at[0, 32 + 8 * dd:36 + 8 * dd].set(gdn_a_log[l, dd])
        dtb = dtb.at[0, 32 + 8 * dd:36 + 8 * dd].set(gdn_dt_bias[l, dd])
    return dict(
        win=_gather_cols(w_in[l], cols).astype(BF16),
        conv=gdn_conv_w[l][:, :768],
        alog=alog, dtb=dtb,
        gdn_g=jnp.concatenate([gdn_norm_g[l], gdn_norm_g[l]]).reshape(1, LANES),
        mla_qn_g=mla_qn_g[l], mla_kvn_g=mla_kvn_g[l],
        wuq=_gather_cols(mla_w_uq[l], uq_cols).astype(BF16),
        wukv=_gather_cols(mla_w_ukv[l], ukv_cols).astype(BF16),
        mqg=_pad_row(mla_qk_g[l, 0]), mkg=_pad_row(mla_qk_g[l, 1, :MLA_NOPE]),
        krg=_pad_row(mla_qk_g[l, 1, MLA_NOPE:]),
        sqg=_pad_row(swa_qk_g[l, 0]), skg=_pad_row(swa_qk_g[l, 1]),
        gqg=_pad_row(gqa_qk_g[l, 0]), gkg=_pad_row(gqa_qk_g[l, 1]),
        wout=_gather_cols(w_out[l].T, orow).T.astype(BF16),
        rw=jnp.concatenate([router_w[l], jnp.zeros((D_MODEL, LANES - N_EXPERTS), F32)], axis=1),
        rb=_pad_row(router_bias[l]),
        sink=swa_sink[l],
    )


def _sink_rows(sink, grp, tq):
    hkv = sink.shape[0] // grp
    return jnp.repeat((sink.astype(F32) * LOG2E).reshape(hkv, grp), tq, axis=1).reshape(hkv, grp * tq, 1)


def kernel(x, c, ctx, c_ctx, w_mod, b_mod, norm1_g, norm2_g, w_in, gdn_conv_w, gdn_a_log, gdn_dt_bias,
           gdn_norm_g, mla_qn_g, mla_kvn_g, mla_w_uq, mla_w_ukv, mla_qk_g, swa_qk_g, swa_sink, gqa_qk_g,
           w_out, router_w, router_bias, exp_w_gu, exp_w_down, shared_w_gu, shared_w_down):
    bsz, t_len, d = x.shape
    n_ctx = ctx.shape[1]
    depth = w_mod.shape[0]
    s = t_len + n_ctx
    assert d == D_MODEL and t_len % TM == 0 and n_ctx % TM == 0 and t_len >= 3 * WINDOW
    assert bsz + 1 <= 8
    nbl = t_len // TM
    nb = s // TM

    tabs = (_rope_slab_tables(t_len, n_ctx, HEAD_DIM, 0) + _rope_slab_tables(t_len, n_ctx, MLA_ROPE, MLA_NOPE)
            + _rope_slab_tables(t_len, n_ctx, MLA_ROPE, 0))
    c8 = jnp.zeros((8, d), F32).at[:bsz].set(c).at[bsz].set(c_ctx)
    xs = jnp.concatenate([x, ctx], axis=1)

    tq_d = 512 if t_len % 512 == 0 else TM
    tk_d = 8448 if (s % 8448 == 0) else TM

    for l in range(depth):
        need_ctx = l < depth - 1
        lw = _layer_weights(l, w_in, gdn_conv_w, gdn_a_log, gdn_dt_bias, gdn_norm_g, mla_qn_g, mla_kvn_g,
                            mla_w_uq, mla_w_ukv, mla_qk_g, swa_qk_g, swa_sink, gqa_qk_g, w_out, router_w,
                            router_bias)
        mod = _modulation(c8, w_mod[l], b_mod[l])
        mod_tab = jnp.stack([mod[:bsz], jnp.broadcast_to(mod[bsz], (bsz, 6 * d))], axis=1)[:, :, None, :]

        (ga, misc, mq, mk, mv, sq, sk, sv, gq, gk, gv) = _inproj(
            xs, mod_tab, norm1_g[l].reshape(1, d), lw['win'], lw, tabs, nbl)

        qkv, lab = _gdn_prep(ga, misc, lw['conv'], lw['alog'], lw['dtb'], nbl)
        labt = jnp.swapaxes(lab[..., :8], 2, 3)
        o_f, o_b = _gdn_scan(qkv, lab, labt, nbl=nbl)

        mo = _flash(mq, mk, mv, grp=1, tq=2 * tq_d, tk=tk_d, q_blk0=0, nq=t_len // (2 * tq_d),
                    k_blk0=0, nk=s // tk_d)
        go = _flash(gq, gk, gv, grp=2, tq=tq_d, tk=tk_d, q_blk0=0, nq=t_len // tq_d, k_blk0=0, nk=s // tk_d)
        so = _swa(sq, sk, sv, _sink_rows(lw['sink'], 2, WINDOW), grp=2, tq=tq_d, t_len=t_len, n_ctx=n_ctx)
        lat = (mo, so, go)
        ctx_out = lat
        if need_ctx:
            cb0 = t_len // n_ctx
            ctx_out = (
                _flash(mq, mk, mv, grp=1, tq=n_ctx, tk=n_ctx, q_blk0=cb0, nq=1, k_blk0=cb0, nk=1),
                _flash(sq, sk, sv, grp=2, tq=n_ctx, tk=n_ctx, q_blk0=cb0, nq=1, k_blk0=cb0, nk=1,
                       sink_rows=_sink_rows(lw['sink'], 2, n_ctx)),
                _flash(gq, gk, gv, grp=2, tq=n_ctx, tk=n_ctx, q_blk0=cb0, nq=1, k_blk0=cb0, nk=1))

        n_blk = nb if need_ctx else nbl
        x1, h2, idx, gate, rank, base, end = _outproj(
            xs, mod_tab, o_f, o_b, ga, lw['gdn_g'], lat, ctx_out, lw['wout'], norm2_g[l].reshape(1, d),
            lw['rw'], lw['rb'], n_blk=n_blk, nbl=nbl)

        n_tok = bsz * n_blk * TM
        assert n_tok % MOE_CHUNK == 0
        g5 = jnp.repeat(mod_tab[:, :, 0, 5 * d:], jnp.array([nbl, nb - nbl]), axis=1,
                        total_repeat_length=nb)[:, :n_blk].reshape(bsz * n_blk, 1, d)
        xs = _moe(x1.reshape(n_tok, d), g5, h2.reshape(n_tok, d), idx.reshape(n_tok, LANES),
                  rank.reshape(n_tok, LANES), gate.reshape(n_tok, LANES), base, end, exp_w_gu, exp_w_down, l,
                  shared_w_gu[l].astype(BF16), shared_w_down[l].astype(BF16)).reshape(bsz, n_blk * TM, d)
    return xs[:, :t_len]
```

```python
import functools
import math

import numpy as np
import jax
import jax.numpy as jnp
from jax import lax
from jax.experimental import pallas as pl
from jax.experimental.pallas import tpu as pltpu

F32 = jnp.float32
BF16 = jnp.bfloat16

LANES = 128
TM = 256
VMEM_LIMIT = 56 * 1024 * 1024

D_MODEL = 1024
GRID_W = 64
HEAD_DIM = 64
ROPE_THETA = 10000.0
NORM_EPS = 1e-6
ATTN_SCALE = HEAD_DIM ** -0.5
GDN_HEADS = 4
GDN_DK = 64
GDN_CHUNK = 64
GDN_CONV = 5
MLA_HEADS = 4
MLA_Q_RANK = 256
MLA_KV_RANK = 128
MLA_NOPE = 64
MLA_ROPE = 32
MLA_V = 64
MLA_SCALE = (MLA_NOPE + MLA_ROPE) ** -0.5
SWA_HEADS = 4
SWA_KV_HEADS = 2
WINDOW = 128
GQA_HEADS = 4
GQA_KV_HEADS = 2
N_EXPERTS = 64
TOP_K = 6
D_EXPERT = 384
ROUTED_SCALE = 2.5
LOG2E = math.log2(math.e)
ONES_LANE = HEAD_DIM
FLASH_KB = 256
MOE_CHUNK = 512
MOE_ROWS = 128
MOE_BR = 512
MOE_SEG = 16
MOE_KSPLIT = 3

O_AQ, O_AK, O_AV, O_AZ, O_AA, O_AB = 0, 256, 512, 768, 1024, 1032
O_MQ, O_MKV, O_MKR = 1040, 1296, 1424
O_SQ, O_SK, O_SV = 1456, 1712, 1840
O_GQ, O_GK, O_GV = 1968, 2224, 2352
D_IN = 2480
C_GDN, C_MLA, C_MISC, C_SWA, C_GQA, N_COL = 0, 1024, 1408, 1536, 2560, 3584


def _cparams(sem):
    return pltpu.CompilerParams(dimension_semantics=sem, vmem_limit_bytes=VMEM_LIMIT)


def _dot(a, b):
    return jnp.dot(a, b, preferred_element_type=F32)


def _dot_nt(a, b):
    return lax.dot_general(a, b, (((1,), (1,)), ((), ())), preferred_element_type=F32)


def _dot_tn(a, b):
    return lax.dot_general(a, b, (((0,), (0,)), ((), ())), preferred_element_type=F32)


def _split3(x):
    hi = x.astype(BF16)
    r1 = x - hi.astype(F32)
    mid = r1.astype(BF16)
    lo = (r1 - mid.astype(F32)).astype(BF16)
    return hi, mid, lo


def _dot_exact_lhs(a_bf16, x):
    hi, mid, lo = _split3(x)
    return _dot(a_bf16, hi) + _dot(a_bf16, mid) + _dot(a_bf16, lo)


def _dot_exact_rhs(x, b_bf16):
    hi, mid, lo = _split3(x)
    return _dot(hi, b_bf16) + _dot(mid, b_bf16) + _dot(lo, b_bf16)


def _dot_hp(a, b):
    ah = a.astype(BF16)
    al = (a - ah.astype(F32)).astype(BF16)
    bh = b.astype(BF16)
    bl = (b - bh.astype(F32)).astype(BF16)
    return _dot(ah, bh) + _dot(ah, bl) + _dot(al, bh)


def _silu(x):
    return x * (1.0 / (1.0 + jnp.exp(-x)))


def _lane(shape):
    return lax.broadcasted_iota(jnp.int32, shape, len(shape) - 1)


def _mod_kernel(c_ref, w_ref, b_ref, o_ref):
    s = _silu(c_ref[...])
    o_ref[...] = _dot(s.astype(BF16), w_ref[...].astype(BF16)) + b_ref[...]


def _modulation(c8, w_mod_l, b_mod_l):
    d = c8.shape[1]
    n = w_mod_l.shape[1]
    return pl.pallas_call(
        _mod_kernel,
        grid=(n // d,),
        in_specs=[pl.BlockSpec((8, d), lambda j: (0, 0)),
                  pl.BlockSpec((d, d), lambda j: (0, j)),
                  pl.BlockSpec((1, d), lambda j: (0, j))],
        out_specs=pl.BlockSpec((8, d), lambda j: (0, j)),
        out_shape=jax.ShapeDtypeStruct((8, n), F32),
        compiler_params=_cparams(("arbitrary",)),
        name="modulation",
    )(c8, w_mod_l, b_mod_l.reshape(1, n))


def _rope_slab(x, cos, sin_signed, half):
    lane = _lane(x.shape)
    fwd = pltpu.roll(x, LANES - half, 1)
    bwd = pltpu.roll(x, half, 1)
    rot = jnp.where(lane % (2 * half) < half, fwd, bwd)
    return x * cos + rot * sin_signed


def _inproj_kernel(x_ref, mod_ref, g1_ref, win_ref, qn_g_ref, kvn_g_ref, wuq_ref, wukv_ref,
                   mqg_ref, mkg_ref, krg_ref, sqg_ref, skg_ref, gqg_ref, gkg_ref,
                   cos_hd_ref, sin_hd_ref, cos_mq_ref, sin_mq_ref, cos_kr_ref, sin_kr_ref,
                   ga_ref, misc_ref, mq_ref, mk_ref, mv_ref, sq_ref, sk_ref, sv_ref,
                   gq_ref, gk_ref, gv_ref):
    d = D_MODEL
    x = x_ref[0]
    m = mod_ref[0, 0]
    shift, scale = m[:, 0:d], m[:, d:2 * d]
    xn = x * lax.rsqrt(jnp.mean(x * x, axis=-1, keepdims=True) + NORM_EPS) * g1_ref[...]
    h = xn * (1.0 + scale) + shift
    hb = h.astype(BF16)

    def proj(c0, width):
        return _dot(hb, win_ref[:, c0:c0 + width])

    ga_ref[0] = proj(C_GDN, 1024)
    pm = proj(C_MLA, C_SWA - C_MLA)
    misc = pm[:, C_MISC - C_MLA:C_MISC - C_MLA + LANES]
    misc_ref[0] = misc

    lane = _lane((TM, LANES))
    cos_hd, sin_hd = cos_hd_ref[...], sin_hd_ref[...]

    cq = pm[:, 0:MLA_Q_RANK]
    cqn = cq * lax.rsqrt(jnp.mean(cq * cq, axis=-1, keepdims=True) + NORM_EPS) * qn_g_ref[...]
    qup = _dot(cqn.astype(BF16), wuq_ref[...])
    ckv = pm[:, MLA_Q_RANK:MLA_Q_RANK + MLA_KV_RANK]
    ckvn = ckv * lax.rsqrt(jnp.mean(ckv * ckv, axis=-1, keepdims=True) + NORM_EPS) * kvn_g_ref[...]
    kvup = _dot(ckvn.astype(BF16), wukv_ref[...])
    kr = jnp.where(lane < MLA_ROPE, misc, 0.0)
    kr = kr * lax.rsqrt(jnp.sum(kr * kr, axis=-1, keepdims=True) / MLA_ROPE + NORM_EPS) * krg_ref[...]
    kr = _rope_slab(kr, cos_kr_ref[...], sin_kr_ref[...], MLA_ROPE // 4)
    kpe = pltpu.roll(kr, MLA_NOPE, 1)
    is_nope = lane < MLA_NOPE
    for hh in range(MLA_HEADS):
        q = qup[:, hh * LANES:(hh + 1) * LANES]
        q2 = q * q
        ss_n = jnp.sum(jnp.where(is_nope, q2, 0.0), axis=-1, keepdims=True)
        ss_p = jnp.sum(jnp.where(is_nope, 0.0, q2), axis=-1, keepdims=True)
        r = jnp.where(is_nope, lax.rsqrt(ss_n / MLA_NOPE + NORM_EPS), lax.rsqrt(ss_p / MLA_ROPE + NORM_EPS))
        q = q * r * mqg_ref[...]
        q = _rope_slab(q, cos_mq_ref[...], sin_mq_ref[...], MLA_ROPE // 4)
        mq_ref[0, hh] = (q * (MLA_SCALE * LOG2E)).astype(BF16)
        k = kvup[:, hh * LANES:(hh + 1) * LANES]
        k = k * lax.rsqrt(jnp.sum(k * k, axis=-1, keepdims=True) / MLA_NOPE + NORM_EPS) * mkg_ref[...]
        mk_ref[0, hh] = (k + kpe).astype(BF16)
        v = kvup[:, (MLA_HEADS + hh) * LANES:(MLA_HEADS + hh + 1) * LANES]
        mv_ref[0, hh] = jnp.where(lane == ONES_LANE, 1.0, v).astype(BF16)

    def gqa_prep(base, qg_ref, kg_ref, q_ref, k_ref, v_ref, nq, nkv):
        p = proj(base, (nq + 2 * nkv) * LANES)
        base = 0
        for hh in range(nq):
            q = p[:, base + hh * LANES:base + (hh + 1) * LANES]
            q = q * lax.rsqrt(jnp.sum(q * q, axis=-1, keepdims=True) / HEAD_DIM + NORM_EPS) * qg_ref[...]
            q = _rope_slab(q, cos_hd, sin_hd, HEAD_DIM // 4) * (ATTN_SCALE * LOG2E)
            q_ref[0, hh] = q.astype(BF16)
        for hh in range(nkv):
            k = p[:, base + (nq + hh) * LANES:base + (nq + hh + 1) * LANES]
            k = k * lax.rsqrt(jnp.sum(k * k, axis=-1, keepdims=True) / HEAD_DIM + NORM_EPS) * kg_ref[...]
            k = _rope_slab(k, cos_hd, sin_hd, HEAD_DIM // 4)
            k_ref[0, hh] = k.astype(BF16)
            v = p[:, base + (nq + nkv + hh) * LANES:base + (nq + nkv + hh + 1) * LANES]
            v_ref[0, hh] = jnp.where(lane == ONES_LANE, 1.0, v).astype(BF16)

    gqa_prep(C_SWA, sqg_ref, skg_ref, sq_ref, sk_ref, sv_ref, SWA_HEADS, SWA_KV_HEADS)
    gqa_prep(C_GQA, gqg_ref, gkg_ref, gq_ref, gk_ref, gv_ref, GQA_HEADS, GQA_KV_HEADS)


def _inproj(x, mod_tab, g1, win, lw, tabs, nbl):
    bsz, s, d = x.shape
    nb = s // TM
    row = lambda a: a.reshape(1, -1)
    full = lambda a: pl.BlockSpec(a.shape, lambda b, i: (0,) * a.ndim)
    tab_spec = pl.BlockSpec((TM, LANES), lambda b, i: (i, 0))
    head_out = lambda nh: pl.BlockSpec((1, nh, TM, LANES), lambda b, i: (b, 0, i, 0))
    head_shape = lambda nh: jax.ShapeDtypeStruct((bsz, nh, s, LANES), BF16)
    small = [row(lw['mla_qn_g']), row(lw['mla_kvn_g']), lw['wuq'], lw['wukv'],
             lw['mqg'], lw['mkg'], lw['krg'], lw['sqg'], lw['skg'], lw['gqg'], lw['gkg']]
    return pl.pallas_call(
        _inproj_kernel,
        grid=(bsz, nb),
        in_specs=[pl.BlockSpec((1, TM, d), lambda b, i: (b, i, 0)),
                  pl.BlockSpec((1, 1, 1, 6 * d), lambda b, i: (b, i // nbl, 0, 0)),
                  full(g1), full(win)] + [full(a) for a in small] + [tab_spec] * 6,
        out_specs=[pl.BlockSpec((1, TM, 1024), lambda b, i: (b, i, 0)),
                   pl.BlockSpec((1, TM, LANES), lambda b, i: (b, i, 0)),
                   head_out(4), head_out(4), head_out(4),
                   head_out(4), head_out(2), head_out(2),
                   head_out(4), head_out(2), head_out(2)],
        out_shape=[jax.ShapeDtypeStruct((bsz, s, 1024), F32),
                   jax.ShapeDtypeStruct((bsz, s, LANES), F32),
                   head_shape(4), head_shape(4), head_shape(4),
                   head_shape(4), head_shape(2), head_shape(2),
                   head_shape(4), head_shape(2), head_shape(2)],
        compiler_params=_cparams(("parallel", "parallel")),
        name="inproj",
    )(x, mod_tab, g1, win, *small, *tabs)


def _flash_kernel(*refs, grp, tq, nk, use_sink):
    refs = list(refs)
    sink_ref = refs.pop(0) if use_sink else None
    q_ref, k_ref, v_ref = refs[0], refs[1], refs[2]
    o_ref, m_sc, acc_sc = refs[-3], refs[-2], refs[-1]
    kj = pl.program_id(3)
    tk = k_ref.shape[2]
    kb = min(FLASH_KB, tk)

    @pl.when(kj == 0)
    def _():
        m_sc[...] = jnp.full(m_sc.shape, -jnp.inf, F32)
        acc_sc[...] = jnp.zeros(acc_sc.shape, F32)

    q = q_ref[0].reshape(grp * tq, LANES)
    m = m_sc[...]
    acc = acc_sc[...]
    for j in range(tk // kb):
        s = _dot_nt(q, k_ref[0, 0, j * kb:(j + 1) * kb, :])
        m_new = jnp.maximum(m, jnp.max(s, axis=-1, keepdims=True))
        alpha = jnp.exp2(m - m_new)
        pr = jnp.exp2(s - jnp.concatenate([m_new] * (kb // LANES), axis=1))
        acc = alpha * acc + _dot(pr.astype(BF16), v_ref[0, 0, j * kb:(j + 1) * kb, :])
        m = m_new
    m_sc[...] = m
    acc_sc[...] = acc

    @pl.when(kj == nk - 1)
    def _():
        l = acc[:, ONES_LANE:ONES_LANE + 1]
        out = acc
        if use_sink:
            sk = sink_ref[0]
            m_old = m[:, 0:1]
            m_fin = jnp.maximum(m_old, sk)
            a = jnp.exp2(m_old - m_fin)
            l = l * a + jnp.exp2(sk - m_fin)
            out = out * a
        o_ref[0] = (out / l).reshape(grp, tq, LANES).astype(o_ref.dtype)


def _flash(q, k, v, *, grp, tq, tk, q_blk0, nq, k_blk0, nk, sink_rows=None):
    bsz, hq, s, _ = q.shape
    hkv = hq // grp
    use_sink = sink_rows is not None
    in_specs = []
    args = []
    if use_sink:
        in_specs.append(pl.BlockSpec((1, grp * tq, 1), lambda b, h, i, j: (h, 0, 0)))
        args.append(sink_rows)
    in_specs += [pl.BlockSpec((1, grp, tq, LANES), lambda b, h, i, j: (b, h, i + q_blk0, 0)),
                 pl.BlockSpec((1, 1, tk, LANES), lambda b, h, i, j: (b, h, j + k_blk0, 0)),
                 pl.BlockSpec((1, 1, tk, LANES), lambda b, h, i, j: (b, h, j + k_blk0, 0))]
    args += [q, k, v]
    rows = grp * tq
    return pl.pallas_call(
        functools.partial(_flash_kernel, grp=grp, tq=tq, nk=nk, use_sink=use_sink),
        grid=(bsz, hkv, nq, nk),
        in_specs=in_specs,
        out_specs=pl.BlockSpec((1, grp, tq, LANES), lambda b, h, i, j: (b, h, i, 0)),
        out_shape=jax.ShapeDtypeStruct((bsz, hq, nq * tq, LANES), BF16),
        scratch_shapes=[pltpu.VMEM((rows, LANES), F32), pltpu.VMEM((rows, LANES), F32)],
        compiler_params=_cparams(("parallel", "parallel", "parallel", "arbitrary")),
        name="flash",
    )(*args)


def _swa_kernel(sink_ref, q_ref, k_ref, v_ref, o_ref, *, grp, tq, t_len, n_ctx):
    w = WINDOW
    n = pl.program_id(2)
    kc = k_ref[0, 0, pl.ds(t_len, n_ctx), :]
    vc = v_ref[0, 0, pl.ds(t_len, n_ctx), :]
    for qb in range(tq // w):
        q0 = n * tq + qb * w
        start = pl.multiple_of(jnp.clip(q0 - w, 0, t_len - 3 * w), w)
        q = q_ref[0, :, qb * w:(qb + 1) * w, :].reshape(grp * w, LANES)
        kl = k_ref[0, 0, pl.ds(start, 3 * w), :]
        vl = v_ref[0, 0, pl.ds(start, 3 * w), :]
        s_loc = _dot_nt(q, kl)
        qpos = q0 + (lax.broadcasted_iota(jnp.int32, s_loc.shape, 0) % w)
        kpos = start + lax.broadcasted_iota(jnp.int32, s_loc.shape, 1)
        s_loc = jnp.where(jnp.abs(qpos - kpos) <= w, s_loc, -jnp.inf)
        s_ctx = _dot_nt(q, kc)
        sk = sink_ref[0]
        m = jnp.maximum(jnp.maximum(jnp.max(s_loc, axis=-1, keepdims=True),
                                    jnp.max(s_ctx, axis=-1, keepdims=True)), sk)
        p_loc = jnp.exp2(s_loc - m)
        p_ctx = jnp.exp2(s_ctx - m)
        o = _dot(p_loc.astype(BF16), vl) + _dot(p_ctx.astype(BF16), vc)
        l = o[:, ONES_LANE:ONES_LANE + 1] + jnp.exp2(sk - m)
        o_ref[0, :, qb * w:(qb + 1) * w, :] = (o / l).reshape(grp, w, LANES).astype(o_ref.dtype)


def _swa(q, k, v, sink_rows, *, grp, tq, t_len, n_ctx):
    bsz, hq, s, _ = q.shape
    hkv = hq // grp
    return pl.pallas_call(
        functools.partial(_swa_kernel, grp=grp, tq=tq, t_len=t_len, n_ctx=n_ctx),
        grid=(bsz, hkv, t_len // tq),
        in_specs=[pl.BlockSpec((1, grp * WINDOW, 1), lambda b, h, n: (h, 0, 0)),
                  pl.BlockSpec((1, grp, tq, LANES), lambda b, h, n: (b, h, n, 0)),
                  pl.BlockSpec((1, 1, s, LANES), lambda b, h, n: (b, h, 0, 0)),
                  pl.BlockSpec((1, 1, s, LANES), lambda b, h, n: (b, h, 0, 0))],
        out_specs=pl.BlockSpec((1, grp, tq, LANES), lambda b, h, n: (b, h, n, 0)),
        out_shape=jax.ShapeDtypeStruct((bsz, hq, t_len, LANES), BF16),
        compiler_params=_cparams(("parallel", "parallel", "arbitrary")),
        name="swa",
    )(sink_rows, q, k, v)


def _gdn_prep_kernel(cur_ref, prev_ref, next_ref, misc_ref, cw_ref, alog_ref, dtb_ref,
                     qkv_ref, lab_ref, *, nbl, nb):
    i = pl.program_id(1)
    first = jnp.logical_or(i == 0, i == nbl)
    last = jnp.logical_or(i == nbl - 1, i == nb - 1)
    cur = cur_ref[0]
    prev = jnp.where(first, 0.0, prev_ref[0])
    nxt = jnp.where(last, 0.0, next_ref[0])
    xe = jnp.concatenate([prev, cur, nxt], axis=0)
    cw = cw_ref[...]
    acc = jnp.zeros(cur.shape, F32)
    for j in range(GDN_CONV):
        off = 8 + j - (GDN_CONV - 1) // 2
        acc = acc + xe[off:off + TM, :] * cw[j:j + 1, :]
    y = _silu(acc)
    lane = _lane((TM, LANES))
    lo = lane < GDN_DK
    for sl in range(6):
        t = y[:, sl * LANES:(sl + 1) * LANES]
        if sl < 4:
            t2 = t * t
            ss0 = jnp.sum(jnp.where(lo, t2, 0.0), axis=-1, keepdims=True)
            ss1 = jnp.sum(jnp.where(lo, 0.0, t2), axis=-1, keepdims=True)
            t = t * jnp.where(lo, lax.rsqrt(ss0 + NORM_EPS), lax.rsqrt(ss1 + NORM_EPS))
            if sl < 2:
                t = t * GDN_DK ** -0.5
        qkv_ref[0, :, sl * LANES:(sl + 1) * LANES] = t
    xm = misc_ref[0]
    za = xm + dtb_ref[...]
    softplus = jnp.maximum(za, 0.0) + jnp.log(1.0 + jnp.exp(-jnp.abs(za)))
    log_a = -jnp.exp(alog_ref[...]) * softplus
    beta = 1.0 / (1.0 + jnp.exp(-xm))
    is_a = (lane % 8) < 4
    vals = jnp.where(is_a, log_a, beta)
    lab_ref[0, 0] = pltpu.roll(vals, LANES - 32, 1)
    lab_ref[0, 1] = pltpu.roll(vals, LANES - 40, 1)


def _gdn_prep(ga, misc, cw, alog_row, dtb_row, nbl):
    bsz, s, _ = ga.shape
    nb = s // TM
    r8 = TM // 8
    n8 = s // 8
    return pl.pallas_call(
        functools.partial(_gdn_prep_kernel, nbl=nbl, nb=nb),
        grid=(bsz, nb),
        in_specs=[pl.BlockSpec((1, TM, 768), lambda b, i: (b, i, 0)),
                  pl.BlockSpec((1, 8, 768), lambda b, i: (b, jnp.maximum(i * r8 - 1, 0), 0)),
                  pl.BlockSpec((1, 8, 768), lambda b, i: (b, jnp.minimum((i + 1) * r8, n8 - 1), 0)),
                  pl.BlockSpec((1, TM, LANES), lambda b, i: (b, i, 0)),
                  pl.BlockSpec(cw.shape, lambda b, i: (0, 0)),
                  pl.BlockSpec((1, LANES), lambda b, i: (0, 0)),
                  pl.BlockSpec((1, LANES), lambda b, i: (0, 0))],
        out_specs=[pl.BlockSpec((1, TM, 768), lambda b, i: (b, i, 0)),
                   pl.BlockSpec((1, 2, TM, LANES), lambda b, i: (b, 0, i, 0))],
        out_shape=[jax.ShapeDtypeStruct((bsz, s, 768), F32),
                   jax.ShapeDtypeStruct((bsz, 2, s, LANES), F32)],
        compiler_params=_cparams(("parallel", "parallel")),
        name="gdn_prep",
    )(ga, ga, ga, misc, cw, alog_row, dtb_row)


def _gdn_scan_kernel(qkvf_ref, labf_ref, labtf_ref, qkvb_ref, labb_ref, labtb_ref, of_ref, ob_ref, s_sc):
    c = GDN_CHUNK
    c2 = 2 * c
    i = pl.program_id(1)

    @pl.when(i == 0)
    def _():
        s_sc[...] = jnp.zeros(s_sc.shape, F32)

    ri = lax.broadcasted_iota(jnp.int32, (c2, c2), 0)
    ci = lax.broadcasted_iota(jnp.int32, (c2, c2), 1)
    same = (ri // c) == (ci // c)
    eye = (ri == ci).astype(F32)
    lane = _lane((c, LANES))
    lo = lane < GDN_DK

    def stack(x):
        return jnp.concatenate([jnp.where(lo, x, 0.0), jnp.where(lo, 0.0, x)], axis=0)

    def fold(mat):
        return mat[0:c, :] + mat[c:c2, :]

    inst_f = _gdn_instances(qkvf_ref, labf_ref, labtf_ref, of_ref, 0, False, stack, fold)
    inst_b = _gdn_instances(qkvb_ref, labb_ref, labtb_ref, ob_ref, 2, True, stack, fold)
    inst = [it for pair in zip(inst_f, inst_b) for it in pair]
    tinv = [eye - it['a'] for it in inst]
    pw = [it['a'] for it in inst]
    for _ in range(5):
        pw = [_dot_hp(x, x) for x in pw]
        tinv = [t + _dot_hp(t, x) for t, x in zip(tinv, pw)]
    eye_b = eye.astype(BF16)
    for it, t in zip(inst, tinv):
        tf = fold(t).astype(BF16)
        it['u'] = _dot(tf, it['vb'])
        it['wq'] = jnp.concatenate([_dot(tf, it['kg']).astype(BF16), it['qg']], axis=0)
        it['kdt'] = _dot_nt(eye_b, it['kd']).astype(BF16)
    state = [s_sc[j] for j in range(4)]
    for it in inst:
        st = state[it['slot']]
        ws = _dot(it['wq'], st.astype(BF16))
        v_new = it['u'] - ws[0:c, :]
        o = ws[c:c2, :] + _dot(it['qkf'], stack(v_new).astype(BF16))
        it['o_ref'][0, it['rows'], it['pr'] * LANES:(it['pr'] + 1) * LANES] = o
        upd = _dot(it['kdt'], v_new.astype(BF16))
        state[it['slot']] = st * it['sdec'] + jnp.where(same, upd, 0.0)
    for j in range(4):
        s_sc[j] = state[j]


def _gdn_instances(qkv_ref, lab_ref, labt_ref, o_ref, slot0, reverse, stack, fold):
    c = GDN_CHUNK
    c2 = 2 * c
    ri = lax.broadcasted_iota(jnp.int32, (c2, c2), 0)
    ci = lax.broadcasted_iota(jnp.int32, (c2, c2), 1)
    same = (ri // c) == (ci // c)
    if reverse:
        incl = jnp.logical_and(same, ri <= ci)
        strict = jnp.logical_and(same, ri < ci)
    else:
        incl = jnp.logical_and(same, ri >= ci)
        strict = jnp.logical_and(same, ri > ci)
    r1 = lax.broadcasted_iota(jnp.int32, (c, c), 0)
    c1 = lax.broadcasted_iota(jnp.int32, (c, c), 1)
    cum_col = ((r1 <= c1) if reverse else (r1 >= c1)).astype(BF16)
    rr = lax.broadcasted_iota(jnp.int32, (c, c2), 0)
    cc = lax.broadcasted_iota(jnp.int32, (c, c2), 1) % c
    cum_row = ((rr >= cc) if reverse else (rr <= cc)).astype(BF16)
    lane = _lane((c, LANES))
    lo = lane < GDN_DK
    lane2 = _lane((1, c2))
    lo2 = lane2 < c
    g_last_row = 0 if reverse else c - 1
    order = list(range(TM // c - 1, -1, -1) if reverse else range(TM // c))
    inst = []
    for ch in order:
        rows = slice(ch * c, (ch + 1) * c)
        lab = lab_ref[0, 0, rows, :]
        g_cols = _dot_exact_lhs(cum_col, lab)
        la_rows = labt_ref[0, 0, :, rows]
        g_rows = _dot_exact_rhs(la_rows, cum_row)
        for pr in range(2):
            h0, h1 = 2 * pr, 2 * pr + 1
            q = qkv_ref[0, rows, pr * LANES:(pr + 1) * LANES]
            k = qkv_ref[0, rows, (2 + pr) * LANES:(3 + pr) * LANES]
            v = qkv_ref[0, rows, (4 + pr) * LANES:(5 + pr) * LANES]
            beta = jnp.where(lo, lab[:, 4 + h0:5 + h0], lab[:, 4 + h1:5 + h1])
            g = jnp.where(lo, g_cols[:, h0:h0 + 1], g_cols[:, h1:h1 + 1])
            g_st_col = jnp.concatenate([g_cols[:, h0:h0 + 1], g_cols[:, h1:h1 + 1]], axis=0)
            g_st_row = jnp.where(lo2, g_rows[h0:h0 + 1, :], g_rows[h1:h1 + 1, :])
            diff = g_st_col - g_st_row
            dec = jnp.where(incl, jnp.exp(jnp.where(incl, diff, 0.0)), 0.0)
            kb = k * beta
            eg = jnp.exp(g)
            k_st = stack(k).astype(BF16)
            a = jnp.where(strict, _dot_nt(stack(kb).astype(BF16), k_st) * dec, 0.0)
            qk = _dot_nt(stack(q).astype(BF16), k_st) * dec
            gl = g[g_last_row:g_last_row + 1, :]
            gl_col = jnp.concatenate([jnp.broadcast_to(g_cols[g_last_row:g_last_row + 1, h0:h0 + 1], (c, 1)),
                                      jnp.broadcast_to(g_cols[g_last_row:g_last_row + 1, h1:h1 + 1], (c, 1))],
                                     axis=0)
            inst.append(dict(rows=rows, pr=pr, slot=slot0 + pr, o_ref=o_ref, a=a,
                             vb=stack(v * beta).astype(BF16),
                             kg=stack(kb * eg).astype(BF16), qg=(q * eg).astype(BF16),
                             qkf=fold(qk).astype(BF16), kd=(k * jnp.exp(gl - g)).astype(BF16),
                             sdec=jnp.exp(gl_col)))
    return inst


def _gdn_scan(qkv, lab, labt, *, nbl):
    bsz, s, _ = qkv.shape
    nb = s // TM
    nbc = nb - nbl

    def blk_f(i):
        return jnp.where(i < nbc, nbl + i, i - nbc)

    def blk_b(i):
        return jnp.where(i < nbc, nb - 1 - i, nbl - 1 - (i - nbc))

    def specs(blk, dsel):
        return [pl.BlockSpec((1, TM, 768), lambda b, i: (b, blk(i), 0)),
                pl.BlockSpec((1, 1, TM, LANES), lambda b, i: (b, dsel, blk(i), 0)),
                pl.BlockSpec((1, 1, 8, TM), lambda b, i: (b, dsel, 0, blk(i)))]

    out = jax.ShapeDtypeStruct((bsz, s, 2 * LANES), F32)
    return pl.pallas_call(
        _gdn_scan_kernel,
        grid=(bsz, nb),
        in_specs=specs(blk_f, 0) + specs(blk_b, 1),
        out_specs=[pl.BlockSpec((1, TM, 2 * LANES), lambda b, i: (b, blk_f(i), 0)),
                   pl.BlockSpec((1, TM, 2 * LANES), lambda b, i: (b, blk_b(i), 0))],
        out_shape=[out, out],
        scratch_shapes=[pltpu.VMEM((4, LANES, LANES), F32)],
        compiler_params=_cparams(("parallel", "arbitrary")),
        name="gdn_scan",
    )(qkv, lab, labt, qkv, lab, labt)


def _outproj_kernel(x_ref, mod_ref, of_ref, ob_ref, z_ref, gg_ref, mo_ref, so_ref, go_ref,
                    mc_ref, sc_ref, gc_ref, wout_ref, g2_ref, rw_ref, rb_ref,
                    x1_ref, h2_ref, idx_ref, gate_ref, rank_ref, base_ref, end_ref, run_sc, *, nbl):
    d = D_MODEL
    first = jnp.logical_and(pl.program_id(0) == 0, pl.program_id(1) == 0)
    is_ctx = pl.program_id(1) >= nbl

    @pl.when(first)
    def _():
        run_sc[...] = jnp.zeros(run_sc.shape, F32)

    m = mod_ref[0, 0]
    lane = _lane((TM, LANES))
    lo = lane < HEAD_DIM
    o = of_ref[0] + ob_ref[0]
    z = z_ref[0]
    mix = []
    for pr in range(2):
        t = o[:, pr * LANES:(pr + 1) * LANES]
        t2 = t * t
        ms0 = jnp.sum(jnp.where(lo, t2, 0.0), axis=-1, keepdims=True) / HEAD_DIM
        ms1 = jnp.sum(jnp.where(lo, 0.0, t2), axis=-1, keepdims=True) / HEAD_DIM
        t = t * jnp.where(lo, lax.rsqrt(ms0 + NORM_EPS), lax.rsqrt(ms1 + NORM_EPS)) * gg_ref[...]
        t = t * _silu(z[:, pr * LANES:(pr + 1) * LANES])
        mix.append(t.astype(BF16))
    for lat_ref, ctx_ref in ((mo_ref, mc_ref), (so_ref, sc_ref), (go_ref, gc_ref)):
        for hh in range(4):
            mix.append(jnp.where(is_ctx, ctx_ref[0, hh], lat_ref[0, hh]))
    proj = _dot(jnp.concatenate(mix, axis=1), wout_ref[...])
    x1 = x_ref[0] + m[:, 2 * d:3 * d] * proj
    x1_ref[0] = x1
    xn = x1 * lax.rsqrt(jnp.mean(x1 * x1, axis=-1, keepdims=True) + NORM_EPS) * g2_ref[...]
    h2 = xn * (1.0 + m[:, 4 * d:5 * d]) + m[:, 3 * d:4 * d]
    h2_ref[0] = h2.astype(BF16)

    logits = _dot_hp(h2, rw_ref[...])
    scores = 1.0 / (1.0 + jnp.exp(-logits))
    valid = lane < N_EXPERTS
    sel = jnp.where(valid, scores + rb_ref[...], -jnp.inf)
    member = jnp.zeros((TM, LANES), F32)
    picks = []
    for _ in range(TOP_K):
        mx = jnp.max(sel, axis=-1, keepdims=True)
        idx = jnp.min(jnp.where(sel == mx, lane, LANES), axis=-1, keepdims=True)
        hit = lane == idx
        gate = jnp.sum(jnp.where(hit, scores, 0.0), axis=-1, keepdims=True)
        sel = jnp.where(hit, -jnp.inf, sel)
        member = member + hit.astype(F32)
        picks.append((idx, hit, gate))
    gsum = picks[0][2]
    for kk in range(1, TOP_K):
        gsum = gsum + picks[kk][2]
    ri = lax.broadcasted_iota(jnp.int32, (TM, TM), 0)
    ci = lax.broadcasted_iota(jnp.int32, (TM, TM), 1)
    before = (ri > ci).astype(BF16)
    run = run_sc[...]
    base_ref[0] = run
    cum = _dot(before, member.astype(BF16)) + run
    idx_out = jnp.full((TM, LANES), -1, jnp.int32)
    gate_out = jnp.zeros((TM, LANES), F32)
    rank_out = jnp.zeros((TM, LANES), F32)
    for kk, (idx, hit, gate) in enumerate(picks):
        rank = jnp.sum(jnp.where(hit, cum, 0.0), axis=-1, keepdims=True)
        here = lane == kk
        idx_out = jnp.where(here, idx, idx_out)
        gate_out = jnp.where(here, gate / gsum * ROUTED_SCALE, gate_out)
        rank_out = jnp.where(here, rank, rank_out)
    idx_ref[0] = idx_out
    gate_ref[0] = gate_out
    rank_ref[0] = rank_out
    run_end = run + jnp.sum(member, axis=0, keepdims=True)
    end_ref[0] = run_end
    run_sc[...] = run_end


def _outproj(x, mod_tab, o_f, o_b, ga, gg_row, lat, ctx, wout, g2, rw, rb, *, n_blk, nbl):
    bsz, _, d = x.shape
    s_out = n_blk * TM
    full = lambda a: pl.BlockSpec(a.shape, lambda b, i: (0,) * a.ndim)
    tok = lambda w: pl.BlockSpec((1, TM, w), lambda b, i: (b, i, 0))
    head = pl.BlockSpec((1, 4, TM, LANES), lambda b, i: (b, 0, jnp.minimum(i, nbl - 1), 0))
    head_c = pl.BlockSpec((1, 4, TM, LANES), lambda b, i: (b, 0, jnp.maximum(i - nbl, 0), 0))
    return pl.pallas_call(
        functools.partial(_outproj_kernel, nbl=nbl),
        grid=(bsz, n_blk),
        in_specs=[tok(d),
                  pl.BlockSpec((1, 1, 1, 6 * d), lambda b, i: (b, i // nbl, 0, 0)),
                  tok(2 * LANES), tok(2 * LANES),
                  pl.BlockSpec((1, TM, 2 * LANES), lambda b, i: (b, i, 3)),
                  full(gg_row), head, head, head, head_c, head_c, head_c,
                  full(wout), full(g2), full(rw), full(rb)],
        out_specs=[tok(d), tok(d), tok(LANES), tok(LANES), tok(LANES),
                   pl.BlockSpec((1, 1, LANES), lambda b, i: (b * n_blk + i, 0, 0)),
                   pl.BlockSpec((1, 1, LANES), lambda b, i: (b * n_blk + i, 0, 0))],
        out_shape=[jax.ShapeDtypeStruct((bsz, s_out, d), F32),
                   jax.ShapeDtypeStruct((bsz, s_out, d), BF16),
                   jax.ShapeDtypeStruct((bsz, s_out, LANES), jnp.int32),
                   jax.ShapeDtypeStruct((bsz, s_out, LANES), F32),
                   jax.ShapeDtypeStruct((bsz, s_out, LANES), F32),
                   jax.ShapeDtypeStruct((bsz * n_blk, 1, LANES), F32),
                   jax.ShapeDtypeStruct((bsz * n_blk, 1, LANES), F32)],
        scratch_shapes=[pltpu.VMEM((1, LANES), F32)],
        compiler_params=_cparams(("arbitrary", "arbitrary")),
        name="outproj_router",
    )(x, mod_tab, o_f, o_b, ga, gg_row, *lat, *ctx, wout, g2, rw, rb)


def _moe_tables_kernel(lo_ref, hi_ref, offl_ref, seg_ref, gbase_ref, be_ref, nused_ref):
    ncp = lo_ref.shape[0]
    nblkp = be_ref.shape[0]
    seg = jnp.ceil((hi_ref[...] - lo_ref[...]) * (1.0 / MOE_SEG)) * MOE_SEG
    ri = lax.broadcasted_iota(jnp.int32, (LANES, LANES), 0)
    ci = lax.broadcasted_iota(jnp.int32, (LANES, LANES), 1)
    before_lane = (ri < ci).astype(BF16)
    rc = lax.broadcasted_iota(jnp.int32, (ncp, ncp), 0)
    cc = lax.broadcasted_iota(jnp.int32, (ncp, ncp), 1)
    before_row = (rc > cc).astype(BF16)
    seg_ref[...] = seg
    offl_ref[...] = _dot_exact_rhs(seg, before_lane)
    region = jnp.ceil(jnp.sum(seg, axis=0, keepdims=True) * (1.0 / MOE_BR)) * MOE_BR
    goff = _dot_exact_rhs(jnp.broadcast_to(region, (8, LANES)), before_lane)[0:1]
    gbase_ref[...] = goff + _dot_exact_lhs(before_row, seg)
    lane = _lane((nblkp, LANES))
    gend = jnp.where(lane < N_EXPERTS, goff + region, jnp.inf)
    row0 = (lax.broadcasted_iota(jnp.int32, (nblkp, LANES), 0) * MOE_BR).astype(F32)
    be = jnp.sum((gend <= row0).astype(F32), axis=-1, keepdims=True)
    be_ref[...] = jnp.broadcast_to(jnp.minimum(be, N_EXPERTS - 1.0), (nblkp, LANES))
    nused_ref[...] = jnp.broadcast_to(jnp.sum(region, axis=-1, keepdims=True) * (1.0 / MOE_BR), (1, LANES))


def _moe_tables(cb_lo, cb_hi, nblk):
    ncp = cb_lo.shape[0]
    nblkp = -(-nblk // 8) * 8
    tab = jax.ShapeDtypeStruct((ncp, LANES), F32)
    return pl.pallas_call(
        _moe_tables_kernel,
        out_shape=[tab, tab, tab, jax.ShapeDtypeStruct((nblkp, LANES), F32),
                   jax.ShapeDtypeStruct((1, LANES), F32)],
        name="moe_tables",
    )(cb_lo, cb_hi)


def _segment_copy(local_ref, glob_ref, sem, lo, go, size, to_global):
    loc = local_ref.at[pl.ds(lo, size)]
    glo = glob_ref.at[pl.ds(go, size)]
    return pltpu.make_async_copy(loc, glo, sem) if to_global else pltpu.make_async_copy(glo, loc, sem)


def _segment_copies_wait(total, local_ref, glob_ref, sem, *, to_global):
    size = 1 << (local_ref.shape[0].bit_length() - 1)
    while size >= MOE_SEG:

        @pl.when((total & size) != 0)
        def _(size=size):
            _segment_copy(local_ref, glob_ref, sem, 0, 0, size, to_global).wait()

        size //= 2


def _segment_copies_start(c, offl_s, seg_s, gbase_s, local_ref, glob_ref, sem, *, to_global):
    def piece(lo, go, size):
        lo = pl.multiple_of(lo, MOE_SEG)
        go = pl.multiple_of(go, MOE_SEG)
        _segment_copy(local_ref, glob_ref, sem, lo, go, size, to_global).start()

    def expert_body(e, carry):
        t = c * N_EXPERTS + e
        off, sg, gb = offl_s[t], seg_s[t], gbase_s[t]
        n_full = sg // MOE_ROWS

        def full_body(w, carry2):
            piece(off + w * MOE_ROWS, gb + w * MOE_ROWS, MOE_ROWS)
            return carry2

        @pl.when(n_full > 0)
        def _():
            lax.fori_loop(0, n_full, full_body, 0)

        done = n_full * MOE_ROWS
        size = MOE_ROWS // 2
        while size >= MOE_SEG:
            bit = sg & size

            @pl.when(bit != 0)
            def _(done=done, size=size):
                piece(off + done, gb + done, size)

            done = done + bit
            size //= 2
        return carry

    lax.fori_loop(0, N_EXPERTS, expert_body, 0)


def _moe_sort_kernel(offl_s, seg_s, gbase_s, h_ref, idx_ref, rank_ref, cb_ref, offv_ref, xs_in_ref,
                     pos_ref, xs_ref, xy_sc, sem):
    del xs_in_ref
    c = pl.program_id(0)
    tc = h_ref.shape[0]
    lane_t = _lane((tc, LANES))
    h = h_ref[...]
    idx = idx_ref[...]
    rank = rank_ref[...]
    adj = offv_ref[0] - cb_ref[0]
    pos = jnp.full((tc, LANES), -1.0, F32)
    for k in range(TOP_K):
        hit = lane_t == idx[:, k:k + 1]
        p_k = rank[:, k:k + 1] + jnp.sum(jnp.where(hit, adj, 0.0), axis=-1, keepdims=True)
        pos = jnp.where(lane_t == k, p_k, pos)
    pos_ref[...] = pos
    sel8 = (lax.broadcasted_iota(jnp.int32, (8, LANES), 0)
            == lax.broadcasted_iota(jnp.int32, (8, LANES), 1)).astype(BF16)
    hi, mid, lo = _split3(pos)
    pos_row = _dot_nt(sel8, hi) + _dot_nt(sel8, mid) + _dot_nt(sel8, lo)

    def gather_body(rb, carry):
        for u in range(2):
            r0 = pl.multiple_of((2 * rb + u) * MOE_ROWS, MOE_ROWS)
            rid = (lax.broadcasted_iota(jnp.int32, (MOE_ROWS, tc), 0) + r0).astype(F32)
            onehot = jnp.zeros((MOE_ROWS, tc), F32)
            for k in range(TOP_K):
                onehot = jnp.where(pos_row[k:k + 1, :] == rid, 1.0, onehot)
            xy_sc[pl.ds(r0, MOE_ROWS), :] = _dot(onehot.astype(BF16), h).astype(BF16)
        return carry

    last = c * N_EXPERTS + N_EXPERTS - 1
    total = offl_s[last] + seg_s[last]
    lax.fori_loop(0, (total + 2 * MOE_ROWS - 1) // (2 * MOE_ROWS), gather_body, 0)
    _segment_copies_start(c, offl_s, seg_s, gbase_s, xy_sc, xs_ref, sem, to_global=True)
    _segment_copies_wait(total, xy_sc, xs_ref, sem, to_global=True)


def _moe_ffn_kernel(be_s, nused_s, x_ref, wgu_ref, wdn_ref, y_ref, wgu_sc, wdn_sc):
    i = pl.program_id(0)
    used = i < nused_s[0]
    changed = jnp.logical_or(i == 0, be_s[i] != be_s[jnp.maximum(i - 1, 0)])

    @pl.when(jnp.logical_and(used, changed))
    def _():
        wgu_sc[...] = wgu_ref[0, 0].astype(BF16)
        wdn_sc[...] = wdn_ref[0, 0].astype(BF16)

    @pl.when(used)
    def _():
        starts = range(0, MOE_BR, MOE_ROWS)
        gus = [_dot(x_ref[r0:r0 + MOE_ROWS, :], wgu_sc[...]) for r0 in starts]
        for r0, gu in zip(starts, gus):
            act = _silu(gu[:, :D_EXPERT]) * gu[:, D_EXPERT:]
            y_ref[r0:r0 + MOE_ROWS, :] = _dot(act.astype(BF16), wdn_sc[...]).astype(BF16)


def _moe_combine_kernel(offl_s, seg_s, gbase_s, h_ref, pos_ref, gate_ref, x1_ref, g5_ref, sgu_ref, sdn_ref,
                        ys_ref, y_ref, yl_sc, sem):
    c = pl.program_id(0)
    tc = h_ref.shape[0]
    n_rows = yl_sc.shape[0]
    yl_sc[...] = jnp.zeros(yl_sc.shape, BF16)
    _segment_copies_start(c, offl_s, seg_s, gbase_s, yl_sc, ys_ref, sem, to_global=False)
    gu = _dot(h_ref[...], sgu_ref[...])
    act = _silu(gu[:, :D_EXPERT]) * gu[:, D_EXPERT:]
    y_ref[...] = _dot(act.astype(BF16), sdn_ref[...])
    last = c * N_EXPERTS + N_EXPERTS - 1
    _segment_copies_wait(offl_s[last] + seg_s[last], yl_sc, ys_ref, sem, to_global=False)
    pos = pos_ref[...]
    gate = gate_ref[...]
    kblk = n_rows // MOE_KSPLIT
    for tb in range(tc // TM):
        pos_t = pos[tb * TM:(tb + 1) * TM]
        gate_t = gate[tb * TM:(tb + 1) * TM]
        out = y_ref[tb * TM:(tb + 1) * TM, :]
        for kb in range(MOE_KSPLIT):
            cid = (lax.broadcasted_iota(jnp.int32, (TM, kblk), 1) + kb * kblk).astype(F32)
            scat = jnp.zeros((TM, kblk), F32)
            for k in range(TOP_K):
                scat = jnp.where(pos_t[:, k:k + 1] == cid, gate_t[:, k:k + 1], scat)
            out = out + _dot(scat.astype(BF16), yl_sc[kb * kblk:(kb + 1) * kblk, :])
        y_ref[tb * TM:(tb + 1) * TM, :] = x1_ref[tb * TM:(tb + 1) * TM, :] + g5_ref[tb] * out


def _moe(x1, g5, h2, idx, rank, gate, base, end, wgu, wdn, layer, sgu, sdn):
    n_tok, d = h2.shape
    tc = MOE_CHUNK
    nc = n_tok // tc
    bpc = tc // TM
    ne = wgu.shape[1]
    assert ne == N_EXPERTS
    ncp = -(-nc // LANES) * LANES
    n_loc = -(-(tc * TOP_K + ne * MOE_SEG) // (MOE_ROWS * MOE_KSPLIT)) * (MOE_ROWS * MOE_KSPLIT)
    assert n_loc >= -(-(tc * TOP_K + ne * (MOE_SEG - 1)) // (2 * MOE_ROWS)) * (2 * MOE_ROWS)
    nblk = -(-(n_tok * TOP_K + nc * ne * (MOE_SEG - 1) + ne * (MOE_BR - 1)) // MOE_BR)
    pad = lambda a: jnp.concatenate([a, jnp.zeros((ncp - nc, LANES), F32)], axis=0)
    cb_lo = pad(base[0::bpc, 0, :])
    cb_hi = pad(end[bpc - 1::bpc, 0, :])
    offl, seg, gbase, be, nused = _moe_tables(cb_lo, cb_hi, nblk)
    to_smem = lambda a: a[:nc, :ne].astype(jnp.int32).reshape(nc * ne)
    tabs = (to_smem(offl), to_smem(seg), to_smem(gbase))
    be_i = be[:nblk, 0].astype(jnp.int32)
    nused_i = nused[0, :1].astype(jnp.int32)
    chunk = lambda w: pl.BlockSpec((tc, w), lambda c, *_: (c, 0))
    row = pl.BlockSpec((1, 1, LANES), lambda c, *_: (c, 0, 0))
    anyspec = pl.BlockSpec(memory_space=pl.ANY)
    const = lambda a: pl.BlockSpec(a.shape, lambda c, *_: (0,) * a.ndim)

    pos, xs = pl.pallas_call(
        _moe_sort_kernel,
        grid_spec=pltpu.PrefetchScalarGridSpec(
            num_scalar_prefetch=3, grid=(nc,),
            in_specs=[chunk(d), chunk(LANES), chunk(LANES), row, row, anyspec],
            out_specs=[chunk(LANES), anyspec],
            scratch_shapes=[pltpu.VMEM((n_loc, d), BF16), pltpu.SemaphoreType.DMA]),
        out_shape=[jax.ShapeDtypeStruct((n_tok, LANES), F32),
                   jax.ShapeDtypeStruct((nblk * MOE_BR, d), BF16)],
        input_output_aliases={8: 1},
        compiler_params=_cparams(("arbitrary",)),
        name="moe_sort",
    )(*tabs, h2, idx, rank, cb_lo[:nc, None, :], offl[:nc, None, :], jnp.zeros((nblk * MOE_BR, d), BF16))

    last = lambda i, be_s, nu_s: jnp.minimum(i, nu_s[0] - 1)
    ys = pl.pallas_call(
        _moe_ffn_kernel,
        grid_spec=pltpu.PrefetchScalarGridSpec(
            num_scalar_prefetch=2, grid=(nblk,),
            in_specs=[pl.BlockSpec((MOE_BR, d), lambda i, be_s, nu_s: (last(i, be_s, nu_s), 0)),
                      pl.BlockSpec((1, 1, d, 2 * D_EXPERT),
                                   lambda i, be_s, nu_s: (layer, be_s[last(i, be_s, nu_s)], 0, 0)),
                      pl.BlockSpec((1, 1, D_EXPERT, d),
                                   lambda i, be_s, nu_s: (layer, be_s[last(i, be_s, nu_s)], 0, 0))],
            out_specs=pl.BlockSpec((MOE_BR, d), lambda i, be_s, nu_s: (last(i, be_s, nu_s), 0)),
            scratch_shapes=[pltpu.VMEM((d, 2 * D_EXPERT), BF16), pltpu.VMEM((D_EXPERT, d), BF16)]),
        out_shape=jax.ShapeDtypeStruct((nblk * MOE_BR, d), BF16),
        input_output_aliases={2: 0},
        compiler_params=_cparams(("arbitrary",)),
        name="moe_ffn",
    )(be_i, nused_i, xs, wgu, wdn)

    return pl.pallas_call(
        _moe_combine_kernel,
        grid_spec=pltpu.PrefetchScalarGridSpec(
            num_scalar_prefetch=3, grid=(nc,),
            in_specs=[chunk(d), chunk(LANES), chunk(LANES), chunk(d),
                      pl.BlockSpec((bpc, 1, d), lambda c, *_: (c, 0, 0)),
                      const(sgu), const(sdn), anyspec],
            out_specs=chunk(d),
            scratch_shapes=[pltpu.VMEM((n_loc, d), BF16), pltpu.SemaphoreType.DMA]),
        out_shape=jax.ShapeDtypeStruct((n_tok, d), F32),
        compiler_params=_cparams(("arbitrary",)),
        name="moe_combine",
    )(*tabs, h2, pos, gate, x1, g5, sgu, sdn, ys)


def _axial_tables(n_rows, rot_dim):
    rows = jnp.repeat(jnp.arange(n_rows), GRID_W).astype(F32)
    cols = jnp.tile(jnp.arange(GRID_W), n_rows).astype(F32)
    axis_dim = rot_dim // 2
    inv_freq = ROPE_THETA ** (-jnp.arange(0, axis_dim, 2, dtype=F32) / axis_dim)
    ang_r = rows[:, None] * inv_freq
    ang_c = cols[:, None] * inv_freq
    ang = jnp.concatenate([ang_r, ang_r, ang_c, ang_c], axis=-1)
    return jnp.cos(ang), jnp.sin(ang)


def _rope_slab_tables(t_len, n_ctx, rot_dim, lane0):
    cos, sin = _axial_tables(t_len // GRID_W, rot_dim)
    half = rot_dim // 4
    sign = jnp.where((jnp.arange(rot_dim) % (2 * half)) < half, -1.0, 1.0).astype(F32)
    cos_t = jnp.ones((t_len + n_ctx, LANES), F32).at[:t_len, lane0:lane0 + rot_dim].set(cos)
    sin_t = jnp.zeros((t_len + n_ctx, LANES), F32).at[:t_len, lane0:lane0 + rot_dim].set(sin * sign)
    return cos_t, sin_t


def _slab_cols(starts, width):
    out = []
    for st in starts:
        out += list(range(st, st + width)) + [-1] * (LANES - width)
    return out


def _gather_cols(w, cols):
    w_ext = jnp.concatenate([w, jnp.zeros((w.shape[0], 1), w.dtype)], axis=1)
    idx = np.array([c if c >= 0 else w.shape[1] for c in cols], np.int32)
    return w_ext[:, idx]


def _pad_row(v, lane0=0):
    return jnp.zeros((1, LANES), F32).at[0, lane0:lane0 + v.shape[0]].set(v.astype(F32))


def _layer_weights(l, w_in, gdn_conv_w, gdn_a_log, gdn_dt_bias, gdn_norm_g, mla_qn_g, mla_kvn_g,
                   mla_w_uq, mla_w_ukv, mla_qk_g, swa_qk_g, swa_sink, gqa_qk_g, w_out, router_w,
                   router_bias):
    hd = HEAD_DIM
    cols = list(range(0, 1024))
    cols += list(range(O_MQ, O_MQ + MLA_Q_RANK)) + list(range(O_MKV, O_MKV + MLA_KV_RANK))
    misc = list(range(O_MKR, O_MKR + MLA_ROPE))
    misc += list(range(O_AA, O_AA + 4)) + list(range(O_AB, O_AB + 4))
    misc += list(range(O_AA + 4, O_AA + 8)) + list(range(O_AB + 4, O_AB + 8))
    cols += misc + [-1] * (LANES - len(misc))
    cols += _slab_cols([O_SQ + hd * i for i in range(4)] + [O_SK + hd * i for i in range(2)]
                       + [O_SV + hd * i for i in range(2)], hd)
    cols += _slab_cols([O_GQ + hd * i for i in range(4)] + [O_GK + hd * i for i in range(2)]
                       + [O_GV + hd * i for i in range(2)], hd)
    assert len(cols) == N_COL
    qd = MLA_NOPE + MLA_ROPE
    uq_cols = []
    for hh in range(MLA_HEADS):
        uq_cols += list(range(hh * qd, hh * qd + qd)) + [-1] * (LANES - qd)
    kvd = MLA_NOPE + MLA_V
    ukv_cols = _slab_cols([hh * kvd for hh in range(MLA_HEADS)], MLA_NOPE)
    ukv_cols += _slab_cols([hh * kvd + MLA_NOPE for hh in range(MLA_HEADS)], MLA_V)
    orow = list(range(0, 256)) + _slab_cols([256 + hd * i for i in range(12)], hd)
    alog = jnp.zeros((1, LANES), F32)
    dtb = jnp.zeros((1, LANES), F32)
    for dd in range(2):
        alog = alog.at[0, 32 + 8 * dd:36 + 8 * dd].set(gdn_a_log[l, dd])
        dtb = dtb.at[0, 32 + 8 * dd:36 + 8 * dd].set(gdn_dt_bias[l, dd])
    return dict(
        win=_gather_cols(w_in[l], cols).astype(BF16),
        conv=gdn_conv_w[l][:, :768],
        alog=alog, dtb=dtb,
        gdn_g=jnp.concatenate([gdn_norm_g[l], gdn_norm_g[l]]).reshape(1, LANES),
        mla_qn_g=mla_qn_g[l], mla_kvn_g=mla_kvn_g[l],
        wuq=_gather_cols(mla_w_uq[l], uq_cols).astype(BF16),
        wukv=_gather_cols(mla_w_ukv[l], ukv_cols).astype(BF16),
        mqg=_pad_row(mla_qk_g[l, 0]), mkg=_pad_row(mla_qk_g[l, 1, :MLA_NOPE]),
        krg=_pad_row(mla_qk_g[l, 1, MLA_NOPE:]),
        sqg=_pad_row(swa_qk_g[l, 0]), skg=_pad_row(swa_qk_g[l, 1]),
        gqg=_pad_row(gqa_qk_g[l, 0]), gkg=_pad_row(gqa_qk_g[l, 1]),
        wout=_gather_cols(w_out[l].T, orow).T.astype(BF16),
        rw=jnp.concatenate([router_w[l], jnp.zeros((D_MODEL, LANES - N_EXPERTS), F32)], axis=1),
        rb=_pad_row(router_bias[l]),
        sink=swa_sink[l],
    )


def _sink_rows(sink, grp, tq):
    hkv = sink.shape[0] // grp
    return jnp.repeat((sink.astype(F32) * LOG2E).reshape(hkv, grp), tq, axis=1).reshape(hkv, grp * tq, 1)


def kernel(x, c, ctx, c_ctx, w_mod, b_mod, norm1_g, norm2_g, w_in, gdn_conv_w, gdn_a_log, gdn_dt_bias,
           gdn_norm_g, mla_qn_g, mla_kvn_g, mla_w_uq, mla_w_ukv, mla_qk_g, swa_qk_g, swa_sink, gqa_qk_g,
           w_out, router_w, router_bias, exp_w_gu, exp_w_down, shared_w_gu, shared_w_down):
    bsz, t_len, d = x.shape
    n_ctx = ctx.shape[1]
    depth = w_mod.shape[0]
    s = t_len + n_ctx
    assert d == D_MODEL and t_len % TM == 0 and n_ctx % TM == 0 and t_len >= 3 * WINDOW
    assert bsz + 1 <= 8
    nbl = t_len // TM
    nb = s // TM

    tabs = (_rope_slab_tables(t_len, n_ctx, HEAD_DIM, 0) + _rope_slab_tables(t_len, n_ctx, MLA_ROPE, MLA_NOPE)
            + _rope_slab_tables(t_len, n_ctx, MLA_ROPE, 0))
    c8 = jnp.zeros((8, d), F32).at[:bsz].set(c).at[bsz].set(c_ctx)
    xs = jnp.concatenate([x, ctx], axis=1)

    tq_d = 512 if t_len % 512 == 0 else TM
    tk_d = 8448 if (s % 8448 == 0) else TM

    for l in range(depth):
        need_ctx = l < depth - 1
        lw = _layer_weights(l, w_in, gdn_conv_w, gdn_a_log, gdn_dt_bias, gdn_norm_g, mla_qn_g, mla_kvn_g,
                            mla_w_uq, mla_w_ukv, mla_qk_g, swa_qk_g, swa_sink, gqa_qk_g, w_out, router_w,
                            router_bias)
        mod = _modulation(c8, w_mod[l], b_mod[l])
        mod_tab = jnp.stack([mod[:bsz], jnp.broadcast_to(mod[bsz], (bsz, 6 * d))], axis=1)[:, :, None, :]

        (ga, misc, mq, mk, mv, sq, sk, sv, gq, gk, gv) = _inproj(
            xs, mod_tab, norm1_g[l].reshape(1, d), lw['win'], lw, tabs, nbl)

        qkv, lab = _gdn_prep(ga, misc, lw['conv'], lw['alog'], lw['dtb'], nbl)
        labt = jnp.swapaxes(lab[..., :8], 2, 3)
        o_f, o_b = _gdn_scan(qkv, lab, labt, nbl=nbl)

        mo = _flash(mq, mk, mv, grp=1, tq=2 * tq_d, tk=tk_d, q_blk0=0, nq=t_len // (2 * tq_d),
                    k_blk0=0, nk=s // tk_d)
        go = _flash(gq, gk, gv, grp=2, tq=tq_d, tk=tk_d, q_blk0=0, nq=t_len // tq_d, k_blk0=0, nk=s // tk_d)
        so = _swa(sq, sk, sv, _sink_rows(lw['sink'], 2, WINDOW), grp=2, tq=tq_d, t_len=t_len, n_ctx=n_ctx)
        lat = (mo, so, go)
        ctx_out = lat
        if need_ctx:
            cb0 = t_len // n_ctx
            ctx_out = (
                _flash(mq, mk, mv, grp=1, tq=n_ctx, tk=n_ctx, q_blk0=cb0, nq=1, k_blk0=cb0, nk=1),
                _flash(sq, sk, sv, grp=2, tq=n_ctx, tk=n_ctx, q_blk0=cb0, nq=1, k_blk0=cb0, nk=1,
                       sink_rows=_sink_rows(lw['sink'], 2, n_ctx)),
                _flash(gq, gk, gv, grp=2, tq=n_ctx, tk=n_ctx, q_blk0=cb0, nq=1, k_blk0=cb0, nk=1))

        n_blk = nb if need_ctx else nbl
        x1, h2, idx, gate, rank, base, end = _outproj(
            xs, mod_tab, o_f, o_b, ga, lw['gdn_g'], lat, ctx_out, lw['wout'], norm2_g[l].reshape(1, d),
            lw['rw'], lw['rb'], n_blk=n_blk, nbl=nbl)

        n_tok = bsz * n_blk * TM
        assert n_tok % MOE_CHUNK == 0
        g5 = jnp.repeat(mod_tab[:, :, 0, 5 * d:], jnp.array([nbl, nb - nbl]), axis=1,
                        total_repeat_length=nb)[:, :n_blk].reshape(bsz * n_blk, 1, d)
        xs = _moe(x1.reshape(n_tok, d), g5, h2.reshape(n_tok, d), idx.reshape(n_tok, LANES),
                  rank.reshape(n_tok, LANES), gate.reshape(n_tok, LANES), base, end, exp_w_gu, exp_w_down, l,
                  shared_w_gu[l].astype(BF16), shared_w_down[l].astype(BF16)).reshape(bsz, n_blk * TM, d)
    return xs[:, :t_len]
```

```python
import functools
import math

import numpy as np
import jax
import jax.numpy as jnp
from jax import lax
from jax.experimental import pallas as pl
from jax.experimental.pallas import tpu as pltpu

F32 = jnp.float32
BF16 = jnp.bfloat16

LANES = 128
TM = 256
VMEM_LIMIT = 56 * 1024 * 1024

D_MODEL = 1024
GRID_W = 64
HEAD_DIM = 64
ROPE_THETA = 10000.0
NORM_EPS = 1e-6
ATTN_SCALE = HEAD_DIM ** -0.5
GDN_HEADS = 4
GDN_DK = 64
GDN_CHUNK = 64
GDN_CONV = 5
MLA_HEADS = 4
MLA_Q_RANK = 256
MLA_KV_RANK = 128
MLA_NOPE = 64
MLA_ROPE = 32
MLA_V = 64
MLA_SCALE = (MLA_NOPE + MLA_ROPE) ** -0.5
SWA_HEADS = 4
SWA_KV_HEADS = 2
WINDOW = 128
GQA_HEADS = 4
GQA_KV_HEADS = 2
N_EXPERTS = 64
TOP_K = 6
D_EXPERT = 384
ROUTED_SCALE = 2.5
LOG2E = math.log2(math.e)
ONES_LANE = HEAD_DIM
FLASH_KB = 256
MOE_CHUNK = 512
MOE_ROWS = 128
MOE_BR = 512
MOE_SEG = 16
MOE_KSPLIT = 3

O_AQ, O_AK, O_AV, O_AZ, O_AA, O_AB = 0, 256, 512, 768, 1024, 1032
O_MQ, O_MKV, O_MKR = 1040, 1296, 1424
O_SQ, O_SK, O_SV = 1456, 1712, 1840
O_GQ, O_GK, O_GV = 1968, 2224, 2352
D_IN = 2480
C_GDN, C_MLA, C_MISC, C_SWA, C_GQA, N_COL = 0, 1024, 1408, 1536, 2560, 3584


def _cparams(sem):
    return pltpu.CompilerParams(dimension_semantics=sem, vmem_limit_bytes=VMEM_LIMIT)


def _dot(a, b):
    return jnp.dot(a, b, preferred_element_type=F32)


def _dot_nt(a, b):
    return lax.dot_general(a, b, (((1,), (1,)), ((), ())), preferred_element_type=F32)


def _dot_tn(a, b):
    return lax.dot_general(a, b, (((0,), (0,)), ((), ())), preferred_element_type=F32)


def _split3(x):
    hi = x.astype(BF16)
    r1 = x - hi.astype(F32)
    mid = r1.astype(BF16)
    lo = (r1 - mid.astype(F32)).astype(BF16)
    return hi, mid, lo


def _dot_exact_lhs(a_bf16, x):
    hi, mid, lo = _split3(x)
    return _dot(a_bf16, hi) + _dot(a_bf16, mid) + _dot(a_bf16, lo)


def _dot_exact_rhs(x, b_bf16):
    hi, mid, lo = _split3(x)
    return _dot(hi, b_bf16) + _dot(mid, b_bf16) + _dot(lo, b_bf16)


def _dot_hp(a, b):
    ah = a.astype(BF16)
    al = (a - ah.astype(F32)).astype(BF16)
    bh = b.astype(BF16)
    bl = (b - bh.astype(F32)).astype(BF16)
    return _dot(ah, bh) + _dot(ah, bl) + _dot(al, bh)


def _silu(x):
    return x * (1.0 / (1.0 + jnp.exp(-x)))


def _lane(shape):
    return lax.broadcasted_iota(jnp.int32, shape, len(shape) - 1)


def _mod_kernel(c_ref, w_ref, b_ref, o_ref):
    s = _silu(c_ref[...])
    o_ref[...] = _dot(s.astype(BF16), w_ref[...].astype(BF16)) + b_ref[...]


def _modulation(c8, w_mod_l, b_mod_l):
    d = c8.shape[1]
    n = w_mod_l.shape[1]
    return pl.pallas_call(
        _mod_kernel,
        grid=(n // d,),
        in_specs=[pl.BlockSpec((8, d), lambda j: (0, 0)),
                  pl.BlockSpec((d, d), lambda j: (0, j)),
                  pl.BlockSpec((1, d), lambda j: (0, j))],
        out_specs=pl.BlockSpec((8, d), lambda j: (0, j)),
        out_shape=jax.ShapeDtypeStruct((8, n), F32),
        compiler_params=_cparams(("arbitrary",)),
        name="modulation",
    )(c8, w_mod_l, b_mod_l.reshape(1, n))


def _rope_slab(x, cos, sin_signed, half):
    lane = _lane(x.shape)
    fwd = pltpu.roll(x, LANES - half, 1)
    bwd = pltpu.roll(x, half, 1)
    rot = jnp.where(lane % (2 * half) < half, fwd, bwd)
    return x * cos + rot * sin_signed


def _inproj_kernel(x_ref, mod_ref, g1_ref, win_ref, qn_g_ref, kvn_g_ref, wuq_ref, wukv_ref,
                   mqg_ref, mkg_ref, krg_ref, sqg_ref, skg_ref, gqg_ref, gkg_ref,
                   cos_hd_ref, sin_hd_ref, cos_mq_ref, sin_mq_ref, cos_kr_ref, sin_kr_ref,
                   ga_ref, misc_ref, mq_ref, mk_ref, mv_ref, sq_ref, sk_ref, sv_ref,
                   gq_ref, gk_ref, gv_ref):
    d = D_MODEL
    x = x_ref[0]
    m = mod_ref[0, 0]
    shift, scale = m[:, 0:d], m[:, d:2 * d]
    xn = x * lax.rsqrt(jnp.mean(x * x, axis=-1, keepdims=True) + NORM_EPS) * g1_ref[...]
    h = xn * (1.0 + scale) + shift
    hb = h.astype(BF16)

    def proj(c0, width):
        return _dot(hb, win_ref[:, c0:c0 + width])

    ga_ref[0] = proj(C_GDN, 1024)
    pm = proj(C_MLA, C_SWA - C_MLA)
    misc = pm[:, C_MISC - C_MLA:C_MISC - C_MLA + LANES]
    misc_ref[0] = misc

    lane = _lane((TM, LANES))
    cos_hd, sin_hd = cos_hd_ref[...], sin_hd_ref[...]

    cq = pm[:, 0:MLA_Q_RANK]
    cqn = cq * lax.rsqrt(jnp.mean(cq * cq, axis=-1, keepdims=True) + NORM_EPS) * qn_g_ref[...]
    qup = _dot(cqn.astype(BF16), wuq_ref[...])
    ckv = pm[:, MLA_Q_RANK:MLA_Q_RANK + MLA_KV_RANK]
    ckvn = ckv * lax.rsqrt(jnp.mean(ckv * ckv, axis=-1, keepdims=True) + NORM_EPS) * kvn_g_ref[...]
    kvup = _dot(ckvn.astype(BF16), wukv_ref[...])
    kr = jnp.where(lane < MLA_ROPE, misc, 0.0)
    kr = kr * lax.rsqrt(jnp.sum(kr * kr, axis=-1, keepdims=True) / MLA_ROPE + NORM_EPS) * krg_ref[...]
    kr = _rope_slab(kr, cos_kr_ref[...], sin_kr_ref[...], MLA_ROPE // 4)
    kpe = pltpu.roll(kr, MLA_NOPE, 1)
    is_nope = lane < MLA_NOPE
    for hh in range(MLA_HEADS):
        q = qup[:, hh * LANES:(hh + 1) * LANES]
        q2 = q * q
        ss_n = jnp.sum(jnp.where(is_nope, q2, 0.0), axis=-1, keepdims=True)
        ss_p = jnp.sum(jnp.where(is_nope, 0.0, q2), axis=-1, keepdims=True)
        r = jnp.where(is_nope, lax.rsqrt(ss_n / MLA_NOPE + NORM_EPS), lax.rsqrt(ss_p / MLA_ROPE + NORM_EPS))
        q = q * r * mqg_ref[...]
        q = _rope_slab(q, cos_mq_ref[...], sin_mq_ref[...], MLA_ROPE // 4)
        mq_ref[0, hh] = (q * (MLA_SCALE * LOG2E)).astype(BF16)
        k = kvup[:, hh * LANES:(hh + 1) * LANES]
        k = k * lax.rsqrt(jnp.sum(k * k, axis=-1, keepdims=True) / MLA_NOPE + NORM_EPS) * mkg_ref[...]
        mk_ref[0, hh] = (k + kpe).astype(BF16)
        v = kvup[:, (MLA_HEADS + hh) * LANES:(MLA_HEADS + hh + 1) * LANES]
        mv_ref[0, hh] = jnp.where(lane == ONES_LANE, 1.0, v).astype(BF16)

    def gqa_prep(base, qg_ref, kg_ref, q_ref, k_ref, v_ref, nq, nkv):
        p = proj(base, (nq + 2 * nkv) * LANES)
        base = 0
        for hh in range(nq):
            q = p[:, base + hh * LANES:base + (hh + 1) * LANES]
            q = q * lax.rsqrt(jnp.sum(q * q, axis=-1, keepdims=True) / HEAD_DIM + NORM_EPS) * qg_ref[...]
            q = _rope_slab(q, cos_hd, sin_hd, HEAD_DIM // 4) * (ATTN_SCALE * LOG2E)
            q_ref[0, hh] = q.astype(BF16)
        for hh in range(nkv):
            k = p[:, base + (nq + hh) * LANES:base + (nq + hh + 1) * LANES]
            k = k * lax.rsqrt(jnp.sum(k * k, axis=-1, keepdims=True) / HEAD_DIM + NORM_EPS) * kg_ref[...]
            k = _rope_slab(k, cos_hd, sin_hd, HEAD_DIM // 4)
            k_ref[0, hh] = k.astype(BF16)
            v = p[:, base + (nq + nkv + hh) * LANES:base + (nq + nkv + hh + 1) * LANES]
            v_ref[0, hh] = jnp.where(lane == ONES_LANE, 1.0, v).astype(BF16)

    gqa_prep(C_SWA, sqg_ref, skg_ref, sq_ref, sk_ref, sv_ref, SWA_HEADS, SWA_KV_HEADS)
    gqa_prep(C_GQA, gqg_ref, gkg_ref, gq_ref, gk_ref, gv_ref, GQA_HEADS, GQA_KV_HEADS)


def _inproj(x, mod_tab, g1, win, lw, tabs, nbl):
    bsz, s, d = x.shape
    nb = s // TM
    row = lambda a: a.reshape(1, -1)
    full = lambda a: pl.BlockSpec(a.shape, lambda b, i: (0,) * a.ndim)
    tab_spec = pl.BlockSpec((TM, LANES), lambda b, i: (i, 0))
    head_out = lambda nh: pl.BlockSpec((1, nh, TM, LANES), lambda b, i: (b, 0, i, 0))
    head_shape = lambda nh: jax.ShapeDtypeStruct((bsz, nh, s, LANES), BF16)
    small = [row(lw['mla_qn_g']), row(lw['mla_kvn_g']), lw['wuq'], lw['wukv'],
             lw['mqg'], lw['mkg'], lw['krg'], lw['sqg'], lw['skg'], lw['gqg'], lw['gkg']]
    return pl.pallas_call(
        _inproj_kernel,
        grid=(bsz, nb),
        in_specs=[pl.BlockSpec((1, TM, d), lambda b, i: (b, i, 0)),
                  pl.BlockSpec((1, 1, 1, 6 * d), lambda b, i: (b, i // nbl, 0, 0)),
                  full(g1), full(win)] + [full(a) for a in small] + [tab_spec] * 6,
        out_specs=[pl.BlockSpec((1, TM, 1024), lambda b, i: (b, i, 0)),
                   pl.BlockSpec((1, TM, LANES), lambda b, i: (b, i, 0)),
                   head_out(4), head_out(4), head_out(4),
                   head_out(4), head_out(2), head_out(2),
                   head_out(4), head_out(2), head_out(2)],
        out_shape=[jax.ShapeDtypeStruct((bsz, s, 1024), F32),
                   jax.ShapeDtypeStruct((bsz, s, LANES), F32),
                   head_shape(4), head_shape(4), head_shape(4),
                   head_shape(4), head_shape(2), head_shape(2),
                   head_shape(4), head_shape(2), head_shape(2)],
        compiler_params=_cparams(("parallel", "parallel")),
        name="inproj",
    )(x, mod_tab, g1, win, *small, *tabs)


def _flash_kernel(*refs, grp, tq, nk, use_sink):
    refs = list(refs)
    sink_ref = refs.pop(0) if use_sink else None
    q_ref, k_ref, v_ref = refs[0], refs[1], refs[2]
    o_ref, m_sc, acc_sc = refs[-3], refs[-2], refs[-1]
    kj = pl.program_id(3)
    tk = k_ref.shape[2]
    kb = min(FLASH_KB, tk)

    @pl.when(kj == 0)
    def _():
        m_sc[...] = jnp.full(m_sc.shape, -jnp.inf, F32)
        acc_sc[...] = jnp.zeros(acc_sc.shape, F32)

    q = q_ref[0].reshape(grp * tq, LANES)
    m = m_sc[...]
    acc = acc_sc[...]
    for j in range(tk // kb):
        s = _dot_nt(q, k_ref[0, 0, j * kb:(j + 1) * kb, :])
        m_new = jnp.maximum(m, jnp.max(s, axis=-1, keepdims=True))
        alpha = jnp.exp2(m - m_new)
        pr = jnp.exp2(s - jnp.concatenate([m_new] * (kb // LANES), axis=1))
        acc = alpha * acc + _dot(pr.astype(BF16), v_ref[0, 0, j * kb:(j + 1) * kb, :])
        m = m_new
    m_sc[...] = m
    acc_sc[...] = acc

    @pl.when(kj == nk - 1)
    def _():
        l = acc[:, ONES_LANE:ONES_LANE + 1]
        out = acc
        if use_sink:
            sk = sink_ref[0]
            m_old = m[:, 0:1]
            m_fin = jnp.maximum(m_old, sk)
            a = jnp.exp2(m_old - m_fin)
            l = l * a + jnp.exp2(sk - m_fin)
            out = out * a
        o_ref[0] = (out / l).reshape(grp, tq, LANES).astype(o_ref.dtype)


def _flash(q, k, v, *, grp, tq, tk, q_blk0, nq, k_blk0, nk, sink_rows=None):
    bsz, hq, s, _ = q.shape
    hkv = hq // grp
    use_sink = sink_rows is not None
    in_specs = []
    args = []
    if use_sink:
        in_specs.append(pl.BlockSpec((1, grp * tq, 1), lambda b, h, i, j: (h, 0, 0)))
        args.append(sink_rows)
    in_specs += [pl.BlockSpec((1, grp, tq, LANES), lambda b, h, i, j: (b, h, i + q_blk0, 0)),
                 pl.BlockSpec((1, 1, tk, LANES), lambda b, h, i, j: (b, h, j + k_blk0, 0)),
                 pl.BlockSpec((1, 1, tk, LANES), lambda b, h, i, j: (b, h, j + k_blk0, 0))]
    args += [q, k, v]
    rows = grp * tq
    return pl.pallas_call(
        functools.partial(_flash_kernel, grp=grp, tq=tq, nk=nk, use_sink=use_sink),
        grid=(bsz, hkv, nq, nk),
        in_specs=in_specs,
        out_specs=pl.BlockSpec((1, grp, tq, LANES), lambda b, h, i, j: (b, h, i, 0)),
        out_shape=jax.ShapeDtypeStruct((bsz, hq, nq * tq, LANES), BF16),
        scratch_shapes=[pltpu.VMEM((rows, LANES), F32), pltpu.VMEM((rows, LANES), F32)],
        compiler_params=_cparams(("parallel", "parallel", "parallel", "arbitrary")),
        name="flash",
    )(*args)


def _swa_kernel(sink_ref, q_ref, k_ref, v_ref, o_ref, *, grp, tq, t_len, n_ctx):
    w = WINDOW
    n = pl.program_id(2)
    kc = k_ref[0, 0, pl.ds(t_len, n_ctx), :]
    vc = v_ref[0, 0, pl.ds(t_len, n_ctx), :]
    for qb in range(tq // w):
        q0 = n * tq + qb * w
        start = pl.multiple_of(jnp.clip(q0 - w, 0, t_len - 3 * w), w)
        q = q_ref[0, :, qb * w:(qb + 1) * w, :].reshape(grp * w, LANES)
        kl = k_ref[0, 0, pl.ds(start, 3 * w), :]
        vl = v_ref[0, 0, pl.ds(start, 3 * w), :]
        s_loc = _dot_nt(q, kl)
        qpos = q0 + (lax.broadcasted_iota(jnp.int32, s_loc.shape, 0) % w)
        kpos = start + lax.broadcasted_iota(jnp.int32, s_loc.shape, 1)
        s_loc = jnp.where(jnp.abs(qpos - kpos) <= w, s_loc, -jnp.inf)
        s_ctx = _dot_nt(q, kc)
        sk = sink_ref[0]
        m = jnp.maximum(jnp.maximum(jnp.max(s_loc, axis=-1, keepdims=True),
                                    jnp.max(s_ctx, axis=-1, keepdims=True)), sk)
        p_loc = jnp.exp2(s_loc - m)
        p_ctx = jnp.exp2(s_ctx - m)
        o = _dot(p_loc.astype(BF16), vl) + _dot(p_ctx.astype(BF16), vc)
        l = o[:, ONES_LANE:ONES_LANE + 1] + jnp.exp2(sk - m)
        o_ref[0, :, qb * w:(qb + 1) * w, :] = (o / l).reshape(grp, w, LANES).astype(o_ref.dtype)


def _swa(q, k, v, sink_rows, *, grp, tq, t_len, n_ctx):
    bsz, hq, s, _ = q.shape
    hkv = hq // grp
    return pl.pallas_call(
        functools.partial(_swa_kernel, grp=grp, tq=tq, t_len=t_len, n_ctx=n_ctx),
        grid=(bsz, hkv, t_len // tq),
        in_specs=[pl.BlockSpec((1, grp * WINDOW, 1), lambda b, h, n: (h, 0, 0)),
                  pl.BlockSpec((1, grp, tq, LANES), lambda b, h, n: (b, h, n, 0)),
                  pl.BlockSpec((1, 1, s, LANES), lambda b, h, n: (b, h, 0, 0)),
                  pl.BlockSpec((1, 1, s, LANES), lambda b, h, n: (b, h, 0, 0))],
        out_specs=pl.BlockSpec((1, grp, tq, LANES), lambda b, h, n: (b, h, n, 0)),
        out_shape=jax.ShapeDtypeStruct((bsz, hq, t_len, LANES), BF16),
        compiler_params=_cparams(("parallel", "parallel", "arbitrary")),
        name="swa",
    )(sink_rows, q, k, v)


def _gdn_prep_kernel(cur_ref, prev_ref, next_ref, misc_ref, cw_ref, alog_ref, dtb_ref,
                     qkv_ref, lab_ref, *, nbl, nb):
    i = pl.program_id(1)
    first = jnp.logical_or(i == 0, i == nbl)
    last = jnp.logical_or(i == nbl - 1, i == nb - 1)
    cur = cur_ref[0]
    prev = jnp.where(first, 0.0, prev_ref[0])
    nxt = jnp.where(last, 0.0, next_ref[0])
    xe = jnp.concatenate([prev, cur, nxt], axis=0)
    cw = cw_ref[...]
    acc = jnp.zeros(cur.shape, F32)
    for j in range(GDN_CONV):
        off = 8 + j - (GDN_CONV - 1) // 2
        acc = acc + xe[off:off + TM, :] * cw[j:j + 1, :]
    y = _silu(acc)
    lane = _lane((TM, LANES))
    lo = lane < GDN_DK
    for sl in range(6):
        t = y[:, sl * LANES:(sl + 1) * LANES]
        if sl < 4:
            t2 = t * t
            ss0 = jnp.sum(jnp.where(lo, t2, 0.0), axis=-1, keepdims=True)
            ss1 = jnp.sum(jnp.where(lo, 0.0, t2), axis=-1, keepdims=True)
            t = t * jnp.where(lo, lax.rsqrt(ss0 + NORM_EPS), lax.rsqrt(ss1 + NORM_EPS))
            if sl < 2:
                t = t * GDN_DK ** -0.5
        qkv_ref[0, :, sl * LANES:(sl + 1) * LANES] = t
    xm = misc_ref[0]
    za = xm + dtb_ref[...]
    softplus = jnp.maximum(za, 0.0) + jnp.log(1.0 + jnp.exp(-jnp.abs(za)))
    log_a = -jnp.exp(alog_ref[...]) * softplus
    beta = 1.0 / (1.0 + jnp.exp(-xm))
    is_a = (lane % 8) < 4
    vals = jnp.where(is_a, log_a, beta)
    lab_ref[0, 0] = pltpu.roll(vals, LANES - 32, 1)
    lab_ref[0, 1] = pltpu.roll(vals, LANES - 40, 1)


def _gdn_prep(ga, misc, cw, alog_row, dtb_row, nbl):
    bsz, s, _ = ga.shape
    nb = s // TM
    r8 = TM // 8
    n8 = s // 8
    return pl.pallas_call(
        functools.partial(_gdn_prep_kernel, nbl=nbl, nb=nb),
        grid=(bsz, nb),
        in_specs=[pl.BlockSpec((1, TM, 768), lambda b, i: (b, i, 0)),
                  pl.BlockSpec((1, 8, 768), lambda b, i: (b, jnp.maximum(i * r8 - 1, 0), 0)),
                  pl.BlockSpec((1, 8, 768), lambda b, i: (b, jnp.minimum((i + 1) * r8, n8 - 1), 0)),
                  pl.BlockSpec((1, TM, LANES), lambda b, i: (b, i, 0)),
                  pl.BlockSpec(cw.shape, lambda b, i: (0, 0)),
                  pl.BlockSpec((1, LANES), lambda b, i: (0, 0)),
                  pl.BlockSpec((1, LANES), lambda b, i: (0, 0))],
        out_specs=[pl.BlockSpec((1, TM, 768), lambda b, i: (b, i, 0)),
                   pl.BlockSpec((1, 2, TM, LANES), lambda b, i: (b, 0, i, 0))],
        out_shape=[jax.ShapeDtypeStruct((bsz, s, 768), F32),
                   jax.ShapeDtypeStruct((bsz, 2, s, LANES), F32)],
        compiler_params=_cparams(("parallel", "parallel")),
        name="gdn_prep",
    )(ga, ga, ga, misc, cw, alog_row, dtb_row)


def _gdn_scan_kernel(qkvf_ref, labf_ref, labtf_ref, qkvb_ref, labb_ref, labtb_ref, of_ref, ob_ref, s_sc):
    c = GDN_CHUNK
    c2 = 2 * c
    i = pl.program_id(1)

    @pl.when(i == 0)
    def _():
        s_sc[...] = jnp.zeros(s_sc.shape, F32)

    ri = lax.broadcasted_iota(jnp.int32, (c2, c2), 0)
    ci = lax.broadcasted_iota(jnp.int32, (c2, c2), 1)
    same = (ri // c) == (ci // c)
    eye = (ri == ci).astype(F32)
    lane = _lane((c, LANES))
    lo = lane < GDN_DK

    def stack(x):
        return jnp.concatenate([jnp.where(lo, x, 0.0), jnp.where(lo, 0.0, x)], axis=0)

    def fold(mat):
        return mat[0:c, :] + mat[c:c2, :]

    inst_f = _gdn_instances(qkvf_ref, labf_ref, labtf_ref, of_ref, 0, False, stack, fold)
    inst_b = _gdn_instances(qkvb_ref, labb_ref, labtb_ref, ob_ref, 2, True, stack, fold)
    inst = [it for pair in zip(inst_f, inst_b) for it in pair]
    tinv = [eye - it['a'] for it in inst]
    pw = [it['a'] for it in inst]
    for _ in range(5):
        pw = [_dot_hp(x, x) for x in pw]
        tinv = [t + _dot_hp(t, x) for t, x in zip(tinv, pw)]
    eye_b = eye.astype(BF16)
    for it, t in zip(inst, tinv):
        tf = fold(t).astype(BF16)
        it['u'] = _dot(tf, it['vb'])
        it['wq'] = jnp.concatenate([_dot(tf, it['kg']).astype(BF16), it['qg']], axis=0)
        it['kdt'] = _dot_nt(eye_b, it['kd']).astype(BF16)
    state = [s_sc[j] for j in range(4)]
    for it in inst:
        st = state[it['slot']]
        ws = _dot(it['wq'], st.astype(BF16))
        v_new = it['u'] - ws[0:c, :]
        o = ws[c:c2, :] + _dot(it['qkf'], stack(v_new).astype(BF16))
        it['o_ref'][0, it['rows'], it['pr'] * LANES:(it['pr'] + 1) * LANES] = o
        upd = _dot(it['kdt'], v_new.astype(BF16))
        state[it['slot']] = st * it['sdec'] + jnp.where(same, upd, 0.0)
    for j in range(4):
        s_sc[j] = state[j]


def _gdn_instances(qkv_ref, lab_ref, labt_ref, o_ref, slot0, reverse, stack, fold):
    c = GDN_CHUNK
    c2 = 2 * c
    ri = lax.broadcasted_iota(jnp.int32, (c2, c2), 0)
    ci = lax.broadcasted_iota(jnp.int32, (c2, c2), 1)
    same = (ri // c) == (ci // c)
    if reverse:
        incl = jnp.logical_and(same, ri <= ci)
        strict = jnp.logical_and(same, ri < ci)
    else:
        incl = jnp.logical_and(same, ri >= ci)
        strict = jnp.logical_and(same, ri > ci)
    r1 = lax.broadcasted_iota(jnp.int32, (c, c), 0)
    c1 = lax.broadcasted_iota(jnp.int32, (c, c), 1)
    cum_col = ((r1 <= c1) if reverse else (r1 >= c1)).astype(BF16)
    rr = lax.broadcasted_iota(jnp.int32, (c, c2), 0)
    cc = lax.broadcasted_iota(jnp.int32, (c, c2), 1) % c
    cum_row = ((rr >= cc) if reverse else (rr <= cc)).astype(BF16)
    lane = _lane((c, LANES))
    lo = lane < GDN_DK
    lane2 = _lane((1, c2))
    lo2 = lane2 < c
    g_last_row = 0 if reverse else c - 1
    order = list(range(TM // c - 1, -1, -1) if reverse else range(TM // c))
    inst = []
    for ch in order:
        rows = slice(ch * c, (ch + 1) * c)
        lab = lab_ref[0, 0, rows, :]
        g_cols = _dot_exact_lhs(cum_col, lab)
        la_rows = labt_ref[0, 0, :, rows]
        g_rows = _dot_exact_rhs(la_rows, cum_row)
        for pr in range(2):
            h0, h1 = 2 * pr, 2 * pr + 1
            q = qkv_ref[0, rows, pr * LANES:(pr + 1) * LANES]
            k = qkv_ref[0, rows, (2 + pr) * LANES:(3 + pr) * LANES]
            v = qkv_ref[0, rows, (4 + pr) * LANES:(5 + pr) * LANES]
            beta = jnp.where(lo, lab[:, 4 + h0:5 + h0], lab[:, 4 + h1:5 + h1])
            g = jnp.where(lo, g_cols[:, h0:h0 + 1], g_cols[:, h1:h1 + 1])
            g_st_col = jnp.concatenate([g_cols[:, h0:h0 + 1], g_cols[:, h1:h1 + 1]], axis=0)
            g_st_row = jnp.where(lo2, g_rows[h0:h0 + 1, :], g_rows[h1:h1 + 1, :])
            diff = g_st_col - g_st_row
            dec = jnp.where(incl, jnp.exp(jnp.where(incl, diff, 0.0)), 0.0)
            kb = k * beta
            eg = jnp.exp(g)
            k_st = stack(k).astype(BF16)
            a = jnp.where(strict, _dot_nt(stack(kb).astype(BF16), k_st) * dec, 0.0)
            qk = _dot_nt(stack(q).astype(BF16), k_st) * dec
            gl = g[g_last_row:g_last_row + 1, :]
            gl_col = jnp.concatenate([jnp.broadcast_to(g_cols[g_last_row:g_last_row + 1, h0:h0 + 1], (c, 1)),
                                      jnp.broadcast_to(g_cols[g_last_row:g_last_row + 1, h1:h1 + 1], (c, 1))],
                                     axis=0)
            inst.append(dict(rows=rows, pr=pr, slot=slot0 + pr, o_ref=o_ref, a=a,
                             vb=stack(v * beta).astype(BF16),
                             kg=stack(kb * eg).astype(BF16), qg=(q * eg).astype(BF16),
                             qkf=fold(qk).astype(BF16), kd=(k * jnp.exp(gl - g)).astype(BF16),
                             sdec=jnp.exp(gl_col)))
    return inst


def _gdn_scan(qkv, lab, labt, *, nbl):
    bsz, s, _ = qkv.shape
    nb = s // TM
    nbc = nb - nbl

    def blk_f(i):
        return jnp.where(i < nbc, nbl + i, i - nbc)

    def blk_b(i):
        return jnp.where(i < nbc, nb - 1 - i, nbl - 1 - (i - nbc))

    def specs(blk, dsel):
        return [pl.BlockSpec((1, TM, 768), lambda b, i: (b, blk(i), 0)),
                pl.BlockSpec((1, 1, TM, LANES), lambda b, i: (b, dsel, blk(i), 0)),
                pl.BlockSpec((1, 1, 8, TM), lambda b, i: (b, dsel, 0, blk(i)))]

    out = jax.ShapeDtypeStruct((bsz, s, 2 * LANES), F32)
    return pl.pallas_call(
        _gdn_scan_kernel,
        grid=(bsz, nb),
        in_specs=specs(blk_f, 0) + specs(blk_b, 1),
        out_specs=[pl.BlockSpec((1, TM, 2 * LANES), lambda b, i: (b, blk_f(i), 0)),
                   pl.BlockSpec((1, TM, 2 * LANES), lambda b, i: (b, blk_b(i), 0))],
        out_shape=[out, out],
        scratch_shapes=[pltpu.VMEM((4, LANES, LANES), F32)],
        compiler_params=_cparams(("parallel", "arbitrary")),
        name="gdn_scan",
    )(qkv, lab, labt, qkv, lab, labt)


def _outproj_kernel(x_ref, mod_ref, of_ref, ob_ref, z_ref, gg_ref, mo_ref, so_ref, go_ref,
                    mc_ref, sc_ref, gc_ref, wout_ref, g2_ref, rw_ref, rb_ref,
                    x1_ref, h2_ref, idx_ref, gate_ref, rank_ref, base_ref, end_ref, run_sc, *, nbl):
    d = D_MODEL
    first = jnp.logical_and(pl.program_id(0) == 0, pl.program_id(1) == 0)
    is_ctx = pl.program_id(1) >= nbl

    @pl.when(first)
    def _():
        run_sc[...] = jnp.zeros(run_sc.shape, F32)

    m = mod_ref[0, 0]
    lane = _lane((TM, LANES))
    lo = lane < HEAD_DIM
    o = of_ref[0] + ob_ref[0]
    z = z_ref[0]
    mix = []
    for pr in range(2):
        t = o[:, pr * LANES:(pr + 1) * LANES]
        t2 = t * t
        ms0 = jnp.sum(jnp.where(lo, t2, 0.0), axis=-1, keepdims=True) / HEAD_DIM
        ms1 = jnp.sum(jnp.where(lo, 0.0, t2), axis=-1, keepdims=True) / HEAD_DIM
        t = t * jnp.where(lo, lax.rsqrt(ms0 + NORM_EPS), lax.rsqrt(ms1 + NORM_EPS)) * gg_ref[...]
        t = t * _silu(z[:, pr * LANES:(pr + 1) * LANES])
        mix.append(t.astype(BF16))
    for lat_ref, ctx_ref in ((mo_ref, mc_ref), (so_ref, sc_ref), (go_ref, gc_ref)):
        for hh in range(4):
            mix.append(jnp.where(is_ctx, ctx_ref[0, hh], lat_ref[0, hh]))
    proj = _dot(jnp.concatenate(mix, axis=1), wout_ref[...])
    x1 = x_ref[0] + m[:, 2 * d:3 * d] * proj
    x1_ref[0] = x1
    xn = x1 * lax.rsqrt(jnp.mean(x1 * x1, axis=-1, keepdims=True) + NORM_EPS) * g2_ref[...]
    h2 = xn * (1.0 + m[:, 4 * d:5 * d]) + m[:, 3 * d:4 * d]
    h2_ref[0] = h2.astype(BF16)

    logits = _dot_hp(h2, rw_ref[...])
    scores = 1.0 / (1.0 + jnp.exp(-logits))
    valid = lane < N_EXPERTS
    sel = jnp.where(valid, scores + rb_ref[...], -jnp.inf)
    member = jnp.zeros((TM, LANES), F32)
    picks = []
    for _ in range(TOP_K):
        mx = jnp.max(sel, axis=-1, keepdims=True)
        idx = jnp.min(jnp.where(sel == mx, lane, LANES), axis=-1, keepdims=True)
        hit = lane == idx
        gate = jnp.sum(jnp.where(hit, scores, 0.0), axis=-1, keepdims=True)
        sel = jnp.where(hit, -jnp.inf, sel)
        member = member + hit.astype(F32)
        picks.append((idx, hit, gate))
    gsum = picks[0][2]
    for kk in range(1, TOP_K):
        gsum = gsum + picks[kk][2]
    ri = lax.broadcasted_iota(jnp.int32, (TM, TM), 0)
    ci = lax.broadcasted_iota(jnp.int32, (TM, TM), 1)
    before = (ri > ci).astype(BF16)
    run = run_sc[...]
    base_ref[0] = run
    cum = _dot(before, member.astype(BF16)) + run
    idx_out = jnp.full((TM, LANES), -1, jnp.int32)
    gate_out = jnp.zeros((TM, LANES), F32)
    rank_out = jnp.zeros((TM, LANES), F32)
    for kk, (idx, hit, gate) in enumerate(picks):
        rank = jnp.sum(jnp.where(hit, cum, 0.0), axis=-1, keepdims=True)
        here = lane == kk
        idx_out = jnp.where(here, idx, idx_out)
        gate_out = jnp.where(here, gate / gsum * ROUTED_SCALE, gate_out)
        rank_out = jnp.where(here, rank, rank_out)
    idx_ref[0] = idx_out
    gate_ref[0] = gate_out
    rank_ref[0] = rank_out
    run_end = run + jnp.sum(member, axis=0, keepdims=True)
    end_ref[0] = run_end
    run_sc[...] = run_end


def _outproj(x, mod_tab, o_f, o_b, ga, gg_row, lat, ctx, wout, g2, rw, rb, *, n_blk, nbl):
    bsz, _, d = x.shape
    s_out = n_blk * TM
    full = lambda a: pl.BlockSpec(a.shape, lambda b, i: (0,) * a.ndim)
    tok = lambda w: pl.BlockSpec((1, TM, w), lambda b, i: (b, i, 0))
    head = pl.BlockSpec((1, 4, TM, LANES), lambda b, i: (b, 0, jnp.minimum(i, nbl - 1), 0))
    head_c = pl.BlockSpec((1, 4, TM, LANES), lambda b, i: (b, 0, jnp.maximum(i - nbl, 0), 0))
    return pl.pallas_call(
        functools.partial(_outproj_kernel, nbl=nbl),
        grid=(bsz, n_blk),
        in_specs=[tok(d),
                  pl.BlockSpec((1, 1, 1, 6 * d), lambda b, i: (b, i // nbl, 0, 0)),
                  tok(2 * LANES), tok(2 * LANES),
                  pl.BlockSpec((1, TM, 2 * LANES), lambda b, i: (b, i, 3)),
                  full(gg_row), head, head, head, head_c, head_c, head_c,
                  full(wout), full(g2), full(rw), full(rb)],
        out_specs=[tok(d), tok(d), tok(LANES), tok(LANES), tok(LANES),
                   pl.BlockSpec((1, 1, LANES), lambda b, i: (b * n_blk + i, 0, 0)),
                   pl.BlockSpec((1, 1, LANES), lambda b, i: (b * n_blk + i, 0, 0))],
        out_shape=[jax.ShapeDtypeStruct((bsz, s_out, d), F32),
                   jax.ShapeDtypeStruct((bsz, s_out, d), BF16),
                   jax.ShapeDtypeStruct((bsz, s_out, LANES), jnp.int32),
                   jax.ShapeDtypeStruct((bsz, s_out, LANES), F32),
                   jax.ShapeDtypeStruct((bsz, s_out, LANES), F32),
                   jax.ShapeDtypeStruct((bsz * n_blk, 1, LANES), F32),
                   jax.ShapeDtypeStruct((bsz * n_blk, 1, LANES), F32)],
        scratch_shapes=[pltpu.VMEM((1, LANES), F32)],
        compiler_params=_cparams(("arbitrary", "arbitrary")),
        name="outproj_router",
    )(x, mod_tab, o_f, o_b, ga, gg_row, *lat, *ctx, wout, g2, rw, rb)


def _moe_tables_kernel(lo_ref, hi_ref, offl_ref, seg_ref, gbase_ref, be_ref, nused_ref):
    ncp = lo_ref.shape[0]
    nblkp = be_ref.shape[0]
    seg = jnp.ceil((hi_ref[...] - lo_ref[...]) * (1.0 / MOE_SEG)) * MOE_SEG
    ri = lax.broadcasted_iota(jnp.int32, (LANES, LANES), 0)
    ci = lax.broadcasted_iota(jnp.int32, (LANES, LANES), 1)
    before_lane = (ri < ci).astype(BF16)
    rc = lax.broadcasted_iota(jnp.int32, (ncp, ncp), 0)
    cc = lax.broadcasted_iota(jnp.int32, (ncp, ncp), 1)
    before_row = (rc > cc).astype(BF16)
    seg_ref[...] = seg
    offl_ref[...] = _dot_exact_rhs(seg, before_lane)
    region = jnp.ceil(jnp.sum(seg, axis=0, keepdims=True) * (1.0 / MOE_BR)) * MOE_BR
    goff = _dot_exact_rhs(jnp.broadcast_to(region, (8, LANES)), before_lane)[0:1]
    gbase_ref[...] = goff + _dot_exact_lhs(before_row, seg)
    lane = _lane((nblkp, LANES))
    gend = jnp.where(lane < N_EXPERTS, goff + region, jnp.inf)
    row0 = (lax.broadcasted_iota(jnp.int32, (nblkp, LANES), 0) * MOE_BR).astype(F32)
    be = jnp.sum((gend <= row0).astype(F32), axis=-1, keepdims=True)
    be_ref[...] = jnp.broadcast_to(jnp.minimum(be, N_EXPERTS - 1.0), (nblkp, LANES))
    nused_ref[...] = jnp.broadcast_to(jnp.sum(region, axis=-1, keepdims=True) * (1.0 / MOE_BR), (1, LANES))


def _moe_tables(cb_lo, cb_hi, nblk):
    ncp = cb_lo.shape[0]
    nblkp = -(-nblk // 8) * 8
    tab = jax.ShapeDtypeStruct((ncp, LANES), F32)
    return pl.pallas_call(
        _moe_tables_kernel,
        out_shape=[tab, tab, tab, jax.ShapeDtypeStruct((nblkp, LANES), F32),
                   jax.ShapeDtypeStruct((1, LANES), F32)],
        name="moe_tables",
    )(cb_lo, cb_hi)


def _segment_copy(local_ref, glob_ref, sem, lo, go, size, to_global):
    loc = local_ref.at[pl.ds(lo, size)]
    glo = glob_ref.at[pl.ds(go, size)]
    return pltpu.make_async_copy(loc, glo, sem) if to_global else pltpu.make_async_copy(glo, loc, sem)


def _segment_copies_wait(total, local_ref, glob_ref, sem, *, to_global):
    size = 1 << (local_ref.shape[0].bit_length() - 1)
    while size >= MOE_SEG:

        @pl.when((total & size) != 0)
        def _(size=size):
            _segment_copy(local_ref, glob_ref, sem, 0, 0, size, to_global).wait()

        size //= 2


def _segment_copies_start(c, offl_s, seg_s, gbase_s, local_ref, glob_ref, sem, *, to_global):
    def piece(lo, go, size):
        lo = pl.multiple_of(lo, MOE_SEG)
        go = pl.multiple_of(go, MOE_SEG)
        _segment_copy(local_ref, glob_ref, sem, lo, go, size, to_global).start()

    def expert_body(e, carry):
        t = c * N_EXPERTS + e
        off, sg, gb = offl_s[t], seg_s[t], gbase_s[t]
        n_full = sg // MOE_ROWS

        def full_body(w, carry2):
            piece(off + w * MOE_ROWS, gb + w * MOE_ROWS, MOE_ROWS)
            return carry2

        @pl.when(n_full > 0)
        def _():
            lax.fori_loop(0, n_full, full_body, 0)

        done = n_full * MOE_ROWS
        size = MOE_ROWS // 2
        while size >= MOE_SEG:
            bit = sg & size

            @pl.when(bit != 0)
            def _(done=done, size=size):
                piece(off + done, gb + done, size)

            done = done + bit
            size //= 2
        return carry

    lax.fori_loop(0, N_EXPERTS, expert_body, 0)


def _moe_sort_kernel(offl_s, seg_s, gbase_s, h_ref, idx_ref, rank_ref, cb_ref, offv_ref, xs_in_ref,
                     pos_ref, xs_ref, xy_sc, sem):
    del xs_in_ref
    c = pl.program_id(0)
    tc = h_ref.shape[0]
    lane_t = _lane((tc, LANES))
    h = h_ref[...]
    idx = idx_ref[...]
    rank = rank_ref[...]
    adj = offv_ref[0] - cb_ref[0]
    pos = jnp.full((tc, LANES), -1.0, F32)
    for k in range(TOP_K):
        hit = lane_t == idx[:, k:k + 1]
        p_k = rank[:, k:k + 1] + jnp.sum(jnp.where(hit, adj, 0.0), axis=-1, keepdims=True)
        pos = jnp.where(lane_t == k, p_k, pos)
    pos_ref[...] = pos
    sel8 = (lax.broadcasted_iota(jnp.int32, (8, LANES), 0)
            == lax.broadcasted_iota(jnp.int32, (8, LANES), 1)).astype(BF16)
    hi, mid, lo = _split3(pos)
    pos_row = _dot_nt(sel8, hi) + _dot_nt(sel8, mid) + _dot_nt(sel8, lo)

    def gather_body(rb, carry):
        for u in range(2):
            r0 = pl.multiple_of((2 * rb + u) * MOE_ROWS, MOE_ROWS)
            rid = (lax.broadcasted_iota(jnp.int32, (MOE_ROWS, tc), 0) + r0).astype(F32)
            onehot = jnp.zeros((MOE_ROWS, tc), F32)
            for k in range(TOP_K):
                onehot = jnp.where(pos_row[k:k + 1, :] == rid, 1.0, onehot)
            xy_sc[pl.ds(r0, MOE_ROWS), :] = _dot(onehot.astype(BF16), h).astype(BF16)
        return carry

    last = c * N_EXPERTS + N_EXPERTS - 1
    total = offl_s[last] + seg_s[last]
    lax.fori_loop(0, (total + 2 * MOE_ROWS - 1) // (2 * MOE_ROWS), gather_body, 0)
    _segment_copies_start(c, offl_s, seg_s, gbase_s, xy_sc, xs_ref, sem, to_global=True)
    _segment_copies_wait(total, xy_sc, xs_ref, sem, to_global=True)


def _moe_ffn_kernel(be_s, nused_s, x_ref, wgu_ref, wdn_ref, y_ref, wgu_sc, wdn_sc):
    i = pl.program_id(0)
    used = i < nused_s[0]
    changed = jnp.logical_or(i == 0, be_s[i] != be_s[jnp.maximum(i - 1, 0)])

    @pl.when(jnp.logical_and(used, changed))
    def _():
        wgu_sc[...] = wgu_ref[0, 0].astype(BF16)
        wdn_sc[...] = wdn_ref[0, 0].astype(BF16)

    @pl.when(used)
    def _():
        starts = range(0, MOE_BR, MOE_ROWS)
        gus = [_dot(x_ref[r0:r0 + MOE_ROWS, :], wgu_sc[...]) for r0 in starts]
        for r0, gu in zip(starts, gus):
            act = _silu(gu[:, :D_EXPERT]) * gu[:, D_EXPERT:]
            y_ref[r0:r0 + MOE_ROWS, :] = _dot(act.astype(BF16), wdn_sc[...]).astype(BF16)


def _moe_combine_kernel(offl_s, seg_s, gbase_s, h_ref, pos_ref, gate_ref, x1_ref, g5_ref, sgu_ref, sdn_ref,
                        ys_ref, y_ref, yl_sc, sem):
    c = pl.program_id(0)
    tc = h_ref.shape[0]
    n_rows = yl_sc.shape[0]
    yl_sc[...] = jnp.zeros(yl_sc.shape, BF16)
    _segment_copies_start(c, offl_s, seg_s, gbase_s, yl_sc, ys_ref, sem, to_global=False)
    gu = _dot(h_ref[...], sgu_ref[...])
    act = _silu(gu[:, :D_EXPERT]) * gu[:, D_EXPERT:]
    y_ref[...] = _dot(act.astype(BF16), sdn_ref[...])
    last = c * N_EXPERTS + N_EXPERTS - 1
    _segment_copies_wait(offl_s[last] + seg_s[last], yl_sc, ys_ref, sem, to_global=False)
    pos = pos_ref[...]
    gate = gate_ref[...]
    kblk = n_rows // MOE_KSPLIT
    for tb in range(tc // TM):
        pos_t = pos[tb * TM:(tb + 1) * TM]
        gate_t = gate[tb * TM:(tb + 1) * TM]
        out = y_ref[tb * TM:(tb + 1) * TM, :]
        for kb in range(MOE_KSPLIT):
            cid = (lax.broadcasted_iota(jnp.int32, (TM, kblk), 1) + kb * kblk).astype(F32)
            scat = jnp.zeros((TM, kblk), F32)
            for k in range(TOP_K):
                scat = jnp.where(pos_t[:, k:k + 1] == cid, gate_t[:, k:k + 1], scat)
            out = out + _dot(scat.astype(BF16), yl_sc[kb * kblk:(kb + 1) * kblk, :])
        y_ref[tb * TM:(tb + 1) * TM, :] = x1_ref[tb * TM:(tb + 1) * TM, :] + g5_ref[tb] * out


def _moe(x1, g5, h2, idx, rank, gate, base, end, wgu, wdn, layer, sgu, sdn, buf=None):
    n_tok, d = h2.shape
    tc = MOE_CHUNK
    nc = n_tok // tc
    bpc = tc // TM
    ne = wgu.shape[1]
    assert ne == N_EXPERTS
    ncp = -(-nc // LANES) * LANES
    n_loc = -(-(tc * TOP_K + ne * MOE_SEG) // (MOE_ROWS * MOE_KSPLIT)) * (MOE_ROWS * MOE_KSPLIT)
    assert n_loc >= -(-(tc * TOP_K + ne * (MOE_SEG - 1)) // (2 * MOE_ROWS)) * (2 * MOE_ROWS)
    nblk = -(-(n_tok * TOP_K + nc * ne * (MOE_SEG - 1) + ne * (MOE_BR - 1)) // MOE_BR)
    if buf is None:
        buf = jnp.zeros((nblk * MOE_BR, d), BF16)
    assert buf.shape[0] >= nblk * MOE_BR and buf.dtype == BF16
    pad = lambda a: jnp.concatenate([a, jnp.zeros((ncp - nc, LANES), F32)], axis=0)
    cb_lo = pad(base[0::bpc, 0, :])
    cb_hi = pad(end[bpc - 1::bpc, 0, :])
    offl, seg, gbase, be, nused = _moe_tables(cb_lo, cb_hi, nblk)
    to_smem = lambda a: a[:nc, :ne].astype(jnp.int32).reshape(nc * ne)
    tabs = (to_smem(offl), to_smem(seg), to_smem(gbase))
    be_i = be[:nblk, 0].astype(jnp.int32)
    nused_i = nused[0, :1].astype(jnp.int32)
    chunk = lambda w: pl.BlockSpec((tc, w), lambda c, *_: (c, 0))
    row = pl.BlockSpec((1, 1, LANES), lambda c, *_: (c, 0, 0))
    anyspec = pl.BlockSpec(memory_space=pl.ANY)
    const = lambda a: pl.BlockSpec(a.shape, lambda c, *_: (0,) * a.ndim)

    pos, xs = pl.pallas_call(
        _moe_sort_kernel,
        grid_spec=pltpu.PrefetchScalarGridSpec(
            num_scalar_prefetch=3, grid=(nc,),
            in_specs=[chunk(d), chunk(LANES), chunk(LANES), row, row, anyspec],
            out_specs=[chunk(LANES), anyspec],
            scratch_shapes=[pltpu.VMEM((n_loc, d), BF16), pltpu.SemaphoreType.DMA]),
        out_shape=[jax.ShapeDtypeStruct((n_tok, LANES), F32), jax.ShapeDtypeStruct(buf.shape, BF16)],
        input_output_aliases={8: 1},
        compiler_params=_cparams(("arbitrary",)),
        name="moe_sort",
    )(*tabs, h2, idx, rank, cb_lo[:nc, None, :], offl[:nc, None, :], buf)

    last = lambda i, be_s, nu_s: jnp.minimum(i, nu_s[0] - 1)
    ys = pl.pallas_call(
        _moe_ffn_kernel,
        grid_spec=pltpu.PrefetchScalarGridSpec(
            num_scalar_prefetch=2, grid=(nblk,),
            in_specs=[pl.BlockSpec((MOE_BR, d), lambda i, be_s, nu_s: (last(i, be_s, nu_s), 0)),
                      pl.BlockSpec((1, 1, d, 2 * D_EXPERT),
                                   lambda i, be_s, nu_s: (layer, be_s[last(i, be_s, nu_s)], 0, 0)),
                      pl.BlockSpec((1, 1, D_EXPERT, d),
                                   lambda i, be_s, nu_s: (layer, be_s[last(i, be_s, nu_s)], 0, 0))],
            out_specs=pl.BlockSpec((MOE_BR, d), lambda i, be_s, nu_s: (last(i, be_s, nu_s), 0)),
            scratch_shapes=[pltpu.VMEM((d, 2 * D_EXPERT), BF16), pltpu.VMEM((D_EXPERT, d), BF16)]),
        out_shape=jax.ShapeDtypeStruct(buf.shape, BF16),
        input_output_aliases={2: 0},
        compiler_params=_cparams(("arbitrary",)),
        name="moe_ffn",
    )(be_i, nused_i, xs, wgu, wdn)

    out = pl.pallas_call(
        _moe_combine_kernel,
        grid_spec=pltpu.PrefetchScalarGridSpec(
            num_scalar_prefetch=3, grid=(nc,),
            in_specs=[chunk(d), chunk(LANES), chunk(LANES), chunk(d),
                      pl.BlockSpec((bpc, 1, d), lambda c, *_: (c, 0, 0)),
                      const(sgu), const(sdn), anyspec],
            out_specs=chunk(d),
            scratch_shapes=[pltpu.VMEM((n_loc, d), BF16), pltpu.SemaphoreType.DMA]),
        out_shape=jax.ShapeDtypeStruct((n_tok, d), F32),
        compiler_params=_cparams(("arbitrary",)),
        name="moe_combine",
    )(*tabs, h2, pos, gate, x1, g5, sgu, sdn, ys)
    return out, ys


def _axial_tables(n_rows, rot_dim):
    rows = jnp.repeat(jnp.arange(n_rows), GRID_W).astype(F32)
    cols = jnp.tile(jnp.arange(GRID_W), n_rows).astype(F32)
    axis_dim = rot_dim // 2
    inv_freq = ROPE_THETA ** (-jnp.arange(0, axis_dim, 2, dtype=F32) / axis_dim)
    ang_r = rows[:, None] * inv_freq
    ang_c = cols[:, None] * inv_freq
    ang = jnp.concatenate([ang_r, ang_r, ang_c, ang_c], axis=-1)
    return jnp.cos(ang), jnp.sin(ang)


def _rope_slab_tables(t_len, n_ctx, rot_dim, lane0):
    cos, sin = _axial_tables(t_len // GRID_W, rot_dim)
    half = rot_dim // 4
    sign = jnp.where((jnp.arange(rot_dim) % (2 * half)) < half, -1.0, 1.0).astype(F32)
    cos_t = jnp.ones((t_len + n_ctx, LANES), F32).at[:t_len, lane0:lane0 + rot_dim].set(cos)
    sin_t = jnp.zeros((t_len + n_ctx, LANES), F32).at[:t_len, lane0:lane0 + rot_dim].set(sin * sign)
    return cos_t, sin_t


def _slab_cols(starts, width):
    out = []
    for st in starts:
        out += list(range(st, st + width)) + [-1] * (LANES - width)
    return out


def _gather_cols(w, cols):
    w_ext = jnp.concatenate([w, jnp.zeros((w.shape[0], 1), w.dtype)], axis=1)
    idx = np.array([c if c >= 0 else w.shape[1] for c in cols], np.int32)
    return w_ext[:, idx]


def _pad_row(v, lane0=0):
    return jnp.zeros((1, LANES), F32).at[0, lane0:lane0 + v.shape[0]].set(v.astype(F32))


def _layer_weights(l, w_in, gdn_conv_w, gdn_a_log, gdn_dt_bias, gdn_norm_g, mla_qn_g, mla_kvn_g,
                   mla_w_uq, mla_w_ukv, mla_qk_g, swa_qk_g, swa_sink, gqa_qk_g, w_out, router_w,
                   router_bias):
    hd = HEAD_DIM
    cols = list(range(0, 1024))
    cols += list(range(O_MQ, O_MQ + MLA_Q_RANK)) + list(range(O_MKV, O_MKV + MLA_KV_RANK))
    misc = list(range(O_MKR, O_MKR + MLA_ROPE))
    misc += list(range(O_AA, O_AA + 4)) + list(range(O_AB, O_AB + 4))
    misc += list(range(O_AA + 4, O_AA + 8)) + list(range(O_AB + 4, O_AB + 8))
    cols += misc + [-1] * (LANES - len(misc))
    cols += _slab_cols([O_SQ + hd * i for i in range(4)] + [O_SK + hd * i for i in range(2)]
                       + [O_SV + hd * i for i in range(2)], hd)
    cols += _slab_cols([O_GQ + hd * i for i in range(4)] + [O_GK + hd * i for i in range(2)]
                       + [O_GV + hd * i for i in range(2)], hd)
    assert len(cols) == N_COL
    qd = MLA_NOPE + MLA_ROPE
    uq_cols = []
    for hh in range(MLA_HEADS):
        uq_cols += list(range(hh * qd, hh * qd + qd)) + [-1] * (LANES - qd)
    kvd = MLA_NOPE + MLA_V
    ukv_cols = _slab_cols([hh * kvd for hh in range(MLA_HEADS)], MLA_NOPE)
    ukv_cols += _slab_cols([hh * kvd + MLA_NOPE for hh in range(MLA_HEADS)], MLA_V)
    orow = list(range(0, 256)) + _slab_cols([256 + hd * i for i in range(12)], hd)
    alog = jnp.zeros((1, LANES), F32)
    dtb = jnp.zeros((1, LANES), F32)
    for dd in range(2):
        alog = alog.at[0, 32 + 8 * dd:36 + 8 * dd].set(gdn_a_log[l, dd])
        dtb = dtb.at[0, 32 + 8 * dd:36 + 8 * dd].set(gdn_dt_bias[l, dd])
    return dict(
        win=_gather_cols(w_in[l], cols).astype(BF16),
        conv=gdn_conv_w[l][:, :768],
        alog=alog, dtb=dtb,
        gdn_g=jnp.concatenate([gdn_norm_g[l], gdn_norm_g[l]]).reshape(1, LANES),
        mla_qn_g=mla_qn_g[l], mla_kvn_g=mla_kvn_g[l],
        wuq=_gather_cols(mla_w_uq[l], uq_cols).astype(BF16),
        wukv=_gather_cols(mla_w_ukv[l], ukv_cols).astype(BF16),
        mqg=_pad_row(mla_qk_g[l, 0]), mkg=_pad_row(mla_qk_g[l, 1, :MLA_NOPE]),
        krg=_pad_row(mla_qk_g[l, 1, MLA_NOPE:]),
        sqg=_pad_row(swa_qk_g[l, 0]), skg=_pad_row(swa_qk_g[l, 1]),
        gqg=_pad_row(gqa_qk_g[l, 0]), gkg=_pad_row(gqa_qk_g[l, 1]),
        wout=_gather_cols(w_out[l].T, orow).T.astype(BF16),
        rw=jnp.concatenate([router_w[l], jnp.zeros((D_MODEL, LANES - N_EXPERTS), F32)], axis=1),
        rb=_pad_row(router_bias[l]),
        sink=swa_sink[l],
    )


def _sink_rows(sink, grp, tq):
    hkv = sink.shape[0] // grp
    return jnp.repeat((sink.astype(F32) * LOG2E).reshape(hkv, grp), tq, axis=1).reshape(hkv, grp * tq, 1)


def kernel(x, c, ctx, c_ctx, w_mod, b_mod, norm1_g, norm2_g, w_in, gdn_conv_w, gdn_a_log, gdn_dt_bias,
           gdn_norm_g, mla_qn_g, mla_kvn_g, mla_w_uq, mla_w_ukv, mla_qk_g, swa_qk_g, swa_sink, gqa_qk_g,
           w_out, router_w, router_bias, exp_w_gu, exp_w_down, shared_w_gu, shared_w_down):
    bsz, t_len, d = x.shape
    n_ctx = ctx.shape[1]
    depth = w_mod.shape[0]
    s = t_len + n_ctx
    assert d == D_MODEL and t_len % TM == 0 and n_ctx % TM == 0 and t_len >= 3 * WINDOW
    assert bsz + 1 <= 8
    nbl = t_len // TM
    nb = s // TM

    tabs = (_rope_slab_tables(t_len, n_ctx, HEAD_DIM, 0) + _rope_slab_tables(t_len, n_ctx, MLA_ROPE, MLA_NOPE)
            + _rope_slab_tables(t_len, n_ctx, MLA_ROPE, 0))
    c8 = jnp.zeros((8, d), F32).at[:bsz].set(c).at[bsz].set(c_ctx)
    xs = jnp.concatenate([x, ctx], axis=1)

    tq_d = 512 if t_len % 512 == 0 else TM
    tk_d = 8448 if (s % 8448 == 0) else TM

    moe_buf = None
    for l in range(depth):
        need_ctx = l < depth - 1
        lw = _layer_weights(l, w_in, gdn_conv_w, gdn_a_log, gdn_dt_bias, gdn_norm_g, mla_qn_g, mla_kvn_g,
                            mla_w_uq, mla_w_ukv, mla_qk_g, swa_qk_g, swa_sink, gqa_qk_g, w_out, router_w,
                            router_bias)
        mod = _modulation(c8, w_mod[l], b_mod[l])
        mod_tab = jnp.stack([mod[:bsz], jnp.broadcast_to(mod[bsz], (bsz, 6 * d))], axis=1)[:, :, None, :]

        (ga, misc, mq, mk, mv, sq, sk, sv, gq, gk, gv) = _inproj(
            xs, mod_tab, norm1_g[l].reshape(1, d), lw['win'], lw, tabs, nbl)

        qkv, lab = _gdn_prep(ga, misc, lw['conv'], lw['alog'], lw['dtb'], nbl)
        labt = jnp.swapaxes(lab[..., :8], 2, 3)
        o_f, o_b = _gdn_scan(qkv, lab, labt, nbl=nbl)

        mo = _flash(mq, mk, mv, grp=1, tq=2 * tq_d, tk=tk_d, q_blk0=0, nq=t_len // (2 * tq_d),
                    k_blk0=0, nk=s // tk_d)
        go = _flash(gq, gk, gv, grp=2, tq=tq_d, tk=tk_d, q_blk0=0, nq=t_len // tq_d, k_blk0=0, nk=s // tk_d)
        so = _swa(sq, sk, sv, _sink_rows(lw['sink'], 2, WINDOW), grp=2, tq=tq_d, t_len=t_len, n_ctx=n_ctx)
        lat = (mo, so, go)
        ctx_out = lat
        if need_ctx:
            cb0 = t_len // n_ctx
            ctx_out = (
                _flash(mq, mk, mv, grp=1, tq=n_ctx, tk=n_ctx, q_blk0=cb0, nq=1, k_blk0=cb0, nk=1),
                _flash(sq, sk, sv, grp=2, tq=n_ctx, tk=n_ctx, q_blk0=cb0, nq=1, k_blk0=cb0, nk=1,
                       sink_rows=_sink_rows(lw['sink'], 2, n_ctx)),
                _flash(gq, gk, gv, grp=2, tq=n_ctx, tk=n_ctx, q_blk0=cb0, nq=1, k_blk0=cb0, nk=1))

        n_blk = nb if need_ctx else nbl
        x1, h2, idx, gate, rank, base, end = _outproj(
            xs, mod_tab, o_f, o_b, ga, lw['gdn_g'], lat, ctx_out, lw['wout'], norm2_g[l].reshape(1, d),
            lw['rw'], lw['rb'], n_blk=n_blk, nbl=nbl)

        n_tok = bsz * n_blk * TM
        assert n_tok % MOE_CHUNK == 0
        g5 = jnp.repeat(mod_tab[:, :, 0, 5 * d:], jnp.array([nbl, nb - nbl]), axis=1,
                        total_repeat_length=nb)[:, :n_blk].reshape(bsz * n_blk, 1, d)
        xs, moe_buf = _moe(x1.reshape(n_tok, d), g5, h2.reshape(n_tok, d), idx.reshape(n_tok, LANES),
                           rank.reshape(n_tok, LANES), gate.reshape(n_tok, LANES), base, end, exp_w_gu,
                           exp_w_down, l, shared_w_gu[l].astype(BF16), shared_w_down[l].astype(BF16),
                           buf=moe_buf)
        xs = xs.reshape(bsz, n_blk * TM, d)
    return xs[:, :t_len]
```

```python
import functools
import math

import numpy as np
import jax
import jax.numpy as jnp
from jax import lax
from jax.experimental import pallas as pl
from jax.experimental.pallas import tpu as pltpu

F32 = jnp.float32
BF16 = jnp.bfloat16

LANES = 128
TM = 256
VMEM_LIMIT = 56 * 1024 * 1024

D_MODEL = 1024
GRID_W = 64
HEAD_DIM = 64
ROPE_THETA = 10000.0
NORM_EPS = 1e-6
ATTN_SCALE = HEAD_DIM ** -0.5
GDN_DK = 64
GDN_CHUNK = 64
GDN_CONV = 5
MLA_HEADS = 4
MLA_Q_RANK = 256
MLA_KV_RANK = 128
MLA_NOPE = 64
MLA_ROPE = 32
MLA_V = 64
MLA_SCALE = (MLA_NOPE + MLA_ROPE) ** -0.5
SWA_HEADS = 4
SWA_KV_HEADS = 2
WINDOW = 128
GQA_HEADS = 4
GQA_KV_HEADS = 2
N_EXPERTS = 64
TOP_K = 6
D_EXPERT = 384
ROUTED_SCALE = 2.5
LOG2E = math.log2(math.e)
ONES_LANE = HEAD_DIM
FLASH_KB = 256
MOE_CHUNK = 512
MOE_ROWS = 128
MOE_BR = 512
MOE_SEG = 16
MOE_KSPLIT = 3

O_AQ, O_AK, O_AV, O_AZ, O_AA, O_AB = 0, 256, 512, 768, 1024, 1032
O_MQ, O_MKV, O_MKR = 1040, 1296, 1424
O_SQ, O_SK, O_SV = 1456, 1712, 1840
O_GQ, O_GK, O_GV = 1968, 2224, 2352
D_IN = 2480
C_GDN, C_MLA, C_MISC, C_SWA, C_GQA, N_COL = 0, 1024, 1408, 1536, 2560, 3584


def _cparams(sem):
    return pltpu.CompilerParams(dimension_semantics=sem, vmem_limit_bytes=VMEM_LIMIT)


def _dot(a, b):
    return jnp.dot(a, b, preferred_element_type=F32)


def _dot_nt(a, b):
    return lax.dot_general(a, b, (((1,), (1,)), ((), ())), preferred_element_type=F32)


def _split3(x):
    hi = x.astype(BF16)
    r1 = x - hi.astype(F32)
    mid = r1.astype(BF16)
    lo = (r1 - mid.astype(F32)).astype(BF16)
    return hi, mid, lo


def _dot_exact_lhs(a_bf16, x):
    hi, mid, lo = _split3(x)
    return _dot(a_bf16, hi) + _dot(a_bf16, mid) + _dot(a_bf16, lo)


def _dot_exact_rhs(x, b_bf16):
    hi, mid, lo = _split3(x)
    return _dot(hi, b_bf16) + _dot(mid, b_bf16) + _dot(lo, b_bf16)


def _dot_hp(a, b):
    ah = a.astype(BF16)
    al = (a - ah.astype(F32)).astype(BF16)
    bh = b.astype(BF16)
    bl = (b - bh.astype(F32)).astype(BF16)
    return _dot(ah, bh) + _dot(ah, bl) + _dot(al, bh)


def _silu(x):
    return x * (1.0 / (1.0 + jnp.exp(-x)))


def _lane(shape):
    return lax.broadcasted_iota(jnp.int32, shape, len(shape) - 1)


def _mod_kernel(c_ref, w_ref, b_ref, o_ref):
    s = _silu(c_ref[...])
    o_ref[...] = _dot(s.astype(BF16), w_ref[...].astype(BF16)) + b_ref[...]


def _modulation(c8, w_mod_l, b_mod_l):
    d = c8.shape[1]
    n = w_mod_l.shape[1]
    return pl.pallas_call(
        _mod_kernel,
        grid=(n // d,),
        in_specs=[pl.BlockSpec((8, d), lambda j: (0, 0)),
                  pl.BlockSpec((d, d), lambda j: (0, j)),
                  pl.BlockSpec((1, d), lambda j: (0, j))],
        out_specs=pl.BlockSpec((8, d), lambda j: (0, j)),
        out_shape=jax.ShapeDtypeStruct((8, n), F32),
        compiler_params=_cparams(("arbitrary",)),
        name="modulation",
    )(c8, w_mod_l, b_mod_l.reshape(1, n))


def _rope_slab(x, cos, sin_signed, half):
    lane = _lane(x.shape)
    fwd = pltpu.roll(x, LANES - half, 1)
    bwd = pltpu.roll(x, half, 1)
    rot = jnp.where(lane % (2 * half) < half, fwd, bwd)
    return x * cos + rot * sin_signed


def _inproj_kernel(x_ref, mod_ref, g1_ref, win_ref, qn_g_ref, kvn_g_ref, wuq_ref, wukv_ref,
                   mqg_ref, mkg_ref, krg_ref, sqg_ref, skg_ref, gqg_ref, gkg_ref,
                   cos_hd_ref, sin_hd_ref, cos_mq_ref, sin_mq_ref, cos_kr_ref, sin_kr_ref,
                   ga_ref, misc_ref, mq_ref, mk_ref, mv_ref, sq_ref, sk_ref, sv_ref,
                   gq_ref, gk_ref, gv_ref):
    d = D_MODEL
    x = x_ref[0]
    m = mod_ref[0, 0]
    shift, scale = m[:, 0:d], m[:, d:2 * d]
    xn = x * lax.rsqrt(jnp.mean(x * x, axis=-1, keepdims=True) + NORM_EPS) * g1_ref[...]
    h = xn * (1.0 + scale) + shift
    hb = h.astype(BF16)

    def proj(c0, width):
        return _dot(hb, win_ref[:, c0:c0 + width])

    ga_ref[0] = proj(C_GDN, 1024)
    pm = proj(C_MLA, C_SWA - C_MLA)
    misc = pm[:, C_MISC - C_MLA:C_MISC - C_MLA + LANES]
    misc_ref[0] = misc

    lane = _lane((TM, LANES))
    cos_hd, sin_hd = cos_hd_ref[...], sin_hd_ref[...]

    cq = pm[:, 0:MLA_Q_RANK]
    cqn = cq * lax.rsqrt(jnp.mean(cq * cq, axis=-1, keepdims=True) + NORM_EPS) * qn_g_ref[...]
    qup = _dot(cqn.astype(BF16), wuq_ref[...])
    ckv = pm[:, MLA_Q_RANK:MLA_Q_RANK + MLA_KV_RANK]
    ckvn = ckv * lax.rsqrt(jnp.mean(ckv * ckv, axis=-1, keepdims=True) + NORM_EPS) * kvn_g_ref[...]
    kvup = _dot(ckvn.astype(BF16), wukv_ref[...])
    kr = jnp.where(lane < MLA_ROPE, misc, 0.0)
    kr = kr * lax.rsqrt(jnp.sum(kr * kr, axis=-1, keepdims=True) / MLA_ROPE + NORM_EPS) * krg_ref[...]
    kr = _rope_slab(kr, cos_kr_ref[...], sin_kr_ref[...], MLA_ROPE // 4)
    kpe = pltpu.roll(kr, MLA_NOPE, 1)
    is_nope = lane < MLA_NOPE
    for hh in range(MLA_HEADS):
        q = qup[:, hh * LANES:(hh + 1) * LANES]
        q2 = q * q
        ss_n = jnp.sum(jnp.where(is_nope, q2, 0.0), axis=-1, keepdims=True)
        ss_p = jnp.sum(jnp.where(is_nope, 0.0, q2), axis=-1, keepdims=True)
        r = jnp.where(is_nope, lax.rsqrt(ss_n / MLA_NOPE + NORM_EPS), lax.rsqrt(ss_p / MLA_ROPE + NORM_EPS))
        q = q * r * mqg_ref[...]
        q = _rope_slab(q, cos_mq_ref[...], sin_mq_ref[...], MLA_ROPE // 4)
        mq_ref[0, hh] = (q * (MLA_SCALE * LOG2E)).astype(BF16)
        k = kvup[:, hh * LANES:(hh + 1) * LANES]
        k = k * lax.rsqrt(jnp.sum(k * k, axis=-1, keepdims=True) / MLA_NOPE + NORM_EPS) * mkg_ref[...]
        mk_ref[0, hh] = (k + kpe).astype(BF16)
        v = kvup[:, (MLA_HEADS + hh) * LANES:(MLA_HEADS + hh + 1) * LANES]
        mv_ref[0, hh] = jnp.where(lane == ONES_LANE, 1.0, v).astype(BF16)

    def gqa_prep(base, qg_ref, kg_ref, q_ref, k_ref, v_ref, nq, nkv):
        p = proj(base, (nq + 2 * nkv) * LANES)
        base = 0
        for hh in range(nq):
            q = p[:, base + hh * LANES:base + (hh + 1) * LANES]
            q = q * lax.rsqrt(jnp.sum(q * q, axis=-1, keepdims=True) / HEAD_DIM + NORM_EPS) * qg_ref[...]
            q = _rope_slab(q, cos_hd, sin_hd, HEAD_DIM // 4) * (ATTN_SCALE * LOG2E)
            q_ref[0, hh] = q.astype(BF16)
        for hh in range(nkv):
            k = p[:, base + (nq + hh) * LANES:base + (nq + hh + 1) * LANES]
            k = k * lax.rsqrt(jnp.sum(k * k, axis=-1, keepdims=True) / HEAD_DIM + NORM_EPS) * kg_ref[...]
            k = _rope_slab(k, cos_hd, sin_hd, HEAD_DIM // 4)
            k_ref[0, hh] = k.astype(BF16)
            v = p[:, base + (nq + nkv + hh) * LANES:base + (nq + nkv + hh + 1) * LANES]
            v_ref[0, hh] = jnp.where(lane == ONES_LANE, 1.0, v).astype(BF16)

    gqa_prep(C_SWA, sqg_ref, skg_ref, sq_ref, sk_ref, sv_ref, SWA_HEADS, SWA_KV_HEADS)
    gqa_prep(C_GQA, gqg_ref, gkg_ref, gq_ref, gk_ref, gv_ref, GQA_HEADS, GQA_KV_HEADS)


def _inproj(x, mod_tab, g1, win, lw, tabs, nbl):
    bsz, s, d = x.shape
    nb = s // TM
    row = lambda a: a.reshape(1, -1)
    full = lambda a: pl.BlockSpec(a.shape, lambda b, i: (0,) * a.ndim)
    tab_spec = pl.BlockSpec((TM, LANES), lambda b, i: (i, 0))
    head_out = lambda nh: pl.BlockSpec((1, nh, TM, LANES), lambda b, i: (b, 0, i, 0))
    head_shape = lambda nh: jax.ShapeDtypeStruct((bsz, nh, s, LANES), BF16)
    small = [row(lw['mla_qn_g']), row(lw['mla_kvn_g']), lw['wuq'], lw['wukv'],
             lw['mqg'], lw['mkg'], lw['krg'], lw['sqg'], lw['skg'], lw['gqg'], lw['gkg']]
    return pl.pallas_call(
        _inproj_kernel,
        grid=(bsz, nb),
        in_specs=[pl.BlockSpec((1, TM, d), lambda b, i: (b, i, 0)),
                  pl.BlockSpec((1, 1, 1, 6 * d), lambda b, i: (b, i // nbl, 0, 0)),
                  full(g1), full(win)] + [full(a) for a in small] + [tab_spec] * 6,
        out_specs=[pl.BlockSpec((1, TM, 1024), lambda b, i: (b, i, 0)),
                   pl.BlockSpec((1, TM, LANES), lambda b, i: (b, i, 0)),
                   head_out(4), head_out(4), head_out(4),
                   head_out(4), head_out(2), head_out(2),
                   head_out(4), head_out(2), head_out(2)],
        out_shape=[jax.ShapeDtypeStruct((bsz, s, 1024), F32),
                   jax.ShapeDtypeStruct((bsz, s, LANES), F32),
                   head_shape(4), head_shape(4), head_shape(4),
                   head_shape(4), head_shape(2), head_shape(2),
                   head_shape(4), head_shape(2), head_shape(2)],
        compiler_params=_cparams(("parallel", "parallel")),
        name="inproj",
    )(x, mod_tab, g1, win, *small, *tabs)


def _flash_kernel(*refs, grp, tq, nk, use_sink):
    refs = list(refs)
    sink_ref = refs.pop(0) if use_sink else None
    q_ref, k_ref, v_ref = refs[0], refs[1], refs[2]
    o_ref, m_sc, acc_sc = refs[-3], refs[-2], refs[-1]
    kj = pl.program_id(3)
    tk = k_ref.shape[2]
    kb = min(FLASH_KB, tk)

    @pl.when(kj == 0)
    def _():
        m_sc[...] = jnp.full(m_sc.shape, -jnp.inf, F32)
        acc_sc[...] = jnp.zeros(acc_sc.shape, F32)

    q = q_ref[0].reshape(grp * tq, LANES)
    m = m_sc[...]
    acc = acc_sc[...]
    for j in range(tk // kb):
        s = _dot_nt(q, k_ref[0, 0, j * kb:(j + 1) * kb, :])
        m_new = jnp.maximum(m, jnp.max(s, axis=-1, keepdims=True))
        alpha = jnp.exp2(m - m_new)
        pr = jnp.exp2(s - jnp.concatenate([m_new] * (kb // LANES), axis=1))
        acc = alpha * acc + _dot(pr.astype(BF16), v_ref[0, 0, j * kb:(j + 1) * kb, :])
        m = m_new
    m_sc[...] = m
    acc_sc[...] = acc

    @pl.when(kj == nk - 1)
    def _():
        l = acc[:, ONES_LANE:ONES_LANE + 1]
        out = acc
        if use_sink:
            sk = sink_ref[0]
            m_old = m[:, 0:1]
            m_fin = jnp.maximum(m_old, sk)
            a = jnp.exp2(m_old - m_fin)
            l = l * a + jnp.exp2(sk - m_fin)
            out = out * a
        o_ref[0] = (out / l).reshape(grp, tq, LANES).astype(o_ref.dtype)


def _flash(q, k, v, *, grp, tq, tk, q_blk0, nq, k_blk0, nk, sink_rows=None):
    bsz, hq, s, _ = q.shape
    hkv = hq // grp
    use_sink = sink_rows is not None
    in_specs = []
    args = []
    if use_sink:
        in_specs.append(pl.BlockSpec((1, grp * tq, 1), lambda b, h, i, j: (h, 0, 0)))
        args.append(sink_rows)
    in_specs += [pl.BlockSpec((1, grp, tq, LANES), lambda b, h, i, j: (b, h, i + q_blk0, 0)),
                 pl.BlockSpec((1, 1, tk, LANES), lambda b, h, i, j: (b, h, j + k_blk0, 0)),
                 pl.BlockSpec((1, 1, tk, LANES), lambda b, h, i, j: (b, h, j + k_blk0, 0))]
    args += [q, k, v]
    rows = grp * tq
    return pl.pallas_call(
        functools.partial(_flash_kernel, grp=grp, tq=tq, nk=nk, use_sink=use_sink),
        grid=(bsz, hkv, nq, nk),
        in_specs=in_specs,
        out_specs=pl.BlockSpec((1, grp, tq, LANES), lambda b, h, i, j: (b, h, i, 0)),
        out_shape=jax.ShapeDtypeStruct((bsz, hq, nq * tq, LANES), BF16),
        scratch_shapes=[pltpu.VMEM((rows, LANES), F32), pltpu.VMEM((rows, LANES), F32)],
        compiler_params=_cparams(("parallel", "parallel", "parallel", "arbitrary")),
        name="flash",
    )(*args)


def _swa_kernel(sink_ref, q_ref, k_ref, v_ref, o_ref, *, grp, tq, t_len, n_ctx):
    w = WINDOW
    n = pl.program_id(2)
    kc = k_ref[0, 0, pl.ds(t_len, n_ctx), :]
    vc = v_ref[0, 0, pl.ds(t_len, n_ctx), :]
    for qb in range(tq // w):
        q0 = n * tq + qb * w
        start = pl.multiple_of(jnp.clip(q0 - w, 0, t_len - 3 * w), w)
        q = q_ref[0, :, qb * w:(qb + 1) * w, :].reshape(grp * w, LANES)
        kl = k_ref[0, 0, pl.ds(start, 3 * w), :]
        vl = v_ref[0, 0, pl.ds(start, 3 * w), :]
        s_loc = _dot_nt(q, kl)
        qpos = q0 + (lax.broadcasted_iota(jnp.int32, s_loc.shape, 0) % w)
        kpos = start + lax.broadcasted_iota(jnp.int32, s_loc.shape, 1)
        s_loc = jnp.where(jnp.abs(qpos - kpos) <= w, s_loc, -jnp.inf)
        s_ctx = _dot_nt(q, kc)
        sk = sink_ref[0]
        m = jnp.maximum(jnp.maximum(jnp.max(s_loc, axis=-1, keepdims=True),
                                    jnp.max(s_ctx, axis=-1, keepdims=True)), sk)
        p_loc = jnp.exp2(s_loc - m)
        p_ctx = jnp.exp2(s_ctx - m)
        o = _dot(p_loc.astype(BF16), vl) + _dot(p_ctx.astype(BF16), vc)
        l = o[:, ONES_LANE:ONES_LANE + 1] + jnp.exp2(sk - m)
        o_ref[0, :, qb * w:(qb + 1) * w, :] = (o / l).reshape(grp, w, LANES).astype(o_ref.dtype)


def _swa(q, k, v, sink_rows, *, grp, tq, t_len, n_ctx):
    bsz, hq, s, _ = q.shape
    hkv = hq // grp
    return pl.pallas_call(
        functools.partial(_swa_kernel, grp=grp, tq=tq, t_len=t_len, n_ctx=n_ctx),
        grid=(bsz, hkv, t_len // tq),
        in_specs=[pl.BlockSpec((1, grp * WINDOW, 1), lambda b, h, n: (h, 0, 0)),
                  pl.BlockSpec((1, grp, tq, LANES), lambda b, h, n: (b, h, n, 0)),
                  pl.BlockSpec((1, 1, s, LANES), lambda b, h, n: (b, h, 0, 0)),
                  pl.BlockSpec((1, 1, s, LANES), lambda b, h, n: (b, h, 0, 0))],
        out_specs=pl.BlockSpec((1, grp, tq, LANES), lambda b, h, n: (b, h, n, 0)),
        out_shape=jax.ShapeDtypeStruct((bsz, hq, t_len, LANES), BF16),
        compiler_params=_cparams(("parallel", "parallel", "arbitrary")),
        name="swa",
    )(sink_rows, q, k, v)


def _gdn_prep_kernel(cur_ref, prev_ref, next_ref, misc_ref, cw_ref, alog_ref, dtb_ref,
                     qkv_ref, lab_ref, *, nbl, nb):
    i = pl.program_id(1)
    first = jnp.logical_or(i == 0, i == nbl)
    last = jnp.logical_or(i == nbl - 1, i == nb - 1)
    cur = cur_ref[0]
    prev = jnp.where(first, 0.0, prev_ref[0])
    nxt = jnp.where(last, 0.0, next_ref[0])
    xe = jnp.concatenate([prev, cur, nxt], axis=0)
    cw = cw_ref[...]
    acc = jnp.zeros(cur.shape, F32)
    for j in range(GDN_CONV):
        off = 8 + j - (GDN_CONV - 1) // 2
        acc = acc + xe[off:off + TM, :] * cw[j:j + 1, :]
    y = _silu(acc)
    lane = _lane((TM, LANES))
    lo = lane < GDN_DK
    for sl in range(6):
        t = y[:, sl * LANES:(sl + 1) * LANES]
        if sl < 4:
            t2 = t * t
            ss0 = jnp.sum(jnp.where(lo, t2, 0.0), axis=-1, keepdims=True)
            ss1 = jnp.sum(jnp.where(lo, 0.0, t2), axis=-1, keepdims=True)
            t = t * jnp.where(lo, lax.rsqrt(ss0 + NORM_EPS), lax.rsqrt(ss1 + NORM_EPS))
            if sl < 2:
                t = t * GDN_DK ** -0.5
        qkv_ref[0, :, sl * LANES:(sl + 1) * LANES] = t
    xm = misc_ref[0]
    za = xm + dtb_ref[...]
    softplus = jnp.maximum(za, 0.0) + jnp.log(1.0 + jnp.exp(-jnp.abs(za)))
    log_a = -jnp.exp(alog_ref[...]) * softplus
    beta = 1.0 / (1.0 + jnp.exp(-xm))
    is_a = (lane % 8) < 4
    vals = jnp.where(is_a, log_a, beta)
    lab_ref[0, 0] = pltpu.roll(vals, LANES - 32, 1)
    lab_ref[0, 1] = pltpu.roll(vals, LANES - 40, 1)


def _gdn_prep(ga, misc, cw, alog_row, dtb_row, nbl):
    bsz, s, _ = ga.shape
    nb = s // TM
    r8 = TM // 8
    n8 = s // 8
    return pl.pallas_call(
        functools.partial(_gdn_prep_kernel, nbl=nbl, nb=nb),
        grid=(bsz, nb),
        in_specs=[pl.BlockSpec((1, TM, 768), lambda b, i: (b, i, 0)),
                  pl.BlockSpec((1, 8, 768), lambda b, i: (b, jnp.maximum(i * r8 - 1, 0), 0)),
                  pl.BlockSpec((1, 8, 768), lambda b, i: (b, jnp.minimum((i + 1) * r8, n8 - 1), 0)),
                  pl.BlockSpec((1, TM, LANES), lambda b, i: (b, i, 0)),
                  pl.BlockSpec(cw.shape, lambda b, i: (0, 0)),
                  pl.BlockSpec((1, LANES), lambda b, i: (0, 0)),
                  pl.BlockSpec((1, LANES), lambda b, i: (0, 0))],
        out_specs=[pl.BlockSpec((1, TM, 768), lambda b, i: (b, i, 0)),
                   pl.BlockSpec((1, 2, TM, LANES), lambda b, i: (b, 0, i, 0))],
        out_shape=[jax.ShapeDtypeStruct((bsz, s, 768), F32),
                   jax.ShapeDtypeStruct((bsz, 2, s, LANES), F32)],
        compiler_params=_cparams(("parallel", "parallel")),
        name="gdn_prep",
    )(ga, ga, ga, misc, cw, alog_row, dtb_row)


def _gdn_scan_kernel(qkvf_ref, labf_ref, labtf_ref, qkvb_ref, labb_ref, labtb_ref, of_ref, ob_ref, s_sc):
    c = GDN_CHUNK
    c2 = 2 * c
    i = pl.program_id(1)

    @pl.when(i == 0)
    def _():
        s_sc[...] = jnp.zeros(s_sc.shape, F32)

    ri = lax.broadcasted_iota(jnp.int32, (c2, c2), 0)
    ci = lax.broadcasted_iota(jnp.int32, (c2, c2), 1)
    same = (ri // c) == (ci // c)
    eye = (ri == ci).astype(F32)
    lane = _lane((c, LANES))
    lo = lane < GDN_DK

    def stack(x):
        return jnp.concatenate([jnp.where(lo, x, 0.0), jnp.where(lo, 0.0, x)], axis=0)

    def fold(mat):
        return mat[0:c, :] + mat[c:c2, :]

    inst_f = _gdn_instances(qkvf_ref, labf_ref, labtf_ref, of_ref, 0, False, stack, fold)
    inst_b = _gdn_instances(qkvb_ref, labb_ref, labtb_ref, ob_ref, 2, True, stack, fold)
    inst = [it for pair in zip(inst_f, inst_b) for it in pair]
    tinv = [eye - it['a'] for it in inst]
    pw = [it['a'] for it in inst]
    for _ in range(5):
        pw = [_dot_hp(x, x) for x in pw]
        tinv = [t + _dot_hp(t, x) for t, x in zip(tinv, pw)]
    eye_b = eye.astype(BF16)
    for it, t in zip(inst, tinv):
        tf = fold(t).astype(BF16)
        it['u'] = _dot(tf, it['vb'])
        it['wq'] = jnp.concatenate([_dot(tf, it['kg']).astype(BF16), it['qg']], axis=0)
        it['kdt'] = _dot_nt(eye_b, it['kd']).astype(BF16)
    state = [s_sc[j] for j in range(4)]
    for it in inst:
        st = state[it['slot']]
        ws = _dot(it['wq'], st.astype(BF16))
        v_new = it['u'] - ws[0:c, :]
        o = ws[c:c2, :] + _dot(it['qkf'], stack(v_new).astype(BF16))
        it['o_ref'][0, it['rows'], it['pr'] * LANES:(it['pr'] + 1) * LANES] = o
        upd = _dot(it['kdt'], v_new.astype(BF16))
        state[it['slot']] = st * it['sdec'] + jnp.where(same, upd, 0.0)
    for j in range(4):
        s_sc[j] = state[j]


def _gdn_instances(qkv_ref, lab_ref, labt_ref, o_ref, slot0, reverse, stack, fold):
    c = GDN_CHUNK
    c2 = 2 * c
    ri = lax.broadcasted_iota(jnp.int32, (c2, c2), 0)
    ci = lax.broadcasted_iota(jnp.int32, (c2, c2), 1)
    same = (ri // c) == (ci // c)
    if reverse:
        incl = jnp.logical_and(same, ri <= ci)
        strict = jnp.logical_and(same, ri < ci)
    else:
        incl = jnp.logical_and(same, ri >= ci)
        strict = jnp.logical_and(same, ri > ci)
    r1 = lax.broadcasted_iota(jnp.int32, (c, c), 0)
    c1 = lax.broadcasted_iota(jnp.int32, (c, c), 1)
    cum_col = ((r1 <= c1) if reverse else (r1 >= c1)).astype(BF16)
    rr = lax.broadcasted_iota(jnp.int32, (c, c2), 0)
    cc = lax.broadcasted_iota(jnp.int32, (c, c2), 1) % c
    cum_row = ((rr >= cc) if reverse else (rr <= cc)).astype(BF16)
    lane = _lane((c, LANES))
    lo = lane < GDN_DK
    lane2 = _lane((1, c2))
    lo2 = lane2 < c
    g_last_row = 0 if reverse else c - 1
    order = list(range(TM // c - 1, -1, -1) if reverse else range(TM // c))
    inst = []
    for ch in order:
        rows = slice(ch * c, (ch + 1) * c)
        lab = lab_ref[0, 0, rows, :]
        g_cols = _dot_exact_lhs(cum_col, lab)
        la_rows = labt_ref[0, 0, :, rows]
        g_rows = _dot_exact_rhs(la_rows, cum_row)
        for pr in range(2):
            h0, h1 = 2 * pr, 2 * pr + 1
            q = qkv_ref[0, rows, pr * LANES:(pr + 1) * LANES]
            k = qkv_ref[0, rows, (2 + pr) * LANES:(3 + pr) * LANES]
            v = qkv_ref[0, rows, (4 + pr) * LANES:(5 + pr) * LANES]
            beta = jnp.where(lo, lab[:, 4 + h0:5 + h0], lab[:, 4 + h1:5 + h1])
            g = jnp.where(lo, g_cols[:, h0:h0 + 1], g_cols[:, h1:h1 + 1])
            g_st_col = jnp.concatenate([g_cols[:, h0:h0 + 1], g_cols[:, h1:h1 + 1]], axis=0)
            g_st_row = jnp.where(lo2, g_rows[h0:h0 + 1, :], g_rows[h1:h1 + 1, :])
            diff = g_st_col - g_st_row
            dec = jnp.where(incl, jnp.exp(jnp.where(incl, diff, 0.0)), 0.0)
            kb = k * beta
            eg = jnp.exp(g)
            k_st = stack(k).astype(BF16)
            a = jnp.where(strict, _dot_nt(stack(kb).astype(BF16), k_st) * dec, 0.0)
            qk = _dot_nt(stack(q).astype(BF16), k_st) * dec
            gl = g[g_last_row:g_last_row + 1, :]
            gl_col = jnp.concatenate([jnp.broadcast_to(g_cols[g_last_row:g_last_row + 1, h0:h0 + 1], (c, 1)),
                                      jnp.broadcast_to(g_cols[g_last_row:g_last_row + 1, h1:h1 + 1], (c, 1))],
                                     axis=0)
            inst.append(dict(rows=rows, pr=pr, slot=slot0 + pr, o_ref=o_ref, a=a,
                             vb=stack(v * beta).astype(BF16),
                             kg=stack(kb * eg).astype(BF16), qg=(q * eg).astype(BF16),
                             qkf=fold(qk).astype(BF16), kd=(k * jnp.exp(gl - g)).astype(BF16),
                             sdec=jnp.exp(gl_col)))
    return inst


def _gdn_scan(qkv, lab, labt, *, nbl):
    bsz, s, _ = qkv.shape
    nb = s // TM
    nbc = nb - nbl

    def blk_f(i):
        return jnp.where(i < nbc, nbl + i, i - nbc)

    def blk_b(i):
        return jnp.where(i < nbc, nb - 1 - i, nbl - 1 - (i - nbc))

    def specs(blk, dsel):
        return [pl.BlockSpec((1, TM, 768), lambda b, i: (b, blk(i), 0)),
                pl.BlockSpec((1, 1, TM, LANES), lambda b, i: (b, dsel, blk(i), 0)),
                pl.BlockSpec((1, 1, 8, TM), lambda b, i: (b, dsel, 0, blk(i)))]

    out = jax.ShapeDtypeStruct((bsz, s, 2 * LANES), F32)
    return pl.pallas_call(
        _gdn_scan_kernel,
        grid=(bsz, nb),
        in_specs=specs(blk_f, 0) + specs(blk_b, 1),
        out_specs=[pl.BlockSpec((1, TM, 2 * LANES), lambda b, i: (b, blk_f(i), 0)),
                   pl.BlockSpec((1, TM, 2 * LANES), lambda b, i: (b, blk_b(i), 0))],
        out_shape=[out, out],
        scratch_shapes=[pltpu.VMEM((4, LANES, LANES), F32)],
        compiler_params=_cparams(("parallel", "arbitrary")),
        name="gdn_scan",
    )(qkv, lab, labt, qkv, lab, labt)


def _outproj_kernel(x_ref, mod_ref, of_ref, ob_ref, z_ref, gg_ref, mo_ref, so_ref, go_ref,
                    mc_ref, sc_ref, gc_ref, wout_ref, g2_ref, rw_ref, rb_ref,
                    x1_ref, h2_ref, idx_ref, gate_ref, rank_ref, base_ref, end_ref, run_sc, *, nbl):
    d = D_MODEL
    first = jnp.logical_and(pl.program_id(0) == 0, pl.program_id(1) == 0)
    is_ctx = pl.program_id(1) >= nbl

    @pl.when(first)
    def _():
        run_sc[...] = jnp.zeros(run_sc.shape, F32)

    m = mod_ref[0, 0]
    lane = _lane((TM, LANES))
    lo = lane < HEAD_DIM
    o = of_ref[0] + ob_ref[0]
    z = z_ref[0]
    mix = []
    for pr in range(2):
        t = o[:, pr * LANES:(pr + 1) * LANES]
        t2 = t * t
        ms0 = jnp.sum(jnp.where(lo, t2, 0.0), axis=-1, keepdims=True) / HEAD_DIM
        ms1 = jnp.sum(jnp.where(lo, 0.0, t2), axis=-1, keepdims=True) / HEAD_DIM
        t = t * jnp.where(lo, lax.rsqrt(ms0 + NORM_EPS), lax.rsqrt(ms1 + NORM_EPS)) * gg_ref[...]
        t = t * _silu(z[:, pr * LANES:(pr + 1) * LANES])
        mix.append(t.astype(BF16))
    for lat_ref, ctx_ref in ((mo_ref, mc_ref), (so_ref, sc_ref), (go_ref, gc_ref)):
        for hh in range(4):
            mix.append(jnp.where(is_ctx, ctx_ref[0, hh], lat_ref[0, hh]))
    proj = _dot(jnp.concatenate(mix, axis=1), wout_ref[...])
    x1 = x_ref[0] + m[:, 2 * d:3 * d] * proj
    x1_ref[0] = x1
    xn = x1 * lax.rsqrt(jnp.mean(x1 * x1, axis=-1, keepdims=True) + NORM_EPS) * g2_ref[...]
    h2 = xn * (1.0 + m[:, 4 * d:5 * d]) + m[:, 3 * d:4 * d]
    h2_ref[0] = h2.astype(BF16)

    logits = _dot_hp(h2, rw_ref[...])
    scores = 1.0 / (1.0 + jnp.exp(-logits))
    valid = lane < N_EXPERTS
    sel = jnp.where(valid, scores + rb_ref[...], -jnp.inf)
    member = jnp.zeros((TM, LANES), F32)
    picks = []
    for _ in range(TOP_K):
        mx = jnp.max(sel, axis=-1, keepdims=True)
        idx = jnp.min(jnp.where(sel == mx, lane, LANES), axis=-1, keepdims=True)
        hit = lane == idx
        gate = jnp.sum(jnp.where(hit, scores, 0.0), axis=-1, keepdims=True)
        sel = jnp.where(hit, -jnp.inf, sel)
        member = member + hit.astype(F32)
        picks.append((idx, hit, gate))
    gsum = picks[0][2]
    for kk in range(1, TOP_K):
        gsum = gsum + picks[kk][2]
    ri = lax.broadcasted_iota(jnp.int32, (TM, TM), 0)
    ci = lax.broadcasted_iota(jnp.int32, (TM, TM), 1)
    before = (ri > ci).astype(BF16)
    run = run_sc[...]
    base_ref[0] = run
    cum = _dot(before, member.astype(BF16)) + run
    idx_out = jnp.full((TM, LANES), -1, jnp.int32)
    gate_out = jnp.zeros((TM, LANES), F32)
    rank_out = jnp.zeros((TM, LANES), F32)
    for kk, (idx, hit, gate) in enumerate(picks):
        rank = jnp.sum(jnp.where(hit, cum, 0.0), axis=-1, keepdims=True)
        here = lane == kk
        idx_out = jnp.where(here, idx, idx_out)
        gate_out = jnp.where(here, gate / gsum * ROUTED_SCALE, gate_out)
        rank_out = jnp.where(here, rank, rank_out)
    idx_ref[0] = idx_out
    gate_ref[0] = gate_out
    rank_ref[0] = rank_out
    run_end = run + jnp.sum(member, axis=0, keepdims=True)
    end_ref[0] = run_end
    run_sc[...] = run_end


def _outproj(x, mod_tab, o_f, o_b, ga, gg_row, lat, ctx, wout, g2, rw, rb, *, n_blk, nbl):
    bsz, _, d = x.shape
    s_out = n_blk * TM
    full = lambda a: pl.BlockSpec(a.shape, lambda b, i: (0,) * a.ndim)
    tok = lambda w: pl.BlockSpec((1, TM, w), lambda b, i: (b, i, 0))
    head = pl.BlockSpec((1, 4, TM, LANES), lambda b, i: (b, 0, jnp.minimum(i, nbl - 1), 0))
    head_c = pl.BlockSpec((1, 4, TM, LANES), lambda b, i: (b, 0, jnp.maximum(i - nbl, 0), 0))
    return pl.pallas_call(
        functools.partial(_outproj_kernel, nbl=nbl),
        grid=(bsz, n_blk),
        in_specs=[tok(d),
                  pl.BlockSpec((1, 1, 1, 6 * d), lambda b, i: (b, i // nbl, 0, 0)),
                  tok(2 * LANES), tok(2 * LANES),
                  pl.BlockSpec((1, TM, 2 * LANES), lambda b, i: (b, i, 3)),
                  full(gg_row), head, head, head, head_c, head_c, head_c,
                  full(wout), full(g2), full(rw), full(rb)],
        out_specs=[tok(d), tok(d), tok(LANES), tok(LANES), tok(LANES),
                   pl.BlockSpec((1, 1, LANES), lambda b, i: (b * n_blk + i, 0, 0)),
                   pl.BlockSpec((1, 1, LANES), lambda b, i: (b * n_blk + i, 0, 0))],
        out_shape=[jax.ShapeDtypeStruct((bsz, s_out, d), F32),
                   jax.ShapeDtypeStruct((bsz, s_out, d), BF16),
                   jax.ShapeDtypeStruct((bsz, s_out, LANES), jnp.int32),
                   jax.ShapeDtypeStruct((bsz, s_out, LANES), F32),
                   jax.ShapeDtypeStruct((bsz, s_out, LANES), F32),
                   jax.ShapeDtypeStruct((bsz * n_blk, 1, LANES), F32),
                   jax.ShapeDtypeStruct((bsz * n_blk, 1, LANES), F32)],
        scratch_shapes=[pltpu.VMEM((1, LANES), F32)],
        compiler_params=_cparams(("arbitrary", "arbitrary")),
        name="outproj_router",
    )(x, mod_tab, o_f, o_b, ga, gg_row, *lat, *ctx, wout, g2, rw, rb)


def _moe_tables_kernel(lo_ref, hi_ref, offl_ref, seg_ref, gbase_ref, be_ref, nused_ref):
    ncp = lo_ref.shape[0]
    nblkp = be_ref.shape[0]
    seg = jnp.ceil((hi_ref[...] - lo_ref[...]) * (1.0 / MOE_SEG)) * MOE_SEG
    ri = lax.broadcasted_iota(jnp.int32, (LANES, LANES), 0)
    ci = lax.broadcasted_iota(jnp.int32, (LANES, LANES), 1)
    before_lane = (ri < ci).astype(BF16)
    rc = lax.broadcasted_iota(jnp.int32, (ncp, ncp), 0)
    cc = lax.broadcasted_iota(jnp.int32, (ncp, ncp), 1)
    before_row = (rc > cc).astype(BF16)
    seg_ref[...] = seg
    offl_ref[...] = _dot_exact_rhs(seg, before_lane)
    region = jnp.ceil(jnp.sum(seg, axis=0, keepdims=True) * (1.0 / MOE_BR)) * MOE_BR
    goff = _dot_exact_rhs(jnp.broadcast_to(region, (8, LANES)), before_lane)[0:1]
    gbase_ref[...] = goff + _dot_exact_lhs(before_row, seg)
    lane = _lane((nblkp, LANES))
    gend = jnp.where(lane < N_EXPERTS, goff + region, jnp.inf)
    row0 = (lax.broadcasted_iota(jnp.int32, (nblkp, LANES), 0) * MOE_BR).astype(F32)
    be = jnp.sum((gend <= row0).astype(F32), axis=-1, keepdims=True)
    be_ref[...] = jnp.broadcast_to(jnp.minimum(be, N_EXPERTS - 1.0), (nblkp, LANES))
    nused_ref[...] = jnp.broadcast_to(jnp.sum(region, axis=-1, keepdims=True) * (1.0 / MOE_BR), (1, LANES))


def _moe_tables(cb_lo, cb_hi, nblk):
    ncp = cb_lo.shape[0]
    nblkp = -(-nblk // 8) * 8
    tab = jax.ShapeDtypeStruct((ncp, LANES), F32)
    return pl.pallas_call(
        _moe_tables_kernel,
        out_shape=[tab, tab, tab, jax.ShapeDtypeStruct((nblkp, LANES), F32),
                   jax.ShapeDtypeStruct((1, LANES), F32)],
        name="moe_tables",
    )(cb_lo, cb_hi)


def _segment_copy(local_ref, glob_ref, sem, lo, go, size, to_global):
    loc = local_ref.at[pl.ds(lo, size)]
    glo = glob_ref.at[pl.ds(go, size)]
    return pltpu.make_async_copy(loc, glo, sem) if to_global else pltpu.make_async_copy(glo, loc, sem)


def _segment_copies_wait(total, local_ref, glob_ref, sem, *, to_global):
    size = 1 << (local_ref.shape[0].bit_length() - 1)
    while size >= MOE_SEG:

        @pl.when((total & size) != 0)
        def _(size=size):
            _segment_copy(local_ref, glob_ref, sem, 0, 0, size, to_global).wait()

        size //= 2


def _segment_copies_start(c, offl_s, seg_s, gbase_s, local_ref, glob_ref, sem, *, to_global):
    def piece(lo, go, size):
        lo = pl.multiple_of(lo, MOE_SEG)
        go = pl.multiple_of(go, MOE_SEG)
        _segment_copy(local_ref, glob_ref, sem, lo, go, size, to_global).start()

    def expert_body(e, carry):
        t = c * N_EXPERTS + e
        off, sg, gb = offl_s[t], seg_s[t], gbase_s[t]
        n_full = sg // MOE_ROWS

        def full_body(w, carry2):
            piece(off + w * MOE_ROWS, gb + w * MOE_ROWS, MOE_ROWS)
            return carry2

        @pl.when(n_full > 0)
        def _():
            lax.fori_loop(0, n_full, full_body, 0)

        done = n_full * MOE_ROWS
        size = MOE_ROWS // 2
        while size >= MOE_SEG:
            bit = sg & size

            @pl.when(bit != 0)
            def _(done=done, size=size):
                piece(off + done, gb + done, size)

            done = done + bit
            size //= 2
        return carry

    lax.fori_loop(0, N_EXPERTS, expert_body, 0)


def _moe_sort_kernel(offl_s, seg_s, gbase_s, h_ref, idx_ref, rank_ref, cb_ref, offv_ref, xs_in_ref,
                     pos_ref, xs_ref, xy_sc, sem):
    del xs_in_ref
    c = pl.program_id(0)
    tc = h_ref.shape[0]
    lane_t = _lane((tc, LANES))
    h = h_ref[...]
    idx = idx_ref[...]
    rank = rank_ref[...]
    adj = offv_ref[0] - cb_ref[0]
    pos = jnp.full((tc, LANES), -1.0, F32)
    for k in range(TOP_K):
        hit = lane_t == idx[:, k:k + 1]
        p_k = rank[:, k:k + 1] + jnp.sum(jnp.where(hit, adj, 0.0), axis=-1, keepdims=True)
        pos = jnp.where(lane_t == k, p_k, pos)
    pos_ref[...] = pos
    sel8 = (lax.broadcasted_iota(jnp.int32, (8, LANES), 0)
            == lax.broadcasted_iota(jnp.int32, (8, LANES), 1)).astype(BF16)
    hi, mid, lo = _split3(pos)
    pos_row = _dot_nt(sel8, hi) + _dot_nt(sel8, mid) + _dot_nt(sel8, lo)

    def gather_body(rb, carry):
        for u in range(2):
            r0 = pl.multiple_of((2 * rb + u) * MOE_ROWS, MOE_ROWS)
            rid = (lax.broadcasted_iota(jnp.int32, (MOE_ROWS, tc), 0) + r0).astype(F32)
            onehot = jnp.zeros((MOE_ROWS, tc), F32)
            for k in range(TOP_K):
                onehot = jnp.where(pos_row[k:k + 1, :] == rid, 1.0, onehot)
            xy_sc[pl.ds(r0, MOE_ROWS), :] = _dot(onehot.astype(BF16), h).astype(BF16)
        return carry

    last = c * N_EXPERTS + N_EXPERTS - 1
    total = offl_s[last] + seg_s[last]
    lax.fori_loop(0, (total + 2 * MOE_ROWS - 1) // (2 * MOE_ROWS), gather_body, 0)
    _segment_copies_start(c, offl_s, seg_s, gbase_s, xy_sc, xs_ref, sem, to_global=True)
    _segment_copies_wait(total, xy_sc, xs_ref, sem, to_global=True)


def _moe_ffn_kernel(be_s, nused_s, x_ref, wgu_ref, wdn_ref, y_ref, wgu_sc, wdn_sc):
    i = pl.program_id(0)
    used = i < nused_s[0]
    changed = jnp.logical_or(i == 0, be_s[i] != be_s[jnp.maximum(i - 1, 0)])

    @pl.when(jnp.logical_and(used, changed))
    def _():
        wgu_sc[...] = wgu_ref[0, 0].astype(BF16)
        wdn_sc[...] = wdn_ref[0, 0].astype(BF16)

    @pl.when(used)
    def _():
        starts = range(0, MOE_BR, MOE_ROWS)
        gus = [_dot(x_ref[r0:r0 + MOE_ROWS, :], wgu_sc[...]) for r0 in starts]
        for r0, gu in zip(starts, gus):
            act = _silu(gu[:, :D_EXPERT]) * gu[:, D_EXPERT:]
            y_ref[r0:r0 + MOE_ROWS, :] = _dot(act.astype(BF16), wdn_sc[...]).astype(BF16)


def _moe_combine_kernel(offl_s, seg_s, gbase_s, h_ref, pos_ref, gate_ref, x1_ref, g5_ref, sgu_ref, sdn_ref,
                        ys_ref, y_ref, yl_sc, sem):
    c = pl.program_id(0)
    tc = h_ref.shape[0]
    n_rows = yl_sc.shape[0]
    yl_sc[...] = jnp.zeros(yl_sc.shape, BF16)
    _segment_copies_start(c, offl_s, seg_s, gbase_s, yl_sc, ys_ref, sem, to_global=False)
    gu = _dot(h_ref[...], sgu_ref[...])
    act = _silu(gu[:, :D_EXPERT]) * gu[:, D_EXPERT:]
    y_ref[...] = _dot(act.astype(BF16), sdn_ref[...])
    last = c * N_EXPERTS + N_EXPERTS - 1
    _segment_copies_wait(offl_s[last] + seg_s[last], yl_sc, ys_ref, sem, to_global=False)
    pos = pos_ref[...]
    gate = gate_ref[...]
    kblk = n_rows // MOE_KSPLIT
    for tb in range(tc // TM):
        pos_t = pos[tb * TM:(tb + 1) * TM]
        gate_t = gate[tb * TM:(tb + 1) * TM]
        out = y_ref[tb * TM:(tb + 1) * TM, :]
        for kb in range(MOE_KSPLIT):
            cid = (lax.broadcasted_iota(jnp.int32, (TM, kblk), 1) + kb * kblk).astype(F32)
            scat = jnp.zeros((TM, kblk), F32)
            for k in range(TOP_K):
                scat = jnp.where(pos_t[:, k:k + 1] == cid, gate_t[:, k:k + 1], scat)
            out = out + _dot(scat.astype(BF16), yl_sc[kb * kblk:(kb + 1) * kblk, :])
        y_ref[tb * TM:(tb + 1) * TM, :] = x1_ref[tb * TM:(tb + 1) * TM, :] + g5_ref[tb] * out


def _moe(x1, g5, h2, idx, rank, gate, base, end, wgu, wdn, layer, sgu, sdn, buf=None):
    n_tok, d = h2.shape
    tc = MOE_CHUNK
    nc = n_tok // tc
    bpc = tc // TM
    ne = wgu.shape[1]
    assert ne == N_EXPERTS
    ncp = -(-nc // LANES) * LANES
    n_loc = -(-(tc * TOP_K + ne * MOE_SEG) // (MOE_ROWS * MOE_KSPLIT)) * (MOE_ROWS * MOE_KSPLIT)
    assert n_loc >= -(-(tc * TOP_K + ne * (MOE_SEG - 1)) // (2 * MOE_ROWS)) * (2 * MOE_ROWS)
    nblk = -(-(n_tok * TOP_K + nc * ne * (MOE_SEG - 1) + ne * (MOE_BR - 1)) // MOE_BR)
    if buf is None:
        buf = jnp.zeros((nblk * MOE_BR, d), BF16)
    assert buf.shape[0] >= nblk * MOE_BR and buf.dtype == BF16
    pad = lambda a: jnp.concatenate([a, jnp.zeros((ncp - nc, LANES), F32)], axis=0)
    cb_lo = pad(base[0::bpc, 0, :])
    cb_hi = pad(end[bpc - 1::bpc, 0, :])
    offl, seg, gbase, be, nused = _moe_tables(cb_lo, cb_hi, nblk)
    to_smem = lambda a: a[:nc, :ne].astype(jnp.int32).reshape(nc * ne)
    tabs = (to_smem(offl), to_smem(seg), to_smem(gbase))
    be_i = be[:nblk, 0].astype(jnp.int32)
    nused_i = nused[0, :1].astype(jnp.int32)
    chunk = lambda w: pl.BlockSpec((tc, w), lambda c, *_: (c, 0))
    row = pl.BlockSpec((1, 1, LANES), lambda c, *_: (c, 0, 0))
    anyspec = pl.BlockSpec(memory_space=pl.ANY)
    const = lambda a: pl.BlockSpec(a.shape, lambda c, *_: (0,) * a.ndim)

    pos, xs = pl.pallas_call(
        _moe_sort_kernel,
        grid_spec=pltpu.PrefetchScalarGridSpec(
            num_scalar_prefetch=3, grid=(nc,),
            in_specs=[chunk(d), chunk(LANES), chunk(LANES), row, row, anyspec],
            out_specs=[chunk(LANES), anyspec],
            scratch_shapes=[pltpu.VMEM((n_loc, d), BF16), pltpu.SemaphoreType.DMA]),
        out_shape=[jax.ShapeDtypeStruct((n_tok, LANES), F32), jax.ShapeDtypeStruct(buf.shape, BF16)],
        input_output_aliases={8: 1},
        compiler_params=_cparams(("arbitrary",)),
        name="moe_sort",
    )(*tabs, h2, idx, rank, cb_lo[:nc, None, :], offl[:nc, None, :], buf)

    last = lambda i, be_s, nu_s: jnp.minimum(i, nu_s[0] - 1)
    ys = pl.pallas_call(
        _moe_ffn_kernel,
        grid_spec=pltpu.PrefetchScalarGridSpec(
            num_scalar_prefetch=2, grid=(nblk,),
            in_specs=[pl.BlockSpec((MOE_BR, d), lambda i, be_s, nu_s: (last(i, be_s, nu_s), 0)),
                      pl.BlockSpec((1, 1, d, 2 * D_EXPERT),
                                   lambda i, be_s, nu_s: (layer, be_s[last(i, be_s, nu_s)], 0, 0)),
                      pl.BlockSpec((1, 1, D_EXPERT, d),
                                   lambda i, be_s, nu_s: (layer, be_s[last(i, be_s, nu_s)], 0, 0))],
            out_specs=pl.BlockSpec((MOE_BR, d), lambda i, be_s, nu_s: (last(i, be_s, nu_s), 0)),
            scratch_shapes=[pltpu.VMEM((d, 2 * D_EXPERT), BF16), pltpu.VMEM((D_EXPERT, d), BF16)]),
        out_shape=jax.ShapeDtypeStruct(buf.shape, BF16),
        input_output_aliases={2: 0},
        compiler_params=_cparams(("arbitrary",)),
        name="moe_ffn",
    )(be_i, nused_i, xs, wgu, wdn)

    out = pl.pallas_call(
        _moe_combine_kernel,
        grid_spec=pltpu.PrefetchScalarGridSpec(
            num_scalar_prefetch=3, grid=(nc,),
            in_specs=[chunk(d), chunk(LANES), chunk(LANES), chunk(d),
                      pl.BlockSpec((bpc, 1, d), lambda c, *_: (c, 0, 0)),
                      const(sgu), const(sdn), anyspec],
            out_specs=chunk(d),
            scratch_shapes=[pltpu.VMEM((n_loc, d), BF16), pltpu.SemaphoreType.DMA]),
        out_shape=jax.ShapeDtypeStruct((n_tok, d), F32),
        compiler_params=_cparams(("arbitrary",)),
        name="moe_combine",
    )(*tabs, h2, pos, gate, x1, g5, sgu, sdn, ys)
    return out, ys


def _axial_tables(n_rows, rot_dim):
    rows = jnp.repeat(jnp.arange(n_rows), GRID_W).astype(F32)
    cols = jnp.tile(jnp.arange(GRID_W), n_rows).astype(F32)
    axis_dim = rot_dim // 2
    inv_freq = ROPE_THETA ** (-jnp.arange(0, axis_dim, 2, dtype=F32) / axis_dim)
    ang_r = rows[:, None] * inv_freq
    ang_c = cols[:, None] * inv_freq
    ang = jnp.concatenate([ang_r, ang_r, ang_c, ang_c], axis=-1)
    return jnp.cos(ang), jnp.sin(ang)


def _rope_slab_tables(t_len, n_ctx, rot_dim, lane0):
    cos, sin = _axial_tables(t_len // GRID_W, rot_dim)
    half = rot_dim // 4
    sign = jnp.where((jnp.arange(rot_dim) % (2 * half)) < half, -1.0, 1.0).astype(F32)
    cos_t = jnp.ones((t_len + n_ctx, LANES), F32).at[:t_len, lane0:lane0 + rot_dim].set(cos)
    sin_t = jnp.zeros((t_len + n_ctx, LANES), F32).at[:t_len, lane0:lane0 + rot_dim].set(sin * sign)
    return cos_t, sin_t


def _slab_cols(starts, width):
    out = []
    for st in starts:
        out += list(range(st, st + width)) + [-1] * (LANES - width)
    return out


def _gather_cols(w, cols):
    w_ext = jnp.concatenate([w, jnp.zeros((w.shape[0], 1), w.dtype)], axis=1)
    idx = np.array([c if c >= 0 else w.shape[1] for c in cols], np.int32)
    return w_ext[:, idx]


def _pad_row(v, lane0=0):
    return jnp.zeros((1, LANES), F32).at[0, lane0:lane0 + v.shape[0]].set(v.astype(F32))


def _layer_weights(l, w_in, gdn_conv_w, gdn_a_log, gdn_dt_bias, gdn_norm_g, mla_qn_g, mla_kvn_g,
                   mla_w_uq, mla_w_ukv, mla_qk_g, swa_qk_g, swa_sink, gqa_qk_g, w_out, router_w,
                   router_bias):
    hd = HEAD_DIM
    cols = list(range(0, 1024))
    cols += list(range(O_MQ, O_MQ + MLA_Q_RANK)) + list(range(O_MKV, O_MKV + MLA_KV_RANK))
    misc = list(range(O_MKR, O_MKR + MLA_ROPE))
    misc += list(range(O_AA, O_AA + 4)) + list(range(O_AB, O_AB + 4))
    misc += list(range(O_AA + 4, O_AA + 8)) + list(range(O_AB + 4, O_AB + 8))
    cols += misc + [-1] * (LANES - len(misc))
    cols += _slab_cols([O_SQ + hd * i for i in range(4)] + [O_SK + hd * i for i in range(2)]
                       + [O_SV + hd * i for i in range(2)], hd)
    cols += _slab_cols([O_GQ + hd * i for i in range(4)] + [O_GK + hd * i for i in range(2)]
                       + [O_GV + hd * i for i in range(2)], hd)
    assert len(cols) == N_COL
    qd = MLA_NOPE + MLA_ROPE
    uq_cols = []
    for hh in range(MLA_HEADS):
        uq_cols += list(range(hh * qd, hh * qd + qd)) + [-1] * (LANES - qd)
    kvd = MLA_NOPE + MLA_V
    ukv_cols = _slab_cols([hh * kvd for hh in range(MLA_HEADS)], MLA_NOPE)
    ukv_cols += _slab_cols([hh * kvd + MLA_NOPE for hh in range(MLA_HEADS)], MLA_V)
    orow = list(range(0, 256)) + _slab_cols([256 + hd * i for i in range(12)], hd)
    alog = jnp.zeros((1, LANES), F32)
    dtb = jnp.zeros((1, LANES), F32)
    for dd in range(2):
        alog = alog.at[0, 32 + 8 * dd:36 + 8 * dd].set(gdn_a_log[l, dd])
        dtb = dtb.at[0, 32 + 8 * dd:36 + 8 * dd].set(gdn_dt_bias[l, dd])
    return dict(
        win=_gather_cols(w_in[l], cols).astype(BF16),
        conv=gdn_conv_w[l][:, :768],
        alog=alog, dtb=dtb,
        gdn_g=jnp.concatenate([gdn_norm_g[l], gdn_norm_g[l]]).reshape(1, LANES),
        mla_qn_g=mla_qn_g[l], mla_kvn_g=mla_kvn_g[l],
        wuq=_gather_cols(mla_w_uq[l], uq_cols).astype(BF16),
        wukv=_gather_cols(mla_w_ukv[l], ukv_cols).astype(BF16),
        mqg=_pad_row(mla_qk_g[l, 0]), mkg=_pad_row(mla_qk_g[l, 1, :MLA_NOPE]),
        krg=_pad_row(mla_qk_g[l, 1, MLA_NOPE:]),
        sqg=_pad_row(swa_qk_g[l, 0]), skg=_pad_row(swa_qk_g[l, 1]),
        gqg=_pad_row(gqa_qk_g[l, 0]), gkg=_pad_row(gqa_qk_g[l, 1]),
        wout=_gather_cols(w_out[l].T, orow).T.astype(BF16),
        rw=jnp.concatenate([router_w[l], jnp.zeros((D_MODEL, LANES - N_EXPERTS), F32)], axis=1),
        rb=_pad_row(router_bias[l]),
        sink=swa_sink[l],
    )


def _sink_rows(sink, grp, tq):
    hkv = sink.shape[0] // grp
    return jnp.repeat((sink.astype(F32) * LOG2E).reshape(hkv, grp), tq, axis=1).reshape(hkv, grp * tq, 1)


def kernel(x, c, ctx, c_ctx, w_mod, b_mod, norm1_g, norm2_g, w_in, gdn_conv_w, gdn_a_log, gdn_dt_bias,
           gdn_norm_g, mla_qn_g, mla_kvn_g, mla_w_uq, mla_w_ukv, mla_qk_g, swa_qk_g, swa_sink, gqa_qk_g,
           w_out, router_w, router_bias, exp_w_gu, exp_w_down, shared_w_gu, shared_w_down):
    bsz, t_len, d = x.shape
    n_ctx = ctx.shape[1]
    depth = w_mod.shape[0]
    s = t_len + n_ctx
    assert d == D_MODEL and t_len % TM == 0 and n_ctx % TM == 0 and t_len >= 3 * WINDOW
    assert bsz + 1 <= 8
    nbl = t_len // TM
    nb = s // TM

    tabs = (_rope_slab_tables(t_len, n_ctx, HEAD_DIM, 0) + _rope_slab_tables(t_len, n_ctx, MLA_ROPE, MLA_NOPE)
            + _rope_slab_tables(t_len, n_ctx, MLA_ROPE, 0))
    c8 = jnp.zeros((8, d), F32).at[:bsz].set(c).at[bsz].set(c_ctx)
    xs = jnp.concatenate([x, ctx], axis=1)

    tq_d = 512 if t_len % 512 == 0 else TM
    tk_d = 8448 if (s % 8448 == 0) else TM

    moe_buf = None
    for l in range(depth):
        need_ctx = l < depth - 1
        lw = _layer_weights(l, w_in, gdn_conv_w, gdn_a_log, gdn_dt_bias, gdn_norm_g, mla_qn_g, mla_kvn_g,
                            mla_w_uq, mla_w_ukv, mla_qk_g, swa_qk_g, swa_sink, gqa_qk_g, w_out, router_w,
                            router_bias)
        mod = _modulation(c8, w_mod[l], b_mod[l])
        mod_tab = jnp.stack([mod[:bsz], jnp.broadcast_to(mod[bsz], (bsz, 6 * d))], axis=1)[:, :, None, :]

        (ga, misc, mq, mk, mv, sq, sk, sv, gq, gk, gv) = _inproj(
            xs, mod_tab, norm1_g[l].reshape(1, d), lw['win'], lw, tabs, nbl)

        qkv, lab = _gdn_prep(ga, misc, lw['conv'], lw['alog'], lw['dtb'], nbl)
        labt = jnp.swapaxes(lab[..., :8], 2, 3)
        o_f, o_b = _gdn_scan(qkv, lab, labt, nbl=nbl)

        mo = _flash(mq, mk, mv, grp=1, tq=2 * tq_d, tk=tk_d, q_blk0=0, nq=t_len // (2 * tq_d),
                    k_blk0=0, nk=s // tk_d)
        go = _flash(gq, gk, gv, grp=2, tq=tq_d, tk=tk_d, q_blk0=0, nq=t_len // tq_d, k_blk0=0, nk=s // tk_d)
        so = _swa(sq, sk, sv, _sink_rows(lw['sink'], 2, WINDOW), grp=2, tq=tq_d, t_len=t_len, n_ctx=n_ctx)
        lat = (mo, so, go)
        ctx_out = lat
        if need_ctx:
            cb0 = t_len // n_ctx
            ctx_out = (
                _flash(mq, mk, mv, grp=1, tq=n_ctx, tk=n_ctx, q_blk0=cb0, nq=1, k_blk0=cb0, nk=1),
                _flash(sq, sk, sv, grp=2, tq=n_ctx, tk=n_ctx, q_blk0=cb0, nq=1, k_blk0=cb0, nk=1,
                       sink_rows=_sink_rows(lw['sink'], 2, n_ctx)),
                _flash(gq, gk, gv, grp=2, tq=n_ctx, tk=n_ctx, q_blk0=cb0, nq=1, k_blk0=cb0, nk=1))

        n_blk = nb if need_ctx else nbl
        x1, h2, idx, gate, rank, base, end = _outproj(
            xs, mod_tab, o_f, o_b, ga, lw['gdn_g'], lat, ctx_out, lw['wout'], norm2_g[l].reshape(1, d),
            lw['rw'], lw['rb'], n_blk=n_blk, nbl=nbl)

        n_tok = bsz * n_blk * TM
        assert n_tok % MOE_CHUNK == 0
        g5 = jnp.repeat(mod_tab[:, :, 0, 5 * d:], jnp.array([nbl, nb - nbl]), axis=1,
                        total_repeat_length=nb)[:, :n_blk].reshape(bsz * n_blk, 1, d)
        xs, moe_buf = _moe(x1.reshape(n_tok, d), g5, h2.reshape(n_tok, d), idx.reshape(n_tok, LANES),
                           rank.reshape(n_tok, LANES), gate.reshape(n_tok, LANES), base, end, exp_w_gu,
                           exp_w_down, l, shared_w_gu[l].astype(BF16), shared_w_down[l].astype(BF16),
                           buf=moe_buf)
        xs = xs.reshape(bsz, n_blk * TM, d)
    return xs[:, :t_len]
```

```python
import functools
import math

import numpy as np
import jax
import jax.numpy as jnp
from jax import lax
from jax.experimental import pallas as pl
from jax.experimental.pallas import tpu as pltpu

F32 = jnp.float32
BF16 = jnp.bfloat16

LANES = 128
TM = 256
VMEM_LIMIT = 56 * 1024 * 1024

D_MODEL = 1024
GRID_W = 64
HEAD_DIM = 64
ROPE_THETA = 10000.0
NORM_EPS = 1e-6
ATTN_SCALE = HEAD_DIM ** -0.5
GDN_DK = 64
GDN_CHUNK = 64
GDN_CONV = 5
MLA_HEADS = 4
MLA_Q_RANK = 256
MLA_KV_RANK = 128
MLA_NOPE = 64
MLA_ROPE = 32
MLA_V = 64
MLA_SCALE = (MLA_NOPE + MLA_ROPE) ** -0.5
SWA_HEADS = 4
SWA_KV_HEADS = 2
WINDOW = 128
GQA_HEADS = 4
GQA_KV_HEADS = 2
N_EXPERTS = 64
TOP_K = 6
D_EXPERT = 384
ROUTED_SCALE = 2.5
LOG2E = math.log2(math.e)
ONES_LANE = HEAD_DIM
FLASH_KB = 256
MOE_CHUNK = 512
MOE_ROWS = 128
MOE_BR = 512
MOE_SEG = 16
MOE_KSPLIT = 3

O_AQ, O_AK, O_AV, O_AZ, O_AA, O_AB = 0, 256, 512, 768, 1024, 1032
O_MQ, O_MKV, O_MKR = 1040, 1296, 1424
O_SQ, O_SK, O_SV = 1456, 1712, 1840
O_GQ, O_GK, O_GV = 1968, 2224, 2352
D_IN = 2480
C_GDN, C_MLA, C_MISC, C_SWA, C_GQA, N_COL = 0, 1024, 1408, 1536, 2560, 3584


def _cparams(sem):
    return pltpu.CompilerParams(dimension_semantics=sem, vmem_limit_bytes=VMEM_LIMIT)


def _dot(a, b):
    return jnp.dot(a, b, preferred_element_type=F32)


def _dot_nt(a, b):
    return lax.dot_general(a, b, (((1,), (1,)), ((), ())), preferred_element_type=F32)


def _split3(x):
    hi = x.astype(BF16)
    r1 = x - hi.astype(F32)
    mid = r1.astype(BF16)
    lo = (r1 - mid.astype(F32)).astype(BF16)
    return hi, mid, lo


def _dot_exact_lhs(a_bf16, x):
    hi, mid, lo = _split3(x)
    return _dot(a_bf16, hi) + _dot(a_bf16, mid) + _dot(a_bf16, lo)


def _dot_exact_rhs(x, b_bf16):
    hi, mid, lo = _split3(x)
    return _dot(hi, b_bf16) + _dot(mid, b_bf16) + _dot(lo, b_bf16)


def _dot_hp(a, b):
    ah = a.astype(BF16)
    al = (a - ah.astype(F32)).astype(BF16)
    bh = b.astype(BF16)
    bl = (b - bh.astype(F32)).astype(BF16)
    return _dot(ah, bh) + _dot(ah, bl) + _dot(al, bh)


def _silu(x):
    return x * (1.0 / (1.0 + jnp.exp(-x)))


def _lane(shape):
    return lax.broadcasted_iota(jnp.int32, shape, len(shape) - 1)


def _mod_kernel(c_ref, w_ref, b_ref, o_ref):
    s = _silu(c_ref[...])
    o_ref[...] = _dot(s.astype(BF16), w_ref[...].astype(BF16)) + b_ref[...]


def _modulation(c8, w_mod_l, b_mod_l):
    d = c8.shape[1]
    n = w_mod_l.shape[1]
    return pl.pallas_call(
        _mod_kernel,
        grid=(n // d,),
        in_specs=[pl.BlockSpec((8, d), lambda j: (0, 0)),
                  pl.BlockSpec((d, d), lambda j: (0, j)),
                  pl.BlockSpec((1, d), lambda j: (0, j))],
        out_specs=pl.BlockSpec((8, d), lambda j: (0, j)),
        out_shape=jax.ShapeDtypeStruct((8, n), F32),
        compiler_params=_cparams(("arbitrary",)),
        name="modulation",
    )(c8, w_mod_l, b_mod_l.reshape(1, n))


def _rope_slab(x, cos, sin_signed, half):
    lane = _lane(x.shape)
    fwd = pltpu.roll(x, LANES - half, 1)
    bwd = pltpu.roll(x, half, 1)
    rot = jnp.where(lane % (2 * half) < half, fwd, bwd)
    return x * cos + rot * sin_signed


def _inproj_kernel(x_ref, mod_ref, g1_ref, win_ref, qn_g_ref, kvn_g_ref, wuq_ref, wukv_ref,
                   mqg_ref, mkg_ref, krg_ref, sqg_ref, skg_ref, gqg_ref, gkg_ref,
                   cos_hd_ref, sin_hd_ref, cos_mq_ref, sin_mq_ref, cos_kr_ref, sin_kr_ref,
                   ga_ref, misc_ref, mq_ref, mk_ref, mv_ref, sq_ref, sk_ref, sv_ref,
                   gq_ref, gk_ref, gv_ref):
    d = D_MODEL
    x = x_ref[0]
    m = mod_ref[0, 0]
    shift, scale = m[:, 0:d], m[:, d:2 * d]
    xn = x * lax.rsqrt(jnp.mean(x * x, axis=-1, keepdims=True) + NORM_EPS) * g1_ref[...]
    h = xn * (1.0 + scale) + shift
    hb = h.astype(BF16)

    def proj(c0, width):
        return _dot(hb, win_ref[:, c0:c0 + width])

    ga_ref[0] = proj(C_GDN, 1024)
    pm = proj(C_MLA, C_SWA - C_MLA)
    misc = pm[:, C_MISC - C_MLA:C_MISC - C_MLA + LANES]
    misc_ref[0] = misc

    lane = _lane((TM, LANES))
    cos_hd, sin_hd = cos_hd_ref[...], sin_hd_ref[...]

    cq = pm[:, 0:MLA_Q_RANK]
    cqn = cq * lax.rsqrt(jnp.mean(cq * cq, axis=-1, keepdims=True) + NORM_EPS) * qn_g_ref[...]
    qup = _dot(cqn.astype(BF16), wuq_ref[...])
    ckv = pm[:, MLA_Q_RANK:MLA_Q_RANK + MLA_KV_RANK]
    ckvn = ckv * lax.rsqrt(jnp.mean(ckv * ckv, axis=-1, keepdims=True) + NORM_EPS) * kvn_g_ref[...]
    kvup = _dot(ckvn.astype(BF16), wukv_ref[...])
    kr = jnp.where(lane < MLA_ROPE, misc, 0.0)
    kr = kr * lax.rsqrt(jnp.sum(kr * kr, axis=-1, keepdims=True) / MLA_ROPE + NORM_EPS) * krg_ref[...]
    kr = _rope_slab(kr, cos_kr_ref[...], sin_kr_ref[...], MLA_ROPE // 4)
    kpe = pltpu.roll(kr, MLA_NOPE, 1)
    is_nope = lane < MLA_NOPE
    for hh in range(MLA_HEADS):
        q = qup[:, hh * LANES:(hh + 1) * LANES]
        q2 = q * q
        ss_n = jnp.sum(jnp.where(is_nope, q2, 0.0), axis=-1, keepdims=True)
        ss_p = jnp.sum(jnp.where(is_nope, 0.0, q2), axis=-1, keepdims=True)
        r = jnp.where(is_nope, lax.rsqrt(ss_n / MLA_NOPE + NORM_EPS), lax.rsqrt(ss_p / MLA_ROPE + NORM_EPS))
        q = q * r * mqg_ref[...]
        q = _rope_slab(q, cos_mq_ref[...], sin_mq_ref[...], MLA_ROPE // 4)
        mq_ref[0, hh] = (q * (MLA_SCALE * LOG2E)).astype(BF16)
        k = kvup[:, hh * LANES:(hh + 1) * LANES]
        k = k * lax.rsqrt(jnp.sum(k * k, axis=-1, keepdims=True) / MLA_NOPE + NORM_EPS) * mkg_ref[...]
        mk_ref[0, hh] = (k + kpe).astype(BF16)
        v = kvup[:, (MLA_HEADS + hh) * LANES:(MLA_HEADS + hh + 1) * LANES]
        mv_ref[0, hh] = jnp.where(lane == ONES_LANE, 1.0, v).astype(BF16)

    def gqa_prep(base, qg_ref, kg_ref, q_ref, k_ref, v_ref, nq, nkv):
        p = proj(base, (nq + 2 * nkv) * LANES)
        base = 0
        for hh in range(nq):
            q = p[:, base + hh * LANES:base + (hh + 1) * LANES]
            q = q * lax.rsqrt(jnp.sum(q * q, axis=-1, keepdims=True) / HEAD_DIM + NORM_EPS) * qg_ref[...]
            q = _rope_slab(q, cos_hd, sin_hd, HEAD_DIM // 4) * (ATTN_SCALE * LOG2E)
            q_ref[0, hh] = q.astype(BF16)
        for hh in range(nkv):
            k = p[:, base + (nq + hh) * LANES:base + (nq + hh + 1) * LANES]
            k = k * lax.rsqrt(jnp.sum(k * k, axis=-1, keepdims=True) / HEAD_DIM + NORM_EPS) * kg_ref[...]
            k = _rope_slab(k, cos_hd, sin_hd, HEAD_DIM // 4)
            k_ref[0, hh] = k.astype(BF16)
            v = p[:, base + (nq + nkv + hh) * LANES:base + (nq + nkv + hh + 1) * LANES]
            v_ref[0, hh] = jnp.where(lane == ONES_LANE, 1.0, v).astype(BF16)

    gqa_prep(C_SWA, sqg_ref, skg_ref, sq_ref, sk_ref, sv_ref, SWA_HEADS, SWA_KV_HEADS)
    gqa_prep(C_GQA, gqg_ref, gkg_ref, gq_ref, gk_ref, gv_ref, GQA_HEADS, GQA_KV_HEADS)


def _inproj(x, mod_tab, g1, win, lw, tabs, nbl):
    bsz, s, d = x.shape
    nb = s // TM
    row = lambda a: a.reshape(1, -1)
    full = lambda a: pl.BlockSpec(a.shape, lambda b, i: (0,) * a.ndim)
    tab_spec = pl.BlockSpec((TM, LANES), lambda b, i: (i, 0))
    head_out = lambda nh: pl.BlockSpec((1, nh, TM, LANES), lambda b, i: (b, 0, i, 0))
    head_shape = lambda nh: jax.ShapeDtypeStruct((bsz, nh, s, LANES), BF16)
    small = [row(lw['mla_qn_g']), row(lw['mla_kvn_g']), lw['wuq'], lw['wukv'],
             lw['mqg'], lw['mkg'], lw['krg'], lw['sqg'], lw['skg'], lw['gqg'], lw['gkg']]
    return pl.pallas_call(
        _inproj_kernel,
        grid=(bsz, nb),
        in_specs=[pl.BlockSpec((1, TM, d), lambda b, i: (b, i, 0)),
                  pl.BlockSpec((1, 1, 1, 6 * d), lambda b, i: (b, i // nbl, 0, 0)),
                  full(g1), full(win)] + [full(a) for a in small] + [tab_spec] * 6,
        out_specs=[pl.BlockSpec((1, TM, 1024), lambda b, i: (b, i, 0)),
                   pl.BlockSpec((1, TM, LANES), lambda b, i: (b, i, 0)),
                   head_out(4), head_out(4), head_out(4),
                   head_out(4), head_out(2), head_out(2),
                   head_out(4), head_out(2), head_out(2)],
        out_shape=[jax.ShapeDtypeStruct((bsz, s, 1024), F32),
                   jax.ShapeDtypeStruct((bsz, s, LANES), F32),
                   head_shape(4), head_shape(4), head_shape(4),
                   head_shape(4), head_shape(2), head_shape(2),
                   head_shape(4), head_shape(2), head_shape(2)],
        compiler_params=_cparams(("parallel", "parallel")),
        name="inproj",
    )(x, mod_tab, g1, win, *small, *tabs)


def _flash_kernel(*refs, grp, tq, nk, use_sink):
    refs = list(refs)
    sink_ref = refs.pop(0) if use_sink else None
    q_ref, k_ref, v_ref = refs[0], refs[1], refs[2]
    o_ref, m_sc, acc_sc = refs[-3], refs[-2], refs[-1]
    kj = pl.program_id(3)
    tk = k_ref.shape[2]
    kb = min(FLASH_KB, tk)

    @pl.when(kj == 0)
    def _():
        m_sc[...] = jnp.full(m_sc.shape, -jnp.inf, F32)
        acc_sc[...] = jnp.zeros(acc_sc.shape, F32)

    q = q_ref[0].reshape(grp * tq, LANES)
    m = m_sc[...]
    acc = acc_sc[...]
    for j in range(tk // kb):
        s = _dot_nt(q, k_ref[0, 0, j * kb:(j + 1) * kb, :])
        m_new = jnp.maximum(m, jnp.max(s, axis=-1, keepdims=True))
        alpha = jnp.exp2(m - m_new)
        pr = jnp.exp2(s - jnp.concatenate([m_new] * (kb // LANES), axis=1))
        acc = alpha * acc + _dot(pr.astype(BF16), v_ref[0, 0, j * kb:(j + 1) * kb, :])
        m = m_new
    m_sc[...] = m
    acc_sc[...] = acc

    @pl.when(kj == nk - 1)
    def _():
        l = acc[:, ONES_LANE:ONES_LANE + 1]
        out = acc
        if use_sink:
            sk = sink_ref[0]
            m_old = m[:, 0:1]
            m_fin = jnp.maximum(m_old, sk)
            a = jnp.exp2(m_old - m_fin)
            l = l * a + jnp.exp2(sk - m_fin)
            out = out * a
        o_ref[0] = (out / l).reshape(grp, tq, LANES).astype(o_ref.dtype)


def _flash(q, k, v, *, grp, tq, tk, q_blk0, nq, k_blk0, nk, sink_rows=None):
    bsz, hq, s, _ = q.shape
    hkv = hq // grp
    use_sink = sink_rows is not None
    in_specs = []
    args = []
    if use_sink:
        in_specs.append(pl.BlockSpec((1, grp * tq, 1), lambda b, h, i, j: (h, 0, 0)))
        args.append(sink_rows)
    in_specs += [pl.BlockSpec((1, grp, tq, LANES), lambda b, h, i, j: (b, h, i + q_blk0, 0)),
                 pl.BlockSpec((1, 1, tk, LANES), lambda b, h, i, j: (b, h, j + k_blk0, 0)),
                 pl.BlockSpec((1, 1, tk, LANES), lambda b, h, i, j: (b, h, j + k_blk0, 0))]
    args += [q, k, v]
    rows = grp * tq
    return pl.pallas_call(
        functools.partial(_flash_kernel, grp=grp, tq=tq, nk=nk, use_sink=use_sink),
        grid=(bsz, hkv, nq, nk),
        in_specs=in_specs,
        out_specs=pl.BlockSpec((1, grp, tq, LANES), lambda b, h, i, j: (b, h, i, 0)),
        out_shape=jax.ShapeDtypeStruct((bsz, hq, nq * tq, LANES), BF16),
        scratch_shapes=[pltpu.VMEM((rows, LANES), F32), pltpu.VMEM((rows, LANES), F32)],
        compiler_params=_cparams(("parallel", "parallel", "parallel", "arbitrary")),
        name="flash",
    )(*args)


def _swa_kernel(sink_ref, q_ref, k_ref, v_ref, o_ref, *, grp, tq, t_len, n_ctx):
    w = WINDOW
    n = pl.program_id(2)
    kc = k_ref[0, 0, pl.ds(t_len, n_ctx), :]
    vc = v_ref[0, 0, pl.ds(t_len, n_ctx), :]
    for qb in range(tq // w):
        q0 = n * tq + qb * w
        start = pl.multiple_of(jnp.clip(q0 - w, 0, t_len - 3 * w), w)
        q = q_ref[0, :, qb * w:(qb + 1) * w, :].reshape(grp * w, LANES)
        kl = k_ref[0, 0, pl.ds(start, 3 * w), :]
        vl = v_ref[0, 0, pl.ds(start, 3 * w), :]
        s_loc = _dot_nt(q, kl)
        qpos = q0 + (lax.broadcasted_iota(jnp.int32, s_loc.shape, 0) % w)
        kpos = start + lax.broadcasted_iota(jnp.int32, s_loc.shape, 1)
        s_loc = jnp.where(jnp.abs(qpos - kpos) <= w, s_loc, -jnp.inf)
        s_ctx = _dot_nt(q, kc)
        sk = sink_ref[0]
        m = jnp.maximum(jnp.maximum(jnp.max(s_loc, axis=-1, keepdims=True),
                                    jnp.max(s_ctx, axis=-1, keepdims=True)), sk)
        p_loc = jnp.exp2(s_loc - m)
        p_ctx = jnp.exp2(s_ctx - m)
        o = _dot(p_loc.astype(BF16), vl) + _dot(p_ctx.astype(BF16), vc)
        l = o[:, ONES_LANE:ONES_LANE + 1] + jnp.exp2(sk - m)
        o_ref[0, :, qb * w:(qb + 1) * w, :] = (o / l).reshape(grp, w, LANES).astype(o_ref.dtype)


def _swa(q, k, v, sink_rows, *, grp, tq, t_len, n_ctx):
    bsz, hq, s, _ = q.shape
    hkv = hq // grp
    return pl.pallas_call(
        functools.partial(_swa_kernel, grp=grp, tq=tq, t_len=t_len, n_ctx=n_ctx),
        grid=(bsz, hkv, t_len // tq),
        in_specs=[pl.BlockSpec((1, grp * WINDOW, 1), lambda b, h, n: (h, 0, 0)),
                  pl.BlockSpec((1, grp, tq, LANES), lambda b, h, n: (b, h, n, 0)),
                  pl.BlockSpec((1, 1, s, LANES), lambda b, h, n: (b, h, 0, 0)),
                  pl.BlockSpec((1, 1, s, LANES), lambda b, h, n: (b, h, 0, 0))],
        out_specs=pl.BlockSpec((1, grp, tq, LANES), lambda b, h, n: (b, h, n, 0)),
        out_shape=jax.ShapeDtypeStruct((bsz, hq, t_len, LANES), BF16),
        compiler_params=_cparams(("parallel", "parallel", "arbitrary")),
        name="swa",
    )(sink_rows, q, k, v)


def _gdn_prep_kernel(cur_ref, prev_ref, next_ref, misc_ref, cw_ref, alog_ref, dtb_ref,
                     qkv_ref, lab_ref, *, nbl, nb):
    i = pl.program_id(1)
    first = jnp.logical_or(i == 0, i == nbl)
    last = jnp.logical_or(i == nbl - 1, i == nb - 1)
    cur = cur_ref[0]
    prev = jnp.where(first, 0.0, prev_ref[0])
    nxt = jnp.where(last, 0.0, next_ref[0])
    xe = jnp.concatenate([prev, cur, nxt], axis=0)
    cw = cw_ref[...]
    acc = jnp.zeros(cur.shape, F32)
    for j in range(GDN_CONV):
        off = 8 + j - (GDN_CONV - 1) // 2
        acc = acc + xe[off:off + TM, :] * cw[j:j + 1, :]
    y = _silu(acc)
    lane = _lane((TM, LANES))
    lo = lane < GDN_DK
    for sl in range(6):
        t = y[:, sl * LANES:(sl + 1) * LANES]
        if sl < 4:
            t2 = t * t
            ss0 = jnp.sum(jnp.where(lo, t2, 0.0), axis=-1, keepdims=True)
            ss1 = jnp.sum(jnp.where(lo, 0.0, t2), axis=-1, keepdims=True)
            t = t * jnp.where(lo, lax.rsqrt(ss0 + NORM_EPS), lax.rsqrt(ss1 + NORM_EPS))
            if sl < 2:
                t = t * GDN_DK ** -0.5
        qkv_ref[0, :, sl * LANES:(sl + 1) * LANES] = t
    xm = misc_ref[0]
    za = xm + dtb_ref[...]
    softplus = jnp.maximum(za, 0.0) + jnp.log(1.0 + jnp.exp(-jnp.abs(za)))
    log_a = -jnp.exp(alog_ref[...]) * softplus
    beta = 1.0 / (1.0 + jnp.exp(-xm))
    is_a = (lane % 8) < 4
    vals = jnp.where(is_a, log_a, beta)
    lab_ref[0, 0] = pltpu.roll(vals, LANES - 32, 1)
    lab_ref[0, 1] = pltpu.roll(vals, LANES - 40, 1)


def _gdn_prep(ga, misc, cw, alog_row, dtb_row, nbl):
    bsz, s, _ = ga.shape
    nb = s // TM
    r8 = TM // 8
    n8 = s // 8
    return pl.pallas_call(
        functools.partial(_gdn_prep_kernel, nbl=nbl, nb=nb),
        grid=(bsz, nb),
        in_specs=[pl.BlockSpec((1, TM, 768), lambda b, i: (b, i, 0)),
                  pl.BlockSpec((1, 8, 768), lambda b, i: (b, jnp.maximum(i * r8 - 1, 0), 0)),
                  pl.BlockSpec((1, 8, 768), lambda b, i: (b, jnp.minimum((i + 1) * r8, n8 - 1), 0)),
                  pl.BlockSpec((1, TM, LANES), lambda b, i: (b, i, 0)),
                  pl.BlockSpec(cw.shape, lambda b, i: (0, 0)),
                  pl.BlockSpec((1, LANES), lambda b, i: (0, 0)),
                  pl.BlockSpec((1, LANES), lambda b, i: (0, 0))],
        out_specs=[pl.BlockSpec((1, TM, 768), lambda b, i: (b, i, 0)),
                   pl.BlockSpec((1, 2, TM, LANES), lambda b, i: (b, 0, i, 0))],
        out_shape=[jax.ShapeDtypeStruct((bsz, s, 768), F32),
                   jax.ShapeDtypeStruct((bsz, 2, s, LANES), F32)],
        compiler_params=_cparams(("parallel", "parallel")),
        name="gdn_prep",
    )(ga, ga, ga, misc, cw, alog_row, dtb_row)


def _gdn_scan_kernel(qkvf_ref, labf_ref, labtf_ref, qkvb_ref, labb_ref, labtb_ref, of_ref, ob_ref, s_sc):
    c = GDN_CHUNK
    c2 = 2 * c
    i = pl.program_id(1)

    @pl.when(i == 0)
    def _():
        s_sc[...] = jnp.zeros(s_sc.shape, F32)

    ri = lax.broadcasted_iota(jnp.int32, (c2, c2), 0)
    ci = lax.broadcasted_iota(jnp.int32, (c2, c2), 1)
    same = (ri // c) == (ci // c)
    eye = (ri == ci).astype(F32)
    lane = _lane((c, LANES))
    lo = lane < GDN_DK

    def stack(x):
        return jnp.concatenate([jnp.where(lo, x, 0.0), jnp.where(lo, 0.0, x)], axis=0)

    def fold(mat):
        return mat[0:c, :] + mat[c:c2, :]

    inst_f = _gdn_instances(qkvf_ref, labf_ref, labtf_ref, of_ref, 0, False, stack, fold)
    inst_b = _gdn_instances(qkvb_ref, labb_ref, labtb_ref, ob_ref, 2, True, stack, fold)
    inst = [it for pair in zip(inst_f, inst_b) for it in pair]
    tinv = [eye - it['a'] for it in inst]
    pw = [it['a'] for it in inst]
    for _ in range(5):
        pw = [_dot_hp(x, x) for x in pw]
        tinv = [t + _dot_hp(t, x) for t, x in zip(tinv, pw)]
    eye_b = eye.astype(BF16)
    for it, t in zip(inst, tinv):
        tf = fold(t).astype(BF16)
        it['u'] = _dot(tf, it['vb'])
        it['wq'] = jnp.concatenate([_dot(tf, it['kg']).astype(BF16), it['qg']], axis=0)
        it['kdt'] = _dot_nt(eye_b, it['kd']).astype(BF16)
    state = [s_sc[j] for j in range(4)]
    for it in inst:
        st = state[it['slot']]
        ws = _dot(it['wq'], st.astype(BF16))
        v_new = it['u'] - ws[0:c, :]
        o = ws[c:c2, :] + _dot(it['qkf'], stack(v_new).astype(BF16))
        it['o_ref'][0, it['rows'], it['pr'] * LANES:(it['pr'] + 1) * LANES] = o
        upd = _dot(it['kdt'], v_new.astype(BF16))
        state[it['slot']] = st * it['sdec'] + jnp.where(same, upd, 0.0)
    for j in range(4):
        s_sc[j] = state[j]


def _gdn_instances(qkv_ref, lab_ref, labt_ref, o_ref, slot0, reverse, stack, fold):
    c = GDN_CHUNK
    c2 = 2 * c
    ri = lax.broadcasted_iota(jnp.int32, (c2, c2), 0)
    ci = lax.broadcasted_iota(jnp.int32, (c2, c2), 1)
    same = (ri // c) == (ci // c)
    if reverse:
        incl = jnp.logical_and(same, ri <= ci)
        strict = jnp.logical_and(same, ri < ci)
    else:
        incl = jnp.logical_and(same, ri >= ci)
        strict = jnp.logical_and(same, ri > ci)
    r1 = lax.broadcasted_iota(jnp.int32, (c, c), 0)
    c1 = lax.broadcasted_iota(jnp.int32, (c, c), 1)
    cum_col = ((r1 <= c1) if reverse else (r1 >= c1)).astype(BF16)
    rr = lax.broadcasted_iota(jnp.int32, (c, c2), 0)
    cc = lax.broadcasted_iota(jnp.int32, (c, c2), 1) % c
    cum_row = ((rr >= cc) if reverse else (rr <= cc)).astype(BF16)
    lane = _lane((c, LANES))
    lo = lane < GDN_DK
    lane2 = _lane((1, c2))
    lo2 = lane2 < c
    g_last_row = 0 if reverse else c - 1
    order = list(range(TM // c - 1, -1, -1) if reverse else range(TM // c))
    inst = []
    for ch in order:
        rows = slice(ch * c, (ch + 1) * c)
        lab = lab_ref[0, 0, rows, :]
        g_cols = _dot_exact_lhs(cum_col, lab)
        la_rows = labt_ref[0, 0, :, rows]
        g_rows = _dot_exact_rhs(la_rows, cum_row)
        for pr in range(2):
            h0, h1 = 2 * pr, 2 * pr + 1
            q = qkv_ref[0, rows, pr * LANES:(pr + 1) * LANES]
            k = qkv_ref[0, rows, (2 + pr) * LANES:(3 + pr) * LANES]
            v = qkv_ref[0, rows, (4 + pr) * LANES:(5 + pr) * LANES]
            beta = jnp.where(lo, lab[:, 4 + h0:5 + h0], lab[:, 4 + h1:5 + h1])
            g = jnp.where(lo, g_cols[:, h0:h0 + 1], g_cols[:, h1:h1 + 1])
            g_st_col = jnp.concatenate([g_cols[:, h0:h0 + 1], g_cols[:, h1:h1 + 1]], axis=0)
            g_st_row = jnp.where(lo2, g_rows[h0:h0 + 1, :], g_rows[h1:h1 + 1, :])
            diff = g_st_col - g_st_row
            dec = jnp.where(incl, jnp.exp(jnp.where(incl, diff, 0.0)), 0.0)
            kb = k * beta
            eg = jnp.exp(g)
            k_st = stack(k).astype(BF16)
            a = jnp.where(strict, _dot_nt(stack(kb).astype(BF16), k_st) * dec, 0.0)
            qk = _dot_nt(stack(q).astype(BF16), k_st) * dec
            gl = g[g_last_row:g_last_row + 1, :]
            gl_col = jnp.concatenate([jnp.broadcast_to(g_cols[g_last_row:g_last_row + 1, h0:h0 + 1], (c, 1)),
                                      jnp.broadcast_to(g_cols[g_last_row:g_last_row + 1, h1:h1 + 1], (c, 1))],
                                     axis=0)
            inst.append(dict(rows=rows, pr=pr, slot=slot0 + pr, o_ref=o_ref, a=a,
                             vb=stack(v * beta).astype(BF16),
                             kg=stack(kb * eg).astype(BF16), qg=(q * eg).astype(BF16),
                             qkf=fold(qk).astype(BF16), kd=(k * jnp.exp(gl - g)).astype(BF16),
                             sdec=jnp.exp(gl_col)))
    return inst


def _gdn_scan(qkv, lab, labt, *, nbl):
    bsz, s, _ = qkv.shape
    nb = s // TM
    nbc = nb - nbl

    def blk_f(i):
        return jnp.where(i < nbc, nbl + i, i - nbc)

    def blk_b(i):
        return jnp.where(i < nbc, nb - 1 - i, nbl - 1 - (i - nbc))

    def specs(blk, dsel):
        return [pl.BlockSpec((1, TM, 768), lambda b, i: (b, blk(i), 0)),
                pl.BlockSpec((1, 1, TM, LANES), lambda b, i: (b, dsel, blk(i), 0)),
                pl.BlockSpec((1, 1, 8, TM), lambda b, i: (b, dsel, 0, blk(i)))]

    out = jax.ShapeDtypeStruct((bsz, s, 2 * LANES), F32)
    return pl.pallas_call(
        _gdn_scan_kernel,
        grid=(bsz, nb),
        in_specs=specs(blk_f, 0) + specs(blk_b, 1),
        out_specs=[pl.BlockSpec((1, TM, 2 * LANES), lambda b, i: (b, blk_f(i), 0)),
                   pl.BlockSpec((1, TM, 2 * LANES), lambda b, i: (b, blk_b(i), 0))],
        out_shape=[out, out],
        scratch_shapes=[pltpu.VMEM((4, LANES, LANES), F32)],
        compiler_params=_cparams(("parallel", "arbitrary")),
        name="gdn_scan",
    )(qkv, lab, labt, qkv, lab, labt)


def _outproj_kernel(x_ref, mod_ref, of_ref, ob_ref, z_ref, gg_ref, mo_ref, so_ref, go_ref,
                    mc_ref, sc_ref, gc_ref, wout_ref, g2_ref, rw_ref, rb_ref,
                    x1_ref, h2_ref, idx_ref, gate_ref, rank_ref, base_ref, end_ref, run_sc, *, nbl):
    d = D_MODEL
    first = jnp.logical_and(pl.program_id(0) == 0, pl.program_id(1) == 0)
    is_ctx = pl.program_id(1) >= nbl

    @pl.when(first)
    def _():
        run_sc[...] = jnp.zeros(run_sc.shape, F32)

    m = mod_ref[0, 0]
    lane = _lane((TM, LANES))
    lo = lane < HEAD_DIM
    o = of_ref[0] + ob_ref[0]
    z = z_ref[0]
    mix = []
    for pr in range(2):
        t = o[:, pr * LANES:(pr + 1) * LANES]
        t2 = t * t
        ms0 = jnp.sum(jnp.where(lo, t2, 0.0), axis=-1, keepdims=True) / HEAD_DIM
        ms1 = jnp.sum(jnp.where(lo, 0.0, t2), axis=-1, keepdims=True) / HEAD_DIM
        t = t * jnp.where(lo, lax.rsqrt(ms0 + NORM_EPS), lax.rsqrt(ms1 + NORM_EPS)) * gg_ref[...]
        t = t * _silu(z[:, pr * LANES:(pr + 1) * LANES])
        mix.append(t.astype(BF16))
    for lat_ref, ctx_ref in ((mo_ref, mc_ref), (so_ref, sc_ref), (go_ref, gc_ref)):
        for hh in range(4):
            mix.append(jnp.where(is_ctx, ctx_ref[0, hh], lat_ref[0, hh]))
    proj = _dot(jnp.concatenate(mix, axis=1), wout_ref[...])
    x1 = x_ref[0] + m[:, 2 * d:3 * d] * proj
    x1_ref[0] = x1
    xn = x1 * lax.rsqrt(jnp.mean(x1 * x1, axis=-1, keepdims=True) + NORM_EPS) * g2_ref[...]
    h2 = xn * (1.0 + m[:, 4 * d:5 * d]) + m[:, 3 * d:4 * d]
    h2_ref[0] = h2.astype(BF16)

    logits = _dot_hp(h2, rw_ref[...])
    scores = 1.0 / (1.0 + jnp.exp(-logits))
    valid = lane < N_EXPERTS
    sel = jnp.where(valid, scores + rb_ref[...], -jnp.inf)
    member = jnp.zeros((TM, LANES), F32)
    picks = []
    for _ in range(TOP_K):
        mx = jnp.max(sel, axis=-1, keepdims=True)
        idx = jnp.min(jnp.where(sel == mx, lane, LANES), axis=-1, keepdims=True)
        hit = lane == idx
        gate = jnp.sum(jnp.where(hit, scores, 0.0), axis=-1, keepdims=True)
        sel = jnp.where(hit, -jnp.inf, sel)
        member = member + hit.astype(F32)
        picks.append((idx, hit, gate))
    gsum = picks[0][2]
    for kk in range(1, TOP_K):
        gsum = gsum + picks[kk][2]
    ri = lax.broadcasted_iota(jnp.int32, (TM, TM), 0)
    ci = lax.broadcasted_iota(jnp.int32, (TM, TM), 1)
    before = (ri > ci).astype(BF16)
    run = run_sc[...]
    base_ref[0] = run
    cum = _dot(before, member.astype(BF16)) + run
    idx_out = jnp.full((TM, LANES), -1, jnp.int32)
    gate_out = jnp.zeros((TM, LANES), F32)
    rank_out = jnp.zeros((TM, LANES), F32)
    for kk, (idx, hit, gate) in enumerate(picks):
        rank = jnp.sum(jnp.where(hit, cum, 0.0), axis=-1, keepdims=True)
        here = lane == kk
        idx_out = jnp.where(here, idx, idx_out)
        gate_out = jnp.where(here, gate / gsum * ROUTED_SCALE, gate_out)
        rank_out = jnp.where(here, rank, rank_out)
    idx_ref[0] = idx_out
    gate_ref[0] = gate_out
    rank_ref[0] = rank_out
    run_end = run + jnp.sum(member, axis=0, keepdims=True)
    end_ref[0] = run_end
    run_sc[...] = run_end


def _outproj(x, mod_tab, o_f, o_b, ga, gg_row, lat, ctx, wout, g2, rw, rb, *, n_blk, nbl):
    bsz, _, d = x.shape
    s_out = n_blk * TM
    full = lambda a: pl.BlockSpec(a.shape, lambda b, i: (0,) * a.ndim)
    tok = lambda w: pl.BlockSpec((1, TM, w), lambda b, i: (b, i, 0))
    head = pl.BlockSpec((1, 4, TM, LANES), lambda b, i: (b, 0, jnp.minimum(i, nbl - 1), 0))
    head_c = pl.BlockSpec((1, 4, TM, LANES), lambda b, i: (b, 0, jnp.maximum(i - nbl, 0), 0))
    return pl.pallas_call(
        functools.partial(_outproj_kernel, nbl=nbl),
        grid=(bsz, n_blk),
        in_specs=[tok(d),
                  pl.BlockSpec((1, 1, 1, 6 * d), lambda b, i: (b, i // nbl, 0, 0)),
                  tok(2 * LANES), tok(2 * LANES),
                  pl.BlockSpec((1, TM, 2 * LANES), lambda b, i: (b, i, 3)),
                  full(gg_row), head, head, head, head_c, head_c, head_c,
                  full(wout), full(g2), full(rw), full(rb)],
        out_specs=[tok(d), tok(d), tok(LANES), tok(LANES), tok(LANES),
                   pl.BlockSpec((1, 1, LANES), lambda b, i: (b * n_blk + i, 0, 0)),
                   pl.BlockSpec((1, 1, LANES), lambda b, i: (b * n_blk + i, 0, 0))],
        out_shape=[jax.ShapeDtypeStruct((bsz, s_out, d), F32),
                   jax.ShapeDtypeStruct((bsz, s_out, d), BF16),
                   jax.ShapeDtypeStruct((bsz, s_out, LANES), jnp.int32),
                   jax.ShapeDtypeStruct((bsz, s_out, LANES), F32),
                   jax.ShapeDtypeStruct((bsz, s_out, LANES), F32),
                   jax.ShapeDtypeStruct((bsz * n_blk, 1, LANES), F32),
                   jax.ShapeDtypeStruct((bsz * n_blk, 1, LANES), F32)],
        scratch_shapes=[pltpu.VMEM((1, LANES), F32)],
        compiler_params=_cparams(("arbitrary", "arbitrary")),
        name="outproj_router",
    )(x, mod_tab, o_f, o_b, ga, gg_row, *lat, *ctx, wout, g2, rw, rb)


def _moe_tables_kernel(lo_ref, hi_ref, offl_ref, seg_ref, gbase_ref, be_ref, nused_ref):
    ncp = lo_ref.shape[0]
    nblkp = be_ref.shape[0]
    seg = jnp.ceil((hi_ref[...] - lo_ref[...]) * (1.0 / MOE_SEG)) * MOE_SEG
    ri = lax.broadcasted_iota(jnp.int32, (LANES, LANES), 0)
    ci = lax.broadcasted_iota(jnp.int32, (LANES, LANES), 1)
    before_lane = (ri < ci).astype(BF16)
    rc = lax.broadcasted_iota(jnp.int32, (ncp, ncp), 0)
    cc = lax.broadcasted_iota(jnp.int32, (ncp, ncp), 1)
    before_row = (rc > cc).astype(BF16)
    seg_ref[...] = seg
    offl_ref[...] = _dot_exact_rhs(seg, before_lane)
    region = jnp.ceil(jnp.sum(seg, axis=0, keepdims=True) * (1.0 / MOE_BR)) * MOE_BR
    goff = _dot_exact_rhs(jnp.broadcast_to(region, (8, LANES)), before_lane)[0:1]
    gbase_ref[...] = goff + _dot_exact_lhs(before_row, seg)
    lane = _lane((nblkp, LANES))
    gend = jnp.where(lane < N_EXPERTS, goff + region, jnp.inf)
    row0 = (lax.broadcasted_iota(jnp.int32, (nblkp, LANES), 0) * MOE_BR).astype(F32)
    be = jnp.sum((gend <= row0).astype(F32), axis=-1, keepdims=True)
    be_ref[...] = jnp.broadcast_to(jnp.minimum(be, N_EXPERTS - 1.0), (nblkp, LANES))
    nused_ref[...] = jnp.broadcast_to(jnp.sum(region, axis=-1, keepdims=True) * (1.0 / MOE_BR), (1, LANES))


def _moe_tables(cb_lo, cb_hi, nblk):
    ncp = cb_lo.shape[0]
    nblkp = -(-nblk // 8) * 8
    tab = jax.ShapeDtypeStruct((ncp, LANES), F32)
    return pl.pallas_call(
        _moe_tables_kernel,
        out_shape=[tab, tab, tab, jax.ShapeDtypeStruct((nblkp, LANES), F32),
                   jax.ShapeDtypeStruct((1, LANES), F32)],
        name="moe_tables",
    )(cb_lo, cb_hi)


def _segment_copy(local_ref, glob_ref, sem, lo, go, size, to_global):
    loc = local_ref.at[pl.ds(lo, size)]
    glo = glob_ref.at[pl.ds(go, size)]
    return pltpu.make_async_copy(loc, glo, sem) if to_global else pltpu.make_async_copy(glo, loc, sem)


def _segment_copies_wait(total, local_ref, glob_ref, sem, *, to_global):
    size = 1 << (local_ref.shape[0].bit_length() - 1)
    while size >= MOE_SEG:

        @pl.when((total & size) != 0)
        def _(size=size):
            _segment_copy(local_ref, glob_ref, sem, 0, 0, size, to_global).wait()

        size //= 2


def _segment_copies_start(c, offl_s, seg_s, gbase_s, local_ref, glob_ref, sem, *, to_global):
    def piece(lo, go, size):
        lo = pl.multiple_of(lo, MOE_SEG)
        go = pl.multiple_of(go, MOE_SEG)
        prio = (size // MOE_SEG).bit_length() % 2
        _segment_copy(local_ref, glob_ref, sem, lo, go, size, to_global).start(priority=prio)

    def expert_body(e, carry):
        t = c * N_EXPERTS + e
        off, sg, gb = offl_s[t], seg_s[t], gbase_s[t]
        n_full = sg // MOE_ROWS

        def full_body(w, carry2):
            piece(off + w * MOE_ROWS, gb + w * MOE_ROWS, MOE_ROWS)
            return carry2

        @pl.when(n_full > 0)
        def _():
            lax.fori_loop(0, n_full, full_body, 0)

        done = n_full * MOE_ROWS
        size = MOE_ROWS // 2
        while size >= MOE_SEG:
            bit = sg & size

            @pl.when(bit != 0)
            def _(done=done, size=size):
                piece(off + done, gb + done, size)

            done = done + bit
            size //= 2
        return carry

    lax.fori_loop(0, N_EXPERTS, expert_body, 0)


def _moe_sort_kernel(offl_s, seg_s, gbase_s, h_ref, idx_ref, rank_ref, cb_ref, offv_ref, xs_in_ref,
                     pos_ref, xs_ref, xy_sc, sem):
    del xs_in_ref
    c = pl.program_id(0)
    tc = h_ref.shape[0]
    lane_t = _lane((tc, LANES))
    h = h_ref[...]
    idx = idx_ref[...]
    rank = rank_ref[...]
    adj = offv_ref[0] - cb_ref[0]
    pos = jnp.full((tc, LANES), -1.0, F32)
    for k in range(TOP_K):
        hit = lane_t == idx[:, k:k + 1]
        p_k = rank[:, k:k + 1] + jnp.sum(jnp.where(hit, adj, 0.0), axis=-1, keepdims=True)
        pos = jnp.where(lane_t == k, p_k, pos)
    pos_ref[...] = pos
    sel8 = (lax.broadcasted_iota(jnp.int32, (8, LANES), 0)
            == lax.broadcasted_iota(jnp.int32, (8, LANES), 1)).astype(BF16)
    hi, mid, lo = _split3(pos)
    pos_row = _dot_nt(sel8, hi) + _dot_nt(sel8, mid) + _dot_nt(sel8, lo)

    def gather_body(rb, carry):
        for u in range(2):
            r0 = pl.multiple_of((2 * rb + u) * MOE_ROWS, MOE_ROWS)
            rid = (lax.broadcasted_iota(jnp.int32, (MOE_ROWS, tc), 0) + r0).astype(F32)
            onehot = jnp.zeros((MOE_ROWS, tc), F32)
            for k in range(TOP_K):
                onehot = jnp.where(pos_row[k:k + 1, :] == rid, 1.0, onehot)
            xy_sc[pl.ds(r0, MOE_ROWS), :] = _dot(onehot.astype(BF16), h).astype(BF16)
        return carry

    last = c * N_EXPERTS + N_EXPERTS - 1
    total = offl_s[last] + seg_s[last]
    lax.fori_loop(0, (total + 2 * MOE_ROWS - 1) // (2 * MOE_ROWS), gather_body, 0)
    _segment_copies_start(c, offl_s, seg_s, gbase_s, xy_sc, xs_ref, sem, to_global=True)
    _segment_copies_wait(total, xy_sc, xs_ref, sem, to_global=True)


def _moe_ffn_kernel(be_s, nused_s, x_ref, wgu_ref, wdn_ref, y_ref, wgu_sc, wdn_sc):
    i = pl.program_id(0)
    used = i < nused_s[0]
    changed = jnp.logical_or(i == 0, be_s[i] != be_s[jnp.maximum(i - 1, 0)])

    @pl.when(jnp.logical_and(used, changed))
    def _():
        wgu_sc[...] = wgu_ref[0, 0].astype(BF16)
        wdn_sc[...] = wdn_ref[0, 0].astype(BF16)

    @pl.when(used)
    def _():
        starts = range(0, MOE_BR, MOE_ROWS)
        gus = [_dot(x_ref[r0:r0 + MOE_ROWS, :], wgu_sc[...]) for r0 in starts]
        for r0, gu in zip(starts, gus):
            act = _silu(gu[:, :D_EXPERT]) * gu[:, D_EXPERT:]
            y_ref[r0:r0 + MOE_ROWS, :] = _dot(act.astype(BF16), wdn_sc[...]).astype(BF16)


def _moe_combine_kernel(offl_s, seg_s, gbase_s, h_ref, pos_ref, gate_ref, x1_ref, g5_ref, sgu_ref, sdn_ref,
                        ys_ref, y_ref, yl_sc, sem):
    c = pl.program_id(0)
    tc = h_ref.shape[0]
    n_rows = yl_sc.shape[0]
    yl_sc[...] = jnp.zeros(yl_sc.shape, BF16)
    _segment_copies_start(c, offl_s, seg_s, gbase_s, yl_sc, ys_ref, sem, to_global=False)
    gu = _dot(h_ref[...], sgu_ref[...])
    act = _silu(gu[:, :D_EXPERT]) * gu[:, D_EXPERT:]
    y_ref[...] = _dot(act.astype(BF16), sdn_ref[...])
    last = c * N_EXPERTS + N_EXPERTS - 1
    _segment_copies_wait(offl_s[last] + seg_s[last], yl_sc, ys_ref, sem, to_global=False)
    pos = pos_ref[...]
    gate = gate_ref[...]
    kblk = n_rows // MOE_KSPLIT
    for tb in range(tc // TM):
        pos_t = pos[tb * TM:(tb + 1) * TM]
        gate_t = gate[tb * TM:(tb + 1) * TM]
        out = y_ref[tb * TM:(tb + 1) * TM, :]
        for kb in range(MOE_KSPLIT):
            cid = (lax.broadcasted_iota(jnp.int32, (TM, kblk), 1) + kb * kblk).astype(F32)
            scat = jnp.zeros((TM, kblk), F32)
            for k in range(TOP_K):
                scat = jnp.where(pos_t[:, k:k + 1] == cid, gate_t[:, k:k + 1], scat)
            out = out + _dot(scat.astype(BF16), yl_sc[kb * kblk:(kb + 1) * kblk, :])
        y_ref[tb * TM:(tb + 1) * TM, :] = x1_ref[tb * TM:(tb + 1) * TM, :] + g5_ref[tb] * out


def _moe(x1, g5, h2, idx, rank, gate, base, end, wgu, wdn, layer, sgu, sdn, buf=None):
    n_tok, d = h2.shape
    tc = MOE_CHUNK
    nc = n_tok // tc
    bpc = tc // TM
    ne = wgu.shape[1]
    assert ne == N_EXPERTS
    ncp = -(-nc // LANES) * LANES
    n_loc = -(-(tc * TOP_K + ne * MOE_SEG) // (MOE_ROWS * MOE_KSPLIT)) * (MOE_ROWS * MOE_KSPLIT)
    assert n_loc >= -(-(tc * TOP_K + ne * (MOE_SEG - 1)) // (2 * MOE_ROWS)) * (2 * MOE_ROWS)
    nblk = -(-(n_tok * TOP_K + nc * ne * (MOE_SEG - 1) + ne * (MOE_BR - 1)) // MOE_BR)
    if buf is None:
        buf = jnp.zeros((nblk * MOE_BR, d), BF16)
    assert buf.shape[0] >= nblk * MOE_BR and buf.dtype == BF16
    pad = lambda a: jnp.concatenate([a, jnp.zeros((ncp - nc, LANES), F32)], axis=0)
    cb_lo = pad(base[0::bpc, 0, :])
    cb_hi = pad(end[bpc - 1::bpc, 0, :])
    offl, seg, gbase, be, nused = _moe_tables(cb_lo, cb_hi, nblk)
    to_smem = lambda a: a[:nc, :ne].astype(jnp.int32).reshape(nc * ne)
    tabs = (to_smem(offl), to_smem(seg), to_smem(gbase))
    be_i = be[:nblk, 0].astype(jnp.int32)
    nused_i = nused[0, :1].astype(jnp.int32)
    chunk = lambda w: pl.BlockSpec((tc, w), lambda c, *_: (c, 0))
    row = pl.BlockSpec((1, 1, LANES), lambda c, *_: (c, 0, 0))
    anyspec = pl.BlockSpec(memory_space=pl.ANY)
    const = lambda a: pl.BlockSpec(a.shape, lambda c, *_: (0,) * a.ndim)

    pos, xs = pl.pallas_call(
        _moe_sort_kernel,
        grid_spec=pltpu.PrefetchScalarGridSpec(
            num_scalar_prefetch=3, grid=(nc,),
            in_specs=[chunk(d), chunk(LANES), chunk(LANES), row, row, anyspec],
            out_specs=[chunk(LANES), anyspec],
            scratch_shapes=[pltpu.VMEM((n_loc, d), BF16), pltpu.SemaphoreType.DMA]),
        out_shape=[jax.ShapeDtypeStruct((n_tok, LANES), F32), jax.ShapeDtypeStruct(buf.shape, BF16)],
        input_output_aliases={8: 1},
        compiler_params=_cparams(("arbitrary",)),
        name="moe_sort",
    )(*tabs, h2, idx, rank, cb_lo[:nc, None, :], offl[:nc, None, :], buf)

    last = lambda i, be_s, nu_s: jnp.minimum(i, nu_s[0] - 1)
    ys = pl.pallas_call(
        _moe_ffn_kernel,
        grid_spec=pltpu.PrefetchScalarGridSpec(
            num_scalar_prefetch=2, grid=(nblk,),
            in_specs=[pl.BlockSpec((MOE_BR, d), lambda i, be_s, nu_s: (last(i, be_s, nu_s), 0)),
                      pl.BlockSpec((1, 1, d, 2 * D_EXPERT),
                                   lambda i, be_s, nu_s: (layer, be_s[last(i, be_s, nu_s)], 0, 0)),
                      pl.BlockSpec((1, 1, D_EXPERT, d),
                                   lambda i, be_s, nu_s: (layer, be_s[last(i, be_s, nu_s)], 0, 0))],
            out_specs=pl.BlockSpec((MOE_BR, d), lambda i, be_s, nu_s: (last(i, be_s, nu_s), 0)),
            scratch_shapes=[pltpu.VMEM((d, 2 * D_EXPERT), BF16), pltpu.VMEM((D_EXPERT, d), BF16)]),
        out_shape=jax.ShapeDtypeStruct(buf.shape, BF16),
        input_output_aliases={2: 0},
        compiler_params=_cparams(("arbitrary",)),
        name="moe_ffn",
    )(be_i, nused_i, xs, wgu, wdn)

    out = pl.pallas_call(
        _moe_combine_kernel,
        grid_spec=pltpu.PrefetchScalarGridSpec(
            num_scalar_prefetch=3, grid=(nc,),
            in_specs=[chunk(d), chunk(LANES), chunk(LANES), chunk(d),
                      pl.BlockSpec((bpc, 1, d), lambda c, *_: (c, 0, 0)),
                      const(sgu), const(sdn), anyspec],
            out_specs=chunk(d),
            scratch_shapes=[pltpu.VMEM((n_loc, d), BF16), pltpu.SemaphoreType.DMA]),
        out_shape=jax.ShapeDtypeStruct((n_tok, d), F32),
        compiler_params=_cparams(("arbitrary",)),
        name="moe_combine",
    )(*tabs, h2, pos, gate, x1, g5, sgu, sdn, ys)
    return out, ys


def _axial_tables(n_rows, rot_dim):
    rows = jnp.repeat(jnp.arange(n_rows), GRID_W).astype(F32)
    cols = jnp.tile(jnp.arange(GRID_W), n_rows).astype(F32)
    axis_dim = rot_dim // 2
    inv_freq = ROPE_THETA ** (-jnp.arange(0, axis_dim, 2, dtype=F32) / axis_dim)
    ang_r = rows[:, None] * inv_freq
    ang_c = cols[:, None] * inv_freq
    ang = jnp.concatenate([ang_r, ang_r, ang_c, ang_c], axis=-1)
    return jnp.cos(ang), jnp.sin(ang)


def _rope_slab_tables(t_len, n_ctx, rot_dim, lane0):
    cos, sin = _axial_tables(t_len // GRID_W, rot_dim)
    half = rot_dim // 4
    sign = jnp.where((jnp.arange(rot_dim) % (2 * half)) < half, -1.0, 1.0).astype(F32)
    cos_t = jnp.ones((t_len + n_ctx, LANES), F32).at[:t_len, lane0:lane0 + rot_dim].set(cos)
    sin_t = jnp.zeros((t_len + n_ctx, LANES), F32).at[:t_len, lane0:lane0 + rot_dim].set(sin * sign)
    return cos_t, sin_t


def _slab_cols(starts, width):
    out = []
    for st in starts:
        out += list(range(st, st + width)) + [-1] * (LANES - width)
    return out


def _gather_cols(w, cols):
    w_ext = jnp.concatenate([w, jnp.zeros((w.shape[0], 1), w.dtype)], axis=1)
    idx = np.array([c if c >= 0 else w.shape[1] for c in cols], np.int32)
    return w_ext[:, idx]


def _pad_row(v, lane0=0):
    return jnp.zeros((1, LANES), F32).at[0, lane0:lane0 + v.shape[0]].set(v.astype(F32))


def _layer_weights(l, w_in, gdn_conv_w, gdn_a_log, gdn_dt_bias, gdn_norm_g, mla_qn_g, mla_kvn_g,
                   mla_w_uq, mla_w_ukv, mla_qk_g, swa_qk_g, swa_sink, gqa_qk_g, w_out, router_w,
                   router_bias):
    hd = HEAD_DIM
    cols = list(range(0, 1024))
    cols += list(range(O_MQ, O_MQ + MLA_Q_RANK)) + list(range(O_MKV, O_MKV + MLA_KV_RANK))
    misc = list(range(O_MKR, O_MKR + MLA_ROPE))
    misc += list(range(O_AA, O_AA + 4)) + list(range(O_AB, O_AB + 4))
    misc += list(range(O_AA + 4, O_AA + 8)) + list(range(O_AB + 4, O_AB + 8))
    cols += misc + [-1] * (LANES - len(misc))
    cols += _slab_cols([O_SQ + hd * i for i in range(4)] + [O_SK + hd * i for i in range(2)]
                       + [O_SV + hd * i for i in range(2)], hd)
    cols += _slab_cols([O_GQ + hd * i for i in range(4)] + [O_GK + hd * i for i in range(2)]
                       + [O_GV + hd * i for i in range(2)], hd)
    assert len(cols) == N_COL
    qd = MLA_NOPE + MLA_ROPE
    uq_cols = []
    for hh in range(MLA_HEADS):
        uq_cols += list(range(hh * qd, hh * qd + qd)) + [-1] * (LANES - qd)
    kvd = MLA_NOPE + MLA_V
    ukv_cols = _slab_cols([hh * kvd for hh in range(MLA_HEADS)], MLA_NOPE)
    ukv_cols += _slab_cols([hh * kvd + MLA_NOPE for hh in range(MLA_HEADS)], MLA_V)
    orow = list(range(0, 256)) + _slab_cols([256 + hd * i for i in range(12)], hd)
    alog = jnp.zeros((1, LANES), F32)
    dtb = jnp.zeros((1, LANES), F32)
    for dd in range(2):
        alog = alog.at[0, 32 + 8 * dd:36 + 8 * dd].set(gdn_a_log[l, dd])
        dtb = dtb.at[0, 32 + 8 * dd:36 + 8 * dd].set(gdn_dt_bias[l, dd])
    return dict(
        win=_gather_cols(w_in[l], cols).astype(BF16),
        conv=gdn_conv_w[l][:, :768],
        alog=alog, dtb=dtb,
        gdn_g=jnp.concatenate([gdn_norm_g[l], gdn_norm_g[l]]).reshape(1, LANES),
        mla_qn_g=mla_qn_g[l], mla_kvn_g=mla_kvn_g[l],
        wuq=_gather_cols(mla_w_uq[l], uq_cols).astype(BF16),
        wukv=_gather_cols(mla_w_ukv[l], ukv_cols).astype(BF16),
        mqg=_pad_row(mla_qk_g[l, 0]), mkg=_pad_row(mla_qk_g[l, 1, :MLA_NOPE]),
        krg=_pad_row(mla_qk_g[l, 1, MLA_NOPE:]),
        sqg=_pad_row(swa_qk_g[l, 0]), skg=_pad_row(swa_qk_g[l, 1]),
        gqg=_pad_row(gqa_qk_g[l, 0]), gkg=_pad_row(gqa_qk_g[l, 1]),
        wout=_gather_cols(w_out[l].T, orow).T.astype(BF16),
        rw=jnp.concatenate([router_w[l], jnp.zeros((D_MODEL, LANES - N_EXPERTS), F32)], axis=1),
        rb=_pad_row(router_bias[l]),
        sink=swa_sink[l],
    )


def _sink_rows(sink, grp, tq):
    hkv = sink.shape[0] // grp
    return jnp.repeat((sink.astype(F32) * LOG2E).reshape(hkv, grp), tq, axis=1).reshape(hkv, grp * tq, 1)


def kernel(x, c, ctx, c_ctx, w_mod, b_mod, norm1_g, norm2_g, w_in, gdn_conv_w, gdn_a_log, gdn_dt_bias,
           gdn_norm_g, mla_qn_g, mla_kvn_g, mla_w_uq, mla_w_ukv, mla_qk_g, swa_qk_g, swa_sink, gqa_qk_g,
           w_out, router_w, router_bias, exp_w_gu, exp_w_down, shared_w_gu, shared_w_down):
    bsz, t_len, d = x.shape
    n_ctx = ctx.shape[1]
    depth = w_mod.shape[0]
    s = t_len + n_ctx
    assert d == D_MODEL and t_len % TM == 0 and n_ctx % TM == 0 and t_len >= 3 * WINDOW
    assert bsz + 1 <= 8
    nbl = t_len // TM
    nb = s // TM

    tabs = (_rope_slab_tables(t_len, n_ctx, HEAD_DIM, 0) + _rope_slab_tables(t_len, n_ctx, MLA_ROPE, MLA_NOPE)
            + _rope_slab_tables(t_len, n_ctx, MLA_ROPE, 0))
    c8 = jnp.zeros((8, d), F32).at[:bsz].set(c).at[bsz].set(c_ctx)
    xs = jnp.concatenate([x, ctx], axis=1)

    tq_d = 512 if t_len % 512 == 0 else TM
    tk_d = 8448 if (s % 8448 == 0) else TM

    moe_buf = None
    for l in range(depth):
        need_ctx = l < depth - 1
        lw = _layer_weights(l, w_in, gdn_conv_w, gdn_a_log, gdn_dt_bias, gdn_norm_g, mla_qn_g, mla_kvn_g,
                            mla_w_uq, mla_w_ukv, mla_qk_g, swa_qk_g, swa_sink, gqa_qk_g, w_out, router_w,
                            router_bias)
        mod = _modulation(c8, w_mod[l], b_mod[l])
        mod_tab = jnp.stack([mod[:bsz], jnp.broadcast_to(mod[bsz], (bsz, 6 * d))], axis=1)[:, :, None, :]

        (ga, misc, mq, mk, mv, sq, sk, sv, gq, gk, gv) = _inproj(
            xs, mod_tab, norm1_g[l].reshape(1, d), lw['win'], lw, tabs, nbl)

        qkv, lab = _gdn_prep(ga, misc, lw['conv'], lw['alog'], lw['dtb'], nbl)
        labt = jnp.swapaxes(lab[..., :8], 2, 3)
        o_f, o_b = _gdn_scan(qkv, lab, labt, nbl=nbl)

        mo = _flash(mq, mk, mv, grp=1, tq=2 * tq_d, tk=tk_d, q_blk0=0, nq=t_len // (2 * tq_d),
                    k_blk0=0, nk=s // tk_d)
        go = _flash(gq, gk, gv, grp=2, tq=tq_d, tk=tk_d, q_blk0=0, nq=t_len // tq_d, k_blk0=0, nk=s // tk_d)
        so = _swa(sq, sk, sv, _sink_rows(lw['sink'], 2, WINDOW), grp=2, tq=tq_d, t_len=t_len, n_ctx=n_ctx)
        lat = (mo, so, go)
        ctx_out = lat
        if need_ctx:
            cb0 = t_len // n_ctx
            ctx_out = (
                _flash(mq, mk, mv, grp=1, tq=n_ctx, tk=n_ctx, q_blk0=cb0, nq=1, k_blk0=cb0, nk=1),
                _flash(sq, sk, sv, grp=2, tq=n_ctx, tk=n_ctx, q_blk0=cb0, nq=1, k_blk0=cb0, nk=1,
                       sink_rows=_sink_rows(lw['sink'], 2, n_ctx)),
                _flash(gq, gk, gv, grp=2, tq=n_ctx, tk=n_ctx, q_blk0=cb0, nq=1, k_blk0=cb0, nk=1))

        n_blk = nb if need_ctx else nbl
        x1, h2, idx, gate, rank, base, end = _outproj(
            xs, mod_tab, o_f, o_b, ga, lw['gdn_g'], lat, ctx_out, lw['wout'], norm2_g[l].reshape(1, d),
            lw['rw'], lw['rb'], n_blk=n_blk, nbl=nbl)

        n_tok = bsz * n_blk * TM
        assert n_tok % MOE_CHUNK == 0
        g5 = jnp.repeat(mod_tab[:, :, 0, 5 * d:], jnp.array([nbl, nb - nbl]), axis=1,
                        total_repeat_length=nb)[:, :n_blk].reshape(bsz * n_blk, 1, d)
        xs, moe_buf = _moe(x1.reshape(n_tok, d), g5, h2.reshape(n_tok, d), idx.reshape(n_tok, LANES),
                           rank.reshape(n_tok, LANES), gate.reshape(n_tok, LANES), base, end, exp_w_gu,
                           exp_w_down, l, shared_w_gu[l].astype(BF16), shared_w_down[l].astype(BF16),
                           buf=moe_buf)
        xs = xs.reshape(bsz, n_blk * TM, d)
    return xs[:, :t_len]
```
